```python
import math
import jax, jax.numpy as jnp
from jax import lax
import numpy as np

D_MODEL = 1024
BATCH = 4
SEQ = 4096
DEPTH = 2

N_A_LAYERS = (DEPTH + 1) // 2
N_B_LAYERS = DEPTH - N_A_LAYERS

A_HEADS = 8
A_EXPAND = 128
A_FDIM = A_HEADS * A_EXPAND
A_VDIM = D_MODEL // A_HEADS
A_CHUNK = 32

B_HEADS = 8
B_HEAD_DIM = D_MODEL // B_HEADS // 2
B_VDIM = 2 * B_HEAD_DIM
B_QK_DIM = B_HEADS * 2 * B_HEAD_DIM
B_QBLOCK = 128

N_GROUPS = 4
EXPERTS_PER_GROUP = 8
N_EXPERTS = N_GROUPS * EXPERTS_PER_GROUP
TOP_K_INNER = 2
D_EXPERT = 512

NORM_EPS = 1e-6
SUBLN_EPS = 1e-5

kernel_name = "yoco_hgrn2_diffattn_hmoe"


def rmsnorm(x, g, eps=NORM_EPS):
    xf = x.astype(jnp.float32)
    y = xf * lax.rsqrt(jnp.mean(xf * xf, axis=-1, keepdims=True) + eps)
    return y.astype(x.dtype) * g


def gla_chunk(q, k, v, log_f):
    Bsz, S, H, K = q.shape
    V = v.shape[-1]
    N = S // A_CHUNK

    def to_chunks(t):
        return t.reshape(Bsz, N, A_CHUNK, H, t.shape[-1]).transpose(1, 0, 3, 2, 4)

    qc, kc, vc, bc = to_chunks(q), to_chunks(k), to_chunks(v), to_chunks(log_f)
    bc = jnp.cumsum(bc, axis=-2)
    causal = jnp.tril(jnp.ones((A_CHUNK, A_CHUNK), dtype=bool))

    def step(state, inp):
        qn, kn, vn, bn = inp
        b_last = bn[..., -1:, :]
        q_dec = qn * jnp.exp(bn)
        k_inv = kn * jnp.exp(-bn)
        scores = jnp.where(causal, jnp.einsum('bhck,bhsk->bhcs', q_dec, k_inv), 0.0)
        o = (jnp.einsum('bhcs,bhsv->bhcv', scores, vn)
             + jnp.einsum('bhck,bhkv->bhcv', q_dec, state))
        k_to_end = kn * jnp.exp(b_last - bn)
        state = (jnp.exp(b_last)[..., 0, :, None] * state
                 + jnp.einsum('bhck,bhcv->bhkv', k_to_end, vn))
        return state, o

    s0 = jnp.zeros((Bsz, H, K, V), jnp.float32)
    _, o = lax.scan(step, s0, (qc, kc, vc, bc))
    return o.transpose(1, 0, 3, 2, 4).reshape(Bsz, S, H, V)


def hgrn2_mixer(h, w_in, lb, onorm_g, w_out):
    Bsz, S, _ = h.shape
    proj = h @ w_in
    q, fz, i_in, g = jnp.split(proj, [A_FDIM, 2 * A_FDIM, 2 * A_FDIM + D_MODEL], axis=-1)
    q = jax.nn.silu(q.astype(jnp.float32))
    fz = fz.astype(jnp.float32)
    lb32 = lb.astype(jnp.float32)
    log_f = jnp.log(lb32 + (1.0 - lb32) * jax.nn.sigmoid(fz))
    k = (1.0 - lb32) * jax.nn.sigmoid(-fz)
    o = gla_chunk(q.reshape(Bsz, S, A_HEADS, A_EXPAND),
                  k.reshape(Bsz, S, A_HEADS, A_EXPAND),
                  i_in.astype(jnp.float32).reshape(Bsz, S, A_HEADS, A_VDIM),
                  log_f.reshape(Bsz, S, A_HEADS, A_EXPAND))
    gate = jax.nn.silu(g.reshape(Bsz, S, A_HEADS, A_VDIM))
    o = (rmsnorm(o, onorm_g) * gate).astype(h.dtype)
    return o.reshape(Bsz, S, D_MODEL) @ w_out


def shared_kv(h, kv_norm_g, w_kv):
    Bsz, S, _ = h.shape
    kv = rmsnorm(h, kv_norm_g) @ w_kv
    k_all, v_all = jnp.split(kv, [B_QK_DIM], axis=-1)
    k_sh = k_all.reshape(Bsz, S, B_HEADS, 2, B_HEAD_DIM).transpose(3, 0, 2, 1, 4)
    v_sh = v_all.reshape(Bsz, S, B_HEADS, B_VDIM).transpose(0, 2, 1, 3)
    return k_sh, v_sh


def diff_attention(h, k_sh, v_sh, w_q, lam, subln_g, w_out, lambda_init):
    Bsz, S, _ = h.shape
    q = (h @ w_q).reshape(Bsz, S, B_HEADS, 2, B_HEAD_DIM)
    lam32 = lam.astype(jnp.float32)
    lam_full = (jnp.exp(jnp.sum(lam32[0] * lam32[1])) - jnp.exp(jnp.sum(lam32[2] * lam32[3]))
                + lambda_init)
    nb = S // B_QBLOCK
    qb = q.reshape(Bsz, nb, B_QBLOCK, B_HEADS, 2, B_HEAD_DIM).transpose(1, 4, 0, 3, 2, 5)
    key_pos = jnp.arange(S)
    scale = B_HEAD_DIM ** -0.5

    def block(args):
        qblk, bi = args
        q_pos = bi * B_QBLOCK + jnp.arange(B_QBLOCK)
        mask = key_pos[None, :] <= q_pos[:, None]
        s = jnp.einsum('mbhqd,mbhkd->mbhqk', qblk, k_sh).astype(jnp.float32) * scale
        p = jax.nn.softmax(jnp.where(mask, s, -jnp.inf), axis=-1)
        p = p[0] - lam_full * p[1]
        return jnp.einsum('bhqk,bhkv->bhqv', p.astype(v_sh.dtype), v_sh)

    o = lax.map(block, (qb, jnp.arange(nb)))
    o = o.transpose(1, 0, 3, 2, 4).reshape(Bsz, S, B_HEADS, B_VDIM)
    o = rmsnorm(o, subln_g, SUBLN_EPS) * (1.0 - lambda_init)
    return o.reshape(Bsz, S, D_MODEL).astype(h.dtype) @ w_out


def hier_moe(h, wg, bg, we, be, w_gu, w_down):
    Bsz, S, D = h.shape
    t = h.reshape(-1, D)
    T = t.shape[0]
    coarse = jax.nn.softmax((t @ wg + bg).astype(jnp.float32), axis=-1)
    grp = jnp.argmax(coarse, axis=-1)
    p_grp = jnp.take_along_axis(coarse, grp[:, None], axis=-1)
    fine = (t @ we + be).astype(jnp.float32).reshape(T, N_GROUPS, EXPERTS_PER_GROUP)
    fine = jnp.take_along_axis(fine, grp[:, None, None], axis=1)[:, 0]
    top_p, top_i = lax.top_k(jax.nn.softmax(fine, axis=-1), TOP_K_INNER)
    w = p_grp * top_p / jnp.sum(top_p, axis=-1, keepdims=True)
    eid = grp[:, None] * EXPERTS_PER_GROUP + top_i
    gates = jnp.sum(jax.nn.one_hot(eid, N_EXPERTS, dtype=jnp.float32) * w[..., None], axis=1)

    def expert(acc, inp):
        wgu, wd, g = inp
        a, u = jnp.split(t @ wgu, 2, axis=-1)
        y = (jax.nn.silu(a) * u) @ wd
        return acc + g[:, None].astype(y.dtype) * y, None

    out, _ = lax.scan(expert, jnp.zeros_like(t), (w_gu, w_down, gates.T))
    return out.reshape(Bsz, S, D)


def setup_inputs(seed: int = 0) -> dict:
    key = jax.random.key(seed)
    ks = jax.random.split(key, 24)

    def dense(k, shape, fan_in):
        return jax.random.normal(k, shape, jnp.float32) * (fan_in ** -0.5)

    def gain(k, shape):
        return 1.0 + 0.02 * jax.random.normal(k, shape, jnp.float32)

    return {
        "x": jax.random.normal(ks[0], (BATCH, SEQ, D_MODEL), jnp.float32),
        "a_norm_g": gain(ks[1], (N_A_LAYERS, D_MODEL)),
        "a_w_in": dense(ks[2], (N_A_LAYERS, D_MODEL, 2 * A_FDIM + 2 * D_MODEL), D_MODEL),
        "a_lb": 0.1 * jax.random.normal(ks[3], (N_A_LAYERS + 1, A_FDIM), jnp.float32),
        "a_onorm_g": gain(ks[4], (N_A_LAYERS, A_VDIM)),
        "a_w_out": dense(ks[5], (N_A_LAYERS, D_MODEL, D_MODEL), D_MODEL),
        "kv_norm_g": gain(ks[6], (D_MODEL,)),
        "w_kv": dense(ks[7], (D_MODEL, B_QK_DIM + B_HEADS * B_VDIM), D_MODEL),
        "b_norm_g": gain(ks[8], (N_B_LAYERS, D_MODEL)),
        "b_w_q": dense(ks[9], (N_B_LAYERS, D_MODEL, B_QK_DIM), D_MODEL),
        "b_lam": 0.1 * jax.random.normal(ks[10], (N_B_LAYERS, 4, B_HEAD_DIM), jnp.float32),
        "b_subln_g": gain(ks[11], (N_B_LAYERS, B_VDIM)),
        "b_w_out": dense(ks[12], (N_B_LAYERS, D_MODEL, D_MODEL), D_MODEL),
        "ffn_norm_g": gain(ks[13], (DEPTH, D_MODEL)),
        "router_g_w": dense(ks[14], (DEPTH, D_MODEL, N_GROUPS), D_MODEL),
        "router_g_b": 0.01 * jax.random.normal(ks[15], (DEPTH, N_GROUPS), jnp.float32),
        "router_e_w": dense(ks[16], (DEPTH, D_MODEL, N_EXPERTS), D_MODEL),
        "router_e_b": 0.01 * jax.random.normal(ks[17], (DEPTH, N_EXPERTS), jnp.float32),
        "expert_w_gu": dense(ks[18], (DEPTH, N_EXPERTS, D_MODEL, 2 * D_EXPERT), D_MODEL),
        "expert_w_down": dense(ks[19], (DEPTH, N_EXPERTS, D_EXPERT, D_MODEL), D_EXPERT),
        "final_norm_g": gain(ks[20], (D_MODEL,)),
    }


def reference(x, a_norm_g, a_w_in, a_lb, a_onorm_g, a_w_out, kv_norm_g, w_kv, b_norm_g, b_w_q,
              b_lam, b_subln_g, b_w_out, ffn_norm_g, router_g_w, router_g_b, router_e_w,
              router_e_b, expert_w_gu, expert_w_down, final_norm_g):
    lb_all = jnp.cumsum(jax.nn.softmax(a_lb.astype(jnp.float32), axis=0), axis=0)
    h = x
    k_sh = v_sh = None
    for layer in range(DEPTH):
        if layer < N_A_LAYERS:
            a = layer
            h = h + hgrn2_mixer(rmsnorm(h, a_norm_g[a]), a_w_in[a], lb_all[a],
                                a_onorm_g[a], a_w_out[a])
        else:
            if layer == N_A_LAYERS:
                k_sh, v_sh = shared_kv(h, kv_norm_g, w_kv)
            b = layer - N_A_LAYERS
            lambda_init = 0.8 - 0.6 * math.exp(-0.3 * layer)
            h = h + diff_attention(rmsnorm(h, b_norm_g[b]), k_sh, v_sh, b_w_q[b], b_lam[b],
                                   b_subln_g[b], b_w_out[b], lambda_init)
        h = h + hier_moe(rmsnorm(h, ffn_norm_g[layer]), router_g_w[layer], router_g_b[layer],
                         router_e_w[layer], router_e_b[layer], expert_w_gu[layer],
                         expert_w_down[layer])
    return rmsnorm(h, final_norm_g)
```

```python
import functools
import math

import jax
import jax.numpy as jnp
from jax import lax
from jax.experimental import pallas as pl
from jax.experimental.pallas import tpu as pltpu

F32 = jnp.float32
BF16 = jnp.bfloat16
I32 = jnp.int32

D_MODEL = 1024
A_HEADS = 8
A_HEAD_DIM = 128
B_HEADS = 8
B_HEAD_DIM = 64
N_GROUPS = 4
EXPERTS_PER_GROUP = 8
N_EXPERTS = N_GROUPS * EXPERTS_PER_GROUP
D_EXPERT = 512
NORM_EPS = 1e-6
SUBLN_EPS = 1e-5
LOG2E = 1.4426950408889634

GLA_BLOCK = 128
GLA_HALF = GLA_BLOCK // 2
ATT_TQ = 128
ATT_TK = 256
ROUTER_TM = 512
ROUTER_ROWS = 40
EXPERT_TM = 256
GATHER_TM = 256
VMEM_LIMIT = 56 * 1024 * 1024


def _nt(a, b):
    return lax.dot_general(a, b, (((1,), (1,)), ((), ())), preferred_element_type=F32)


def _nn(a, b):
    return jnp.dot(a, b, preferred_element_type=F32)


def _params(*sem):
    return pltpu.CompilerParams(dimension_semantics=sem, vmem_limit_bytes=VMEM_LIMIT)


def _rms_proj_kernel(x_ref, g_ref, w_ref, o_ref, *, n_chunks):
    x = x_ref[...]
    y = x * lax.rsqrt(jnp.mean(x * x, axis=-1, keepdims=True) + NORM_EPS)
    xn = (y * g_ref[...]).astype(BF16)
    n = w_ref.shape[1] // n_chunks
    for c in range(n_chunks):
        o_ref[:, c * n:(c + 1) * n] = _nn(xn, w_ref[:, c * n:(c + 1) * n])


def rms_proj(x, g, w_bf16, tm=256):
    t, d = x.shape
    n = w_bf16.shape[1]
    return pl.pallas_call(
        functools.partial(_rms_proj_kernel, n_chunks=n // d),
        grid=(t // tm,),
        in_specs=[pl.BlockSpec((tm, d), lambda i: (i, 0)),
                  pl.BlockSpec((1, d), lambda i: (0, 0)),
                  pl.BlockSpec((d, n), lambda i: (0, 0))],
        out_specs=pl.BlockSpec((tm, n), lambda i: (i, 0)),
        out_shape=jax.ShapeDtypeStruct((t, n), F32),
        compiler_params=_params("parallel"),
        name="rms_proj",
    )(x, g.reshape(1, d), w_bf16)


def _gla_kernel(q_ref, f_ref, i_ref, g_ref, alb_ref, og_ref, o_ref, st_ref):
    c = pl.program_id(1)

    @pl.when(c == 0)
    def _():
        st_ref[...] = jnp.zeros_like(st_ref)

    alb = alb_ref[...]
    e = jnp.exp(alb - jnp.max(alb, axis=0, keepdims=True))
    lb = e[0:1] / jnp.sum(e, axis=0, keepdims=True)
    fz = f_ref[...]
    logf = jnp.log(lb + (1.0 - lb) * jax.nn.sigmoid(fz))
    kk = (1.0 - lb) * jax.nn.sigmoid(-fz)
    qq = jax.nn.silu(q_ref[...])

    r = lax.broadcasted_iota(I32, (GLA_BLOCK, GLA_BLOCK), 0)
    s = lax.broadcasted_iota(I32, (GLA_BLOCK, GLA_BLOCK), 1)
    tril = (s <= r).astype(BF16)
    hi = logf.astype(BF16)
    lo = (logf - hi.astype(F32)).astype(BF16)
    b = _nn(tril, hi) + _nn(tril, lo)

    h0, h1 = slice(0, GLA_HALF), slice(GLA_HALF, GLA_BLOCK)
    b_a_mid = b[GLA_HALF // 2 - 1:GLA_HALF // 2]
    b_a_end = b[GLA_HALF - 1:GLA_HALF]
    b_b_mid = b[GLA_HALF + GLA_HALF // 2 - 1:GLA_HALF + GLA_HALF // 2]
    b_end = b[GLA_BLOCK - 1:GLA_BLOCK]

    qa_n = (qq[h0] * jnp.exp(b[h0] - b_a_mid)).astype(BF16)
    ka_n = (kk[h0] * jnp.exp(b_a_mid - b[h0])).astype(BF16)
    qb_n = (qq[h1] * jnp.exp(b[h1] - b_b_mid)).astype(BF16)
    kb_n = (kk[h1] * jnp.exp(b_b_mid - b[h1])).astype(BF16)
    qb_x = (qq[h1] * jnp.exp(b[h1] - b_a_end)).astype(BF16)
    ka_x = (kk[h0] * jnp.exp(b_a_end - b[h0])).astype(BF16)
    q_dec = (qq * jnp.exp(b)).astype(BF16)
    k_end = (kk * jnp.exp(b_end - b)).astype(BF16)
    d_end = jnp.exp(b_end)

    rr = lax.broadcasted_iota(I32, (GLA_HALF, GLA_HALF), 0)
    ss = lax.broadcasted_iota(I32, (GLA_HALF, GLA_HALF), 1)
    causal = ss <= rr
    og = og_ref[...]

    for h in range(A_HEADS):
        hs = slice(h * A_HEAD_DIM, (h + 1) * A_HEAD_DIM)
        v = i_ref[:, hs]
        vb = v.astype(BF16)
        st = st_ref[h]
        st_b = st.astype(BF16)
        p_aa = jnp.where(causal, _nt(qa_n[:, hs], ka_n[:, hs]), 0.0).astype(BF16)
        p_bb = jnp.where(causal, _nt(qb_n[:, hs], kb_n[:, hs]), 0.0).astype(BF16)
        p_ba = _nt(qb_x[:, hs], ka_x[:, hs]).astype(BF16)
        inter = _nt(q_dec[:, hs], st_b)
        o_a = _nn(p_aa, vb[h0]) + inter[h0]
        o_b = _nn(p_bb, vb[h1]) + _nn(p_ba, vb[h0]) + inter[h1]
        st_ref[h] = d_end[:, hs] * st + _nn(v.T.astype(BF16), k_end[:, hs])
        gate = jax.nn.silu(g_ref[:, hs])
        for rows, o in ((h0, o_a), (h1, o_b)):
            on = o * lax.rsqrt(jnp.mean(o * o, axis=-1, keepdims=True) + NORM_EPS)
            o_ref[rows, hs] = (on * og * gate[rows]).astype(o_ref.dtype)


def gla(proj, a_lb, onorm_g, batch, seq):
    t = proj.shape[0]
    d = D_MODEL
    nblk = seq // GLA_BLOCK

    def col(j):
        return pl.BlockSpec((GLA_BLOCK, d), lambda b, c, j=j: (b * nblk + c, j))

    return pl.pallas_call(
        _gla_kernel,
        grid=(batch, nblk),
        in_specs=[col(0), col(1), col(2), col(3),
                  pl.BlockSpec(a_lb.shape, lambda b, c: (0, 0)),
                  pl.BlockSpec((1, A_HEAD_DIM), lambda b, c: (0, 0))],
        out_specs=pl.BlockSpec((GLA_BLOCK, d), lambda b, c: (b * nblk + c, 0)),
        out_shape=jax.ShapeDtypeStruct((t, d), BF16),
        scratch_shapes=[pltpu.VMEM((A_HEADS, A_HEAD_DIM, A_HEAD_DIM), F32)],
        compiler_params=_params("parallel", "arbitrary"),
        name="gla",
    )(proj, proj, proj, proj, a_lb, onorm_g.reshape(1, A_HEAD_DIM))


def _proj_res_kernel(a_ref, w_ref, r_ref, o_ref):
    o_ref[...] = r_ref[...] + _nn(a_ref[...], w_ref[...])


def proj_res(a_bf16, w_bf16, res, tm=512):
    t, d = res.shape
    return pl.pallas_call(
        _proj_res_kernel,
        grid=(t // tm,),
        in_specs=[pl.BlockSpec((tm, d), lambda i: (i, 0)),
                  pl.BlockSpec((d, d), lambda i: (0, 0)),
                  pl.BlockSpec((tm, d), lambda i: (i, 0))],
        out_specs=pl.BlockSpec((tm, d), lambda i: (i, 0)),
        out_shape=jax.ShapeDtypeStruct((t, d), F32),
        compiler_params=_params("parallel"),
        name="proj_res",
    )(a_bf16, w_bf16, res)


def _router_kernel(h_ref, g_ref, wh_ref, wl_ref, b_ref, xn_ref, eid_ref, wgt_ref, rank_ref, cnt_ref):
    i = pl.program_id(0)
    tm = h_ref.shape[0]

    @pl.when(i == 0)
    def _():
        cnt_ref[...] = jnp.zeros_like(cnt_ref)

    x = h_ref[...]
    xn = x * lax.rsqrt(jnp.mean(x * x, axis=-1, keepdims=True) + NORM_EPS) * g_ref[...]
    xn_ref[...] = xn
    xh = xn.astype(BF16)
    xl = (xn - xh.astype(F32)).astype(BF16)
    wh = wh_ref[...]
    lg = _nt(wh, xh) + _nt(wl_ref[...], xh) + _nt(wh, xl) + b_ref[...]

    gl = lg[0:N_GROUPS]
    r4 = lax.broadcasted_iota(I32, gl.shape, 0)
    gmax = jnp.max(gl, axis=0, keepdims=True)
    grp = jnp.min(jnp.where(gl == gmax, r4, N_GROUPS), axis=0, keepdims=True)
    p_grp = 1.0 / jnp.sum(jnp.exp(gl - gmax), axis=0, keepdims=True)

    fine = lg[8:8 + EXPERTS_PER_GROUP]
    for gi in range(1, N_GROUPS):
        fine = jnp.where(grp == gi, lg[8 + gi * EXPERTS_PER_GROUP:8 + (gi + 1) * EXPERTS_PER_GROUP], fine)
    r8 = lax.broadcasted_iota(I32, fine.shape, 0)
    m1 = jnp.max(fine, axis=0, keepdims=True)
    i1 = jnp.min(jnp.where(fine == m1, r8, EXPERTS_PER_GROUP), axis=0, keepdims=True)
    rest = jnp.where(r8 == i1, -jnp.inf, fine)
    m2 = jnp.max(rest, axis=0, keepdims=True)
    i2 = jnp.min(jnp.where(rest == m2, r8, EXPERTS_PER_GROUP), axis=0, keepdims=True)
    e21 = jnp.exp(m2 - m1)
    t1 = 1.0 / (1.0 + e21)
    wgt_ref[0:1, :] = p_grp * t1
    wgt_ref[1:2, :] = p_grp * (e21 * t1)
    e1 = grp * EXPERTS_PER_GROUP + i1
    e2 = grp * EXPERTS_PER_GROUP + i2
    eid_ref[0:1, :] = e1
    eid_ref[1:2, :] = e2

    r32 = lax.broadcasted_iota(I32, (N_EXPERTS, tm), 0)
    is1 = r32 == e1
    is2 = r32 == e2
    member = jnp.logical_or(is1, is2)
    ta = lax.broadcasted_iota(I32, (tm, tm), 0)
    tb = lax.broadcasted_iota(I32, (tm, tm), 1)
    before = (ta < tb).astype(BF16)
    prior = _nn(member.astype(BF16), before) + cnt_ref[:, 0:1]
    rank_ref[0:1, :] = jnp.sum(jnp.where(is1, prior, 0.0), axis=0, keepdims=True).astype(I32)
    rank_ref[1:2, :] = jnp.sum(jnp.where(is2, prior, 0.0), axis=0, keepdims=True).astype(I32)
    cnt_ref[...] = cnt_ref[...] + jnp.sum(member.astype(F32), axis=1, keepdims=True)


def router(h, g, wg, bg, we, be, tm=ROUTER_TM):
    t, d = h.shape
    w_all = jnp.zeros((ROUTER_ROWS, d), F32).at[0:N_GROUPS].set(wg.T).at[8:8 + N_EXPERTS].set(we.T)
    b_all = jnp.zeros((ROUTER_ROWS, 1), F32).at[0:N_GROUPS, 0].set(bg).at[8:8 + N_EXPERTS, 0].set(be)
    wh = w_all.astype(BF16)
    wl = (w_all - wh.astype(F32)).astype(BF16)
    row2 = lambda i: (0, i)
    return pl.pallas_call(
        _router_kernel,
        grid=(t // tm,),
        in_specs=[pl.BlockSpec((tm, d), lambda i: (i, 0)),
                  pl.BlockSpec((1, d), lambda i: (0, 0)),
                  pl.BlockSpec((ROUTER_ROWS, d), lambda i: (0, 0)),
                  pl.BlockSpec((ROUTER_ROWS, d), lambda i: (0, 0)),
                  pl.BlockSpec((ROUTER_ROWS, 1), lambda i: (0, 0))],
        out_specs=[pl.BlockSpec((tm, d), lambda i: (i, 0)),
                   pl.BlockSpec((2, tm), row2),
                   pl.BlockSpec((2, tm), row2),
                   pl.BlockSpec((2, tm), row2),
                   pl.BlockSpec((N_EXPERTS, 128), lambda i: (0, 0))],
        out_shape=[jax.ShapeDtypeStruct((t, d), F32),
                   jax.ShapeDtypeStruct((2, t), I32),
                   jax.ShapeDtypeStruct((2, t), F32),
                   jax.ShapeDtypeStruct((2, t), I32),
                   jax.ShapeDtypeStruct((N_EXPERTS, 128), F32)],
        compiler_params=_params("arbitrary"),
        name="router",
    )(h, g.reshape(1, d), wh, wl, b_all)


def _dispatch_kernel(pos_ref, x_ref, init_ref, o_ref, sem):
    del init_ref
    tm = x_ref.shape[0]

    def row_copy(r, k):
        return pltpu.make_async_copy(x_ref.at[pl.ds(r, 1)],
                                     o_ref.at[pl.ds(pos_ref[0, 0, k * tm + r], 1)], sem)

    def issue(r, carry):
        row_copy(r, 0).start()
        row_copy(r, 1).start()
        return carry

    def drain(r, carry):
        row_copy(r, 0).wait()
        row_copy(r, 1).wait()
        return carry

    lax.fori_loop(0, tm, issue, 0)
    lax.fori_loop(0, tm, drain, 0)


def dispatch(xn, pos3, n_rows, tm=GATHER_TM):
    t, d = xn.shape
    init = jnp.zeros((n_rows, d), xn.dtype)
    return pl.pallas_call(
        _dispatch_kernel,
        grid=(t // tm,),
        in_specs=[pl.BlockSpec((1, 1, 2 * tm), lambda i: (i, 0, 0), memory_space=pltpu.SMEM),
                  pl.BlockSpec((tm, d), lambda i: (i, 0)),
                  pl.BlockSpec(memory_space=pl.ANY)],
        out_specs=pl.BlockSpec(memory_space=pl.ANY),
        out_shape=jax.ShapeDtypeStruct((n_rows, d), xn.dtype),
        scratch_shapes=[pltpu.SemaphoreType.DMA(())],
        input_output_aliases={2: 0},
        compiler_params=_params("arbitrary"),
        name="moe_dispatch",
    )(pos3, xn, init)


def _expert_kernel(te_ref, nv_ref, x_ref, wgu_ref, wd_ref, y_ref, wgu_b, wd_b):
    i = pl.program_id(0)
    valid = i < nv_ref[0]
    changed = te_ref[i] != te_ref[jnp.maximum(i - 1, 0)]
    first = jnp.logical_or(i == 0, changed)

    @pl.when(jnp.logical_and(valid, first))
    def _():
        wgu_b[...] = wgu_ref[0].astype(BF16)
        wd_b[...] = wd_ref[0].astype(BF16)

    @pl.when(valid)
    def _():
        x = x_ref[...].astype(BF16)
        au = _nn(x, wgu_b[...])
        a = au[:, :D_EXPERT]
        u = au[:, D_EXPERT:]
        mid = (jax.nn.silu(a) * u).astype(BF16)
        y_ref[...] = _nn(mid, wd_b[...])

    @pl.when(jnp.logical_not(valid))
    def _():
        y_ref[...] = jnp.zeros_like(y_ref)


def experts(xg, w_gu, w_down, tile_expert, n_valid, tm=EXPERT_TM):
    p, d = xg.shape
    n_tiles = p // tm
    row = lambda i, te, nv: (jnp.minimum(i, nv[0] - 1), 0)
    wsel = lambda i, te, nv: (te[i], 0, 0)
    return pl.pallas_call(
        _expert_kernel,
        grid_spec=pltpu.PrefetchScalarGridSpec(
            num_scalar_prefetch=2,
            grid=(n_tiles,),
            in_specs=[pl.BlockSpec((tm, d), row),
                      pl.BlockSpec((1, d, 2 * D_EXPERT), wsel),
                      pl.BlockSpec((1, D_EXPERT, d), wsel)],
            out_specs=pl.BlockSpec((tm, d), lambda i, te, nv: (i, 0)),
            scratch_shapes=[pltpu.VMEM((d, 2 * D_EXPERT), BF16),
                            pltpu.VMEM((D_EXPERT, d), BF16)]),
        out_shape=jax.ShapeDtypeStruct((p, d), F32),
        compiler_params=_params("arbitrary"),
        name="moe_experts",
    )(tile_expert, n_valid, xg, w_gu, w_down)


def _combine_kernel(pos_ref, h_ref, w_ref, y_ref, g_ref, o_ref, buf, sem, *, final_norm):
    tm = h_ref.shape[0]

    def row_copy(r, k):
        return pltpu.make_async_copy(y_ref.at[pl.ds(pos_ref[0, 0, k * tm + r], 1)],
                                     buf.at[k, pl.ds(r, 1)], sem)

    def issue(r, carry):
        row_copy(r, 0).start()
        row_copy(r, 1).start()
        return carry

    def drain(r, carry):
        row_copy(r, 0).wait()
        row_copy(r, 1).wait()
        return carry

    lax.fori_loop(0, tm, issue, 0)
    lax.fori_loop(0, tm, drain, 0)
    w = w_ref[...]
    out = h_ref[...] + w[:, 0:1] * buf[0] + w[:, 1:2] * buf[1]
    if final_norm:
        out = out * lax.rsqrt(jnp.mean(out * out, axis=-1, keepdims=True) + NORM_EPS) * g_ref[...]
    o_ref[...] = out


def combine(h, wgt_t, pos3, y, g, final_norm, tm=GATHER_TM):
    t, d = h.shape
    return pl.pallas_call(
        functools.partial(_combine_kernel, final_norm=final_norm),
        grid=(t // tm,),
        in_specs=[pl.BlockSpec((1, 1, 2 * tm), lambda i: (i, 0, 0), memory_space=pltpu.SMEM),
                  pl.BlockSpec((tm, d), lambda i: (i, 0)),
                  pl.BlockSpec((tm, 2), lambda i: (i, 0)),
                  pl.BlockSpec(memory_space=pl.ANY),
                  pl.BlockSpec((1, d), lambda i: (0, 0))],
        out_specs=pl.BlockSpec((tm, d), lambda i: (i, 0)),
        out_shape=jax.ShapeDtypeStruct((t, d), F32),
        scratch_shapes=[pltpu.VMEM((2, tm, d), F32), pltpu.SemaphoreType.DMA(())],
        compiler_params=_params("arbitrary"),
        name="moe_combine",
    )(pos3, h, wgt_t, y, g.reshape(1, d))


def hier_moe_layer(h, norm_g, wg, bg, we, be, w_gu, w_down, out_g, final_norm):
    t, d = h.shape
    xn, eid, wgt, rank, cnt = router(h, norm_g, wg, bg, we, be)
    counts = cnt[:, 0].astype(I32)
    padded = ((counts + EXPERT_TM - 1) // EXPERT_TM) * EXPERT_TM
    ends = jnp.cumsum(padded)
    offs = ends - padded
    n_rows = 2 * t + N_EXPERTS * EXPERT_TM
    n_tiles = n_rows // EXPERT_TM
    n_valid = (ends[-1] // EXPERT_TM).astype(I32).reshape(1)
    tile_start = jnp.arange(n_tiles, dtype=I32) * EXPERT_TM
    tile_start = jnp.minimum(tile_start, ends[-1] - 1)
    tile_expert = jnp.sum((ends[None, :] <= tile_start[:, None]).astype(I32), axis=1)
    pos = offs[eid] + rank
    nt = t // GATHER_TM
    pos3 = pos.reshape(2, nt, GATHER_TM).transpose(1, 0, 2).reshape(nt, 1, 2 * GATHER_TM)

    xg = dispatch(xn, pos3, n_rows)
    y = experts(xg, w_gu, w_down, tile_expert, n_valid)
    return combine(h, wgt.T, pos3, y, out_g, final_norm)


def _qkv_kernel(h_ref, gq_ref, gkv_ref, wq_ref, wk_ref, wvt_ref, q_ref, k_ref, vt_ref):
    x = h_ref[...]
    y = x * lax.rsqrt(jnp.mean(x * x, axis=-1, keepdims=True) + NORM_EPS)
    xq = (y * gq_ref[...]).astype(BF16)
    xkv = (y * gkv_ref[...]).astype(BF16)
    q_ref[...] = (_nn(xq, wq_ref[...]) * (B_HEAD_DIM ** -0.5 * LOG2E)).astype(q_ref.dtype)
    k_ref[...] = _nn(xkv, wk_ref[...]).astype(k_ref.dtype)
    vt_ref[0] = _nt(wvt_ref[...], xkv).astype(vt_ref.dtype)


def qkv_proj(h, g_q, g_kv, wq, wk, wvt, batch, seq, tm=256):
    t, d = h.shape
    nblk = seq // tm
    full = pl.BlockSpec((d, d), lambda i: (0, 0))
    vec = pl.BlockSpec((1, d), lambda i: (0, 0))
    return pl.pallas_call(
        _qkv_kernel,
        grid=(t // tm,),
        in_specs=[pl.BlockSpec((tm, d), lambda i: (i, 0)), vec, vec, full, full, full],
        out_specs=[pl.BlockSpec((tm, d), lambda i: (i, 0)),
                   pl.BlockSpec((tm, d), lambda i: (i, 0)),
                   pl.BlockSpec((1, d, tm), lambda i: (i // nblk, 0, i % nblk))],
        out_shape=[jax.ShapeDtypeStruct((t, d), BF16),
                   jax.ShapeDtypeStruct((t, d), BF16),
                   jax.ShapeDtypeStruct((batch, d, seq), BF16)],
        compiler_params=_params("parallel"),
        name="qkv_proj",
    )(h, g_q.reshape(1, d), g_kv.reshape(1, d), wq, wk, wvt)


def _attn_kernel(q_ref, k_ref, vt_ref, lam_ref, g_ref, o_ref, *, lambda_init):
    qi = pl.program_id(2)
    tq, tk = ATT_TQ, ATT_TK
    q = q_ref[...]
    lane = lax.broadcasted_iota(I32, q.shape, 1)
    zero = jnp.zeros_like(q)
    qq = jnp.concatenate([jnp.where(lane < B_HEAD_DIM, q, zero),
                          jnp.where(lane >= B_HEAD_DIM, q, zero)], axis=0)
    krow = lax.broadcasted_iota(I32, (tk, 2 * tq), 0)
    qcol = lax.broadcasted_iota(I32, (tk, 2 * tq), 1)
    qpos = qi * tq + jnp.where(qcol >= tq, qcol - tq, qcol)

    def step(j, carry):
        m, l, acc = carry
        kb = k_ref[pl.ds(pl.multiple_of(j * tk, tk), tk), :]
        s = _nt(kb, qq)
        s = jnp.where(j * tk + krow <= qpos, s, -jnp.inf)
        m_new = jnp.maximum(m, jnp.max(s, axis=0, keepdims=True))
        alpha = jnp.exp2(m - m_new)
        p = jnp.exp2(s - m_new)
        l = alpha * l + jnp.sum(p, axis=0, keepdims=True)
        vtb = vt_ref[0, :, pl.ds(pl.multiple_of(j * tk, tk), tk)]
        acc = alpha * acc + _nn(vtb, p.astype(BF16))
        return m_new, l, acc

    n_kv = (qi * tq + tq + tk - 1) // tk
    m0 = jnp.full((1, 2 * tq), -jnp.inf, F32)
    l0 = jnp.zeros((1, 2 * tq), F32)
    a0 = jnp.zeros((2 * B_HEAD_DIM, 2 * tq), F32)
    _, l, acc = lax.fori_loop(0, n_kv, step, (m0, l0, a0))

    lam = lam_ref[...]
    lam_full = (jnp.exp(jnp.sum(lam[0:1] * lam[1:2], axis=-1, keepdims=True))
                - jnp.exp(jnp.sum(lam[2:3] * lam[3:4], axis=-1, keepdims=True)) + lambda_init)
    on = acc / l
    ot = on[:, :tq] - lam_full * on[:, tq:]
    o = ot.T
    o = o * lax.rsqrt(jnp.mean(o * o, axis=-1, keepdims=True) + SUBLN_EPS) * g_ref[...]
    o_ref[...] = (o * (1.0 - lambda_init)).astype(o_ref.dtype)


def diff_attn(q, k, vt, lam, subln_g, batch, seq, lambda_init):
    t, d = q.shape
    nq = seq // ATT_TQ
    hd = 2 * B_HEAD_DIM
    return pl.pallas_call(
        functools.partial(_attn_kernel, lambda_init=lambda_init),
        grid=(batch, B_HEADS, nq),
        in_specs=[pl.BlockSpec((ATT_TQ, hd), lambda b, h, i: (b * nq + i, h)),
                  pl.BlockSpec((seq, hd), lambda b, h, i: (b, h)),
                  pl.BlockSpec((1, hd, seq), lambda b, h, i: (b, h, 0)),
                  pl.BlockSpec(lam.shape, lambda b, h, i: (0, 0)),
                  pl.BlockSpec((1, hd), lambda b, h, i: (0, 0))],
        out_specs=pl.BlockSpec((ATT_TQ, hd), lambda b, h, i: (b * nq + i, h)),
        out_shape=jax.ShapeDtypeStruct((t, d), BF16),
        compiler_params=_params("parallel", "parallel", "arbitrary"),
        name="diff_attn",
    )(q, k, vt, lam, subln_g.reshape(1, hd))


def kernel(x, a_norm_g, a_w_in, a_lb, a_onorm_g, a_w_out, kv_norm_g, w_kv, b_norm_g, b_w_q, b_lam,
           b_subln_g, b_w_out, ffn_norm_g, router_g_w, router_g_b, router_e_w, router_e_b,
           expert_w_gu, expert_w_down, final_norm_g):
    batch, seq, d = x.shape
    assert d == D_MODEL and a_norm_g.shape[0] == 1 and b_norm_g.shape[0] == 1
    assert seq % max(GLA_BLOCK, ATT_TK, ROUTER_TM) == 0
    t = batch * seq
    h = x.reshape(t, d)

    proj = rms_proj(h, a_norm_g[0], a_w_in[0].astype(BF16))
    o = gla(proj, a_lb, a_onorm_g[0], batch, seq)
    h = proj_res(o, a_w_out[0].astype(BF16), h)
    h = hier_moe_layer(h, ffn_norm_g[0], router_g_w[0], router_g_b[0], router_e_w[0], router_e_b[0],
                       expert_w_gu[0], expert_w_down[0], final_norm_g, final_norm=False)

    lambda_init = 0.8 - 0.6 * math.exp(-0.3 * 1)
    q, k, vt = qkv_proj(h, b_norm_g[0], kv_norm_g, b_w_q[0].astype(BF16),
                        w_kv[:, :d].astype(BF16), w_kv[:, d:].T.astype(BF16), batch, seq)
    o = diff_attn(q, k, vt, b_lam[0], b_subln_g[0], batch, seq, lambda_init)
    h = proj_res(o, b_w_out[0].astype(BF16), h)
    h = hier_moe_layer(h, ffn_norm_g[1], router_g_w[1], router_g_b[1], router_e_w[1], router_e_b[1],
                       expert_w_gu[1], expert_w_down[1], final_norm_g, final_norm=True)
    return h.reshape(batch, seq, d)
```

```python
import functools
import math

import jax
import jax.numpy as jnp
from jax import lax
from jax.experimental import pallas as pl
from jax.experimental.pallas import tpu as pltpu

F32 = jnp.float32
BF16 = jnp.bfloat16
I32 = jnp.int32

D_MODEL = 1024
A_HEADS = 8
A_HEAD_DIM = 128
B_HEADS = 8
B_HEAD_DIM = 64
N_GROUPS = 4
EXPERTS_PER_GROUP = 8
N_EXPERTS = N_GROUPS * EXPERTS_PER_GROUP
D_EXPERT = 512
NORM_EPS = 1e-6
SUBLN_EPS = 1e-5
LOG2E = 1.4426950408889634

GLA_BLOCK = 128
GLA_HALF = GLA_BLOCK // 2
ATT_TQ = 128
ATT_TK = 512
ATT_HEADS = 8
ATT_CHUNK = 128
ATT_SUM_ROWS = 16
ROUTER_TM = 512
ROUTER_ROWS = 40
EXPERT_TM = 256
GATHER_TM = 256
VMEM_LIMIT = 56 * 1024 * 1024


def _nt(a, b):
    return lax.dot_general(a, b, (((1,), (1,)), ((), ())), preferred_element_type=F32)


def _nn(a, b):
    return jnp.dot(a, b, preferred_element_type=F32)


def _params(*sem, flags=None):
    return pltpu.CompilerParams(dimension_semantics=sem, vmem_limit_bytes=VMEM_LIMIT, flags=flags)


def _rms_proj_kernel(x_ref, g_ref, w_ref, o_ref, *, n_chunks):
    x = x_ref[...]
    y = x * lax.rsqrt(jnp.mean(x * x, axis=-1, keepdims=True) + NORM_EPS)
    xn = (y * g_ref[...]).astype(BF16)
    n = w_ref.shape[1] // n_chunks
    for c in range(n_chunks):
        o_ref[:, c * n:(c + 1) * n] = _nn(xn, w_ref[:, c * n:(c + 1) * n])


def rms_proj(x, g, w_bf16, tm=256):
    t, d = x.shape
    n = w_bf16.shape[1]
    return pl.pallas_call(
        functools.partial(_rms_proj_kernel, n_chunks=n // d),
        grid=(t // tm,),
        in_specs=[pl.BlockSpec((tm, d), lambda i: (i, 0)),
                  pl.BlockSpec((1, d), lambda i: (0, 0)),
                  pl.BlockSpec((d, n), lambda i: (0, 0))],
        out_specs=pl.BlockSpec((tm, n), lambda i: (i, 0)),
        out_shape=jax.ShapeDtypeStruct((t, n), F32),
        compiler_params=_params("parallel"),
        name="rms_proj",
    )(x, g.reshape(1, d), w_bf16)


def _gla_kernel(q_ref, f_ref, i_ref, g_ref, alb_ref, og_ref, o_ref, st_ref):
    c = pl.program_id(1)

    @pl.when(c == 0)
    def _():
        st_ref[...] = jnp.zeros_like(st_ref)

    alb = alb_ref[...]
    e = jnp.exp(alb - jnp.max(alb, axis=0, keepdims=True))
    lb = e[0:1] / jnp.sum(e, axis=0, keepdims=True)
    fz = f_ref[...]
    logf = jnp.log(lb + (1.0 - lb) * jax.nn.sigmoid(fz))
    kk = (1.0 - lb) * jax.nn.sigmoid(-fz)
    qq = jax.nn.silu(q_ref[...])

    r = lax.broadcasted_iota(I32, (GLA_BLOCK, GLA_BLOCK), 0)
    s = lax.broadcasted_iota(I32, (GLA_BLOCK, GLA_BLOCK), 1)
    tril = (s <= r).astype(BF16)
    hi = logf.astype(BF16)
    lo = (logf - hi.astype(F32)).astype(BF16)
    b = _nn(tril, hi) + _nn(tril, lo)

    h0, h1 = slice(0, GLA_HALF), slice(GLA_HALF, GLA_BLOCK)
    b_a_mid = b[GLA_HALF // 2 - 1:GLA_HALF // 2]
    b_a_end = b[GLA_HALF - 1:GLA_HALF]
    b_b_mid = b[GLA_HALF + GLA_HALF // 2 - 1:GLA_HALF + GLA_HALF // 2]
    b_end = b[GLA_BLOCK - 1:GLA_BLOCK]

    qa_n = (qq[h0] * jnp.exp(b[h0] - b_a_mid)).astype(BF16)
    ka_n = (kk[h0] * jnp.exp(b_a_mid - b[h0])).astype(BF16)
    qb_n = (qq[h1] * jnp.exp(b[h1] - b_b_mid)).astype(BF16)
    kb_n = (kk[h1] * jnp.exp(b_b_mid - b[h1])).astype(BF16)
    qb_x = (qq[h1] * jnp.exp(b[h1] - b_a_end)).astype(BF16)
    ka_x = (kk[h0] * jnp.exp(b_a_end - b[h0])).astype(BF16)
    q_dec = (qq * jnp.exp(b)).astype(BF16)
    k_end = (kk * jnp.exp(b_end - b)).astype(BF16)
    d_end = jnp.exp(b_end)

    rr = lax.broadcasted_iota(I32, (GLA_HALF, GLA_HALF), 0)
    ss = lax.broadcasted_iota(I32, (GLA_HALF, GLA_HALF), 1)
    causal = ss <= rr
    og = og_ref[...]

    for h in range(A_HEADS):
        hs = slice(h * A_HEAD_DIM, (h + 1) * A_HEAD_DIM)
        v = i_ref[:, hs]
        vb = v.astype(BF16)
        st = st_ref[h]
        st_b = st.astype(BF16)
        p_aa = jnp.where(causal, _nt(qa_n[:, hs], ka_n[:, hs]), 0.0).astype(BF16)
        p_bb = jnp.where(causal, _nt(qb_n[:, hs], kb_n[:, hs]), 0.0).astype(BF16)
        p_ba = _nt(qb_x[:, hs], ka_x[:, hs]).astype(BF16)
        inter = _nt(q_dec[:, hs], st_b)
        o_a = _nn(p_aa, vb[h0]) + inter[h0]
        o_b = _nn(p_bb, vb[h1]) + _nn(p_ba, vb[h0]) + inter[h1]
        st_ref[h] = d_end[:, hs] * st + _nn(v.T.astype(BF16), k_end[:, hs])
        gate = jax.nn.silu(g_ref[:, hs])
        for rows, o in ((h0, o_a), (h1, o_b)):
            on = o * lax.rsqrt(jnp.mean(o * o, axis=-1, keepdims=True) + NORM_EPS)
            o_ref[rows, hs] = (on * og * gate[rows]).astype(o_ref.dtype)


def gla(proj, a_lb, onorm_g, batch, seq):
    t = proj.shape[0]
    d = D_MODEL
    nblk = seq // GLA_BLOCK

    def col(j):
        return pl.BlockSpec((GLA_BLOCK, d), lambda b, c, j=j: (b * nblk + c, j))

    return pl.pallas_call(
        _gla_kernel,
        grid=(batch, nblk),
        in_specs=[col(0), col(1), col(2), col(3),
                  pl.BlockSpec(a_lb.shape, lambda b, c: (0, 0)),
                  pl.BlockSpec((1, A_HEAD_DIM), lambda b, c: (0, 0))],
        out_specs=pl.BlockSpec((GLA_BLOCK, d), lambda b, c: (b * nblk + c, 0)),
        out_shape=jax.ShapeDtypeStruct((t, d), BF16),
        scratch_shapes=[pltpu.VMEM((A_HEADS, A_HEAD_DIM, A_HEAD_DIM), F32)],
        compiler_params=_params("parallel", "arbitrary"),
        name="gla",
    )(proj, proj, proj, proj, a_lb, onorm_g.reshape(1, A_HEAD_DIM))


def _proj_res_kernel(a_ref, w_ref, r_ref, o_ref):
    o_ref[...] = r_ref[...] + _nn(a_ref[...], w_ref[...])


def proj_res(a_bf16, w_bf16, res, tm=512):
    t, d = res.shape
    return pl.pallas_call(
        _proj_res_kernel,
        grid=(t // tm,),
        in_specs=[pl.BlockSpec((tm, d), lambda i: (i, 0)),
                  pl.BlockSpec((d, d), lambda i: (0, 0)),
                  pl.BlockSpec((tm, d), lambda i: (i, 0))],
        out_specs=pl.BlockSpec((tm, d), lambda i: (i, 0)),
        out_shape=jax.ShapeDtypeStruct((t, d), F32),
        compiler_params=_params("parallel"),
        name="proj_res",
    )(a_bf16, w_bf16, res)


def _router_kernel(h_ref, g_ref, wh_ref, wl_ref, b_ref, xn_ref, eid_ref, wgt_ref, rank_ref, cnt_ref):
    i = pl.program_id(0)
    tm = h_ref.shape[0]

    @pl.when(i == 0)
    def _():
        cnt_ref[...] = jnp.zeros_like(cnt_ref)

    x = h_ref[...]
    xn = x * lax.rsqrt(jnp.mean(x * x, axis=-1, keepdims=True) + NORM_EPS) * g_ref[...]
    xn_ref[...] = xn
    xh = xn.astype(BF16)
    xl = (xn - xh.astype(F32)).astype(BF16)
    wh = wh_ref[...]
    lg = _nt(wh, xh) + _nt(wl_ref[...], xh) + _nt(wh, xl) + b_ref[...]

    gl = lg[0:N_GROUPS]
    r4 = lax.broadcasted_iota(I32, gl.shape, 0)
    gmax = jnp.max(gl, axis=0, keepdims=True)
    grp = jnp.min(jnp.where(gl == gmax, r4, N_GROUPS), axis=0, keepdims=True)
    p_grp = 1.0 / jnp.sum(jnp.exp(gl - gmax), axis=0, keepdims=True)

    fine = lg[8:8 + EXPERTS_PER_GROUP]
    for gi in range(1, N_GROUPS):
        fine = jnp.where(grp == gi, lg[8 + gi * EXPERTS_PER_GROUP:8 + (gi + 1) * EXPERTS_PER_GROUP], fine)
    r8 = lax.broadcasted_iota(I32, fine.shape, 0)
    m1 = jnp.max(fine, axis=0, keepdims=True)
    i1 = jnp.min(jnp.where(fine == m1, r8, EXPERTS_PER_GROUP), axis=0, keepdims=True)
    rest = jnp.where(r8 == i1, -jnp.inf, fine)
    m2 = jnp.max(rest, axis=0, keepdims=True)
    i2 = jnp.min(jnp.where(rest == m2, r8, EXPERTS_PER_GROUP), axis=0, keepdims=True)
    e21 = jnp.exp(m2 - m1)
    t1 = 1.0 / (1.0 + e21)
    wgt_ref[0:1, :] = p_grp * t1
    wgt_ref[1:2, :] = p_grp * (e21 * t1)
    e1 = grp * EXPERTS_PER_GROUP + i1
    e2 = grp * EXPERTS_PER_GROUP + i2
    eid_ref[0:1, :] = e1
    eid_ref[1:2, :] = e2

    r32 = lax.broadcasted_iota(I32, (N_EXPERTS, tm), 0)
    is1 = r32 == e1
    is2 = r32 == e2
    member = jnp.logical_or(is1, is2)
    ta = lax.broadcasted_iota(I32, (tm, tm), 0)
    tb = lax.broadcasted_iota(I32, (tm, tm), 1)
    before = (ta < tb).astype(BF16)
    prior = _nn(member.astype(BF16), before) + cnt_ref[:, 0:1]
    rank_ref[0:1, :] = jnp.sum(jnp.where(is1, prior, 0.0), axis=0, keepdims=True).astype(I32)
    rank_ref[1:2, :] = jnp.sum(jnp.where(is2, prior, 0.0), axis=0, keepdims=True).astype(I32)
    cnt_ref[...] = cnt_ref[...] + jnp.sum(member.astype(F32), axis=1, keepdims=True)


def router(h, g, wg, bg, we, be, tm=ROUTER_TM):
    t, d = h.shape
    w_all = jnp.zeros((ROUTER_ROWS, d), F32).at[0:N_GROUPS].set(wg.T).at[8:8 + N_EXPERTS].set(we.T)
    b_all = jnp.zeros((ROUTER_ROWS, 1), F32).at[0:N_GROUPS, 0].set(bg).at[8:8 + N_EXPERTS, 0].set(be)
    wh = w_all.astype(BF16)
    wl = (w_all - wh.astype(F32)).astype(BF16)
    row2 = lambda i: (0, i)
    return pl.pallas_call(
        _router_kernel,
        grid=(t // tm,),
        in_specs=[pl.BlockSpec((tm, d), lambda i: (i, 0)),
                  pl.BlockSpec((1, d), lambda i: (0, 0)),
                  pl.BlockSpec((ROUTER_ROWS, d), lambda i: (0, 0)),
                  pl.BlockSpec((ROUTER_ROWS, d), lambda i: (0, 0)),
                  pl.BlockSpec((ROUTER_ROWS, 1), lambda i: (0, 0))],
        out_specs=[pl.BlockSpec((tm, d), lambda i: (i, 0)),
                   pl.BlockSpec((2, tm), row2),
                   pl.BlockSpec((2, tm), row2),
                   pl.BlockSpec((2, tm), row2),
                   pl.BlockSpec((N_EXPERTS, 128), lambda i: (0, 0))],
        out_shape=[jax.ShapeDtypeStruct((t, d), F32),
                   jax.ShapeDtypeStruct((2, t), I32),
                   jax.ShapeDtypeStruct((2, t), F32),
                   jax.ShapeDtypeStruct((2, t), I32),
                   jax.ShapeDtypeStruct((N_EXPERTS, 128), F32)],
        compiler_params=_params("arbitrary"),
        name="router",
    )(h, g.reshape(1, d), wh, wl, b_all)


def _dispatch_kernel(pos_ref, x_ref, init_ref, o_ref, sem):
    del init_ref
    tm = x_ref.shape[0]

    def row_copy(r, k):
        return pltpu.make_async_copy(x_ref.at[pl.ds(r, 1)],
                                     o_ref.at[pl.ds(pos_ref[0, 0, k * tm + r], 1)], sem)

    def issue(r, carry):
        row_copy(r, 0).start()
        row_copy(r, 1).start()
        return carry

    def drain(r, carry):
        row_copy(r, 0).wait()
        row_copy(r, 1).wait()
        return carry

    lax.fori_loop(0, tm, issue, 0)
    lax.fori_loop(0, tm, drain, 0)


def dispatch(xn, pos3, n_rows, tm=GATHER_TM):
    t, d = xn.shape
    init = jnp.zeros((n_rows, d), xn.dtype)
    return pl.pallas_call(
        _dispatch_kernel,
        grid=(t // tm,),
        in_specs=[pl.BlockSpec((1, 1, 2 * tm), lambda i: (i, 0, 0), memory_space=pltpu.SMEM),
                  pl.BlockSpec((tm, d), lambda i: (i, 0)),
                  pl.BlockSpec(memory_space=pl.ANY)],
        out_specs=pl.BlockSpec(memory_space=pl.ANY),
        out_shape=jax.ShapeDtypeStruct((n_rows, d), xn.dtype),
        scratch_shapes=[pltpu.SemaphoreType.DMA(())],
        input_output_aliases={2: 0},
        compiler_params=_params("arbitrary"),
        name="moe_dispatch",
    )(pos3, xn, init)


def _expert_kernel(te_ref, nv_ref, x_ref, wgu_ref, wd_ref, y_ref, wgu_b, wd_b):
    i = pl.program_id(0)
    valid = i < nv_ref[0]
    changed = te_ref[i] != te_ref[jnp.maximum(i - 1, 0)]
    first = jnp.logical_or(i == 0, changed)

    @pl.when(jnp.logical_and(valid, first))
    def _():
        wgu_b[...] = wgu_ref[0, 0].astype(BF16)
        wd_b[...] = wd_ref[0, 0].astype(BF16)

    @pl.when(valid)
    def _():
        x = x_ref[...].astype(BF16)
        au = _nn(x, wgu_b[...])
        a = au[:, :D_EXPERT]
        u = au[:, D_EXPERT:]
        mid = (jax.nn.silu(a) * u).astype(BF16)
        y_ref[...] = _nn(mid, wd_b[...])

    @pl.when(jnp.logical_not(valid))
    def _():
        y_ref[...] = jnp.zeros_like(y_ref)


def experts(xg, w_gu, w_down, layer, tile_expert, n_valid, tm=EXPERT_TM):
    p, d = xg.shape
    n_tiles = p // tm
    row = lambda i, te, nv: (jnp.minimum(i, nv[0] - 1), 0)
    wsel = lambda i, te, nv: (layer, te[i], 0, 0)
    return pl.pallas_call(
        _expert_kernel,
        grid_spec=pltpu.PrefetchScalarGridSpec(
            num_scalar_prefetch=2,
            grid=(n_tiles,),
            in_specs=[pl.BlockSpec((tm, d), row),
                      pl.BlockSpec((1, 1, d, 2 * D_EXPERT), wsel),
                      pl.BlockSpec((1, 1, D_EXPERT, d), wsel)],
            out_specs=pl.BlockSpec((tm, d), lambda i, te, nv: (i, 0)),
            scratch_shapes=[pltpu.VMEM((d, 2 * D_EXPERT), BF16),
                            pltpu.VMEM((D_EXPERT, d), BF16)]),
        out_shape=jax.ShapeDtypeStruct((p, d), F32),
        compiler_params=_params("arbitrary"),
        name="moe_experts",
    )(tile_expert, n_valid, xg, w_gu, w_down)


def _combine_kernel(pos_ref, h_ref, w_ref, y_ref, g_ref, o_ref, buf, sem, *, final_norm):
    tm = h_ref.shape[0]

    def row_copy(r, k):
        return pltpu.make_async_copy(y_ref.at[pl.ds(pos_ref[0, 0, k * tm + r], 1)],
                                     buf.at[k, pl.ds(r, 1)], sem)

    def issue(r, carry):
        row_copy(r, 0).start()
        row_copy(r, 1).start()
        return carry

    def drain(r, carry):
        row_copy(r, 0).wait()
        row_copy(r, 1).wait()
        return carry

    lax.fori_loop(0, tm, issue, 0)
    lax.fori_loop(0, tm, drain, 0)
    w = w_ref[...]
    out = h_ref[...] + w[:, 0:1] * buf[0] + w[:, 1:2] * buf[1]
    if final_norm:
        out = out * lax.rsqrt(jnp.mean(out * out, axis=-1, keepdims=True) + NORM_EPS) * g_ref[...]
    o_ref[...] = out


def combine(h, wgt_t, pos3, y, g, final_norm, tm=GATHER_TM):
    t, d = h.shape
    return pl.pallas_call(
        functools.partial(_combine_kernel, final_norm=final_norm),
        grid=(t // tm,),
        in_specs=[pl.BlockSpec((1, 1, 2 * tm), lambda i: (i, 0, 0), memory_space=pltpu.SMEM),
                  pl.BlockSpec((tm, d), lambda i: (i, 0)),
                  pl.BlockSpec((tm, 2), lambda i: (i, 0)),
                  pl.BlockSpec(memory_space=pl.ANY),
                  pl.BlockSpec((1, d), lambda i: (0, 0))],
        out_specs=pl.BlockSpec((tm, d), lambda i: (i, 0)),
        out_shape=jax.ShapeDtypeStruct((t, d), F32),
        scratch_shapes=[pltpu.VMEM((2, tm, d), F32), pltpu.SemaphoreType.DMA(())],
        compiler_params=_params("arbitrary"),
        name="moe_combine",
    )(pos3, h, wgt_t, y, g.reshape(1, d))


def hier_moe_layer(h, norm_g, wg, bg, we, be, w_gu, w_down, layer, out_g, final_norm):
    t, d = h.shape
    xn, eid, wgt, rank, cnt = router(h, norm_g, wg, bg, we, be)
    counts = cnt[:, 0].astype(I32)
    padded = ((counts + EXPERT_TM - 1) // EXPERT_TM) * EXPERT_TM
    ends = jnp.cumsum(padded)
    offs = ends - padded
    n_rows = 2 * t + N_EXPERTS * EXPERT_TM
    n_tiles = n_rows // EXPERT_TM
    n_valid = (ends[-1] // EXPERT_TM).astype(I32).reshape(1)
    tile_start = jnp.arange(n_tiles, dtype=I32) * EXPERT_TM
    tile_start = jnp.minimum(tile_start, ends[-1] - 1)
    tile_expert = jnp.sum((ends[None, :] <= tile_start[:, None]).astype(I32), axis=1)
    onehot = eid[None] == jnp.arange(N_EXPERTS, dtype=I32)[:, None, None]
    pos = jnp.sum(jnp.where(onehot, offs[:, None, None], 0), axis=0) + rank
    nt = t // GATHER_TM
    pos3 = pos.reshape(2, nt, GATHER_TM).transpose(1, 0, 2).reshape(nt, 1, 2 * GATHER_TM)

    xg = dispatch(xn, pos3, n_rows)
    y = experts(xg, w_gu, w_down, layer, tile_expert, n_valid)
    return combine(h, wgt.T, pos3, y, out_g, final_norm)


def _qkv_kernel(h_ref, gq_ref, gkv_ref, wq_ref, wk_ref, wvt_ref, q_ref, k_ref, vt_ref):
    x = h_ref[...]
    y = x * lax.rsqrt(jnp.mean(x * x, axis=-1, keepdims=True) + NORM_EPS)
    xq = (y * gq_ref[...]).astype(BF16)
    xkv = (y * gkv_ref[...]).astype(BF16)
    q_ref[...] = (_nn(xq, wq_ref[...]) * (B_HEAD_DIM ** -0.5 * LOG2E)).astype(q_ref.dtype)
    k_ref[...] = _nn(xkv, wk_ref[...]).astype(k_ref.dtype)
    vt_ref[0] = _nt(wvt_ref[...], xkv).astype(vt_ref.dtype)


def qkv_proj(h, g_q, g_kv, wq, wk, wvt, batch, seq, tm=256):
    t, d = h.shape
    nblk = seq // tm
    full = pl.BlockSpec((d, d), lambda i: (0, 0))
    vec = pl.BlockSpec((1, d), lambda i: (0, 0))
    return pl.pallas_call(
        _qkv_kernel,
        grid=(t // tm,),
        in_specs=[pl.BlockSpec((tm, d), lambda i: (i, 0)), vec, vec, full, full, full],
        out_specs=[pl.BlockSpec((tm, d), lambda i: (i, 0)),
                   pl.BlockSpec((tm, d), lambda i: (i, 0)),
                   pl.BlockSpec((1, d, tm), lambda i: (i // nblk, 0, i % nblk))],
        out_shape=[jax.ShapeDtypeStruct((t, d), BF16),
                   jax.ShapeDtypeStruct((t, d), BF16),
                   jax.ShapeDtypeStruct((batch, d, seq), BF16)],
        compiler_params=_params("parallel"),
        name="qkv_proj",
    )(h, g_q.reshape(1, d), g_kv.reshape(1, d), wq, wk, wvt)


def _attn_kernel(q_ref, k_ref, vt_ref, lam_ref, g_ref, o_ref, qq_ref, m_ref, acc_ref,
                 a_ref, c_ref, *sp_refs, lambda_init, heads):
    s_refs, p_refs = sp_refs[:heads], sp_refs[heads:]
    qi = pl.program_id(2)
    tq, tk, hd = ATT_TQ, ATT_TK, 2 * B_HEAD_DIM
    lane = lax.broadcasted_iota(I32, (tq, hd), 1)
    for g in range(heads):
        q = q_ref[:, g * hd:(g + 1) * hd]
        zero = jnp.zeros_like(q)
        qq_ref[g, 0:tq, :] = jnp.where(lane < B_HEAD_DIM, q, zero)
        qq_ref[g, tq:2 * tq, :] = jnp.where(lane >= B_HEAD_DIM, q, zero)
    m_ref[...] = jnp.full_like(m_ref, -jnp.inf)
    acc_ref[...] = jnp.zeros_like(acc_ref)
    ones = jnp.ones((ATT_SUM_ROWS, tk), BF16)
    dyn0 = jnp.minimum(qi, 0)

    def step(j, masked):
        off = pl.multiple_of(j * tk, tk)
        if masked:
            krow = lax.broadcasted_iota(I32, (tk, 2 * tq), 0)
            qcol = lax.broadcasted_iota(I32, (tk, 2 * tq), 1)
            visible = off + krow <= qi * tq + jnp.where(qcol >= tq, qcol - tq, qcol)
        for g in range(heads):
            kb = k_ref[pl.ds(off, tk), g * hd:(g + 1) * hd]
            s = _nt(kb, qq_ref[g])
            if masked:
                s = jnp.where(visible, s, -jnp.inf)
            s_refs[g][0] = s
            c_ref[g] = jnp.max(s, axis=0, keepdims=True)
        for g in range(heads):
            m_old = m_ref[g]
            m_new = jnp.maximum(m_old, c_ref[g])
            a_ref[g] = jnp.exp2(m_old - m_new)
            m_ref[g] = m_new
            for c in range(0, tk, ATT_CHUNK):
                p_refs[g][0, c:c + ATT_CHUNK, :] = jnp.exp2(
                    s_refs[g][dyn0, c:c + ATT_CHUNK, :] - m_new).astype(BF16)
        for g in range(heads):
            vtb = vt_ref[0, g * hd:(g + 1) * hd, pl.ds(off, tk)]
            lhs = jnp.concatenate([vtb, ones], axis=0)
            acc_ref[g] = a_ref[g] * acc_ref[g] + _nn(lhs, p_refs[g][dyn0])

    n_full = (qi * tq) // tk

    def full_step(j, carry):
        step(j, False)
        return carry

    lax.fori_loop(0, n_full, full_step, 0)
    step(n_full, True)

    lam = lam_ref[...]
    lam_full = (jnp.exp(jnp.sum(lam[0:1] * lam[1:2], axis=-1, keepdims=True))
                - jnp.exp(jnp.sum(lam[2:3] * lam[3:4], axis=-1, keepdims=True)) + lambda_init)
    for g in range(heads):
        acc = acc_ref[g]
        on = acc[:hd] / acc[hd:hd + 1]
        ot = on[:, :tq] - lam_full * on[:, tq:]
        o = ot.T
        o = o * lax.rsqrt(jnp.mean(o * o, axis=-1, keepdims=True) + SUBLN_EPS) * g_ref[...]
        o_ref[:, g * hd:(g + 1) * hd] = (o * (1.0 - lambda_init)).astype(o_ref.dtype)


def diff_attn(q, k, vt, lam, subln_g, batch, seq, lambda_init, heads=ATT_HEADS):
    t, d = q.shape
    nq = seq // ATT_TQ
    hd = 2 * B_HEAD_DIM
    w = heads * hd
    return pl.pallas_call(
        functools.partial(_attn_kernel, lambda_init=lambda_init, heads=heads),
        grid=(batch, B_HEADS // heads, nq),
        in_specs=[pl.BlockSpec((ATT_TQ, w), lambda b, h, i: (b * nq + i, h)),
                  pl.BlockSpec((seq, w), lambda b, h, i: (b, h)),
                  pl.BlockSpec((1, w, seq), lambda b, h, i: (b, h, 0)),
                  pl.BlockSpec(lam.shape, lambda b, h, i: (0, 0)),
                  pl.BlockSpec((1, hd), lambda b, h, i: (0, 0))],
        out_specs=pl.BlockSpec((ATT_TQ, w), lambda b, h, i: (b * nq + i, h)),
        out_shape=jax.ShapeDtypeStruct((t, d), BF16),
        scratch_shapes=[pltpu.VMEM((heads, 2 * ATT_TQ, hd), BF16),
                        pltpu.VMEM((heads, 1, 2 * ATT_TQ), F32),
                        pltpu.VMEM((heads, hd + ATT_SUM_ROWS, 2 * ATT_TQ), F32),
                        pltpu.VMEM((heads, 1, 2 * ATT_TQ), F32),
                        pltpu.VMEM((heads, 1, 2 * ATT_TQ), F32)]
        + [pltpu.VMEM((1, ATT_TK, 2 * ATT_TQ), F32) for _ in range(heads)]
        + [pltpu.VMEM((1, ATT_TK, 2 * ATT_TQ), BF16) for _ in range(heads)],
        compiler_params=_params("parallel", "parallel", "arbitrary"),
        name="diff_attn",
    )(q, k, vt, lam, subln_g.reshape(1, hd))


def kernel(x, a_norm_g, a_w_in, a_lb, a_onorm_g, a_w_out, kv_norm_g, w_kv, b_norm_g, b_w_q, b_lam,
           b_subln_g, b_w_out, ffn_norm_g, router_g_w, router_g_b, router_e_w, router_e_b,
           expert_w_gu, expert_w_down, final_norm_g):
    batch, seq, d = x.shape
    assert d == D_MODEL and a_norm_g.shape[0] == 1 and b_norm_g.shape[0] == 1
    assert seq % max(GLA_BLOCK, ATT_TK, ROUTER_TM) == 0
    t = batch * seq
    h = x.reshape(t, d)

    proj = rms_proj(h, a_norm_g[0], a_w_in[0].astype(BF16))
    o = gla(proj, a_lb, a_onorm_g[0], batch, seq)
    h = proj_res(o, a_w_out[0].astype(BF16), h)
    h = hier_moe_layer(h, ffn_norm_g[0], router_g_w[0], router_g_b[0], router_e_w[0], router_e_b[0],
                       expert_w_gu, expert_w_down, 0, final_norm_g, final_norm=False)

    lambda_init = 0.8 - 0.6 * math.exp(-0.3 * 1)
    q, k, vt = qkv_proj(h, b_norm_g[0], kv_norm_g, b_w_q[0].astype(BF16),
                        w_kv[:, :d].astype(BF16), w_kv[:, d:].T.astype(BF16), batch, seq)
    o = diff_attn(q, k, vt, b_lam[0], b_subln_g[0], batch, seq, lambda_init)
    h = proj_res(o, b_w_out[0].astype(BF16), h)
    h = hier_moe_layer(h, ffn_norm_g[1], router_g_w[1], router_g_b[1], router_e_w[1], router_e_b[1],
                       expert_w_gu, expert_w_down, 1, final_norm_g, final_norm=True)
    return h.reshape(batch, seq, d)
```

```python
import functools
import math

import jax
import jax.numpy as jnp
from jax import lax
from jax.experimental import pallas as pl
from jax.experimental.pallas import tpu as pltpu

F32 = jnp.float32
BF16 = jnp.bfloat16
I32 = jnp.int32

D_MODEL = 1024
A_HEADS = 8
A_HEAD_DIM = 128
B_HEADS = 8
B_HEAD_DIM = 64
N_GROUPS = 4
EXPERTS_PER_GROUP = 8
N_EXPERTS = N_GROUPS * EXPERTS_PER_GROUP
D_EXPERT = 512
NORM_EPS = 1e-6
SUBLN_EPS = 1e-5
LOG2E = 1.4426950408889634

GLA_BLOCK = 128
GLA_HALF = GLA_BLOCK // 2
ATT_TQ = 128
ATT_TK = 512
ATT_HEADS = 8
ATT_CHUNK = 128
ATT_SUM_ROWS = 16
ROUTER_TM = 512
ROUTER_ROWS = 40
EXPERT_TM = 256
GATHER_TM = 256
ROW_DMA_UNROLL = 8
VMEM_LIMIT = 56 * 1024 * 1024


def _nt(a, b):
    return lax.dot_general(a, b, (((1,), (1,)), ((), ())), preferred_element_type=F32)


def _nn(a, b):
    return jnp.dot(a, b, preferred_element_type=F32)


def _params(*sem, flags=None):
    return pltpu.CompilerParams(dimension_semantics=sem, vmem_limit_bytes=VMEM_LIMIT, flags=flags)


def _rms_proj_kernel(x_ref, g_ref, w_ref, o_ref, *, n_chunks):
    x = x_ref[...]
    y = x * lax.rsqrt(jnp.mean(x * x, axis=-1, keepdims=True) + NORM_EPS)
    xn = (y * g_ref[...]).astype(BF16)
    n = w_ref.shape[1] // n_chunks
    for c in range(n_chunks):
        o_ref[:, c * n:(c + 1) * n] = _nn(xn, w_ref[:, c * n:(c + 1) * n])


def rms_proj(x, g, w_bf16, tm=256):
    t, d = x.shape
    n = w_bf16.shape[1]
    return pl.pallas_call(
        functools.partial(_rms_proj_kernel, n_chunks=n // d),
        grid=(t // tm,),
        in_specs=[pl.BlockSpec((tm, d), lambda i: (i, 0)),
                  pl.BlockSpec((1, d), lambda i: (0, 0)),
                  pl.BlockSpec((d, n), lambda i: (0, 0))],
        out_specs=pl.BlockSpec((tm, n), lambda i: (i, 0)),
        out_shape=jax.ShapeDtypeStruct((t, n), F32),
        compiler_params=_params("parallel"),
        name="rms_proj",
    )(x, g.reshape(1, d), w_bf16)


def _gla_kernel(q_ref, f_ref, i_ref, g_ref, alb_ref, og_ref, o_ref, st_ref):
    c = pl.program_id(1)

    @pl.when(c == 0)
    def _():
        st_ref[...] = jnp.zeros_like(st_ref)

    alb = alb_ref[...]
    e = jnp.exp(alb - jnp.max(alb, axis=0, keepdims=True))
    lb = e[0:1] / jnp.sum(e, axis=0, keepdims=True)
    fz = f_ref[...]
    logf = jnp.log(lb + (1.0 - lb) * jax.nn.sigmoid(fz))
    kk = (1.0 - lb) * jax.nn.sigmoid(-fz)
    qq = jax.nn.silu(q_ref[...])

    r = lax.broadcasted_iota(I32, (GLA_BLOCK, GLA_BLOCK), 0)
    s = lax.broadcasted_iota(I32, (GLA_BLOCK, GLA_BLOCK), 1)
    tril = (s <= r).astype(BF16)
    hi = logf.astype(BF16)
    lo = (logf - hi.astype(F32)).astype(BF16)
    b = _nn(tril, hi) + _nn(tril, lo)

    h0, h1 = slice(0, GLA_HALF), slice(GLA_HALF, GLA_BLOCK)
    b_a_mid = b[GLA_HALF // 2 - 1:GLA_HALF // 2]
    b_a_end = b[GLA_HALF - 1:GLA_HALF]
    b_b_mid = b[GLA_HALF + GLA_HALF // 2 - 1:GLA_HALF + GLA_HALF // 2]
    b_end = b[GLA_BLOCK - 1:GLA_BLOCK]

    qa_n = (qq[h0] * jnp.exp(b[h0] - b_a_mid)).astype(BF16)
    ka_n = (kk[h0] * jnp.exp(b_a_mid - b[h0])).astype(BF16)
    qb_n = (qq[h1] * jnp.exp(b[h1] - b_b_mid)).astype(BF16)
    kb_n = (kk[h1] * jnp.exp(b_b_mid - b[h1])).astype(BF16)
    qb_x = (qq[h1] * jnp.exp(b[h1] - b_a_end)).astype(BF16)
    ka_x = (kk[h0] * jnp.exp(b_a_end - b[h0])).astype(BF16)
    q_dec = (qq * jnp.exp(b)).astype(BF16)
    k_end = (kk * jnp.exp(b_end - b)).astype(BF16)
    d_end = jnp.exp(b_end)

    rr = lax.broadcasted_iota(I32, (GLA_HALF, GLA_HALF), 0)
    ss = lax.broadcasted_iota(I32, (GLA_HALF, GLA_HALF), 1)
    causal = ss <= rr
    og = og_ref[...]

    for h in range(A_HEADS):
        hs = slice(h * A_HEAD_DIM, (h + 1) * A_HEAD_DIM)
        v = i_ref[:, hs]
        vb = v.astype(BF16)
        st = st_ref[h]
        st_b = st.astype(BF16)
        p_aa = jnp.where(causal, _nt(qa_n[:, hs], ka_n[:, hs]), 0.0).astype(BF16)
        p_bb = jnp.where(causal, _nt(qb_n[:, hs], kb_n[:, hs]), 0.0).astype(BF16)
        p_ba = _nt(qb_x[:, hs], ka_x[:, hs]).astype(BF16)
        inter = _nt(q_dec[:, hs], st_b)
        o_a = _nn(p_aa, vb[h0]) + inter[h0]
        o_b = _nn(p_bb, vb[h1]) + _nn(p_ba, vb[h0]) + inter[h1]
        st_ref[h] = d_end[:, hs] * st + _nn(v.T.astype(BF16), k_end[:, hs])
        gate = jax.nn.silu(g_ref[:, hs])
        for rows, o in ((h0, o_a), (h1, o_b)):
            on = o * lax.rsqrt(jnp.mean(o * o, axis=-1, keepdims=True) + NORM_EPS)
            o_ref[rows, hs] = (on * og * gate[rows]).astype(o_ref.dtype)


def gla(proj, a_lb, onorm_g, batch, seq):
    t = proj.shape[0]
    d = D_MODEL
    nblk = seq // GLA_BLOCK

    def col(j):
        return pl.BlockSpec((GLA_BLOCK, d), lambda b, c, j=j: (b * nblk + c, j))

    return pl.pallas_call(
        _gla_kernel,
        grid=(batch, nblk),
        in_specs=[col(0), col(1), col(2), col(3),
                  pl.BlockSpec(a_lb.shape, lambda b, c: (0, 0)),
                  pl.BlockSpec((1, A_HEAD_DIM), lambda b, c: (0, 0))],
        out_specs=pl.BlockSpec((GLA_BLOCK, d), lambda b, c: (b * nblk + c, 0)),
        out_shape=jax.ShapeDtypeStruct((t, d), BF16),
        scratch_shapes=[pltpu.VMEM((A_HEADS, A_HEAD_DIM, A_HEAD_DIM), F32)],
        compiler_params=_params("parallel", "arbitrary"),
        name="gla",
    )(proj, proj, proj, proj, a_lb, onorm_g.reshape(1, A_HEAD_DIM))


def _proj_res_kernel(a_ref, w_ref, r_ref, o_ref):
    o_ref[...] = r_ref[...] + _nn(a_ref[...], w_ref[...])


def proj_res(a_bf16, w_bf16, res, tm=512):
    t, d = res.shape
    return pl.pallas_call(
        _proj_res_kernel,
        grid=(t // tm,),
        in_specs=[pl.BlockSpec((tm, d), lambda i: (i, 0)),
                  pl.BlockSpec((d, d), lambda i: (0, 0)),
                  pl.BlockSpec((tm, d), lambda i: (i, 0))],
        out_specs=pl.BlockSpec((tm, d), lambda i: (i, 0)),
        out_shape=jax.ShapeDtypeStruct((t, d), F32),
        compiler_params=_params("parallel"),
        name="proj_res",
    )(a_bf16, w_bf16, res)


def _router_kernel(h_ref, g_ref, wh_ref, wl_ref, b_ref, xn_ref, eid_ref, wgt_ref, rank_ref, cnt_ref):
    i = pl.program_id(0)
    tm = h_ref.shape[0]

    @pl.when(i == 0)
    def _():
        cnt_ref[...] = jnp.zeros_like(cnt_ref)

    x = h_ref[...]
    xn = x * lax.rsqrt(jnp.mean(x * x, axis=-1, keepdims=True) + NORM_EPS) * g_ref[...]
    xn_ref[...] = xn
    xh = xn.astype(BF16)
    xl = (xn - xh.astype(F32)).astype(BF16)
    wh = wh_ref[...]
    lg = _nt(wh, xh) + _nt(wl_ref[...], xh) + _nt(wh, xl) + b_ref[...]

    gl = lg[0:N_GROUPS]
    r4 = lax.broadcasted_iota(I32, gl.shape, 0)
    gmax = jnp.max(gl, axis=0, keepdims=True)
    grp = jnp.min(jnp.where(gl == gmax, r4, N_GROUPS), axis=0, keepdims=True)
    p_grp = 1.0 / jnp.sum(jnp.exp(gl - gmax), axis=0, keepdims=True)

    fine = lg[8:8 + EXPERTS_PER_GROUP]
    for gi in range(1, N_GROUPS):
        fine = jnp.where(grp == gi, lg[8 + gi * EXPERTS_PER_GROUP:8 + (gi + 1) * EXPERTS_PER_GROUP], fine)
    r8 = lax.broadcasted_iota(I32, fine.shape, 0)
    m1 = jnp.max(fine, axis=0, keepdims=True)
    i1 = jnp.min(jnp.where(fine == m1, r8, EXPERTS_PER_GROUP), axis=0, keepdims=True)
    rest = jnp.where(r8 == i1, -jnp.inf, fine)
    m2 = jnp.max(rest, axis=0, keepdims=True)
    i2 = jnp.min(jnp.where(rest == m2, r8, EXPERTS_PER_GROUP), axis=0, keepdims=True)
    e21 = jnp.exp(m2 - m1)
    t1 = 1.0 / (1.0 + e21)
    wgt_ref[0:1, :] = p_grp * t1
    wgt_ref[1:2, :] = p_grp * (e21 * t1)
    e1 = grp * EXPERTS_PER_GROUP + i1
    e2 = grp * EXPERTS_PER_GROUP + i2
    eid_ref[0:1, :] = e1
    eid_ref[1:2, :] = e2

    r32 = lax.broadcasted_iota(I32, (N_EXPERTS, tm), 0)
    is1 = r32 == e1
    is2 = r32 == e2
    member = jnp.logical_or(is1, is2)
    ta = lax.broadcasted_iota(I32, (tm, tm), 0)
    tb = lax.broadcasted_iota(I32, (tm, tm), 1)
    before = (ta < tb).astype(BF16)
    prior = _nn(member.astype(BF16), before) + cnt_ref[:, 0:1]
    rank_ref[0:1, :] = jnp.sum(jnp.where(is1, prior, 0.0), axis=0, keepdims=True).astype(I32)
    rank_ref[1:2, :] = jnp.sum(jnp.where(is2, prior, 0.0), axis=0, keepdims=True).astype(I32)
    cnt_ref[...] = cnt_ref[...] + jnp.sum(member.astype(F32), axis=1, keepdims=True)


def router(h, g, wg, bg, we, be, tm=ROUTER_TM):
    t, d = h.shape
    w_all = jnp.zeros((ROUTER_ROWS, d), F32).at[0:N_GROUPS].set(wg.T).at[8:8 + N_EXPERTS].set(we.T)
    b_all = jnp.zeros((ROUTER_ROWS, 1), F32).at[0:N_GROUPS, 0].set(bg).at[8:8 + N_EXPERTS, 0].set(be)
    wh = w_all.astype(BF16)
    wl = (w_all - wh.astype(F32)).astype(BF16)
    row2 = lambda i: (0, i)
    return pl.pallas_call(
        _router_kernel,
        grid=(t // tm,),
        in_specs=[pl.BlockSpec((tm, d), lambda i: (i, 0)),
                  pl.BlockSpec((1, d), lambda i: (0, 0)),
                  pl.BlockSpec((ROUTER_ROWS, d), lambda i: (0, 0)),
                  pl.BlockSpec((ROUTER_ROWS, d), lambda i: (0, 0)),
                  pl.BlockSpec((ROUTER_ROWS, 1), lambda i: (0, 0))],
        out_specs=[pl.BlockSpec((tm, d), lambda i: (i, 0)),
                   pl.BlockSpec((2, tm), row2),
                   pl.BlockSpec((2, tm), row2),
                   pl.BlockSpec((2, tm), row2),
                   pl.BlockSpec((N_EXPERTS, 128), lambda i: (0, 0))],
        out_shape=[jax.ShapeDtypeStruct((t, d), F32),
                   jax.ShapeDtypeStruct((2, t), I32),
                   jax.ShapeDtypeStruct((2, t), F32),
                   jax.ShapeDtypeStruct((2, t), I32),
                   jax.ShapeDtypeStruct((N_EXPERTS, 128), F32)],
        compiler_params=_params("arbitrary"),
        name="router",
    )(h, g.reshape(1, d), wh, wl, b_all)


def _dispatch_kernel(last_ref, pos_ref, x_ref, o_ref, zero_ref, sem):
    tm = x_ref.shape[0]

    @pl.when(pl.program_id(0) == 0)
    def _():
        zero_ref[...] = jnp.zeros_like(zero_ref)

        def tile_fill(e):
            row = pl.multiple_of(last_ref[e], EXPERT_TM)
            return pltpu.make_async_copy(zero_ref, o_ref.at[pl.ds(row, EXPERT_TM)], sem)

        for e in range(N_EXPERTS):
            @pl.when(last_ref[e] >= 0)
            def _():
                tile_fill(e).start()
        for e in range(N_EXPERTS):
            @pl.when(last_ref[e] >= 0)
            def _():
                tile_fill(e).wait()

        def spare_fill(j):
            row = pl.multiple_of(j * EXPERT_TM, EXPERT_TM)
            return pltpu.make_async_copy(zero_ref, o_ref.at[pl.ds(row, EXPERT_TM)], sem)

        n_tiles = o_ref.shape[0] // EXPERT_TM
        lax.fori_loop(last_ref[N_EXPERTS], n_tiles, lambda j, c: (spare_fill(j).start(), c)[1], 0)
        lax.fori_loop(last_ref[N_EXPERTS], n_tiles, lambda j, c: (spare_fill(j).wait(), c)[1], 0)

    def row_copy(r, k):
        return pltpu.make_async_copy(x_ref.at[pl.ds(r, 1)],
                                     o_ref.at[pl.ds(pos_ref[0, 0, k * tm + r], 1)], sem)

    def issue(blk, carry):
        for u in range(ROW_DMA_UNROLL):
            r = blk * ROW_DMA_UNROLL + u
            row_copy(r, 0).start(priority=u % 2)
            row_copy(r, 1).start(priority=(u + 1) % 2)
        return carry

    lax.fori_loop(0, tm // ROW_DMA_UNROLL, issue, 0)
    for _ in range(2):
        pltpu.make_async_copy(x_ref, o_ref.at[pl.ds(0, tm)], sem).wait()


def dispatch(xn, pos3, last_tile_row, n_rows, tm=GATHER_TM):
    t, d = xn.shape
    return pl.pallas_call(
        _dispatch_kernel,
        grid_spec=pltpu.PrefetchScalarGridSpec(
            num_scalar_prefetch=1,
            grid=(t // tm,),
            in_specs=[pl.BlockSpec((1, 1, 2 * tm), lambda i, last: (i, 0, 0), memory_space=pltpu.SMEM),
                      pl.BlockSpec((tm, d), lambda i, last: (i, 0))],
            out_specs=pl.BlockSpec(memory_space=pl.ANY),
            scratch_shapes=[pltpu.VMEM((EXPERT_TM, d), xn.dtype), pltpu.SemaphoreType.DMA(())]),
        out_shape=jax.ShapeDtypeStruct((n_rows, d), xn.dtype),
        compiler_params=_params("arbitrary"),
        name="moe_dispatch",
    )(last_tile_row, pos3, xn)


def _expert_kernel(te_ref, nv_ref, x_ref, wgu_ref, wd_ref, y_ref, wgu_b, wd_b):
    i = pl.program_id(0)
    valid = i < nv_ref[0]
    changed = te_ref[i] != te_ref[jnp.maximum(i - 1, 0)]
    first = jnp.logical_or(i == 0, changed)

    @pl.when(jnp.logical_and(valid, first))
    def _():
        wgu_b[...] = wgu_ref[0, 0].astype(BF16)
        wd_b[...] = wd_ref[0, 0].astype(BF16)

    @pl.when(valid)
    def _():
        x = x_ref[...].astype(BF16)
        au = _nn(x, wgu_b[...])
        a = au[:, :D_EXPERT]
        u = au[:, D_EXPERT:]
        mid = (jax.nn.silu(a) * u).astype(BF16)
        y_ref[...] = _nn(mid, wd_b[...])

    @pl.when(jnp.logical_not(valid))
    def _():
        y_ref[...] = jnp.zeros_like(y_ref)


def experts(xg, w_gu, w_down, layer, tile_expert, n_valid, tm=EXPERT_TM):
    p, d = xg.shape
    n_tiles = p // tm
    row = lambda i, te, nv: (jnp.minimum(i, nv[0] - 1), 0)
    wsel = lambda i, te, nv: (layer, te[i], 0, 0)
    return pl.pallas_call(
        _expert_kernel,
        grid_spec=pltpu.PrefetchScalarGridSpec(
            num_scalar_prefetch=2,
            grid=(n_tiles,),
            in_specs=[pl.BlockSpec((tm, d), row),
                      pl.BlockSpec((1, 1, d, 2 * D_EXPERT), wsel),
                      pl.BlockSpec((1, 1, D_EXPERT, d), wsel)],
            out_specs=pl.BlockSpec((tm, d), lambda i, te, nv: (i, 0)),
            scratch_shapes=[pltpu.VMEM((d, 2 * D_EXPERT), BF16),
                            pltpu.VMEM((D_EXPERT, d), BF16)]),
        out_shape=jax.ShapeDtypeStruct((p, d), F32),
        compiler_params=_params("arbitrary"),
        name="moe_experts",
    )(tile_expert, n_valid, xg, w_gu, w_down)


def _combine_kernel(pos_ref, h_ref, w_ref, y_ref, g_ref, o_ref, buf, sem, *, final_norm):
    tm = h_ref.shape[0]

    def row_copy(r, k):
        return pltpu.make_async_copy(y_ref.at[pl.ds(pos_ref[0, 0, k * tm + r], 1)],
                                     buf.at[k, pl.ds(r, 1)], sem)

    def issue(blk, carry):
        for u in range(ROW_DMA_UNROLL):
            r = blk * ROW_DMA_UNROLL + u
            row_copy(r, 0).start(priority=u % 2)
            row_copy(r, 1).start(priority=(u + 1) % 2)
        return carry

    lax.fori_loop(0, tm // ROW_DMA_UNROLL, issue, 0)
    for k in range(2):
        pltpu.make_async_copy(y_ref.at[pl.ds(0, tm)], buf.at[k], sem).wait()
    w = w_ref[...]
    out = h_ref[...] + w[:, 0:1] * buf[0] + w[:, 1:2] * buf[1]
    if final_norm:
        out = out * lax.rsqrt(jnp.mean(out * out, axis=-1, keepdims=True) + NORM_EPS) * g_ref[...]
    o_ref[...] = out


def combine(h, wgt_t, pos3, y, g, final_norm, tm=GATHER_TM):
    t, d = h.shape
    return pl.pallas_call(
        functools.partial(_combine_kernel, final_norm=final_norm),
        grid=(t // tm,),
        in_specs=[pl.BlockSpec((1, 1, 2 * tm), lambda i: (i, 0, 0), memory_space=pltpu.SMEM),
                  pl.BlockSpec((tm, d), lambda i: (i, 0)),
                  pl.BlockSpec((tm, 2), lambda i: (i, 0)),
                  pl.BlockSpec(memory_space=pl.ANY),
                  pl.BlockSpec((1, d), lambda i: (0, 0))],
        out_specs=pl.BlockSpec((tm, d), lambda i: (i, 0)),
        out_shape=jax.ShapeDtypeStruct((t, d), F32),
        scratch_shapes=[pltpu.VMEM((2, tm, d), F32), pltpu.SemaphoreType.DMA(())],
        compiler_params=_params("arbitrary"),
        name="moe_combine",
    )(pos3, h, wgt_t, y, g.reshape(1, d))


def hier_moe_layer(h, norm_g, wg, bg, we, be, w_gu, w_down, layer, out_g, final_norm):
    t, d = h.shape
    xn, eid, wgt, rank, cnt = router(h, norm_g, wg, bg, we, be)
    counts = cnt[:, 0].astype(I32)
    padded = ((counts + EXPERT_TM - 1) // EXPERT_TM) * EXPERT_TM
    ends = jnp.cumsum(padded)
    offs = ends - padded
    n_rows = 2 * t + N_EXPERTS * EXPERT_TM
    n_tiles = n_rows // EXPERT_TM
    n_valid = (ends[-1] // EXPERT_TM).astype(I32).reshape(1)
    tile_start = jnp.arange(n_tiles, dtype=I32) * EXPERT_TM
    tile_start = jnp.minimum(tile_start, ends[-1] - 1)
    tile_expert = jnp.sum((ends[None, :] <= tile_start[:, None]).astype(I32), axis=1)
    onehot = eid[None] == jnp.arange(N_EXPERTS, dtype=I32)[:, None, None]
    pos = jnp.sum(jnp.where(onehot, offs[:, None, None], 0), axis=0) + rank
    nt = t // GATHER_TM
    pos3 = pos.reshape(2, nt, GATHER_TM).transpose(1, 0, 2).reshape(nt, 1, 2 * GATHER_TM)

    last_tile_row = jnp.concatenate([jnp.where(padded > 0, ends - EXPERT_TM, -1).astype(I32), n_valid])
    xg = dispatch(xn, pos3, last_tile_row, n_rows)
    y = experts(xg, w_gu, w_down, layer, tile_expert, n_valid)
    return combine(h, wgt.T, pos3, y, out_g, final_norm)


def _qkv_kernel(h_ref, gq_ref, gkv_ref, wq_ref, wk_ref, wvt_ref, q_ref, k_ref, vt_ref):
    x = h_ref[...]
    y = x * lax.rsqrt(jnp.mean(x * x, axis=-1, keepdims=True) + NORM_EPS)
    xq = (y * gq_ref[...]).astype(BF16)
    xkv = (y * gkv_ref[...]).astype(BF16)
    q_ref[...] = (_nn(xq, wq_ref[...]) * (B_HEAD_DIM ** -0.5 * LOG2E)).astype(q_ref.dtype)
    k_ref[...] = _nn(xkv, wk_ref[...]).astype(k_ref.dtype)
    vt_ref[0] = _nt(wvt_ref[...], xkv).astype(vt_ref.dtype)


def qkv_proj(h, g_q, g_kv, wq, wk, wvt, batch, seq, tm=256):
    t, d = h.shape
    nblk = seq // tm
    full = pl.BlockSpec((d, d), lambda i: (0, 0))
    vec = pl.BlockSpec((1, d), lambda i: (0, 0))
    return pl.pallas_call(
        _qkv_kernel,
        grid=(t // tm,),
        in_specs=[pl.BlockSpec((tm, d), lambda i: (i, 0)), vec, vec, full, full, full],
        out_specs=[pl.BlockSpec((tm, d), lambda i: (i, 0)),
                   pl.BlockSpec((tm, d), lambda i: (i, 0)),
                   pl.BlockSpec((1, d, tm), lambda i: (i // nblk, 0, i % nblk))],
        out_shape=[jax.ShapeDtypeStruct((t, d), BF16),
                   jax.ShapeDtypeStruct((t, d), BF16),
                   jax.ShapeDtypeStruct((batch, d, seq), BF16)],
        compiler_params=_params("parallel"),
        name="qkv_proj",
    )(h, g_q.reshape(1, d), g_kv.reshape(1, d), wq, wk, wvt)


def _attn_kernel(q_ref, k_ref, vt_ref, lam_ref, g_ref, o_ref, qq_ref, m_ref, acc_ref,
                 a_ref, c_ref, *sp_refs, lambda_init, heads):
    s_refs, p_refs = sp_refs[:heads], sp_refs[heads:]
    qi = pl.program_id(2)
    tq, tk, hd = ATT_TQ, ATT_TK, 2 * B_HEAD_DIM
    lane = lax.broadcasted_iota(I32, (tq, hd), 1)
    for g in range(heads):
        q = q_ref[:, g * hd:(g + 1) * hd]
        zero = jnp.zeros_like(q)
        qq_ref[g, 0:tq, :] = jnp.where(lane < B_HEAD_DIM, q, zero)
        qq_ref[g, tq:2 * tq, :] = jnp.where(lane >= B_HEAD_DIM, q, zero)
    m_ref[...] = jnp.full_like(m_ref, -jnp.inf)
    acc_ref[...] = jnp.zeros_like(acc_ref)
    ones = jnp.ones((ATT_SUM_ROWS, tk), BF16)
    dyn0 = jnp.minimum(qi, 0)

    def step(j, masked):
        off = pl.multiple_of(j * tk, tk)
        if masked:
            krow = lax.broadcasted_iota(I32, (tk, 2 * tq), 0)
            qcol = lax.broadcasted_iota(I32, (tk, 2 * tq), 1)
            visible = off + krow <= qi * tq + jnp.where(qcol >= tq, qcol - tq, qcol)
        for g in range(heads):
            kb = k_ref[pl.ds(off, tk), g * hd:(g + 1) * hd]
            s = _nt(kb, qq_ref[g])
            if masked:
                s = jnp.where(visible, s, -jnp.inf)
            s_refs[g][0] = s
            c_ref[g] = jnp.max(s, axis=0, keepdims=True)
        for g in range(heads):
            m_old = m_ref[g]
            m_new = jnp.maximum(m_old, c_ref[g])
            a_ref[g] = jnp.exp2(m_old - m_new)
            m_ref[g] = m_new
            for c in range(0, tk, ATT_CHUNK):
                p_refs[g][0, c:c + ATT_CHUNK, :] = jnp.exp2(
                    s_refs[g][dyn0, c:c + ATT_CHUNK, :] - m_new).astype(BF16)
        for g in range(heads):
            vtb = vt_ref[0, g * hd:(g + 1) * hd, pl.ds(off, tk)]
            lhs = jnp.concatenate([vtb, ones], axis=0)
            acc_ref[g] = a_ref[g] * acc_ref[g] + _nn(lhs, p_refs[g][dyn0])

    n_full = (qi * tq) // tk

    def full_step(j, carry):
        step(j, False)
        return carry

    lax.fori_loop(0, n_full, full_step, 0)
    step(n_full, True)

    lam = lam_ref[...]
    lam_full = (jnp.exp(jnp.sum(lam[0:1] * lam[1:2], axis=-1, keepdims=True))
                - jnp.exp(jnp.sum(lam[2:3] * lam[3:4], axis=-1, keepdims=True)) + lambda_init)
    for g in range(heads):
        acc = acc_ref[g]
        on = acc[:hd] / acc[hd:hd + 1]
        ot = on[:, :tq] - lam_full * on[:, tq:]
        o = ot.T
        o = o * lax.rsqrt(jnp.mean(o * o, axis=-1, keepdims=True) + SUBLN_EPS) * g_ref[...]
        o_ref[:, g * hd:(g + 1) * hd] = (o * (1.0 - lambda_init)).astype(o_ref.dtype)


def diff_attn(q, k, vt, lam, subln_g, batch, seq, lambda_init, heads=ATT_HEADS):
    t, d = q.shape
    nq = seq // ATT_TQ
    hd = 2 * B_HEAD_DIM
    w = heads * hd
    return pl.pallas_call(
        functools.partial(_attn_kernel, lambda_init=lambda_init, heads=heads),
        grid=(batch, B_HEADS // heads, nq),
        in_specs=[pl.BlockSpec((ATT_TQ, w), lambda b, h, i: (b * nq + i, h)),
                  pl.BlockSpec((seq, w), lambda b, h, i: (b, h)),
                  pl.BlockSpec((1, w, seq), lambda b, h, i: (b, h, 0)),
                  pl.BlockSpec(lam.shape, lambda b, h, i: (0, 0)),
                  pl.BlockSpec((1, hd), lambda b, h, i: (0, 0))],
        out_specs=pl.BlockSpec((ATT_TQ, w), lambda b, h, i: (b * nq + i, h)),
        out_shape=jax.ShapeDtypeStruct((t, d), BF16),
        scratch_shapes=[pltpu.VMEM((heads, 2 * ATT_TQ, hd), BF16),
                        pltpu.VMEM((heads, 1, 2 * ATT_TQ), F32),
                        pltpu.VMEM((heads, hd + ATT_SUM_ROWS, 2 * ATT_TQ), F32),
                        pltpu.VMEM((heads, 1, 2 * ATT_TQ), F32),
                        pltpu.VMEM((heads, 1, 2 * ATT_TQ), F32)]
        + [pltpu.VMEM((1, ATT_TK, 2 * ATT_TQ), F32) for _ in range(heads)]
        + [pltpu.VMEM((1, ATT_TK, 2 * ATT_TQ), BF16) for _ in range(heads)],
        compiler_params=_params("parallel", "parallel", "arbitrary"),
        name="diff_attn",
    )(q, k, vt, lam, subln_g.reshape(1, hd))


def kernel(x, a_norm_g, a_w_in, a_lb, a_onorm_g, a_w_out, kv_norm_g, w_kv, b_norm_g, b_w_q, b_lam,
           b_subln_g, b_w_out, ffn_norm_g, router_g_w, router_g_b, router_e_w, router_e_b,
           expert_w_gu, expert_w_down, final_norm_g):
    batch, seq, d = x.shape
    assert d == D_MODEL and a_norm_g.shape[0] == 1 and b_norm_g.shape[0] == 1
    assert seq % max(GLA_BLOCK, ATT_TK, ROUTER_TM) == 0
    t = batch * seq
    h = x.reshape(t, d)

    proj = rms_proj(h, a_norm_g[0], a_w_in[0].astype(BF16))
    o = gla(proj, a_lb, a_onorm_g[0], batch, seq)
    h = proj_res(o, a_w_out[0].astype(BF16), h)
    h = hier_moe_layer(h, ffn_norm_g[0], router_g_w[0], router_g_b[0], router_e_w[0], router_e_b[0],
                       expert_w_gu, expert_w_down, 0, final_norm_g, final_norm=False)

    lambda_init = 0.8 - 0.6 * math.exp(-0.3 * 1)
    q, k, vt = qkv_proj(h, b_norm_g[0], kv_norm_g, b_w_q[0].astype(BF16),
                        w_kv[:, :d].astype(BF16), w_kv[:, d:].T.astype(BF16), batch, seq)
    o = diff_attn(q, k, vt, b_lam[0], b_subln_g[0], batch, seq, lambda_init)
    h = proj_res(o, b_w_out[0].astype(BF16), h)
    h = hier_moe_layer(h, ffn_norm_g[1], router_g_w[1], router_g_b[1], router_e_w[1], router_e_b[1],
                       expert_w_gu, expert_w_down, 1, final_norm_g, final_norm=True)
    return h.reshape(batch, seq, d)
```

```python
import functools
import math

import jax
import jax.numpy as jnp
from jax import lax
from jax.experimental import pallas as pl
from jax.experimental.pallas import tpu as pltpu

F32 = jnp.float32
BF16 = jnp.bfloat16
I32 = jnp.int32

D_MODEL = 1024
A_HEADS = 8
A_HEAD_DIM = 128
B_HEADS = 8
B_HEAD_DIM = 64
N_GROUPS = 4
EXPERTS_PER_GROUP = 8
N_EXPERTS = N_GROUPS * EXPERTS_PER_GROUP
D_EXPERT = 512
NORM_EPS = 1e-6
SUBLN_EPS = 1e-5
LOG2E = 1.4426950408889634

GLA_BLOCK = 128
GLA_HALF = GLA_BLOCK // 2
ATT_TQ = 128
ATT_TK = 512
ATT_HEADS = 8
ATT_CHUNK = 128
ATT_SUM_ROWS = 16
ROUTER_TM = 512
ROUTER_ROWS = 40
EXPERT_TM = 512
GATHER_TM = 256
ROW_DMA_UNROLL = 8
TOKEN_TILE = 8
VMEM_LIMIT = 56 * 1024 * 1024


def _nt(a, b):
    return lax.dot_general(a, b, (((1,), (1,)), ((), ())), preferred_element_type=F32)


def _nn(a, b):
    return jnp.dot(a, b, preferred_element_type=F32)


def _to_tiles(ref, x):
    tm = x.shape[0]
    for s in range(TOKEN_TILE):
        ref[pl.ds(s, tm, stride=TOKEN_TILE), :] = x[:, s * 128:(s + 1) * 128]


def _from_tiles(ref, tm):
    return jnp.concatenate([ref[pl.ds(s, tm, stride=TOKEN_TILE), :] for s in range(TOKEN_TILE)], axis=1)


def _params(*sem, flags=None):
    return pltpu.CompilerParams(dimension_semantics=sem, vmem_limit_bytes=VMEM_LIMIT, flags=flags)


def _rms_proj_kernel(x_ref, g_ref, w_ref, o_ref, *, n_chunks):
    x = x_ref[...]
    y = x * lax.rsqrt(jnp.mean(x * x, axis=-1, keepdims=True) + NORM_EPS)
    xn = (y * g_ref[...]).astype(BF16)
    n = w_ref.shape[1] // n_chunks
    for c in range(n_chunks):
        o_ref[:, c * n:(c + 1) * n] = _nn(xn, w_ref[:, c * n:(c + 1) * n])


def rms_proj(x, g, w_bf16, tm=256):
    t, d = x.shape
    n = w_bf16.shape[1]
    return pl.pallas_call(
        functools.partial(_rms_proj_kernel, n_chunks=n // d),
        grid=(t // tm,),
        in_specs=[pl.BlockSpec((tm, d), lambda i: (i, 0)),
                  pl.BlockSpec((1, d), lambda i: (0, 0)),
                  pl.BlockSpec((d, n), lambda i: (0, 0))],
        out_specs=pl.BlockSpec((tm, n), lambda i: (i, 0)),
        out_shape=jax.ShapeDtypeStruct((t, n), F32),
        compiler_params=_params("parallel"),
        name="rms_proj",
    )(x, g.reshape(1, d), w_bf16)


def _gla_kernel(q_ref, f_ref, i_ref, g_ref, alb_ref, og_ref, o_ref, st_ref):
    c = pl.program_id(1)

    @pl.when(c == 0)
    def _():
        st_ref[...] = jnp.zeros_like(st_ref)

    alb = alb_ref[...]
    e = jnp.exp(alb - jnp.max(alb, axis=0, keepdims=True))
    lb = e[0:1] / jnp.sum(e, axis=0, keepdims=True)
    fz = f_ref[...]
    logf = jnp.log(lb + (1.0 - lb) * jax.nn.sigmoid(fz))
    kk = (1.0 - lb) * jax.nn.sigmoid(-fz)
    qq = jax.nn.silu(q_ref[...])

    r = lax.broadcasted_iota(I32, (GLA_BLOCK, GLA_BLOCK), 0)
    s = lax.broadcasted_iota(I32, (GLA_BLOCK, GLA_BLOCK), 1)
    tril = (s <= r).astype(BF16)
    hi = logf.astype(BF16)
    lo = (logf - hi.astype(F32)).astype(BF16)
    b = _nn(tril, hi) + _nn(tril, lo)

    h0, h1 = slice(0, GLA_HALF), slice(GLA_HALF, GLA_BLOCK)
    b_a_mid = b[GLA_HALF // 2 - 1:GLA_HALF // 2]
    b_a_end = b[GLA_HALF - 1:GLA_HALF]
    b_b_mid = b[GLA_HALF + GLA_HALF // 2 - 1:GLA_HALF + GLA_HALF // 2]
    b_end = b[GLA_BLOCK - 1:GLA_BLOCK]

    qa_n = (qq[h0] * jnp.exp(b[h0] - b_a_mid)).astype(BF16)
    ka_n = (kk[h0] * jnp.exp(b_a_mid - b[h0])).astype(BF16)
    qb_n = (qq[h1] * jnp.exp(b[h1] - b_b_mid)).astype(BF16)
    kb_n = (kk[h1] * jnp.exp(b_b_mid - b[h1])).astype(BF16)
    qb_x = (qq[h1] * jnp.exp(b[h1] - b_a_end)).astype(BF16)
    ka_x = (kk[h0] * jnp.exp(b_a_end - b[h0])).astype(BF16)
    q_dec = (qq * jnp.exp(b)).astype(BF16)
    k_end = (kk * jnp.exp(b_end - b)).astype(BF16)
    d_end = jnp.exp(b_end)

    rr = lax.broadcasted_iota(I32, (GLA_HALF, GLA_HALF), 0)
    ss = lax.broadcasted_iota(I32, (GLA_HALF, GLA_HALF), 1)
    causal = ss <= rr
    og = og_ref[...]

    for h in range(A_HEADS):
        hs = slice(h * A_HEAD_DIM, (h + 1) * A_HEAD_DIM)
        v = i_ref[:, hs]
        vb = v.astype(BF16)
        st = st_ref[h]
        st_b = st.astype(BF16)
        p_aa = jnp.where(causal, _nt(qa_n[:, hs], ka_n[:, hs]), 0.0).astype(BF16)
        p_bb = jnp.where(causal, _nt(qb_n[:, hs], kb_n[:, hs]), 0.0).astype(BF16)
        p_ba = _nt(qb_x[:, hs], ka_x[:, hs]).astype(BF16)
        inter = _nt(q_dec[:, hs], st_b)
        o_a = _nn(p_aa, vb[h0]) + inter[h0]
        o_b = _nn(p_bb, vb[h1]) + _nn(p_ba, vb[h0]) + inter[h1]
        st_ref[h] = d_end[:, hs] * st + _nn(v.T.astype(BF16), k_end[:, hs])
        gate = jax.nn.silu(g_ref[:, hs])
        for rows, o in ((h0, o_a), (h1, o_b)):
            on = o * lax.rsqrt(jnp.mean(o * o, axis=-1, keepdims=True) + NORM_EPS)
            o_ref[rows, hs] = (on * og * gate[rows]).astype(o_ref.dtype)


def gla(proj, a_lb, onorm_g, batch, seq):
    t = proj.shape[0]
    d = D_MODEL
    nblk = seq // GLA_BLOCK

    def col(j):
        return pl.BlockSpec((GLA_BLOCK, d), lambda b, c, j=j: (b * nblk + c, j))

    return pl.pallas_call(
        _gla_kernel,
        grid=(batch, nblk),
        in_specs=[col(0), col(1), col(2), col(3),
                  pl.BlockSpec(a_lb.shape, lambda b, c: (0, 0)),
                  pl.BlockSpec((1, A_HEAD_DIM), lambda b, c: (0, 0))],
        out_specs=pl.BlockSpec((GLA_BLOCK, d), lambda b, c: (b * nblk + c, 0)),
        out_shape=jax.ShapeDtypeStruct((t, d), BF16),
        scratch_shapes=[pltpu.VMEM((A_HEADS, A_HEAD_DIM, A_HEAD_DIM), F32)],
        compiler_params=_params("parallel", "arbitrary"),
        name="gla",
    )(proj, proj, proj, proj, a_lb, onorm_g.reshape(1, A_HEAD_DIM))


def _proj_res_kernel(a_ref, w_ref, r_ref, o_ref):
    o_ref[...] = r_ref[...] + _nn(a_ref[...], w_ref[...])


def proj_res(a_bf16, w_bf16, res, tm=512):
    t, d = res.shape
    return pl.pallas_call(
        _proj_res_kernel,
        grid=(t // tm,),
        in_specs=[pl.BlockSpec((tm, d), lambda i: (i, 0)),
                  pl.BlockSpec((d, d), lambda i: (0, 0)),
                  pl.BlockSpec((tm, d), lambda i: (i, 0))],
        out_specs=pl.BlockSpec((tm, d), lambda i: (i, 0)),
        out_shape=jax.ShapeDtypeStruct((t, d), F32),
        compiler_params=_params("parallel"),
        name="proj_res",
    )(a_bf16, w_bf16, res)


def _router_kernel(h_ref, g_ref, wh_ref, wl_ref, b_ref, xn_ref, eid_ref, wgt_ref, rank_ref, cnt_ref):
    i = pl.program_id(0)
    tm = h_ref.shape[0]

    @pl.when(i == 0)
    def _():
        cnt_ref[...] = jnp.zeros_like(cnt_ref)

    x = h_ref[...]
    xn = x * lax.rsqrt(jnp.mean(x * x, axis=-1, keepdims=True) + NORM_EPS) * g_ref[...]
    _to_tiles(xn_ref, xn)
    xh = xn.astype(BF16)
    xl = (xn - xh.astype(F32)).astype(BF16)
    wh = wh_ref[...]
    lg = _nt(wh, xh) + _nt(wl_ref[...], xh) + _nt(wh, xl) + b_ref[...]

    gl = lg[0:N_GROUPS]
    r4 = lax.broadcasted_iota(I32, gl.shape, 0)
    gmax = jnp.max(gl, axis=0, keepdims=True)
    grp = jnp.min(jnp.where(gl == gmax, r4, N_GROUPS), axis=0, keepdims=True)
    p_grp = 1.0 / jnp.sum(jnp.exp(gl - gmax), axis=0, keepdims=True)

    fine = lg[8:8 + EXPERTS_PER_GROUP]
    for gi in range(1, N_GROUPS):
        fine = jnp.where(grp == gi, lg[8 + gi * EXPERTS_PER_GROUP:8 + (gi + 1) * EXPERTS_PER_GROUP], fine)
    r8 = lax.broadcasted_iota(I32, fine.shape, 0)
    m1 = jnp.max(fine, axis=0, keepdims=True)
    i1 = jnp.min(jnp.where(fine == m1, r8, EXPERTS_PER_GROUP), axis=0, keepdims=True)
    rest = jnp.where(r8 == i1, -jnp.inf, fine)
    m2 = jnp.max(rest, axis=0, keepdims=True)
    i2 = jnp.min(jnp.where(rest == m2, r8, EXPERTS_PER_GROUP), axis=0, keepdims=True)
    e21 = jnp.exp(m2 - m1)
    t1 = 1.0 / (1.0 + e21)
    wgt_ref[0:1, :] = p_grp * t1
    wgt_ref[1:2, :] = p_grp * (e21 * t1)
    e1 = grp * EXPERTS_PER_GROUP + i1
    e2 = grp * EXPERTS_PER_GROUP + i2
    eid_ref[0:1, :] = e1
    eid_ref[1:2, :] = e2

    r32 = lax.broadcasted_iota(I32, (N_EXPERTS, tm), 0)
    is1 = r32 == e1
    is2 = r32 == e2
    member = jnp.logical_or(is1, is2)
    ta = lax.broadcasted_iota(I32, (tm, tm), 0)
    tb = lax.broadcasted_iota(I32, (tm, tm), 1)
    before = (ta < tb).astype(BF16)
    prior = _nn(member.astype(BF16), before) + cnt_ref[:, 0:1]
    rank_ref[0:1, :] = jnp.sum(jnp.where(is1, prior, 0.0), axis=0, keepdims=True).astype(I32)
    rank_ref[1:2, :] = jnp.sum(jnp.where(is2, prior, 0.0), axis=0, keepdims=True).astype(I32)
    cnt_ref[...] = cnt_ref[...] + jnp.sum(member.astype(F32), axis=1, keepdims=True)


def router(h, g, wg, bg, we, be, tm=ROUTER_TM):
    t, d = h.shape
    w_all = jnp.zeros((ROUTER_ROWS, d), F32).at[0:N_GROUPS].set(wg.T).at[8:8 + N_EXPERTS].set(we.T)
    b_all = jnp.zeros((ROUTER_ROWS, 1), F32).at[0:N_GROUPS, 0].set(bg).at[8:8 + N_EXPERTS, 0].set(be)
    wh = w_all.astype(BF16)
    wl = (w_all - wh.astype(F32)).astype(BF16)
    row2 = lambda i: (0, i)
    return pl.pallas_call(
        _router_kernel,
        grid=(t // tm,),
        in_specs=[pl.BlockSpec((tm, d), lambda i: (i, 0)),
                  pl.BlockSpec((1, d), lambda i: (0, 0)),
                  pl.BlockSpec((ROUTER_ROWS, d), lambda i: (0, 0)),
                  pl.BlockSpec((ROUTER_ROWS, d), lambda i: (0, 0)),
                  pl.BlockSpec((ROUTER_ROWS, 1), lambda i: (0, 0))],
        out_specs=[pl.BlockSpec((tm * TOKEN_TILE, 128), lambda i: (i, 0)),
                   pl.BlockSpec((2, tm), row2),
                   pl.BlockSpec((2, tm), row2),
                   pl.BlockSpec((2, tm), row2),
                   pl.BlockSpec((N_EXPERTS, 128), lambda i: (0, 0))],
        out_shape=[jax.ShapeDtypeStruct((t * TOKEN_TILE, 128), F32),
                   jax.ShapeDtypeStruct((2, t), I32),
                   jax.ShapeDtypeStruct((2, t), F32),
                   jax.ShapeDtypeStruct((2, t), I32),
                   jax.ShapeDtypeStruct((N_EXPERTS, 128), F32)],
        compiler_params=_params("arbitrary"),
        name="router",
    )(h, g.reshape(1, d), wh, wl, b_all)


def _dispatch_kernel(last_ref, pos_ref, x_ref, o_ref, zero_ref, sem):
    tm = x_ref.shape[0] // TOKEN_TILE
    tile_rows = EXPERT_TM * TOKEN_TILE

    @pl.when(pl.program_id(0) == 0)
    def _():
        zero_ref[...] = jnp.zeros_like(zero_ref)

        def tile_fill(e):
            row = pl.multiple_of(last_ref[e] * TOKEN_TILE, tile_rows)
            return pltpu.make_async_copy(zero_ref, o_ref.at[pl.ds(row, tile_rows)], sem)

        for e in range(N_EXPERTS):
            @pl.when(last_ref[e] >= 0)
            def _():
                tile_fill(e).start()
        for e in range(N_EXPERTS):
            @pl.when(last_ref[e] >= 0)
            def _():
                tile_fill(e).wait()

        def spare_fill(j):
            row = pl.multiple_of(j * tile_rows, tile_rows)
            return pltpu.make_async_copy(zero_ref, o_ref.at[pl.ds(row, tile_rows)], sem)

        n_tiles = o_ref.shape[0] // tile_rows
        lax.fori_loop(last_ref[N_EXPERTS], n_tiles, lambda j, c: (spare_fill(j).start(), c)[1], 0)
        lax.fori_loop(last_ref[N_EXPERTS], n_tiles, lambda j, c: (spare_fill(j).wait(), c)[1], 0)

    def row_copy(r, k):
        dst = pl.multiple_of(pos_ref[0, 0, k * tm + r] * TOKEN_TILE, TOKEN_TILE)
        return pltpu.make_async_copy(x_ref.at[pl.ds(r * TOKEN_TILE, TOKEN_TILE)],
                                     o_ref.at[pl.ds(dst, TOKEN_TILE)], sem)

    def issue(blk, carry):
        for u in range(ROW_DMA_UNROLL):
            r = blk * ROW_DMA_UNROLL + u
            row_copy(r, 0).start(priority=u % 2)
            row_copy(r, 1).start(priority=(u + 1) % 2)
        return carry

    lax.fori_loop(0, tm // ROW_DMA_UNROLL, issue, 0)
    for _ in range(2):
        pltpu.make_async_copy(x_ref, o_ref.at[pl.ds(0, tm * TOKEN_TILE)], sem).wait()


def dispatch(xn, pos3, last_tile_row, n_rows, tm=GATHER_TM):
    t = xn.shape[0] // TOKEN_TILE
    return pl.pallas_call(
        _dispatch_kernel,
        grid_spec=pltpu.PrefetchScalarGridSpec(
            num_scalar_prefetch=1,
            grid=(t // tm,),
            in_specs=[pl.BlockSpec((1, 1, 2 * tm), lambda i, last: (i, 0, 0), memory_space=pltpu.SMEM),
                      pl.BlockSpec((tm * TOKEN_TILE, 128), lambda i, last: (i, 0))],
            out_specs=pl.BlockSpec(memory_space=pl.ANY),
            scratch_shapes=[pltpu.VMEM((EXPERT_TM * TOKEN_TILE, 128), xn.dtype),
                            pltpu.SemaphoreType.DMA(())]),
        out_shape=jax.ShapeDtypeStruct((n_rows * TOKEN_TILE, 128), xn.dtype),
        compiler_params=_params("arbitrary"),
        name="moe_dispatch",
    )(last_tile_row, pos3, xn)


def _expert_kernel(te_ref, nv_ref, x_ref, wgu_ref, wd_ref, y_ref, wgu_b, wd_b):
    i = pl.program_id(0)
    tm = x_ref.shape[0] // TOKEN_TILE
    valid = i < nv_ref[0]
    changed = te_ref[i] != te_ref[jnp.maximum(i - 1, 0)]
    first = jnp.logical_or(i == 0, changed)

    @pl.when(jnp.logical_and(valid, first))
    def _():
        wgu_b[...] = wgu_ref[0, 0].astype(BF16)
        wd_b[...] = wd_ref[0, 0].astype(BF16)

    @pl.when(valid)
    def _():
        x = _from_tiles(x_ref, tm).astype(BF16)
        au = _nn(x, wgu_b[...])
        a = au[:, :D_EXPERT]
        u = au[:, D_EXPERT:]
        mid = (jax.nn.silu(a) * u).astype(BF16)
        _to_tiles(y_ref, _nn(mid, wd_b[...]))

    @pl.when(jnp.logical_not(valid))
    def _():
        y_ref[...] = jnp.zeros_like(y_ref)


def experts(xg, w_gu, w_down, layer, tile_expert, n_valid, tm=EXPERT_TM):
    d = D_MODEL
    p = xg.shape[0] // TOKEN_TILE
    n_tiles = p // tm
    row = lambda i, te, nv: (jnp.minimum(i, nv[0] - 1), 0)
    wsel = lambda i, te, nv: (layer, te[i], 0, 0)
    return pl.pallas_call(
        _expert_kernel,
        grid_spec=pltpu.PrefetchScalarGridSpec(
            num_scalar_prefetch=2,
            grid=(n_tiles,),
            in_specs=[pl.BlockSpec((tm * TOKEN_TILE, 128), row),
                      pl.BlockSpec((1, 1, d, 2 * D_EXPERT), wsel),
                      pl.BlockSpec((1, 1, D_EXPERT, d), wsel)],
            out_specs=pl.BlockSpec((tm * TOKEN_TILE, 128), lambda i, te, nv: (i, 0)),
            scratch_shapes=[pltpu.VMEM((d, 2 * D_EXPERT), BF16),
                            pltpu.VMEM((D_EXPERT, d), BF16)]),
        out_shape=jax.ShapeDtypeStruct((p * TOKEN_TILE, 128), F32),
        compiler_params=_params("arbitrary"),
        name="moe_experts",
    )(tile_expert, n_valid, xg, w_gu, w_down)


def _combine_kernel(pos_ref, h_ref, w_ref, y_ref, g_ref, o_ref, buf, sem, *, final_norm):
    tm = h_ref.shape[0]

    def row_copy(r, k):
        src = pl.multiple_of(pos_ref[0, 0, k * tm + r] * TOKEN_TILE, TOKEN_TILE)
        return pltpu.make_async_copy(y_ref.at[pl.ds(src, TOKEN_TILE)],
                                     buf.at[k, pl.ds(r * TOKEN_TILE, TOKEN_TILE)], sem)

    def issue(blk, carry):
        for u in range(ROW_DMA_UNROLL):
            r = blk * ROW_DMA_UNROLL + u
            row_copy(r, 0).start(priority=u % 2)
            row_copy(r, 1).start(priority=(u + 1) % 2)
        return carry

    lax.fori_loop(0, tm // ROW_DMA_UNROLL, issue, 0)
    for k in range(2):
        pltpu.make_async_copy(y_ref.at[pl.ds(0, tm * TOKEN_TILE)], buf.at[k], sem).wait()
    w = w_ref[...]
    out = h_ref[...] + w[:, 0:1] * _from_tiles(buf.at[0], tm) + w[:, 1:2] * _from_tiles(buf.at[1], tm)
    if final_norm:
        out = out * lax.rsqrt(jnp.mean(out * out, axis=-1, keepdims=True) + NORM_EPS) * g_ref[...]
    o_ref[...] = out


def combine(h, wgt_t, pos3, y, g, final_norm, tm=GATHER_TM):
    t, d = h.shape
    return pl.pallas_call(
        functools.partial(_combine_kernel, final_norm=final_norm),
        grid=(t // tm,),
        in_specs=[pl.BlockSpec((1, 1, 2 * tm), lambda i: (i, 0, 0), memory_space=pltpu.SMEM),
                  pl.BlockSpec((tm, d), lambda i: (i, 0)),
                  pl.BlockSpec((tm, 2), lambda i: (i, 0)),
                  pl.BlockSpec(memory_space=pl.ANY),
                  pl.BlockSpec((1, d), lambda i: (0, 0))],
        out_specs=pl.BlockSpec((tm, d), lambda i: (i, 0)),
        out_shape=jax.ShapeDtypeStruct((t, d), F32),
        scratch_shapes=[pltpu.VMEM((2, tm * TOKEN_TILE, 128), F32), pltpu.SemaphoreType.DMA(())],
        compiler_params=_params("arbitrary"),
        name="moe_combine",
    )(pos3, h, wgt_t, y, g.reshape(1, d))


def hier_moe_layer(h, norm_g, wg, bg, we, be, w_gu, w_down, layer, out_g, final_norm):
    t, d = h.shape
    xn, eid, wgt, rank, cnt = router(h, norm_g, wg, bg, we, be)
    counts = cnt[:, 0].astype(I32)
    padded = ((counts + EXPERT_TM - 1) // EXPERT_TM) * EXPERT_TM
    ends = jnp.cumsum(padded)
    offs = ends - padded
    n_rows = 2 * t + N_EXPERTS * EXPERT_TM
    n_tiles = n_rows // EXPERT_TM
    n_valid = (ends[-1] // EXPERT_TM).astype(I32).reshape(1)
    tile_start = jnp.arange(n_tiles, dtype=I32) * EXPERT_TM
    tile_start = jnp.minimum(tile_start, ends[-1] - 1)
    tile_expert = jnp.sum((ends[None, :] <= tile_start[:, None]).astype(I32), axis=1)
    onehot = eid[None] == jnp.arange(N_EXPERTS, dtype=I32)[:, None, None]
    pos = jnp.sum(jnp.where(onehot, offs[:, None, None], 0), axis=0) + rank
    nt = t // GATHER_TM
    pos3 = pos.reshape(2, nt, GATHER_TM).transpose(1, 0, 2).reshape(nt, 1, 2 * GATHER_TM)

    last_tile_row = jnp.concatenate([jnp.where(padded > 0, ends - EXPERT_TM, -1).astype(I32), n_valid])
    xg = dispatch(xn, pos3, last_tile_row, n_rows)
    y = experts(xg, w_gu, w_down, layer, tile_expert, n_valid)
    return combine(h, wgt.T, pos3, y, out_g, final_norm)


def _qkv_kernel(h_ref, gq_ref, gkv_ref, wq_ref, wk_ref, wvt_ref, q_ref, k_ref, vt_ref):
    x = h_ref[...]
    y = x * lax.rsqrt(jnp.mean(x * x, axis=-1, keepdims=True) + NORM_EPS)
    xq = (y * gq_ref[...]).astype(BF16)
    xkv = (y * gkv_ref[...]).astype(BF16)
    q_ref[...] = (_nn(xq, wq_ref[...]) * (B_HEAD_DIM ** -0.5 * LOG2E)).astype(q_ref.dtype)
    k_ref[...] = _nn(xkv, wk_ref[...]).astype(k_ref.dtype)
    vt_ref[0] = _nt(wvt_ref[...], xkv).astype(vt_ref.dtype)


def qkv_proj(h, g_q, g_kv, wq, wk, wvt, batch, seq, tm=256):
    t, d = h.shape
    nblk = seq // tm
    full = pl.BlockSpec((d, d), lambda i: (0, 0))
    vec = pl.BlockSpec((1, d), lambda i: (0, 0))
    return pl.pallas_call(
        _qkv_kernel,
        grid=(t // tm,),
        in_specs=[pl.BlockSpec((tm, d), lambda i: (i, 0)), vec, vec, full, full, full],
        out_specs=[pl.BlockSpec((tm, d), lambda i: (i, 0)),
                   pl.BlockSpec((tm, d), lambda i: (i, 0)),
                   pl.BlockSpec((1, d, tm), lambda i: (i // nblk, 0, i % nblk))],
        out_shape=[jax.ShapeDtypeStruct((t, d), BF16),
                   jax.ShapeDtypeStruct((t, d), BF16),
                   jax.ShapeDtypeStruct((batch, d, seq), BF16)],
        compiler_params=_params("parallel"),
        name="qkv_proj",
    )(h, g_q.reshape(1, d), g_kv.reshape(1, d), wq, wk, wvt)


def _attn_kernel(q_ref, k_ref, vt_ref, lam_ref, g_ref, o_ref, qq_ref, m_ref, acc_ref,
                 a_ref, c_ref, *sp_refs, lambda_init, heads):
    s_refs, p_refs = sp_refs[:heads], sp_refs[heads:]
    qi = pl.program_id(2)
    tq, tk, hd = ATT_TQ, ATT_TK, 2 * B_HEAD_DIM
    lane = lax.broadcasted_iota(I32, (tq, hd), 1)
    for g in range(heads):
        q = q_ref[:, g * hd:(g + 1) * hd]
        zero = jnp.zeros_like(q)
        qq_ref[g, 0:tq, :] = jnp.where(lane < B_HEAD_DIM, q, zero)
        qq_ref[g, tq:2 * tq, :] = jnp.where(lane >= B_HEAD_DIM, q, zero)
    m_ref[...] = jnp.full_like(m_ref, -jnp.inf)
    acc_ref[...] = jnp.zeros_like(acc_ref)
    ones = jnp.ones((ATT_SUM_ROWS, tk), BF16)
    dyn0 = jnp.minimum(qi, 0)

    def step(j, masked):
        off = pl.multiple_of(j * tk, tk)
        if masked:
            krow = lax.broadcasted_iota(I32, (tk, 2 * tq), 0)
            qcol = lax.broadcasted_iota(I32, (tk, 2 * tq), 1)
            visible = off + krow <= qi * tq + jnp.where(qcol >= tq, qcol - tq, qcol)
        for g in range(heads):
            kb = k_ref[pl.ds(off, tk), g * hd:(g + 1) * hd]
            s = _nt(kb, qq_ref[g])
            if masked:
                s = jnp.where(visible, s, -jnp.inf)
            s_refs[g][0] = s
            c_ref[g] = jnp.max(s, axis=0, keepdims=True)
        for g in range(heads):
            m_old = m_ref[g]
            m_new = jnp.maximum(m_old, c_ref[g])
            a_ref[g] = jnp.exp2(m_old - m_new)
            m_ref[g] = m_new
            for c in range(0, tk, ATT_CHUNK):
                p_refs[g][0, c:c + ATT_CHUNK, :] = jnp.exp2(
                    s_refs[g][dyn0, c:c + ATT_CHUNK, :] - m_new).astype(BF16)
        for g in range(heads):
            vtb = vt_ref[0, g * hd:(g + 1) * hd, pl.ds(off, tk)]
            lhs = jnp.concatenate([vtb, ones], axis=0)
            acc_ref[g] = a_ref[g] * acc_ref[g] + _nn(lhs, p_refs[g][dyn0])

    n_full = (qi * tq) // tk

    def full_step(j, carry):
        step(j, False)
        return carry

    lax.fori_loop(0, n_full, full_step, 0)
    step(n_full, True)

    lam = lam_ref[...]
    lam_full = (jnp.exp(jnp.sum(lam[0:1] * lam[1:2], axis=-1, keepdims=True))
                - jnp.exp(jnp.sum(lam[2:3] * lam[3:4], axis=-1, keepdims=True)) + lambda_init)
    for g in range(heads):
        acc = acc_ref[g]
        on = acc[:hd] / acc[hd:hd + 1]
        ot = on[:, :tq] - lam_full * on[:, tq:]
        o = ot.T
        o = o * lax.rsqrt(jnp.mean(o * o, axis=-1, keepdims=True) + SUBLN_EPS) * g_ref[...]
        o_ref[:, g * hd:(g + 1) * hd] = (o * (1.0 - lambda_init)).astype(o_ref.dtype)


def diff_attn(q, k, vt, lam, subln_g, batch, seq, lambda_init, heads=ATT_HEADS):
    t, d = q.shape
    nq = seq // ATT_TQ
    hd = 2 * B_HEAD_DIM
    w = heads * hd
    return pl.pallas_call(
        functools.partial(_attn_kernel, lambda_init=lambda_init, heads=heads),
        grid=(batch, B_HEADS // heads, nq),
        in_specs=[pl.BlockSpec((ATT_TQ, w), lambda b, h, i: (b * nq + i, h)),
                  pl.BlockSpec((seq, w), lambda b, h, i: (b, h)),
                  pl.BlockSpec((1, w, seq), lambda b, h, i: (b, h, 0)),
                  pl.BlockSpec(lam.shape, lambda b, h, i: (0, 0)),
                  pl.BlockSpec((1, hd), lambda b, h, i: (0, 0))],
        out_specs=pl.BlockSpec((ATT_TQ, w), lambda b, h, i: (b * nq + i, h)),
        out_shape=jax.ShapeDtypeStruct((t, d), BF16),
        scratch_shapes=[pltpu.VMEM((heads, 2 * ATT_TQ, hd), BF16),
                        pltpu.VMEM((heads, 1, 2 * ATT_TQ), F32),
                        pltpu.VMEM((heads, hd + ATT_SUM_ROWS, 2 * ATT_TQ), F32),
                        pltpu.VMEM((heads, 1, 2 * ATT_TQ), F32),
                        pltpu.VMEM((heads, 1, 2 * ATT_TQ), F32)]
        + [pltpu.VMEM((1, ATT_TK, 2 * ATT_TQ), F32) for _ in range(heads)]
        + [pltpu.VMEM((1, ATT_TK, 2 * ATT_TQ), BF16) for _ in range(heads)],
        compiler_params=_params("parallel", "parallel", "arbitrary"),
        name="diff_attn",
    )(q, k, vt, lam, subln_g.reshape(1, hd))


def kernel(x, a_norm_g, a_w_in, a_lb, a_onorm_g, a_w_out, kv_norm_g, w_kv, b_norm_g, b_w_q, b_lam,
           b_subln_g, b_w_out, ffn_norm_g, router_g_w, router_g_b, router_e_w, router_e_b,
           expert_w_gu, expert_w_down, final_norm_g):
    batch, seq, d = x.shape
    assert d == D_MODEL and a_norm_g.shape[0] == 1 and b_norm_g.shape[0] == 1
    assert seq % max(GLA_BLOCK, ATT_TK, ROUTER_TM) == 0
    t = batch * seq
    h = x.reshape(t, d)

    proj = rms_proj(h, a_norm_g[0], a_w_in[0].astype(BF16))
    o = gla(proj, a_lb, a_onorm_g[0], batch, seq)
    h = proj_res(o, a_w_out[0].astype(BF16), h)
    h = hier_moe_layer(h, ffn_norm_g[0], router_g_w[0], router_g_b[0], router_e_w[0], router_e_b[0],
                       expert_w_gu, expert_w_down, 0, final_norm_g, final_norm=False)

    lambda_init = 0.8 - 0.6 * math.exp(-0.3 * 1)
    q, k, vt = qkv_proj(h, b_norm_g[0], kv_norm_g, b_w_q[0].astype(BF16),
                        w_kv[:, :d].astype(BF16), w_kv[:, d:].T.astype(BF16), batch, seq)
    o = diff_attn(q, k, vt, b_lam[0], b_subln_g[0], batch, seq, lambda_init)
    h = proj_res(o, b_w_out[0].astype(BF16), h)
    h = hier_moe_layer(h, ffn_norm_g[1], router_g_w[1], router_g_b[1], router_e_w[1], router_e_b[1],
                       expert_w_gu, expert_w_down, 1, final_norm_g, final_norm=True)
    return h.reshape(batch, seq, d)
```

```python
import functools
import math

import jax
import jax.numpy as jnp
from jax import lax
from jax.experimental import pallas as pl
from jax.experimental.pallas import tpu as pltpu

F32 = jnp.float32
BF16 = jnp.bfloat16
I32 = jnp.int32

D_MODEL = 1024
A_HEADS = 8
A_HEAD_DIM = 128
B_HEADS = 8
B_HEAD_DIM = 64
N_GROUPS = 4
EXPERTS_PER_GROUP = 8
N_EXPERTS = N_GROUPS * EXPERTS_PER_GROUP
D_EXPERT = 512
NORM_EPS = 1e-6
SUBLN_EPS = 1e-5
LOG2E = 1.4426950408889634

GLA_BLOCK = 128
GLA_HALF = GLA_BLOCK // 2
ATT_TQ = 256
ATT_TK = 512
ATT_HEADS = 8
ATT_CHUNK = 128
ATT_SUM_ROWS = 16
ROUTER_TM = 512
ROUTER_ROWS = 40
EXPERT_TM = 512
GATHER_TM = 256
ROW_DMA_UNROLL = 8
TOKEN_TILE = 8
VMEM_LIMIT = 56 * 1024 * 1024


def _nt(a, b):
    return lax.dot_general(a, b, (((1,), (1,)), ((), ())), preferred_element_type=F32)


def _nn(a, b):
    return jnp.dot(a, b, preferred_element_type=F32)


def _to_tiles(ref, x):
    tm = x.shape[0]
    for s in range(TOKEN_TILE):
        ref[pl.ds(s, tm, stride=TOKEN_TILE), :] = x[:, s * 128:(s + 1) * 128]


def _from_tiles(ref, tm):
    return jnp.concatenate([ref[pl.ds(s, tm, stride=TOKEN_TILE), :] for s in range(TOKEN_TILE)], axis=1)


def _params(*sem, flags=None):
    return pltpu.CompilerParams(dimension_semantics=sem, vmem_limit_bytes=VMEM_LIMIT, flags=flags)


def _rms_proj_kernel(x_ref, g_ref, w_ref, o_ref, *, n_chunks):
    x = x_ref[...]
    y = x * lax.rsqrt(jnp.mean(x * x, axis=-1, keepdims=True) + NORM_EPS)
    xn = (y * g_ref[...]).astype(BF16)
    n = w_ref.shape[1] // n_chunks
    for c in range(n_chunks):
        o_ref[:, c * n:(c + 1) * n] = _nn(xn, w_ref[:, c * n:(c + 1) * n])


def rms_proj(x, g, w_bf16, tm=256):
    t, d = x.shape
    n = w_bf16.shape[1]
    return pl.pallas_call(
        functools.partial(_rms_proj_kernel, n_chunks=n // d),
        grid=(t // tm,),
        in_specs=[pl.BlockSpec((tm, d), lambda i: (i, 0)),
                  pl.BlockSpec((1, d), lambda i: (0, 0)),
                  pl.BlockSpec((d, n), lambda i: (0, 0))],
        out_specs=pl.BlockSpec((tm, n), lambda i: (i, 0)),
        out_shape=jax.ShapeDtypeStruct((t, n), F32),
        compiler_params=_params("parallel"),
        name="rms_proj",
    )(x, g.reshape(1, d), w_bf16)


def _gla_kernel(q_ref, f_ref, i_ref, g_ref, alb_ref, og_ref, o_ref, st_ref):
    c = pl.program_id(1)

    @pl.when(c == 0)
    def _():
        st_ref[...] = jnp.zeros_like(st_ref)

    alb = alb_ref[...]
    e = jnp.exp(alb - jnp.max(alb, axis=0, keepdims=True))
    lb = e[0:1] / jnp.sum(e, axis=0, keepdims=True)
    fz = f_ref[...]
    logf = jnp.log(lb + (1.0 - lb) * jax.nn.sigmoid(fz))
    kk = (1.0 - lb) * jax.nn.sigmoid(-fz)
    qq = jax.nn.silu(q_ref[...])

    r = lax.broadcasted_iota(I32, (GLA_BLOCK, GLA_BLOCK), 0)
    s = lax.broadcasted_iota(I32, (GLA_BLOCK, GLA_BLOCK), 1)
    tril = (s <= r).astype(BF16)
    hi = logf.astype(BF16)
    lo = (logf - hi.astype(F32)).astype(BF16)
    b = _nn(tril, hi) + _nn(tril, lo)

    h0, h1 = slice(0, GLA_HALF), slice(GLA_HALF, GLA_BLOCK)
    b_a_mid = b[GLA_HALF // 2 - 1:GLA_HALF // 2]
    b_a_end = b[GLA_HALF - 1:GLA_HALF]
    b_b_mid = b[GLA_HALF + GLA_HALF // 2 - 1:GLA_HALF + GLA_HALF // 2]
    b_end = b[GLA_BLOCK - 1:GLA_BLOCK]

    qa_n = (qq[h0] * jnp.exp(b[h0] - b_a_mid)).astype(BF16)
    ka_n = (kk[h0] * jnp.exp(b_a_mid - b[h0])).astype(BF16)
    qb_n = (qq[h1] * jnp.exp(b[h1] - b_b_mid)).astype(BF16)
    kb_n = (kk[h1] * jnp.exp(b_b_mid - b[h1])).astype(BF16)
    qb_x = (qq[h1] * jnp.exp(b[h1] - b_a_end)).astype(BF16)
    ka_x = (kk[h0] * jnp.exp(b_a_end - b[h0])).astype(BF16)
    q_dec = (qq * jnp.exp(b)).astype(BF16)
    k_end = (kk * jnp.exp(b_end - b)).astype(BF16)
    d_end = jnp.exp(b_end)

    rr = lax.broadcasted_iota(I32, (GLA_HALF, GLA_HALF), 0)
    ss = lax.broadcasted_iota(I32, (GLA_HALF, GLA_HALF), 1)
    causal = ss <= rr
    og = og_ref[...]

    for h in range(A_HEADS):
        hs = slice(h * A_HEAD_DIM, (h + 1) * A_HEAD_DIM)
        v = i_ref[:, hs]
        vb = v.astype(BF16)
        st = st_ref[h]
        st_b = st.astype(BF16)
        p_aa = jnp.where(causal, _nt(qa_n[:, hs], ka_n[:, hs]), 0.0).astype(BF16)
        p_bb = jnp.where(causal, _nt(qb_n[:, hs], kb_n[:, hs]), 0.0).astype(BF16)
        p_ba = _nt(qb_x[:, hs], ka_x[:, hs]).astype(BF16)
        inter = _nt(q_dec[:, hs], st_b)
        o_a = _nn(p_aa, vb[h0]) + inter[h0]
        o_b = _nn(p_bb, vb[h1]) + _nn(p_ba, vb[h0]) + inter[h1]
        st_ref[h] = d_end[:, hs] * st + _nn(v.T.astype(BF16), k_end[:, hs])
        gate = jax.nn.silu(g_ref[:, hs])
        for rows, o in ((h0, o_a), (h1, o_b)):
            on = o * lax.rsqrt(jnp.mean(o * o, axis=-1, keepdims=True) + NORM_EPS)
            o_ref[rows, hs] = (on * og * gate[rows]).astype(o_ref.dtype)


def gla(proj, a_lb, onorm_g, batch, seq):
    t = proj.shape[0]
    d = D_MODEL
    nblk = seq // GLA_BLOCK

    def col(j):
        return pl.BlockSpec((GLA_BLOCK, d), lambda b, c, j=j: (b * nblk + c, j))

    return pl.pallas_call(
        _gla_kernel,
        grid=(batch, nblk),
        in_specs=[col(0), col(1), col(2), col(3),
                  pl.BlockSpec(a_lb.shape, lambda b, c: (0, 0)),
                  pl.BlockSpec((1, A_HEAD_DIM), lambda b, c: (0, 0))],
        out_specs=pl.BlockSpec((GLA_BLOCK, d), lambda b, c: (b * nblk + c, 0)),
        out_shape=jax.ShapeDtypeStruct((t, d), BF16),
        scratch_shapes=[pltpu.VMEM((A_HEADS, A_HEAD_DIM, A_HEAD_DIM), F32)],
        compiler_params=_params("parallel", "arbitrary"),
        name="gla",
    )(proj, proj, proj, proj, a_lb, onorm_g.reshape(1, A_HEAD_DIM))


def _proj_res_kernel(a_ref, w_ref, r_ref, o_ref):
    o_ref[...] = r_ref[...] + _nn(a_ref[...], w_ref[...])


def proj_res(a_bf16, w_bf16, res, tm=512):
    t, d = res.shape
    return pl.pallas_call(
        _proj_res_kernel,
        grid=(t // tm,),
        in_specs=[pl.BlockSpec((tm, d), lambda i: (i, 0)),
                  pl.BlockSpec((d, d), lambda i: (0, 0)),
                  pl.BlockSpec((tm, d), lambda i: (i, 0))],
        out_specs=pl.BlockSpec((tm, d), lambda i: (i, 0)),
        out_shape=jax.ShapeDtypeStruct((t, d), F32),
        compiler_params=_params("parallel"),
        name="proj_res",
    )(a_bf16, w_bf16, res)


def _router_kernel(h_ref, g_ref, wh_ref, wl_ref, b_ref, xn_ref, eid_ref, wgt_ref, rank_ref, cnt_ref):
    i = pl.program_id(0)
    tm = h_ref.shape[0]

    @pl.when(i == 0)
    def _():
        cnt_ref[...] = jnp.zeros_like(cnt_ref)

    x = h_ref[...]
    xn = x * lax.rsqrt(jnp.mean(x * x, axis=-1, keepdims=True) + NORM_EPS) * g_ref[...]
    _to_tiles(xn_ref, xn)
    xh = xn.astype(BF16)
    xl = (xn - xh.astype(F32)).astype(BF16)
    wh = wh_ref[...]
    lg = _nt(wh, xh) + _nt(wl_ref[...], xh) + _nt(wh, xl) + b_ref[...]

    gl = lg[0:N_GROUPS]
    r4 = lax.broadcasted_iota(I32, gl.shape, 0)
    gmax = jnp.max(gl, axis=0, keepdims=True)
    grp = jnp.min(jnp.where(gl == gmax, r4, N_GROUPS), axis=0, keepdims=True)
    p_grp = 1.0 / jnp.sum(jnp.exp(gl - gmax), axis=0, keepdims=True)

    fine = lg[8:8 + EXPERTS_PER_GROUP]
    for gi in range(1, N_GROUPS):
        fine = jnp.where(grp == gi, lg[8 + gi * EXPERTS_PER_GROUP:8 + (gi + 1) * EXPERTS_PER_GROUP], fine)
    r8 = lax.broadcasted_iota(I32, fine.shape, 0)
    m1 = jnp.max(fine, axis=0, keepdims=True)
    i1 = jnp.min(jnp.where(fine == m1, r8, EXPERTS_PER_GROUP), axis=0, keepdims=True)
    rest = jnp.where(r8 == i1, -jnp.inf, fine)
    m2 = jnp.max(rest, axis=0, keepdims=True)
    i2 = jnp.min(jnp.where(rest == m2, r8, EXPERTS_PER_GROUP), axis=0, keepdims=True)
    e21 = jnp.exp(m2 - m1)
    t1 = 1.0 / (1.0 + e21)
    wgt_ref[0:1, :] = p_grp * t1
    wgt_ref[1:2, :] = p_grp * (e21 * t1)
    e1 = grp * EXPERTS_PER_GROUP + i1
    e2 = grp * EXPERTS_PER_GROUP + i2
    eid_ref[0:1, :] = e1
    eid_ref[1:2, :] = e2

    r32 = lax.broadcasted_iota(I32, (N_EXPERTS, tm), 0)
    is1 = r32 == e1
    is2 = r32 == e2
    member = jnp.logical_or(is1, is2)
    ta = lax.broadcasted_iota(I32, (tm, tm), 0)
    tb = lax.broadcasted_iota(I32, (tm, tm), 1)
    before = (ta < tb).astype(BF16)
    prior = _nn(member.astype(BF16), before) + cnt_ref[:, 0:1]
    rank_ref[0:1, :] = jnp.sum(jnp.where(is1, prior, 0.0), axis=0, keepdims=True).astype(I32)
    rank_ref[1:2, :] = jnp.sum(jnp.where(is2, prior, 0.0), axis=0, keepdims=True).astype(I32)
    cnt_ref[...] = cnt_ref[...] + jnp.sum(member.astype(F32), axis=1, keepdims=True)


def router(h, g, wg, bg, we, be, tm=ROUTER_TM):
    t, d = h.shape
    w_all = jnp.zeros((ROUTER_ROWS, d), F32).at[0:N_GROUPS].set(wg.T).at[8:8 + N_EXPERTS].set(we.T)
    b_all = jnp.zeros((ROUTER_ROWS, 1), F32).at[0:N_GROUPS, 0].set(bg).at[8:8 + N_EXPERTS, 0].set(be)
    wh = w_all.astype(BF16)
    wl = (w_all - wh.astype(F32)).astype(BF16)
    row2 = lambda i: (0, i)
    return pl.pallas_call(
        _router_kernel,
        grid=(t // tm,),
        in_specs=[pl.BlockSpec((tm, d), lambda i: (i, 0)),
                  pl.BlockSpec((1, d), lambda i: (0, 0)),
                  pl.BlockSpec((ROUTER_ROWS, d), lambda i: (0, 0)),
                  pl.BlockSpec((ROUTER_ROWS, d), lambda i: (0, 0)),
                  pl.BlockSpec((ROUTER_ROWS, 1), lambda i: (0, 0))],
        out_specs=[pl.BlockSpec((tm * TOKEN_TILE, 128), lambda i: (i, 0)),
                   pl.BlockSpec((2, tm), row2),
                   pl.BlockSpec((2, tm), row2),
                   pl.BlockSpec((2, tm), row2),
                   pl.BlockSpec((N_EXPERTS, 128), lambda i: (0, 0))],
        out_shape=[jax.ShapeDtypeStruct((t * TOKEN_TILE, 128), F32),
                   jax.ShapeDtypeStruct((2, t), I32),
                   jax.ShapeDtypeStruct((2, t), F32),
                   jax.ShapeDtypeStruct((2, t), I32),
                   jax.ShapeDtypeStruct((N_EXPERTS, 128), F32)],
        compiler_params=_params("arbitrary"),
        name="router",
    )(h, g.reshape(1, d), wh, wl, b_all)


def _dispatch_kernel(last_ref, pos_ref, x_ref, o_ref, zero_ref, sem):
    tm = x_ref.shape[0] // TOKEN_TILE
    tile_rows = EXPERT_TM * TOKEN_TILE

    @pl.when(pl.program_id(0) == 0)
    def _():
        zero_ref[...] = jnp.zeros_like(zero_ref)

        def tile_fill(e):
            row = pl.multiple_of(last_ref[e] * TOKEN_TILE, tile_rows)
            return pltpu.make_async_copy(zero_ref, o_ref.at[pl.ds(row, tile_rows)], sem)

        for e in range(N_EXPERTS):
            @pl.when(last_ref[e] >= 0)
            def _():
                tile_fill(e).start()
        for e in range(N_EXPERTS):
            @pl.when(last_ref[e] >= 0)
            def _():
                tile_fill(e).wait()

        def spare_fill(j):
            row = pl.multiple_of(j * tile_rows, tile_rows)
            return pltpu.make_async_copy(zero_ref, o_ref.at[pl.ds(row, tile_rows)], sem)

        n_tiles = o_ref.shape[0] // tile_rows
        lax.fori_loop(last_ref[N_EXPERTS], n_tiles, lambda j, c: (spare_fill(j).start(), c)[1], 0)
        lax.fori_loop(last_ref[N_EXPERTS], n_tiles, lambda j, c: (spare_fill(j).wait(), c)[1], 0)

    def row_copy(r, k):
        dst = pl.multiple_of(pos_ref[0, 0, k * tm + r] * TOKEN_TILE, TOKEN_TILE)
        return pltpu.make_async_copy(x_ref.at[pl.ds(r * TOKEN_TILE, TOKEN_TILE)],
                                     o_ref.at[pl.ds(dst, TOKEN_TILE)], sem)

    def issue(blk, carry):
        for u in range(ROW_DMA_UNROLL):
            r = blk * ROW_DMA_UNROLL + u
            row_copy(r, 0).start(priority=u % 2)
            row_copy(r, 1).start(priority=(u + 1) % 2)
        return carry

    lax.fori_loop(0, tm // ROW_DMA_UNROLL, issue, 0)
    for _ in range(2):
        pltpu.make_async_copy(x_ref, o_ref.at[pl.ds(0, tm * TOKEN_TILE)], sem).wait()


def dispatch(xn, pos3, last_tile_row, n_rows, tm=GATHER_TM):
    t = xn.shape[0] // TOKEN_TILE
    return pl.pallas_call(
        _dispatch_kernel,
        grid_spec=pltpu.PrefetchScalarGridSpec(
            num_scalar_prefetch=1,
            grid=(t // tm,),
            in_specs=[pl.BlockSpec((1, 1, 2 * tm), lambda i, last: (i, 0, 0), memory_space=pltpu.SMEM),
                      pl.BlockSpec((tm * TOKEN_TILE, 128), lambda i, last: (i, 0))],
            out_specs=pl.BlockSpec(memory_space=pl.ANY),
            scratch_shapes=[pltpu.VMEM((EXPERT_TM * TOKEN_TILE, 128), xn.dtype),
                            pltpu.SemaphoreType.DMA(())]),
        out_shape=jax.ShapeDtypeStruct((n_rows * TOKEN_TILE, 128), xn.dtype),
        compiler_params=_params("arbitrary"),
        name="moe_dispatch",
    )(last_tile_row, pos3, xn)


def _expert_kernel(te_ref, nv_ref, x_ref, wgu_ref, wd_ref, y_ref, wgu_b, wd_b):
    i = pl.program_id(0)
    tm = x_ref.shape[0] // TOKEN_TILE
    valid = i < nv_ref[0]
    changed = te_ref[i] != te_ref[jnp.maximum(i - 1, 0)]
    first = jnp.logical_or(i == 0, changed)

    @pl.when(jnp.logical_and(valid, first))
    def _():
        wgu_b[...] = wgu_ref[0, 0].astype(BF16)
        wd_b[...] = wd_ref[0, 0].astype(BF16)

    @pl.when(valid)
    def _():
        x = _from_tiles(x_ref, tm).astype(BF16)
        au = _nn(x, wgu_b[...])
        a = au[:, :D_EXPERT]
        u = au[:, D_EXPERT:]
        mid = (jax.nn.silu(a) * u).astype(BF16)
        _to_tiles(y_ref, _nn(mid, wd_b[...]))

    @pl.when(jnp.logical_not(valid))
    def _():
        y_ref[...] = jnp.zeros_like(y_ref)


def experts(xg, w_gu, w_down, layer, tile_expert, n_valid, tm=EXPERT_TM):
    d = D_MODEL
    p = xg.shape[0] // TOKEN_TILE
    n_tiles = p // tm
    row = lambda i, te, nv: (jnp.minimum(i, nv[0] - 1), 0)
    wsel = lambda i, te, nv: (layer, te[i], 0, 0)
    return pl.pallas_call(
        _expert_kernel,
        grid_spec=pltpu.PrefetchScalarGridSpec(
            num_scalar_prefetch=2,
            grid=(n_tiles,),
            in_specs=[pl.BlockSpec((tm * TOKEN_TILE, 128), row),
                      pl.BlockSpec((1, 1, d, 2 * D_EXPERT), wsel),
                      pl.BlockSpec((1, 1, D_EXPERT, d), wsel)],
            out_specs=pl.BlockSpec((tm * TOKEN_TILE, 128), lambda i, te, nv: (i, 0)),
            scratch_shapes=[pltpu.VMEM((d, 2 * D_EXPERT), BF16),
                            pltpu.VMEM((D_EXPERT, d), BF16)]),
        out_shape=jax.ShapeDtypeStruct((p * TOKEN_TILE, 128), F32),
        compiler_params=_params("arbitrary"),
        name="moe_experts",
    )(tile_expert, n_valid, xg, w_gu, w_down)


def _combine_kernel(pos_ref, h_ref, w_ref, y_ref, g_ref, o_ref, buf, sem, *, final_norm):
    tm = h_ref.shape[0]

    def row_copy(r, k):
        src = pl.multiple_of(pos_ref[0, 0, k * tm + r] * TOKEN_TILE, TOKEN_TILE)
        return pltpu.make_async_copy(y_ref.at[pl.ds(src, TOKEN_TILE)],
                                     buf.at[k, pl.ds(r * TOKEN_TILE, TOKEN_TILE)], sem)

    def issue(blk, carry):
        for u in range(ROW_DMA_UNROLL):
            r = blk * ROW_DMA_UNROLL + u
            row_copy(r, 0).start(priority=u % 2)
            row_copy(r, 1).start(priority=(u + 1) % 2)
        return carry

    lax.fori_loop(0, tm // ROW_DMA_UNROLL, issue, 0)
    for k in range(2):
        pltpu.make_async_copy(y_ref.at[pl.ds(0, tm * TOKEN_TILE)], buf.at[k], sem).wait()
    w = w_ref[...]
    out = h_ref[...] + w[:, 0:1] * _from_tiles(buf.at[0], tm) + w[:, 1:2] * _from_tiles(buf.at[1], tm)
    if final_norm:
        out = out * lax.rsqrt(jnp.mean(out * out, axis=-1, keepdims=True) + NORM_EPS) * g_ref[...]
    o_ref[...] = out


def combine(h, wgt_t, pos3, y, g, final_norm, tm=GATHER_TM):
    t, d = h.shape
    return pl.pallas_call(
        functools.partial(_combine_kernel, final_norm=final_norm),
        grid=(t // tm,),
        in_specs=[pl.BlockSpec((1, 1, 2 * tm), lambda i: (i, 0, 0), memory_space=pltpu.SMEM),
                  pl.BlockSpec((tm, d), lambda i: (i, 0)),
                  pl.BlockSpec((tm, 2), lambda i: (i, 0)),
                  pl.BlockSpec(memory_space=pl.ANY),
                  pl.BlockSpec((1, d), lambda i: (0, 0))],
        out_specs=pl.BlockSpec((tm, d), lambda i: (i, 0)),
        out_shape=jax.ShapeDtypeStruct((t, d), F32),
        scratch_shapes=[pltpu.VMEM((2, tm * TOKEN_TILE, 128), F32), pltpu.SemaphoreType.DMA(())],
        compiler_params=_params("arbitrary"),
        name="moe_combine",
    )(pos3, h, wgt_t, y, g.reshape(1, d))


def hier_moe_layer(h, norm_g, wg, bg, we, be, w_gu, w_down, layer, out_g, final_norm):
    t, d = h.shape
    xn, eid, wgt, rank, cnt = router(h, norm_g, wg, bg, we, be)
    counts = cnt[:, 0].astype(I32)
    padded = ((counts + EXPERT_TM - 1) // EXPERT_TM) * EXPERT_TM
    ends = jnp.cumsum(padded)
    offs = ends - padded
    n_rows = 2 * t + N_EXPERTS * EXPERT_TM
    n_tiles = n_rows // EXPERT_TM
    n_valid = (ends[-1] // EXPERT_TM).astype(I32).reshape(1)
    tile_start = jnp.arange(n_tiles, dtype=I32) * EXPERT_TM
    tile_start = jnp.minimum(tile_start, ends[-1] - 1)
    tile_expert = jnp.sum((ends[None, :] <= tile_start[:, None]).astype(I32), axis=1)
    onehot = eid[None] == jnp.arange(N_EXPERTS, dtype=I32)[:, None, None]
    pos = jnp.sum(jnp.where(onehot, offs[:, None, None], 0), axis=0) + rank
    nt = t // GATHER_TM
    pos3 = pos.reshape(2, nt, GATHER_TM).transpose(1, 0, 2).reshape(nt, 1, 2 * GATHER_TM)

    last_tile_row = jnp.concatenate([jnp.where(padded > 0, ends - EXPERT_TM, -1).astype(I32), n_valid])
    xg = dispatch(xn, pos3, last_tile_row, n_rows)
    y = experts(xg, w_gu, w_down, layer, tile_expert, n_valid)
    return combine(h, wgt.T, pos3, y, out_g, final_norm)


def _qkv_kernel(h_ref, gq_ref, gkv_ref, wqt_ref, wk_ref, wvt_ref, qt_ref, k_ref, vt_ref):
    x = h_ref[...]
    y = x * lax.rsqrt(jnp.mean(x * x, axis=-1, keepdims=True) + NORM_EPS)
    xq = (y * gq_ref[...]).astype(BF16)
    xkv = (y * gkv_ref[...]).astype(BF16)
    qt_ref[0] = (_nt(wqt_ref[...], xq) * (B_HEAD_DIM ** -0.5 * LOG2E)).astype(qt_ref.dtype)
    k_ref[...] = _nn(xkv, wk_ref[...]).astype(k_ref.dtype)
    vt_ref[0] = _nt(wvt_ref[...], xkv).astype(vt_ref.dtype)


def qkv_proj(h, g_q, g_kv, wq, wk, wvt, batch, seq, tm=256):
    t, d = h.shape
    nblk = seq // tm
    full = pl.BlockSpec((d, d), lambda i: (0, 0))
    vec = pl.BlockSpec((1, d), lambda i: (0, 0))
    return pl.pallas_call(
        _qkv_kernel,
        grid=(t // tm,),
        in_specs=[pl.BlockSpec((tm, d), lambda i: (i, 0)), vec, vec, full, full, full],
        out_specs=[pl.BlockSpec((1, d, tm), lambda i: (i // nblk, 0, i % nblk)),
                   pl.BlockSpec((tm, d), lambda i: (i, 0)),
                   pl.BlockSpec((1, d, tm), lambda i: (i // nblk, 0, i % nblk))],
        out_shape=[jax.ShapeDtypeStruct((batch, d, seq), BF16),
                   jax.ShapeDtypeStruct((t, d), BF16),
                   jax.ShapeDtypeStruct((batch, d, seq), BF16)],
        compiler_params=_params("parallel"),
        name="qkv_proj",
    )(h, g_q.reshape(1, d), g_kv.reshape(1, d), wq, wk, wvt)


def _attn_kernel(q_ref, k_ref, vt_ref, lam_ref, g_ref, o_ref, qq_ref, m_ref, acc_ref,
                 a_ref, c_ref, *sp_refs, lambda_init, heads):
    s_refs, p_refs = sp_refs[:heads], sp_refs[heads:]
    qi = pl.program_id(2)
    tq, tk, hd = ATT_TQ, ATT_TK, 2 * B_HEAD_DIM
    feat = lax.broadcasted_iota(I32, (hd, tq), 0)
    for g in range(heads):
        qt = q_ref[0, g * hd:(g + 1) * hd, :]
        zero = jnp.zeros_like(qt)
        qq_ref[g, :, 0:tq] = jnp.where(feat < B_HEAD_DIM, qt, zero)
        qq_ref[g, :, tq:2 * tq] = jnp.where(feat >= B_HEAD_DIM, qt, zero)
    m_ref[...] = jnp.full_like(m_ref, -jnp.inf)
    acc_ref[...] = jnp.zeros_like(acc_ref)
    ones = jnp.ones((ATT_SUM_ROWS, tk), BF16)
    dyn0 = jnp.minimum(qi, 0)

    def step(j, masked):
        off = pl.multiple_of(j * tk, tk)
        if masked:
            krow = lax.broadcasted_iota(I32, (tk, 2 * tq), 0)
            qcol = lax.broadcasted_iota(I32, (tk, 2 * tq), 1)
            visible = off + krow <= qi * tq + jnp.where(qcol >= tq, qcol - tq, qcol)
        for g in range(heads):
            kb = k_ref[pl.ds(off, tk), g * hd:(g + 1) * hd]
            s = _nn(kb, qq_ref[g])
            if masked:
                s = jnp.where(visible, s, -jnp.inf)
            s_refs[g][0] = s
            c_ref[g] = jnp.max(s, axis=0, keepdims=True)
        for g in range(heads):
            m_old = m_ref[g]
            m_new = jnp.maximum(m_old, c_ref[g])
            a_ref[g] = jnp.exp2(m_old - m_new)
            m_ref[g] = m_new
            for c in range(0, tk, ATT_CHUNK):
                p_refs[g][0, c:c + ATT_CHUNK, :] = jnp.exp2(
                    s_refs[g][dyn0, c:c + ATT_CHUNK, :] - m_new).astype(BF16)
        for g in range(heads):
            vtb = vt_ref[0, g * hd:(g + 1) * hd, pl.ds(off, tk)]
            lhs = jnp.concatenate([vtb, ones], axis=0)
            acc_ref[g] = a_ref[g] * acc_ref[g] + _nn(lhs, p_refs[g][dyn0])

    n_full = (qi * tq) // tk

    def full_step(j, carry):
        step(j, False)
        return carry

    lax.fori_loop(0, n_full, full_step, 0)
    step(n_full, True)

    lam = lam_ref[...]
    lam_full = (jnp.exp(jnp.sum(lam[0:1] * lam[1:2], axis=-1, keepdims=True))
                - jnp.exp(jnp.sum(lam[2:3] * lam[3:4], axis=-1, keepdims=True)) + lambda_init)
    for g in range(heads):
        acc = acc_ref[g]
        on = acc[:hd] / acc[hd:hd + 1]
        ot = on[:, :tq] - lam_full * on[:, tq:]
        o = ot.T
        o = o * lax.rsqrt(jnp.mean(o * o, axis=-1, keepdims=True) + SUBLN_EPS) * g_ref[...]
        o_ref[:, g * hd:(g + 1) * hd] = (o * (1.0 - lambda_init)).astype(o_ref.dtype)


def diff_attn(qt, k, vt, lam, subln_g, batch, seq, lambda_init, heads=ATT_HEADS):
    t, d = k.shape
    nq = seq // ATT_TQ
    hd = 2 * B_HEAD_DIM
    w = heads * hd
    return pl.pallas_call(
        functools.partial(_attn_kernel, lambda_init=lambda_init, heads=heads),
        grid=(batch, B_HEADS // heads, nq),
        in_specs=[pl.BlockSpec((1, w, ATT_TQ), lambda b, h, i: (b, h, i)),
                  pl.BlockSpec((seq, w), lambda b, h, i: (b, h)),
                  pl.BlockSpec((1, w, seq), lambda b, h, i: (b, h, 0)),
                  pl.BlockSpec(lam.shape, lambda b, h, i: (0, 0)),
                  pl.BlockSpec((1, hd), lambda b, h, i: (0, 0))],
        out_specs=pl.BlockSpec((ATT_TQ, w), lambda b, h, i: (b * nq + i, h)),
        out_shape=jax.ShapeDtypeStruct((t, d), BF16),
        scratch_shapes=[pltpu.VMEM((heads, hd, 2 * ATT_TQ), BF16),
                        pltpu.VMEM((heads, 1, 2 * ATT_TQ), F32),
                        pltpu.VMEM((heads, hd + ATT_SUM_ROWS, 2 * ATT_TQ), F32),
                        pltpu.VMEM((heads, 1, 2 * ATT_TQ), F32),
                        pltpu.VMEM((heads, 1, 2 * ATT_TQ), F32)]
        + [pltpu.VMEM((1, ATT_TK, 2 * ATT_TQ), F32) for _ in range(heads)]
        + [pltpu.VMEM((1, ATT_TK, 2 * ATT_TQ), BF16) for _ in range(heads)],
        compiler_params=_params("parallel", "parallel", "arbitrary"),
        name="diff_attn",
    )(qt, k, vt, lam, subln_g.reshape(1, hd))


def kernel(x, a_norm_g, a_w_in, a_lb, a_onorm_g, a_w_out, kv_norm_g, w_kv, b_norm_g, b_w_q, b_lam,
           b_subln_g, b_w_out, ffn_norm_g, router_g_w, router_g_b, router_e_w, router_e_b,
           expert_w_gu, expert_w_down, final_norm_g):
    batch, seq, d = x.shape
    assert d == D_MODEL and a_norm_g.shape[0] == 1 and b_norm_g.shape[0] == 1
    assert seq % max(GLA_BLOCK, ATT_TK, ROUTER_TM) == 0
    t = batch * seq
    h = x.reshape(t, d)

    proj = rms_proj(h, a_norm_g[0], a_w_in[0].astype(BF16))
    o = gla(proj, a_lb, a_onorm_g[0], batch, seq)
    h = proj_res(o, a_w_out[0].astype(BF16), h)
    h = hier_moe_layer(h, ffn_norm_g[0], router_g_w[0], router_g_b[0], router_e_w[0], router_e_b[0],
                       expert_w_gu, expert_w_down, 0, final_norm_g, final_norm=False)

    lambda_init = 0.8 - 0.6 * math.exp(-0.3 * 1)
    qt, k, vt = qkv_proj(h, b_norm_g[0], kv_norm_g, b_w_q[0].T.astype(BF16),
                         w_kv[:, :d].astype(BF16), w_kv[:, d:].T.astype(BF16), batch, seq)
    o = diff_attn(qt, k, vt, b_lam[0], b_subln_g[0], batch, seq, lambda_init)
    h = proj_res(o, b_w_out[0].astype(BF16), h)
    h = hier_moe_layer(h, ffn_norm_g[1], router_g_w[1], router_g_b[1], router_e_w[1], router_e_b[1],
                       expert_w_gu, expert_w_down, 1, final_norm_g, final_norm=True)
    return h.reshape(batch, seq, d)
```

```python
import functools
import math

import jax
import jax.numpy as jnp
from jax import lax
from jax.experimental import pallas as pl
from jax.experimental.pallas import tpu as pltpu

F32 = jnp.float32
BF16 = jnp.bfloat16
I32 = jnp.int32

D_MODEL = 1024
A_HEADS = 8
A_HEAD_DIM = 128
B_HEADS = 8
B_HEAD_DIM = 64
N_GROUPS = 4
EXPERTS_PER_GROUP = 8
N_EXPERTS = N_GROUPS * EXPERTS_PER_GROUP
D_EXPERT = 512
NORM_EPS = 1e-6
SUBLN_EPS = 1e-5
LOG2E = 1.4426950408889634

GLA_BLOCK = 128
GLA_HALF = GLA_BLOCK // 2
HG_BLOCK = 256
ATT_TQ = 256
ATT_TK = 512
ATT_HEADS = 8
ATT_CHUNK = 128
ATT_SUM_ROWS = 16
ROUTER_TM = 512
ROUTER_ROWS = 40
EXPERT_TM = 512
GATHER_TM = 256
ROW_DMA_UNROLL = 8
TOKEN_TILE = 8
VMEM_LIMIT = 56 * 1024 * 1024


def _nt(a, b):
    return lax.dot_general(a, b, (((1,), (1,)), ((), ())), preferred_element_type=F32)


def _nn(a, b):
    return jnp.dot(a, b, preferred_element_type=F32)


def _to_tiles(ref, x):
    tm = x.shape[0]
    for s in range(TOKEN_TILE):
        ref[pl.ds(s, tm, stride=TOKEN_TILE), :] = x[:, s * 128:(s + 1) * 128]


def _from_tiles(ref, tm):
    return jnp.concatenate([ref[pl.ds(s, tm, stride=TOKEN_TILE), :] for s in range(TOKEN_TILE)], axis=1)


def _params(*sem, flags=None):
    return pltpu.CompilerParams(dimension_semantics=sem, vmem_limit_bytes=VMEM_LIMIT, flags=flags)


def _hgrn2_project(x, g_ref, w_ref, proj_ref):
    d = x.shape[1]
    y = x * lax.rsqrt(jnp.mean(x * x, axis=-1, keepdims=True) + NORM_EPS)
    xn = (y * g_ref[...]).astype(BF16)
    for c in range(w_ref.shape[1] // d):
        proj_ref[:, c * d:(c + 1) * d] = _nn(xn, w_ref[:, c * d:(c + 1) * d])


def _gla_block(proj_ref, row0, lb, og, st_ref, out_ref, out_row0):
    d = D_MODEL
    rows = pl.ds(pl.multiple_of(row0, GLA_BLOCK), GLA_BLOCK)
    fz = proj_ref[rows, d:2 * d]
    logf = jnp.log(lb + (1.0 - lb) * jax.nn.sigmoid(fz))
    kk = (1.0 - lb) * jax.nn.sigmoid(-fz)
    qq = jax.nn.silu(proj_ref[rows, 0:d])

    r = lax.broadcasted_iota(I32, (GLA_BLOCK, GLA_BLOCK), 0)
    s = lax.broadcasted_iota(I32, (GLA_BLOCK, GLA_BLOCK), 1)
    tril = (s <= r).astype(BF16)
    hi = logf.astype(BF16)
    lo = (logf - hi.astype(F32)).astype(BF16)
    b = _nn(tril, hi) + _nn(tril, lo)

    h0, h1 = slice(0, GLA_HALF), slice(GLA_HALF, GLA_BLOCK)
    b_a_mid = b[GLA_HALF // 2 - 1:GLA_HALF // 2]
    b_a_end = b[GLA_HALF - 1:GLA_HALF]
    b_b_mid = b[GLA_HALF + GLA_HALF // 2 - 1:GLA_HALF + GLA_HALF // 2]
    b_end = b[GLA_BLOCK - 1:GLA_BLOCK]

    qa_n = (qq[h0] * jnp.exp(b[h0] - b_a_mid)).astype(BF16)
    ka_n = (kk[h0] * jnp.exp(b_a_mid - b[h0])).astype(BF16)
    qb_n = (qq[h1] * jnp.exp(b[h1] - b_b_mid)).astype(BF16)
    kb_n = (kk[h1] * jnp.exp(b_b_mid - b[h1])).astype(BF16)
    qb_x = (qq[h1] * jnp.exp(b[h1] - b_a_end)).astype(BF16)
    ka_x = (kk[h0] * jnp.exp(b_a_end - b[h0])).astype(BF16)
    q_dec = (qq * jnp.exp(b)).astype(BF16)
    k_end = (kk * jnp.exp(b_end - b)).astype(BF16)
    d_end = jnp.exp(b_end)

    rr = lax.broadcasted_iota(I32, (GLA_HALF, GLA_HALF), 0)
    ss = lax.broadcasted_iota(I32, (GLA_HALF, GLA_HALF), 1)
    causal = ss <= rr

    for h in range(A_HEADS):
        hs = slice(h * A_HEAD_DIM, (h + 1) * A_HEAD_DIM)
        v = proj_ref[rows, 2 * d + h * A_HEAD_DIM:2 * d + (h + 1) * A_HEAD_DIM]
        vb = v.astype(BF16)
        st = st_ref[h]
        st_b = st.astype(BF16)
        p_aa = jnp.where(causal, _nt(qa_n[:, hs], ka_n[:, hs]), 0.0).astype(BF16)
        p_bb = jnp.where(causal, _nt(qb_n[:, hs], kb_n[:, hs]), 0.0).astype(BF16)
        p_ba = _nt(qb_x[:, hs], ka_x[:, hs]).astype(BF16)
        inter = _nt(q_dec[:, hs], st_b)
        o_a = _nn(p_aa, vb[h0]) + inter[h0]
        o_b = _nn(p_bb, vb[h1]) + _nn(p_ba, vb[h0]) + inter[h1]
        st_ref[h] = d_end[:, hs] * st + _nn(v.T.astype(BF16), k_end[:, hs])
        gate = jax.nn.silu(proj_ref[rows, 3 * d + h * A_HEAD_DIM:3 * d + (h + 1) * A_HEAD_DIM])
        for half, o in ((0, o_a), (1, o_b)):
            on = o * lax.rsqrt(jnp.mean(o * o, axis=-1, keepdims=True) + NORM_EPS)
            r0 = out_row0 + half * GLA_HALF
            out_ref[r0:r0 + GLA_HALF, hs] = (on * og * gate[half * GLA_HALF:(half + 1) * GLA_HALF]
                                             ).astype(out_ref.dtype)


def _hgrn2_kernel(x_ref, xnext_ref, g_ref, win_ref, alb_ref, og_ref, wout_ref, o_ref,
                  proj_a, proj_b, gated_a, gated_b, st_ref, *, blocks_per_seq):
    step = pl.program_id(0)
    dyn0 = jnp.minimum(step, 0)
    alb = alb_ref[...]
    e = jnp.exp(alb - jnp.max(alb, axis=0, keepdims=True))
    lb = e[0:1] / jnp.sum(e, axis=0, keepdims=True)
    og = og_ref[...]

    @pl.when(step == 0)
    def _():
        _hgrn2_project(x_ref[0:HG_BLOCK, :], g_ref, win_ref, proj_a)

    @pl.when((2 * step) % blocks_per_seq == 0)
    def _():
        st_ref[...] = jnp.zeros_like(st_ref)

    _hgrn2_project(x_ref[HG_BLOCK:2 * HG_BLOCK, :], g_ref, win_ref, proj_b)
    for sub in range(HG_BLOCK // GLA_BLOCK):
        _gla_block(proj_a, dyn0 + sub * GLA_BLOCK, lb, og, st_ref, gated_a, sub * GLA_BLOCK)
    o_ref[0:HG_BLOCK, :] = x_ref[0:HG_BLOCK, :] + _nn(gated_a[...], wout_ref[...])

    _hgrn2_project(xnext_ref[...], g_ref, win_ref, proj_a)
    for sub in range(HG_BLOCK // GLA_BLOCK):
        _gla_block(proj_b, dyn0 + sub * GLA_BLOCK, lb, og, st_ref, gated_b, sub * GLA_BLOCK)
    o_ref[HG_BLOCK:2 * HG_BLOCK, :] = x_ref[HG_BLOCK:2 * HG_BLOCK, :] + _nn(gated_b[...], wout_ref[...])


def hgrn2_layer(x, norm_g, w_in_bf16, a_lb, onorm_g, w_out_bf16, seq):
    t, d = x.shape
    n = w_in_bf16.shape[1]
    nblocks = t // HG_BLOCK
    return pl.pallas_call(
        functools.partial(_hgrn2_kernel, blocks_per_seq=seq // HG_BLOCK),
        grid=(nblocks // 2,),
        in_specs=[pl.BlockSpec((2 * HG_BLOCK, d), lambda i: (i, 0)),
                  pl.BlockSpec((HG_BLOCK, d), lambda i: (jnp.minimum(2 * i + 2, nblocks - 1), 0)),
                  pl.BlockSpec((1, d), lambda i: (0, 0)),
                  pl.BlockSpec((d, n), lambda i: (0, 0)),
                  pl.BlockSpec(a_lb.shape, lambda i: (0, 0)),
                  pl.BlockSpec((1, A_HEAD_DIM), lambda i: (0, 0)),
                  pl.BlockSpec((d, d), lambda i: (0, 0))],
        out_specs=pl.BlockSpec((2 * HG_BLOCK, d), lambda i: (i, 0)),
        out_shape=jax.ShapeDtypeStruct((t, d), F32),
        scratch_shapes=[pltpu.VMEM((HG_BLOCK, n), F32),
                        pltpu.VMEM((HG_BLOCK, n), F32),
                        pltpu.VMEM((HG_BLOCK, d), BF16),
                        pltpu.VMEM((HG_BLOCK, d), BF16),
                        pltpu.VMEM((A_HEADS, A_HEAD_DIM, A_HEAD_DIM), F32)],
        compiler_params=_params("arbitrary"),
        name="hgrn2_layer",
    )(x, x, norm_g.reshape(1, d), w_in_bf16, a_lb, onorm_g.reshape(1, A_HEAD_DIM), w_out_bf16)


def _proj_res_kernel(a_ref, w_ref, r_ref, o_ref):
    o_ref[...] = r_ref[...] + _nn(a_ref[...], w_ref[...])


def proj_res(a_bf16, w_bf16, res, tm=512):
    t, d = res.shape
    return pl.pallas_call(
        _proj_res_kernel,
        grid=(t // tm,),
        in_specs=[pl.BlockSpec((tm, d), lambda i: (i, 0)),
                  pl.BlockSpec((d, d), lambda i: (0, 0)),
                  pl.BlockSpec((tm, d), lambda i: (i, 0))],
        out_specs=pl.BlockSpec((tm, d), lambda i: (i, 0)),
        out_shape=jax.ShapeDtypeStruct((t, d), F32),
        compiler_params=_params("parallel"),
        name="proj_res",
    )(a_bf16, w_bf16, res)


def _router_kernel(h_ref, g_ref, wh_ref, wl_ref, b_ref, xn_ref, eid_ref, wgt_ref, rank_ref, cnt_ref):
    i = pl.program_id(0)
    tm = h_ref.shape[0]

    @pl.when(i == 0)
    def _():
        cnt_ref[...] = jnp.zeros_like(cnt_ref)

    x = h_ref[...]
    xn = x * lax.rsqrt(jnp.mean(x * x, axis=-1, keepdims=True) + NORM_EPS) * g_ref[...]
    _to_tiles(xn_ref, xn)
    xh = xn.astype(BF16)
    xl = (xn - xh.astype(F32)).astype(BF16)
    wh = wh_ref[...]
    lg = _nt(wh, xh) + _nt(wl_ref[...], xh) + _nt(wh, xl) + b_ref[...]

    gl = lg[0:N_GROUPS]
    r4 = lax.broadcasted_iota(I32, gl.shape, 0)
    gmax = jnp.max(gl, axis=0, keepdims=True)
    grp = jnp.min(jnp.where(gl == gmax, r4, N_GROUPS), axis=0, keepdims=True)
    p_grp = 1.0 / jnp.sum(jnp.exp(gl - gmax), axis=0, keepdims=True)

    fine = lg[8:8 + EXPERTS_PER_GROUP]
    for gi in range(1, N_GROUPS):
        fine = jnp.where(grp == gi, lg[8 + gi * EXPERTS_PER_GROUP:8 + (gi + 1) * EXPERTS_PER_GROUP], fine)
    r8 = lax.broadcasted_iota(I32, fine.shape, 0)
    m1 = jnp.max(fine, axis=0, keepdims=True)
    i1 = jnp.min(jnp.where(fine == m1, r8, EXPERTS_PER_GROUP), axis=0, keepdims=True)
    rest = jnp.where(r8 == i1, -jnp.inf, fine)
    m2 = jnp.max(rest, axis=0, keepdims=True)
    i2 = jnp.min(jnp.where(rest == m2, r8, EXPERTS_PER_GROUP), axis=0, keepdims=True)
    e21 = jnp.exp(m2 - m1)
    t1 = 1.0 / (1.0 + e21)
    wgt_ref[0:1, :] = p_grp * t1
    wgt_ref[1:2, :] = p_grp * (e21 * t1)
    e1 = grp * EXPERTS_PER_GROUP + i1
    e2 = grp * EXPERTS_PER_GROUP + i2
    eid_ref[0:1, :] = e1
    eid_ref[1:2, :] = e2

    r32 = lax.broadcasted_iota(I32, (N_EXPERTS, tm), 0)
    is1 = r32 == e1
    is2 = r32 == e2
    member = jnp.logical_or(is1, is2)
    ta = lax.broadcasted_iota(I32, (tm, tm), 0)
    tb = lax.broadcasted_iota(I32, (tm, tm), 1)
    before = (ta < tb).astype(BF16)
    prior = _nn(member.astype(BF16), before) + cnt_ref[:, 0:1]
    rank_ref[0:1, :] = jnp.sum(jnp.where(is1, prior, 0.0), axis=0, keepdims=True).astype(I32)
    rank_ref[1:2, :] = jnp.sum(jnp.where(is2, prior, 0.0), axis=0, keepdims=True).astype(I32)
    cnt_ref[...] = cnt_ref[...] + jnp.sum(member.astype(F32), axis=1, keepdims=True)


def router(h, g, wg, bg, we, be, tm=ROUTER_TM):
    t, d = h.shape
    w_all = jnp.zeros((ROUTER_ROWS, d), F32).at[0:N_GROUPS].set(wg.T).at[8:8 + N_EXPERTS].set(we.T)
    b_all = jnp.zeros((ROUTER_ROWS, 1), F32).at[0:N_GROUPS, 0].set(bg).at[8:8 + N_EXPERTS, 0].set(be)
    wh = w_all.astype(BF16)
    wl = (w_all - wh.astype(F32)).astype(BF16)
    row2 = lambda i: (0, i)
    return pl.pallas_call(
        _router_kernel,
        grid=(t // tm,),
        in_specs=[pl.BlockSpec((tm, d), lambda i: (i, 0)),
                  pl.BlockSpec((1, d), lambda i: (0, 0)),
                  pl.BlockSpec((ROUTER_ROWS, d), lambda i: (0, 0)),
                  pl.BlockSpec((ROUTER_ROWS, d), lambda i: (0, 0)),
                  pl.BlockSpec((ROUTER_ROWS, 1), lambda i: (0, 0))],
        out_specs=[pl.BlockSpec((tm * TOKEN_TILE, 128), lambda i: (i, 0)),
                   pl.BlockSpec((2, tm), row2),
                   pl.BlockSpec((2, tm), row2),
                   pl.BlockSpec((2, tm), row2),
                   pl.BlockSpec((N_EXPERTS, 128), lambda i: (0, 0))],
        out_shape=[jax.ShapeDtypeStruct((t * TOKEN_TILE, 128), F32),
                   jax.ShapeDtypeStruct((2, t), I32),
                   jax.ShapeDtypeStruct((2, t), F32),
                   jax.ShapeDtypeStruct((2, t), I32),
                   jax.ShapeDtypeStruct((N_EXPERTS, 128), F32)],
        compiler_params=_params("arbitrary"),
        name="router",
    )(h, g.reshape(1, d), wh, wl, b_all)


def _dispatch_kernel(last_ref, pos_ref, x_ref, o_ref, zero_ref, sem):
    tm = x_ref.shape[0] // TOKEN_TILE
    tile_rows = EXPERT_TM * TOKEN_TILE

    @pl.when(pl.program_id(0) == 0)
    def _():
        zero_ref[...] = jnp.zeros_like(zero_ref)

        def tile_fill(e):
            row = pl.multiple_of(last_ref[e] * TOKEN_TILE, tile_rows)
            return pltpu.make_async_copy(zero_ref, o_ref.at[pl.ds(row, tile_rows)], sem)

        for e in range(N_EXPERTS):
            @pl.when(last_ref[e] >= 0)
            def _():
                tile_fill(e).start()
        for e in range(N_EXPERTS):
            @pl.when(last_ref[e] >= 0)
            def _():
                tile_fill(e).wait()

        def spare_fill(j):
            row = pl.multiple_of(j * tile_rows, tile_rows)
            return pltpu.make_async_copy(zero_ref, o_ref.at[pl.ds(row, tile_rows)], sem)

        n_tiles = o_ref.shape[0] // tile_rows
        lax.fori_loop(last_ref[N_EXPERTS], n_tiles, lambda j, c: (spare_fill(j).start(), c)[1], 0)
        lax.fori_loop(last_ref[N_EXPERTS], n_tiles, lambda j, c: (spare_fill(j).wait(), c)[1], 0)

    def row_copy(r, k):
        dst = pl.multiple_of(pos_ref[0, 0, k * tm + r] * TOKEN_TILE, TOKEN_TILE)
        return pltpu.make_async_copy(x_ref.at[pl.ds(r * TOKEN_TILE, TOKEN_TILE)],
                                     o_ref.at[pl.ds(dst, TOKEN_TILE)], sem)

    def issue(blk, carry):
        for u in range(ROW_DMA_UNROLL):
            r = blk * ROW_DMA_UNROLL + u
            row_copy(r, 0).start(priority=u % 2)
            row_copy(r, 1).start(priority=(u + 1) % 2)
        return carry

    lax.fori_loop(0, tm // ROW_DMA_UNROLL, issue, 0)
    for _ in range(2):
        pltpu.make_async_copy(x_ref, o_ref.at[pl.ds(0, tm * TOKEN_TILE)], sem).wait()


def dispatch(xn, pos3, last_tile_row, n_rows, tm=GATHER_TM):
    t = xn.shape[0] // TOKEN_TILE
    return pl.pallas_call(
        _dispatch_kernel,
        grid_spec=pltpu.PrefetchScalarGridSpec(
            num_scalar_prefetch=1,
            grid=(t // tm,),
            in_specs=[pl.BlockSpec((1, 1, 2 * tm), lambda i, last: (i, 0, 0), memory_space=pltpu.SMEM),
                      pl.BlockSpec((tm * TOKEN_TILE, 128), lambda i, last: (i, 0))],
            out_specs=pl.BlockSpec(memory_space=pl.ANY),
            scratch_shapes=[pltpu.VMEM((EXPERT_TM * TOKEN_TILE, 128), xn.dtype),
                            pltpu.SemaphoreType.DMA(())]),
        out_shape=jax.ShapeDtypeStruct((n_rows * TOKEN_TILE, 128), xn.dtype),
        compiler_params=_params("arbitrary"),
        name="moe_dispatch",
    )(last_tile_row, pos3, xn)


def _expert_kernel(te_ref, nv_ref, x_ref, wgu_ref, wd_ref, y_ref, wgu_b, wd_b):
    i = pl.program_id(0)
    tm = x_ref.shape[0] // TOKEN_TILE
    valid = i < nv_ref[0]
    changed = te_ref[i] != te_ref[jnp.maximum(i - 1, 0)]
    first = jnp.logical_or(i == 0, changed)

    @pl.when(jnp.logical_and(valid, first))
    def _():
        wgu_b[...] = wgu_ref[0, 0].astype(BF16)
        wd_b[...] = wd_ref[0, 0].astype(BF16)

    @pl.when(valid)
    def _():
        x = _from_tiles(x_ref, tm).astype(BF16)
        au = _nn(x, wgu_b[...])
        a = au[:, :D_EXPERT]
        u = au[:, D_EXPERT:]
        mid = (jax.nn.silu(a) * u).astype(BF16)
        _to_tiles(y_ref, _nn(mid, wd_b[...]))

    @pl.when(jnp.logical_not(valid))
    def _():
        y_ref[...] = jnp.zeros_like(y_ref)


def experts(xg, w_gu, w_down, layer, tile_expert, n_valid, tm=EXPERT_TM):
    d = D_MODEL
    p = xg.shape[0] // TOKEN_TILE
    n_tiles = p // tm
    row = lambda i, te, nv: (jnp.minimum(i, nv[0] - 1), 0)
    wsel = lambda i, te, nv: (layer, te[i], 0, 0)
    return pl.pallas_call(
        _expert_kernel,
        grid_spec=pltpu.PrefetchScalarGridSpec(
            num_scalar_prefetch=2,
            grid=(n_tiles,),
            in_specs=[pl.BlockSpec((tm * TOKEN_TILE, 128), row),
                      pl.BlockSpec((1, 1, d, 2 * D_EXPERT), wsel),
                      pl.BlockSpec((1, 1, D_EXPERT, d), wsel)],
            out_specs=pl.BlockSpec((tm * TOKEN_TILE, 128), lambda i, te, nv: (i, 0)),
            scratch_shapes=[pltpu.VMEM((d, 2 * D_EXPERT), BF16),
                            pltpu.VMEM((D_EXPERT, d), BF16)]),
        out_shape=jax.ShapeDtypeStruct((p * TOKEN_TILE, 128), F32),
        compiler_params=_params("arbitrary"),
        name="moe_experts",
    )(tile_expert, n_valid, xg, w_gu, w_down)


def _combine_kernel(pos_ref, pos_next_ref, h_ref, w_ref, y_ref, g_ref, o_ref, buf, sem, *, final_norm):
    i = pl.program_id(0)
    tm = h_ref.shape[0]
    cur = i % 2

    def gather_tile(p_ref, half):
        def row_copy(r, k):
            src = pl.multiple_of(p_ref[0, 0, k * tm + r] * TOKEN_TILE, TOKEN_TILE)
            return pltpu.make_async_copy(y_ref.at[pl.ds(src, TOKEN_TILE)],
                                         buf.at[2 * half + k, pl.ds(r * TOKEN_TILE, TOKEN_TILE)],
                                         sem.at[half])

        def issue(blk, carry):
            for u in range(ROW_DMA_UNROLL):
                r = blk * ROW_DMA_UNROLL + u
                row_copy(r, 0).start(priority=u % 2)
                row_copy(r, 1).start(priority=(u + 1) % 2)
            return carry

        lax.fori_loop(0, tm // ROW_DMA_UNROLL, issue, 0)

    @pl.when(i == 0)
    def _():
        gather_tile(pos_ref, 0)

    @pl.when(i + 1 < pl.num_programs(0))
    def _():
        gather_tile(pos_next_ref, 1 - cur)

    for k in range(2):
        pltpu.make_async_copy(y_ref.at[pl.ds(0, tm * TOKEN_TILE)], buf.at[2 * cur + k], sem.at[cur]).wait()
    w = w_ref[...]
    out = (h_ref[...] + w[:, 0:1] * _from_tiles(buf.at[2 * cur], tm)
           + w[:, 1:2] * _from_tiles(buf.at[2 * cur + 1], tm))
    if final_norm:
        out = out * lax.rsqrt(jnp.mean(out * out, axis=-1, keepdims=True) + NORM_EPS) * g_ref[...]
    o_ref[...] = out


def combine(h, wgt_t, pos3, y, g, final_norm, tm=GATHER_TM):
    t, d = h.shape
    nt = t // tm
    return pl.pallas_call(
        functools.partial(_combine_kernel, final_norm=final_norm),
        grid=(nt,),
        in_specs=[pl.BlockSpec((1, 1, 2 * tm), lambda i: (i, 0, 0), memory_space=pltpu.SMEM),
                  pl.BlockSpec((1, 1, 2 * tm), lambda i: (jnp.minimum(i + 1, nt - 1), 0, 0),
                               memory_space=pltpu.SMEM),
                  pl.BlockSpec((tm, d), lambda i: (i, 0)),
                  pl.BlockSpec((tm, 2), lambda i: (i, 0)),
                  pl.BlockSpec(memory_space=pl.ANY),
                  pl.BlockSpec((1, d), lambda i: (0, 0))],
        out_specs=pl.BlockSpec((tm, d), lambda i: (i, 0)),
        out_shape=jax.ShapeDtypeStruct((t, d), F32),
        scratch_shapes=[pltpu.VMEM((4, tm * TOKEN_TILE, 128), F32), pltpu.SemaphoreType.DMA((2,))],
        compiler_params=_params("arbitrary"),
        name="moe_combine",
    )(pos3, pos3, h, wgt_t, y, g.reshape(1, d))


def hier_moe_layer(h, norm_g, wg, bg, we, be, w_gu, w_down, layer, out_g, final_norm):
    t, d = h.shape
    xn, eid, wgt, rank, cnt = router(h, norm_g, wg, bg, we, be)
    counts = cnt[:, 0].astype(I32)
    padded = ((counts + EXPERT_TM - 1) // EXPERT_TM) * EXPERT_TM
    ends = jnp.cumsum(padded)
    offs = ends - padded
    n_rows = 2 * t + N_EXPERTS * EXPERT_TM
    n_tiles = n_rows // EXPERT_TM
    n_valid = (ends[-1] // EXPERT_TM).astype(I32).reshape(1)
    tile_start = jnp.arange(n_tiles, dtype=I32) * EXPERT_TM
    tile_start = jnp.minimum(tile_start, ends[-1] - 1)
    tile_expert = jnp.sum((ends[None, :] <= tile_start[:, None]).astype(I32), axis=1)
    onehot = eid[None] == jnp.arange(N_EXPERTS, dtype=I32)[:, None, None]
    pos = jnp.sum(jnp.where(onehot, offs[:, None, None], 0), axis=0) + rank
    nt = t // GATHER_TM
    pos3 = pos.reshape(2, nt, GATHER_TM).transpose(1, 0, 2).reshape(nt, 1, 2 * GATHER_TM)

    last_tile_row = jnp.concatenate([jnp.where(padded > 0, ends - EXPERT_TM, -1).astype(I32), n_valid])
    xg = dispatch(xn, pos3, last_tile_row, n_rows)
    y = experts(xg, w_gu, w_down, layer, tile_expert, n_valid)
    return combine(h, wgt.T, pos3, y, out_g, final_norm)


def _qkv_kernel(h_ref, gq_ref, gkv_ref, wqt_ref, wk_ref, wvt_ref, qt_ref, k_ref, vt_ref):
    x = h_ref[...]
    y = x * lax.rsqrt(jnp.mean(x * x, axis=-1, keepdims=True) + NORM_EPS)
    xq = (y * gq_ref[...]).astype(BF16)
    xkv = (y * gkv_ref[...]).astype(BF16)
    qt_ref[0] = (_nt(wqt_ref[...], xq) * (B_HEAD_DIM ** -0.5 * LOG2E)).astype(qt_ref.dtype)
    k_ref[...] = _nn(xkv, wk_ref[...]).astype(k_ref.dtype)
    vt_ref[0] = _nt(wvt_ref[...], xkv).astype(vt_ref.dtype)


def qkv_proj(h, g_q, g_kv, wq, wk, wvt, batch, seq, tm=256):
    t, d = h.shape
    nblk = seq // tm
    full = pl.BlockSpec((d, d), lambda i: (0, 0))
    vec = pl.BlockSpec((1, d), lambda i: (0, 0))
    return pl.pallas_call(
        _qkv_kernel,
        grid=(t // tm,),
        in_specs=[pl.BlockSpec((tm, d), lambda i: (i, 0)), vec, vec, full, full, full],
        out_specs=[pl.BlockSpec((1, d, tm), lambda i: (i // nblk, 0, i % nblk)),
                   pl.BlockSpec((tm, d), lambda i: (i, 0)),
                   pl.BlockSpec((1, d, tm), lambda i: (i // nblk, 0, i % nblk))],
        out_shape=[jax.ShapeDtypeStruct((batch, d, seq), BF16),
                   jax.ShapeDtypeStruct((t, d), BF16),
                   jax.ShapeDtypeStruct((batch, d, seq), BF16)],
        compiler_params=_params("parallel"),
        name="qkv_proj",
    )(h, g_q.reshape(1, d), g_kv.reshape(1, d), wq, wk, wvt)


def _attn_kernel(q_ref, k_ref, vt_ref, lam_ref, g_ref, o_ref, qq_ref, m_ref, acc_ref,
                 a_ref, c_ref, *sp_refs, lambda_init, heads):
    s_refs, p_refs = sp_refs[:heads], sp_refs[heads:]
    qi = pl.program_id(2)
    tq, tk, hd = ATT_TQ, ATT_TK, 2 * B_HEAD_DIM
    feat = lax.broadcasted_iota(I32, (hd, tq), 0)
    for g in range(heads):
        qt = q_ref[0, g * hd:(g + 1) * hd, :]
        zero = jnp.zeros_like(qt)
        qq_ref[g, :, 0:tq] = jnp.where(feat < B_HEAD_DIM, qt, zero)
        qq_ref[g, :, tq:2 * tq] = jnp.where(feat >= B_HEAD_DIM, qt, zero)
    m_ref[...] = jnp.full_like(m_ref, -jnp.inf)
    acc_ref[...] = jnp.zeros_like(acc_ref)
    ones = jnp.ones((ATT_SUM_ROWS, tk), BF16)
    dyn0 = jnp.minimum(qi, 0)

    def step(j, masked):
        off = pl.multiple_of(j * tk, tk)
        if masked:
            krow = lax.broadcasted_iota(I32, (tk, 2 * tq), 0)
            qcol = lax.broadcasted_iota(I32, (tk, 2 * tq), 1)
            visible = off + krow <= qi * tq + jnp.where(qcol >= tq, qcol - tq, qcol)
        for g in range(heads):
            kb = k_ref[pl.ds(off, tk), g * hd:(g + 1) * hd]
            s = _nn(kb, qq_ref[g])
            if masked:
                s = jnp.where(visible, s, -jnp.inf)
            s_refs[g][0] = s
            c_ref[g] = jnp.max(s, axis=0, keepdims=True)
        for g in range(heads):
            m_old = m_ref[g]
            m_new = jnp.maximum(m_old, c_ref[g])
            a_ref[g] = jnp.exp2(m_old - m_new)
            m_ref[g] = m_new
            for c in range(0, tk, ATT_CHUNK):
                p_refs[g][0, c:c + ATT_CHUNK, :] = jnp.exp2(
                    s_refs[g][dyn0, c:c + ATT_CHUNK, :] - m_new).astype(BF16)
        for g in range(heads):
            vtb = vt_ref[0, g * hd:(g + 1) * hd, pl.ds(off, tk)]
            lhs = jnp.concatenate([vtb, ones], axis=0)
            acc_ref[g] = a_ref[g] * acc_ref[g] + _nn(lhs, p_refs[g][dyn0])

    n_full = (qi * tq) // tk

    def full_step(j, carry):
        step(j, False)
        return carry

    lax.fori_loop(0, n_full, full_step, 0)
    step(n_full, True)

    lam = lam_ref[...]
    lam_full = (jnp.exp(jnp.sum(lam[0:1] * lam[1:2], axis=-1, keepdims=True))
                - jnp.exp(jnp.sum(lam[2:3] * lam[3:4], axis=-1, keepdims=True)) + lambda_init)
    for g in range(heads):
        acc = acc_ref[g]
        on = acc[:hd] / acc[hd:hd + 1]
        ot = on[:, :tq] - lam_full * on[:, tq:]
        o = ot.T
        o = o * lax.rsqrt(jnp.mean(o * o, axis=-1, keepdims=True) + SUBLN_EPS) * g_ref[...]
        o_ref[:, g * hd:(g + 1) * hd] = (o * (1.0 - lambda_init)).astype(o_ref.dtype)


def diff_attn(qt, k, vt, lam, subln_g, batch, seq, lambda_init, heads=ATT_HEADS):
    t, d = k.shape
    nq = seq // ATT_TQ
    hd = 2 * B_HEAD_DIM
    w = heads * hd
    return pl.pallas_call(
        functools.partial(_attn_kernel, lambda_init=lambda_init, heads=heads),
        grid=(batch, B_HEADS // heads, nq),
        in_specs=[pl.BlockSpec((1, w, ATT_TQ), lambda b, h, i: (b, h, i)),
                  pl.BlockSpec((seq, w), lambda b, h, i: (b, h)),
                  pl.BlockSpec((1, w, seq), lambda b, h, i: (b, h, 0)),
                  pl.BlockSpec(lam.shape, lambda b, h, i: (0, 0)),
                  pl.BlockSpec((1, hd), lambda b, h, i: (0, 0))],
        out_specs=pl.BlockSpec((ATT_TQ, w), lambda b, h, i: (b * nq + i, h)),
        out_shape=jax.ShapeDtypeStruct((t, d), BF16),
        scratch_shapes=[pltpu.VMEM((heads, hd, 2 * ATT_TQ), BF16),
                        pltpu.VMEM((heads, 1, 2 * ATT_TQ), F32),
                        pltpu.VMEM((heads, hd + ATT_SUM_ROWS, 2 * ATT_TQ), F32),
                        pltpu.VMEM((heads, 1, 2 * ATT_TQ), F32),
                        pltpu.VMEM((heads, 1, 2 * ATT_TQ), F32)]
        + [pltpu.VMEM((1, ATT_TK, 2 * ATT_TQ), F32) for _ in range(heads)]
        + [pltpu.VMEM((1, ATT_TK, 2 * ATT_TQ), BF16) for _ in range(heads)],
        compiler_params=_params("parallel", "parallel", "arbitrary"),
        name="diff_attn",
    )(qt, k, vt, lam, subln_g.reshape(1, hd))


def kernel(x, a_norm_g, a_w_in, a_lb, a_onorm_g, a_w_out, kv_norm_g, w_kv, b_norm_g, b_w_q, b_lam,
           b_subln_g, b_w_out, ffn_norm_g, router_g_w, router_g_b, router_e_w, router_e_b,
           expert_w_gu, expert_w_down, final_norm_g):
    batch, seq, d = x.shape
    assert d == D_MODEL and a_norm_g.shape[0] == 1 and b_norm_g.shape[0] == 1
    assert seq % max(2 * HG_BLOCK, ATT_TK, ROUTER_TM) == 0
    t = batch * seq
    h = x.reshape(t, d)

    h = hgrn2_layer(h, a_norm_g[0], a_w_in[0].astype(BF16), a_lb, a_onorm_g[0],
                    a_w_out[0].astype(BF16), seq)
    h = hier_moe_layer(h, ffn_norm_g[0], router_g_w[0], router_g_b[0], router_e_w[0], router_e_b[0],
                       expert_w_gu, expert_w_down, 0, final_norm_g, final_norm=False)

    lambda_init = 0.8 - 0.6 * math.exp(-0.3 * 1)
    qt, k, vt = qkv_proj(h, b_norm_g[0], kv_norm_g, b_w_q[0].T.astype(BF16),
                         w_kv[:, :d].astype(BF16), w_kv[:, d:].T.astype(BF16), batch, seq)
    o = diff_attn(qt, k, vt, b_lam[0], b_subln_g[0], batch, seq, lambda_init)
    h = proj_res(o, b_w_out[0].astype(BF16), h)
    h = hier_moe_layer(h, ffn_norm_g[1], router_g_w[1], router_g_b[1], router_e_w[1], router_e_b[1],
                       expert_w_gu, expert_w_down, 1, final_norm_g, final_norm=True)
    return h.reshape(batch, seq, d)
```

```python
import functools
import math

import jax
import jax.numpy as jnp
from jax import lax
from jax.experimental import pallas as pl
from jax.experimental.pallas import tpu as pltpu

F32 = jnp.float32
BF16 = jnp.bfloat16
I32 = jnp.int32

D_MODEL = 1024
A_HEADS = 8
A_HEAD_DIM = 128
B_HEADS = 8
B_HEAD_DIM = 64
N_GROUPS = 4
EXPERTS_PER_GROUP = 8
N_EXPERTS = N_GROUPS * EXPERTS_PER_GROUP
D_EXPERT = 512
NORM_EPS = 1e-6
SUBLN_EPS = 1e-5
LOG2E = 1.4426950408889634

GLA_BLOCK = 128
GLA_HALF = GLA_BLOCK // 2
HG_BLOCK = 256
ATT_TQ = 256
ATT_TK = 512
ATT_HEADS = 8
ATT_CHUNK = 128
ATT_SUM_ROWS = 16
ROUTER_TM = 512
ROUTER_ROWS = 40
EXPERT_TM = 512
EXPERT_X_BUFS = 3
GATHER_TM = 256
ROW_DMA_UNROLL = 8
TOKEN_TILE = 8
VMEM_LIMIT = 56 * 1024 * 1024


def _nt(a, b):
    return lax.dot_general(a, b, (((1,), (1,)), ((), ())), preferred_element_type=F32)


def _nn(a, b):
    return jnp.dot(a, b, preferred_element_type=F32)


def _to_tiles(ref, x):
    tm = x.shape[0]
    for s in range(TOKEN_TILE):
        ref[pl.ds(s, tm, stride=TOKEN_TILE), :] = x[:, s * 128:(s + 1) * 128]


def _from_tiles(ref, tm):
    return jnp.concatenate([ref[pl.ds(s, tm, stride=TOKEN_TILE), :] for s in range(TOKEN_TILE)], axis=1)


def _params(*sem, flags=None):
    return pltpu.CompilerParams(dimension_semantics=sem, vmem_limit_bytes=VMEM_LIMIT, flags=flags)


def _hgrn2_project(x, g_ref, w_ref, proj_ref):
    d = x.shape[1]
    y = x * lax.rsqrt(jnp.mean(x * x, axis=-1, keepdims=True) + NORM_EPS)
    xn = (y * g_ref[...]).astype(BF16)
    for c in range(w_ref.shape[1] // d):
        proj_ref[:, c * d:(c + 1) * d] = _nn(xn, w_ref[:, c * d:(c + 1) * d])


def _gla_block(proj_ref, row0, lb, og, st_ref, out_ref, out_row0):
    d = D_MODEL
    rows = pl.ds(pl.multiple_of(row0, GLA_BLOCK), GLA_BLOCK)
    fz = proj_ref[rows, d:2 * d]
    logf = jnp.log(lb + (1.0 - lb) * jax.nn.sigmoid(fz))
    kk = (1.0 - lb) * jax.nn.sigmoid(-fz)
    qq = jax.nn.silu(proj_ref[rows, 0:d])

    r = lax.broadcasted_iota(I32, (GLA_BLOCK, GLA_BLOCK), 0)
    s = lax.broadcasted_iota(I32, (GLA_BLOCK, GLA_BLOCK), 1)
    tril = (s <= r).astype(BF16)
    hi = logf.astype(BF16)
    lo = (logf - hi.astype(F32)).astype(BF16)
    b = _nn(tril, hi) + _nn(tril, lo)

    h0, h1 = slice(0, GLA_HALF), slice(GLA_HALF, GLA_BLOCK)
    b_a_mid = b[GLA_HALF // 2 - 1:GLA_HALF // 2]
    b_a_end = b[GLA_HALF - 1:GLA_HALF]
    b_b_mid = b[GLA_HALF + GLA_HALF // 2 - 1:GLA_HALF + GLA_HALF // 2]
    b_end = b[GLA_BLOCK - 1:GLA_BLOCK]

    qa_n = (qq[h0] * jnp.exp(b[h0] - b_a_mid)).astype(BF16)
    ka_n = (kk[h0] * jnp.exp(b_a_mid - b[h0])).astype(BF16)
    qb_n = (qq[h1] * jnp.exp(b[h1] - b_b_mid)).astype(BF16)
    kb_n = (kk[h1] * jnp.exp(b_b_mid - b[h1])).astype(BF16)
    qb_x = (qq[h1] * jnp.exp(b[h1] - b_a_end)).astype(BF16)
    ka_x = (kk[h0] * jnp.exp(b_a_end - b[h0])).astype(BF16)
    q_dec = (qq * jnp.exp(b)).astype(BF16)
    k_end = (kk * jnp.exp(b_end - b)).astype(BF16)
    d_end = jnp.exp(b_end)

    rr = lax.broadcasted_iota(I32, (GLA_HALF, GLA_HALF), 0)
    ss = lax.broadcasted_iota(I32, (GLA_HALF, GLA_HALF), 1)
    causal = ss <= rr

    for h in range(A_HEADS):
        hs = slice(h * A_HEAD_DIM, (h + 1) * A_HEAD_DIM)
        v = proj_ref[rows, 2 * d + h * A_HEAD_DIM:2 * d + (h + 1) * A_HEAD_DIM]
        vb = v.astype(BF16)
        st = st_ref[h]
        st_b = st.astype(BF16)
        p_aa = jnp.where(causal, _nt(qa_n[:, hs], ka_n[:, hs]), 0.0).astype(BF16)
        p_bb = jnp.where(causal, _nt(qb_n[:, hs], kb_n[:, hs]), 0.0).astype(BF16)
        p_ba = _nt(qb_x[:, hs], ka_x[:, hs]).astype(BF16)
        inter = _nt(q_dec[:, hs], st_b)
        o_a = _nn(p_aa, vb[h0]) + inter[h0]
        o_b = _nn(p_bb, vb[h1]) + _nn(p_ba, vb[h0]) + inter[h1]
        st_ref[h] = d_end[:, hs] * st + _nn(v.T.astype(BF16), k_end[:, hs])
        gate = jax.nn.silu(proj_ref[rows, 3 * d + h * A_HEAD_DIM:3 * d + (h + 1) * A_HEAD_DIM])
        for half, o in ((0, o_a), (1, o_b)):
            on = o * lax.rsqrt(jnp.mean(o * o, axis=-1, keepdims=True) + NORM_EPS)
            r0 = out_row0 + half * GLA_HALF
            out_ref[r0:r0 + GLA_HALF, hs] = (on * og * gate[half * GLA_HALF:(half + 1) * GLA_HALF]
                                             ).astype(out_ref.dtype)


def _hgrn2_kernel(x_ref, xnext_ref, g_ref, win_ref, alb_ref, og_ref, wout_ref, o_ref,
                  proj_a, proj_b, gated_a, gated_b, st_ref, *, blocks_per_seq):
    step = pl.program_id(0)
    dyn0 = jnp.minimum(step, 0)
    alb = alb_ref[...]
    e = jnp.exp(alb - jnp.max(alb, axis=0, keepdims=True))
    lb = e[0:1] / jnp.sum(e, axis=0, keepdims=True)
    og = og_ref[...]

    @pl.when(step == 0)
    def _():
        _hgrn2_project(x_ref[0:HG_BLOCK, :], g_ref, win_ref, proj_a)

    @pl.when((2 * step) % blocks_per_seq == 0)
    def _():
        st_ref[...] = jnp.zeros_like(st_ref)

    _hgrn2_project(x_ref[HG_BLOCK:2 * HG_BLOCK, :], g_ref, win_ref, proj_b)
    for sub in range(HG_BLOCK // GLA_BLOCK):
        _gla_block(proj_a, dyn0 + sub * GLA_BLOCK, lb, og, st_ref, gated_a, sub * GLA_BLOCK)
    o_ref[0:HG_BLOCK, :] = x_ref[0:HG_BLOCK, :] + _nn(gated_a[...], wout_ref[...])

    _hgrn2_project(xnext_ref[...], g_ref, win_ref, proj_a)
    for sub in range(HG_BLOCK // GLA_BLOCK):
        _gla_block(proj_b, dyn0 + sub * GLA_BLOCK, lb, og, st_ref, gated_b, sub * GLA_BLOCK)
    o_ref[HG_BLOCK:2 * HG_BLOCK, :] = x_ref[HG_BLOCK:2 * HG_BLOCK, :] + _nn(gated_b[...], wout_ref[...])


def hgrn2_layer(x, norm_g, w_in_bf16, a_lb, onorm_g, w_out_bf16, seq):
    t, d = x.shape
    n = w_in_bf16.shape[1]
    nblocks = t // HG_BLOCK
    return pl.pallas_call(
        functools.partial(_hgrn2_kernel, blocks_per_seq=seq // HG_BLOCK),
        grid=(nblocks // 2,),
        in_specs=[pl.BlockSpec((2 * HG_BLOCK, d), lambda i: (i, 0)),
                  pl.BlockSpec((HG_BLOCK, d), lambda i: (jnp.minimum(2 * i + 2, nblocks - 1), 0)),
                  pl.BlockSpec((1, d), lambda i: (0, 0)),
                  pl.BlockSpec((d, n), lambda i: (0, 0)),
                  pl.BlockSpec(a_lb.shape, lambda i: (0, 0)),
                  pl.BlockSpec((1, A_HEAD_DIM), lambda i: (0, 0)),
                  pl.BlockSpec((d, d), lambda i: (0, 0))],
        out_specs=pl.BlockSpec((2 * HG_BLOCK, d), lambda i: (i, 0)),
        out_shape=jax.ShapeDtypeStruct((t, d), F32),
        scratch_shapes=[pltpu.VMEM((HG_BLOCK, n), F32),
                        pltpu.VMEM((HG_BLOCK, n), F32),
                        pltpu.VMEM((HG_BLOCK, d), BF16),
                        pltpu.VMEM((HG_BLOCK, d), BF16),
                        pltpu.VMEM((A_HEADS, A_HEAD_DIM, A_HEAD_DIM), F32)],
        compiler_params=_params("arbitrary"),
        name="hgrn2_layer",
    )(x, x, norm_g.reshape(1, d), w_in_bf16, a_lb, onorm_g.reshape(1, A_HEAD_DIM), w_out_bf16)


def _proj_res_kernel(a_ref, w_ref, r_ref, o_ref):
    o_ref[...] = r_ref[...] + _nn(a_ref[...], w_ref[...])


def proj_res(a_bf16, w_bf16, res, tm=512):
    t, d = res.shape
    return pl.pallas_call(
        _proj_res_kernel,
        grid=(t // tm,),
        in_specs=[pl.BlockSpec((tm, d), lambda i: (i, 0)),
                  pl.BlockSpec((d, d), lambda i: (0, 0)),
                  pl.BlockSpec((tm, d), lambda i: (i, 0))],
        out_specs=pl.BlockSpec((tm, d), lambda i: (i, 0)),
        out_shape=jax.ShapeDtypeStruct((t, d), F32),
        compiler_params=_params("parallel"),
        name="proj_res",
    )(a_bf16, w_bf16, res)


def _router_kernel(h_ref, g_ref, wh_ref, wl_ref, b_ref, xn_ref, eid_ref, wgt_ref, rank_ref, cnt_ref):
    i = pl.program_id(0)
    tm = h_ref.shape[0]

    @pl.when(i == 0)
    def _():
        cnt_ref[...] = jnp.zeros_like(cnt_ref)

    x = h_ref[...]
    xn = x * lax.rsqrt(jnp.mean(x * x, axis=-1, keepdims=True) + NORM_EPS) * g_ref[...]
    _to_tiles(xn_ref, xn)
    xh = xn.astype(BF16)
    xl = (xn - xh.astype(F32)).astype(BF16)
    wh = wh_ref[...]
    lg = _nt(wh, xh) + _nt(wl_ref[...], xh) + _nt(wh, xl) + b_ref[...]

    gl = lg[0:N_GROUPS]
    r4 = lax.broadcasted_iota(I32, gl.shape, 0)
    gmax = jnp.max(gl, axis=0, keepdims=True)
    grp = jnp.min(jnp.where(gl == gmax, r4, N_GROUPS), axis=0, keepdims=True)
    p_grp = 1.0 / jnp.sum(jnp.exp(gl - gmax), axis=0, keepdims=True)

    fine = lg[8:8 + EXPERTS_PER_GROUP]
    for gi in range(1, N_GROUPS):
        fine = jnp.where(grp == gi, lg[8 + gi * EXPERTS_PER_GROUP:8 + (gi + 1) * EXPERTS_PER_GROUP], fine)
    r8 = lax.broadcasted_iota(I32, fine.shape, 0)
    m1 = jnp.max(fine, axis=0, keepdims=True)
    i1 = jnp.min(jnp.where(fine == m1, r8, EXPERTS_PER_GROUP), axis=0, keepdims=True)
    rest = jnp.where(r8 == i1, -jnp.inf, fine)
    m2 = jnp.max(rest, axis=0, keepdims=True)
    i2 = jnp.min(jnp.where(rest == m2, r8, EXPERTS_PER_GROUP), axis=0, keepdims=True)
    e21 = jnp.exp(m2 - m1)
    t1 = 1.0 / (1.0 + e21)
    wgt_ref[0:1, :] = p_grp * t1
    wgt_ref[1:2, :] = p_grp * (e21 * t1)
    e1 = grp * EXPERTS_PER_GROUP + i1
    e2 = grp * EXPERTS_PER_GROUP + i2
    eid_ref[0:1, :] = e1
    eid_ref[1:2, :] = e2

    r32 = lax.broadcasted_iota(I32, (N_EXPERTS, tm), 0)
    is1 = r32 == e1
    is2 = r32 == e2
    member = jnp.logical_or(is1, is2)
    ta = lax.broadcasted_iota(I32, (tm, tm), 0)
    tb = lax.broadcasted_iota(I32, (tm, tm), 1)
    before = (ta < tb).astype(BF16)
    prior = _nn(member.astype(BF16), before) + cnt_ref[:, 0:1]
    rank_ref[0:1, :] = jnp.sum(jnp.where(is1, prior, 0.0), axis=0, keepdims=True).astype(I32)
    rank_ref[1:2, :] = jnp.sum(jnp.where(is2, prior, 0.0), axis=0, keepdims=True).astype(I32)
    cnt_ref[...] = cnt_ref[...] + jnp.sum(member.astype(F32), axis=1, keepdims=True)


def router(h, g, wg, bg, we, be, tm=ROUTER_TM):
    t, d = h.shape
    w_all = jnp.zeros((ROUTER_ROWS, d), F32).at[0:N_GROUPS].set(wg.T).at[8:8 + N_EXPERTS].set(we.T)
    b_all = jnp.zeros((ROUTER_ROWS, 1), F32).at[0:N_GROUPS, 0].set(bg).at[8:8 + N_EXPERTS, 0].set(be)
    wh = w_all.astype(BF16)
    wl = (w_all - wh.astype(F32)).astype(BF16)
    row2 = lambda i: (0, i)
    return pl.pallas_call(
        _router_kernel,
        grid=(t // tm,),
        in_specs=[pl.BlockSpec((tm, d), lambda i: (i, 0)),
                  pl.BlockSpec((1, d), lambda i: (0, 0)),
                  pl.BlockSpec((ROUTER_ROWS, d), lambda i: (0, 0)),
                  pl.BlockSpec((ROUTER_ROWS, d), lambda i: (0, 0)),
                  pl.BlockSpec((ROUTER_ROWS, 1), lambda i: (0, 0))],
        out_specs=[pl.BlockSpec((tm * TOKEN_TILE, 128), lambda i: (i, 0)),
                   pl.BlockSpec((2, tm), row2),
                   pl.BlockSpec((2, tm), row2),
                   pl.BlockSpec((2, tm), row2),
                   pl.BlockSpec((N_EXPERTS, 128), lambda i: (0, 0))],
        out_shape=[jax.ShapeDtypeStruct((t * TOKEN_TILE, 128), F32),
                   jax.ShapeDtypeStruct((2, t), I32),
                   jax.ShapeDtypeStruct((2, t), F32),
                   jax.ShapeDtypeStruct((2, t), I32),
                   jax.ShapeDtypeStruct((N_EXPERTS, 128), F32)],
        compiler_params=_params("arbitrary"),
        name="router",
    )(h, g.reshape(1, d), wh, wl, b_all)


def _dispatch_kernel(last_ref, pos_ref, x_hbm, o_ref, xbuf, zero_ref, load_sem, row_sem):
    i = pl.program_id(0)
    n = pl.num_programs(0)
    tm = xbuf.shape[1] // TOKEN_TILE
    in_rows = tm * TOKEN_TILE
    tile_rows = EXPERT_TM * TOKEN_TILE
    cur = i % 2
    sem = row_sem.at[0]

    def load(j, slot):
        src = pl.multiple_of(j * in_rows, in_rows)
        return pltpu.make_async_copy(x_hbm.at[pl.ds(src, in_rows)], xbuf.at[slot], load_sem.at[slot])

    def drain(slot):
        for _ in range(2):
            pltpu.make_async_copy(xbuf.at[slot], o_ref.at[pl.ds(0, in_rows)], row_sem.at[slot]).wait()

    @pl.when(i == 0)
    def _():
        load(0, 0).start()
        zero_ref[...] = jnp.zeros_like(zero_ref)

        def tile_fill(e):
            row = pl.multiple_of(last_ref[e] * TOKEN_TILE, tile_rows)
            return pltpu.make_async_copy(zero_ref, o_ref.at[pl.ds(row, tile_rows)], sem)

        for e in range(N_EXPERTS):
            @pl.when(last_ref[e] >= 0)
            def _():
                tile_fill(e).start()
        for e in range(N_EXPERTS):
            @pl.when(last_ref[e] >= 0)
            def _():
                tile_fill(e).wait()

        def spare_fill(j):
            row = pl.multiple_of(j * tile_rows, tile_rows)
            return pltpu.make_async_copy(zero_ref, o_ref.at[pl.ds(row, tile_rows)], sem)

        n_tiles = o_ref.shape[0] // tile_rows
        lax.fori_loop(last_ref[N_EXPERTS], n_tiles, lambda j, c: (spare_fill(j).start(), c)[1], 0)
        lax.fori_loop(last_ref[N_EXPERTS], n_tiles, lambda j, c: (spare_fill(j).wait(), c)[1], 0)

    load(i, cur).wait()

    @pl.when(i > 0)
    def _():
        drain(1 - cur)

    @pl.when(i + 1 < n)
    def _():
        load(i + 1, 1 - cur).start()

    def row_copy(r, k):
        dst = pl.multiple_of(pos_ref[0, 0, k * tm + r] * TOKEN_TILE, TOKEN_TILE)
        return pltpu.make_async_copy(xbuf.at[cur, pl.ds(r * TOKEN_TILE, TOKEN_TILE)],
                                     o_ref.at[pl.ds(dst, TOKEN_TILE)], row_sem.at[cur])

    def issue(blk, carry):
        for u in range(ROW_DMA_UNROLL):
            r = blk * ROW_DMA_UNROLL + u
            row_copy(r, 0).start(priority=u % 2)
            row_copy(r, 1).start(priority=(u + 1) % 2)
        return carry

    lax.fori_loop(0, tm // ROW_DMA_UNROLL, issue, 0)

    @pl.when(i == n - 1)
    def _():
        drain(cur)


def dispatch(xn, pos3, last_tile_row, n_rows, tm=GATHER_TM):
    t = xn.shape[0] // TOKEN_TILE
    return pl.pallas_call(
        _dispatch_kernel,
        grid_spec=pltpu.PrefetchScalarGridSpec(
            num_scalar_prefetch=1,
            grid=(t // tm,),
            in_specs=[pl.BlockSpec((1, 1, 2 * tm), lambda i, last: (i, 0, 0), memory_space=pltpu.SMEM),
                      pl.BlockSpec(memory_space=pl.ANY)],
            out_specs=pl.BlockSpec(memory_space=pl.ANY),
            scratch_shapes=[pltpu.VMEM((2, tm * TOKEN_TILE, 128), xn.dtype),
                            pltpu.VMEM((EXPERT_TM * TOKEN_TILE, 128), xn.dtype),
                            pltpu.SemaphoreType.DMA((2,)),
                            pltpu.SemaphoreType.DMA((2,))]),
        out_shape=jax.ShapeDtypeStruct((n_rows * TOKEN_TILE, 128), xn.dtype),
        compiler_params=_params("arbitrary"),
        name="moe_dispatch",
    )(last_tile_row, pos3, xn)


def _expert_kernel(te_ref, nv_ref, x_hbm, wgu_ref, wd_ref, y_ref, xbuf, wgu_b, wd_b, xsem):
    i = pl.program_id(0)
    n_valid = nv_ref[0]
    tm = xbuf.shape[1] // TOKEN_TILE
    in_rows = tm * TOKEN_TILE
    valid = i < n_valid
    changed = te_ref[i] != te_ref[jnp.maximum(i - 1, 0)]
    first = jnp.logical_or(i == 0, changed)

    def xload(j):
        slot = j % EXPERT_X_BUFS
        src = pl.multiple_of(j * in_rows, in_rows)
        return pltpu.make_async_copy(x_hbm.at[pl.ds(src, in_rows)], xbuf.at[slot], xsem.at[slot])

    @pl.when(i == 0)
    def _():
        for j in range(EXPERT_X_BUFS - 1):
            @pl.when(j < n_valid)
            def _():
                xload(j).start()

    @pl.when(i + EXPERT_X_BUFS - 1 < n_valid)
    def _():
        xload(i + EXPERT_X_BUFS - 1).start()

    @pl.when(jnp.logical_and(valid, first))
    def _():
        wgu_b[...] = wgu_ref[0, 0].astype(BF16)
        wd_b[...] = wd_ref[0, 0].astype(BF16)

    @pl.when(valid)
    def _():
        xload(i).wait()
        x = _from_tiles(xbuf.at[i % EXPERT_X_BUFS], tm).astype(BF16)
        au = _nn(x, wgu_b[...])
        a = au[:, :D_EXPERT]
        u = au[:, D_EXPERT:]
        mid = (jax.nn.silu(a) * u).astype(BF16)
        _to_tiles(y_ref, _nn(mid, wd_b[...]))

    @pl.when(jnp.logical_not(valid))
    def _():
        y_ref[...] = jnp.zeros_like(y_ref)


def experts(xg, w_gu, w_down, layer, tile_expert, n_valid, tm=EXPERT_TM):
    d = D_MODEL
    p = xg.shape[0] // TOKEN_TILE
    n_tiles = p // tm
    wsel = lambda i, te, nv: (layer, te[i], 0, 0)
    return pl.pallas_call(
        _expert_kernel,
        grid_spec=pltpu.PrefetchScalarGridSpec(
            num_scalar_prefetch=2,
            grid=(n_tiles,),
            in_specs=[pl.BlockSpec(memory_space=pl.ANY),
                      pl.BlockSpec((1, 1, d, 2 * D_EXPERT), wsel),
                      pl.BlockSpec((1, 1, D_EXPERT, d), wsel)],
            out_specs=pl.BlockSpec((tm * TOKEN_TILE, 128), lambda i, te, nv: (i, 0)),
            scratch_shapes=[pltpu.VMEM((EXPERT_X_BUFS, tm * TOKEN_TILE, 128), F32),
                            pltpu.VMEM((d, 2 * D_EXPERT), BF16),
                            pltpu.VMEM((D_EXPERT, d), BF16),
                            pltpu.SemaphoreType.DMA((EXPERT_X_BUFS,))]),
        out_shape=jax.ShapeDtypeStruct((p * TOKEN_TILE, 128), F32),
        compiler_params=_params("arbitrary"),
        name="moe_experts",
    )(tile_expert, n_valid, xg, w_gu, w_down)


def _combine_kernel(pos_ref, pos_next_ref, h_ref, w_ref, y_ref, g_ref, o_ref, buf, sem, *, final_norm):
    i = pl.program_id(0)
    tm = h_ref.shape[0]
    cur = i % 2

    def gather_tile(p_ref, half):
        def row_copy(r, k):
            src = pl.multiple_of(p_ref[0, 0, k * tm + r] * TOKEN_TILE, TOKEN_TILE)
            return pltpu.make_async_copy(y_ref.at[pl.ds(src, TOKEN_TILE)],
                                         buf.at[2 * half + k, pl.ds(r * TOKEN_TILE, TOKEN_TILE)],
                                         sem.at[half])

        def issue(blk, carry):
            for u in range(ROW_DMA_UNROLL):
                r = blk * ROW_DMA_UNROLL + u
                row_copy(r, 0).start(priority=u % 2)
                row_copy(r, 1).start(priority=(u + 1) % 2)
            return carry

        lax.fori_loop(0, tm // ROW_DMA_UNROLL, issue, 0)

    @pl.when(i == 0)
    def _():
        gather_tile(pos_ref, 0)

    @pl.when(i + 1 < pl.num_programs(0))
    def _():
        gather_tile(pos_next_ref, 1 - cur)

    for k in range(2):
        pltpu.make_async_copy(y_ref.at[pl.ds(0, tm * TOKEN_TILE)], buf.at[2 * cur + k], sem.at[cur]).wait()
    w = w_ref[...]
    out = (h_ref[...] + w[:, 0:1] * _from_tiles(buf.at[2 * cur], tm)
           + w[:, 1:2] * _from_tiles(buf.at[2 * cur + 1], tm))
    if final_norm:
        out = out * lax.rsqrt(jnp.mean(out * out, axis=-1, keepdims=True) + NORM_EPS) * g_ref[...]
    o_ref[...] = out


def combine(h, wgt_t, pos3, y, g, final_norm, tm=GATHER_TM):
    t, d = h.shape
    nt = t // tm
    return pl.pallas_call(
        functools.partial(_combine_kernel, final_norm=final_norm),
        grid=(nt,),
        in_specs=[pl.BlockSpec((1, 1, 2 * tm), lambda i: (i, 0, 0), memory_space=pltpu.SMEM),
                  pl.BlockSpec((1, 1, 2 * tm), lambda i: (jnp.minimum(i + 1, nt - 1), 0, 0),
                               memory_space=pltpu.SMEM),
                  pl.BlockSpec((tm, d), lambda i: (i, 0)),
                  pl.BlockSpec((tm, 2), lambda i: (i, 0)),
                  pl.BlockSpec(memory_space=pl.ANY),
                  pl.BlockSpec((1, d), lambda i: (0, 0))],
        out_specs=pl.BlockSpec((tm, d), lambda i: (i, 0)),
        out_shape=jax.ShapeDtypeStruct((t, d), F32),
        scratch_shapes=[pltpu.VMEM((4, tm * TOKEN_TILE, 128), F32), pltpu.SemaphoreType.DMA((2,))],
        compiler_params=_params("arbitrary"),
        name="moe_combine",
    )(pos3, pos3, h, wgt_t, y, g.reshape(1, d))


def hier_moe_layer(h, norm_g, wg, bg, we, be, w_gu, w_down, layer, out_g, final_norm):
    t, d = h.shape
    xn, eid, wgt, rank, cnt = router(h, norm_g, wg, bg, we, be)
    counts = cnt[:, 0].astype(I32)
    padded = ((counts + EXPERT_TM - 1) // EXPERT_TM) * EXPERT_TM
    ends = jnp.cumsum(padded)
    offs = ends - padded
    n_rows = 2 * t + N_EXPERTS * EXPERT_TM
    n_tiles = n_rows // EXPERT_TM
    n_valid = (ends[-1] // EXPERT_TM).astype(I32).reshape(1)
    tile_start = jnp.arange(n_tiles, dtype=I32) * EXPERT_TM
    tile_start = jnp.minimum(tile_start, ends[-1] - 1)
    tile_expert = jnp.sum((ends[None, :] <= tile_start[:, None]).astype(I32), axis=1)
    onehot = eid[None] == jnp.arange(N_EXPERTS, dtype=I32)[:, None, None]
    pos = jnp.sum(jnp.where(onehot, offs[:, None, None], 0), axis=0) + rank
    nt = t // GATHER_TM
    pos3 = pos.reshape(2, nt, GATHER_TM).transpose(1, 0, 2).reshape(nt, 1, 2 * GATHER_TM)

    last_tile_row = jnp.concatenate([jnp.where(padded > 0, ends - EXPERT_TM, -1).astype(I32), n_valid])
    xg = dispatch(xn, pos3, last_tile_row, n_rows)
    y = experts(xg, w_gu, w_down, layer, tile_expert, n_valid)
    return combine(h, wgt.T, pos3, y, out_g, final_norm)


def _qkv_kernel(h_ref, gq_ref, gkv_ref, wqt_ref, wk_ref, wvt_ref, qt_ref, k_ref, vt_ref):
    x = h_ref[...]
    y = x * lax.rsqrt(jnp.mean(x * x, axis=-1, keepdims=True) + NORM_EPS)
    xq = (y * gq_ref[...]).astype(BF16)
    xkv = (y * gkv_ref[...]).astype(BF16)
    qt_ref[0] = (_nt(wqt_ref[...], xq) * (B_HEAD_DIM ** -0.5 * LOG2E)).astype(qt_ref.dtype)
    k_ref[...] = _nn(xkv, wk_ref[...]).astype(k_ref.dtype)
    vt_ref[0] = _nt(wvt_ref[...], xkv).astype(vt_ref.dtype)


def qkv_proj(h, g_q, g_kv, wq, wk, wvt, batch, seq, tm=256):
    t, d = h.shape
    nblk = seq // tm
    full = pl.BlockSpec((d, d), lambda i: (0, 0))
    vec = pl.BlockSpec((1, d), lambda i: (0, 0))
    return pl.pallas_call(
        _qkv_kernel,
        grid=(t // tm,),
        in_specs=[pl.BlockSpec((tm, d), lambda i: (i, 0)), vec, vec, full, full, full],
        out_specs=[pl.BlockSpec((1, d, tm), lambda i: (i // nblk, 0, i % nblk)),
                   pl.BlockSpec((tm, d), lambda i: (i, 0)),
                   pl.BlockSpec((1, d, tm), lambda i: (i // nblk, 0, i % nblk))],
        out_shape=[jax.ShapeDtypeStruct((batch, d, seq), BF16),
                   jax.ShapeDtypeStruct((t, d), BF16),
                   jax.ShapeDtypeStruct((batch, d, seq), BF16)],
        compiler_params=_params("parallel"),
        name="qkv_proj",
    )(h, g_q.reshape(1, d), g_kv.reshape(1, d), wq, wk, wvt)


def _attn_kernel(q_ref, k_ref, vt_ref, lam_ref, g_ref, o_ref, qq_ref, m_ref, acc_ref,
                 a_ref, c_ref, *sp_refs, lambda_init, heads):
    s_refs, p_refs = sp_refs[:heads], sp_refs[heads:]
    qi = pl.program_id(2)
    tq, tk, hd = ATT_TQ, ATT_TK, 2 * B_HEAD_DIM
    feat = lax.broadcasted_iota(I32, (hd, tq), 0)
    for g in range(heads):
        qt = q_ref[0, g * hd:(g + 1) * hd, :]
        zero = jnp.zeros_like(qt)
        qq_ref[g, :, 0:tq] = jnp.where(feat < B_HEAD_DIM, qt, zero)
        qq_ref[g, :, tq:2 * tq] = jnp.where(feat >= B_HEAD_DIM, qt, zero)
    m_ref[...] = jnp.full_like(m_ref, -jnp.inf)
    acc_ref[...] = jnp.zeros_like(acc_ref)
    ones = jnp.ones((ATT_SUM_ROWS, tk), BF16)
    dyn0 = jnp.minimum(qi, 0)

    def step(j, masked, nk=tk):
        off = pl.multiple_of(j * tk, tk)
        if masked:
            krow = lax.broadcasted_iota(I32, (nk, 2 * tq), 0)
            qcol = lax.broadcasted_iota(I32, (nk, 2 * tq), 1)
            visible = off + krow <= qi * tq + jnp.where(qcol >= tq, qcol - tq, qcol)
        for g in range(heads):
            kb = k_ref[pl.ds(off, nk), g * hd:(g + 1) * hd]
            s = _nn(kb, qq_ref[g])
            if masked:
                s = jnp.where(visible, s, -jnp.inf)
            s_refs[g][0, 0:nk, :] = s
            c_ref[g] = jnp.max(s, axis=0, keepdims=True)
        for g in range(heads):
            m_old = m_ref[g]
            m_new = jnp.maximum(m_old, c_ref[g])
            a_ref[g] = jnp.exp2(m_old - m_new)
            m_ref[g] = m_new
            for c in range(0, nk, ATT_CHUNK):
                p_refs[g][0, c:c + ATT_CHUNK, :] = jnp.exp2(
                    s_refs[g][dyn0, c:c + ATT_CHUNK, :] - m_new).astype(BF16)
        for g in range(heads):
            vtb = vt_ref[0, g * hd:(g + 1) * hd, pl.ds(off, nk)]
            lhs = jnp.concatenate([vtb, ones[:, 0:nk]], axis=0)
            acc_ref[g] = a_ref[g] * acc_ref[g] + _nn(lhs, p_refs[g][dyn0, 0:nk, :])

    n_full = (qi * tq) // tk

    def full_step(j, carry):
        step(j, False)
        return carry

    lax.fori_loop(0, n_full, full_step, 0)

    first_part = (qi * tq) % tk + tq <= tk // 2

    @pl.when(first_part)
    def _():
        step(n_full, True, tk // 2)

    @pl.when(jnp.logical_not(first_part))
    def _():
        step(n_full, True)

    lam = lam_ref[...]
    lam_full = (jnp.exp(jnp.sum(lam[0:1] * lam[1:2], axis=-1, keepdims=True))
                - jnp.exp(jnp.sum(lam[2:3] * lam[3:4], axis=-1, keepdims=True)) + lambda_init)
    for g in range(heads):
        acc = acc_ref[g]
        on = acc[:hd] / acc[hd:hd + 1]
        ot = on[:, :tq] - lam_full * on[:, tq:]
        o = ot.T
        o = o * lax.rsqrt(jnp.mean(o * o, axis=-1, keepdims=True) + SUBLN_EPS) * g_ref[...]
        o_ref[:, g * hd:(g + 1) * hd] = (o * (1.0 - lambda_init)).astype(o_ref.dtype)


def diff_attn(qt, k, vt, lam, subln_g, batch, seq, lambda_init, heads=ATT_HEADS):
    t, d = k.shape
    nq = seq // ATT_TQ
    hd = 2 * B_HEAD_DIM
    w = heads * hd
    return pl.pallas_call(
        functools.partial(_attn_kernel, lambda_init=lambda_init, heads=heads),
        grid=(batch, B_HEADS // heads, nq),
        in_specs=[pl.BlockSpec((1, w, ATT_TQ), lambda b, h, i: (b, h, i)),
                  pl.BlockSpec((seq, w), lambda b, h, i: (b, h)),
                  pl.BlockSpec((1, w, seq), lambda b, h, i: (b, h, 0)),
                  pl.BlockSpec(lam.shape, lambda b, h, i: (0, 0)),
                  pl.BlockSpec((1, hd), lambda b, h, i: (0, 0))],
        out_specs=pl.BlockSpec((ATT_TQ, w), lambda b, h, i: (b * nq + i, h)),
        out_shape=jax.ShapeDtypeStruct((t, d), BF16),
        scratch_shapes=[pltpu.VMEM((heads, hd, 2 * ATT_TQ), BF16),
                        pltpu.VMEM((heads, 1, 2 * ATT_TQ), F32),
                        pltpu.VMEM((heads, hd + ATT_SUM_ROWS, 2 * ATT_TQ), F32),
                        pltpu.VMEM((heads, 1, 2 * ATT_TQ), F32),
                        pltpu.VMEM((heads, 1, 2 * ATT_TQ), F32)]
        + [pltpu.VMEM((1, ATT_TK, 2 * ATT_TQ), F32) for _ in range(heads)]
        + [pltpu.VMEM((1, ATT_TK, 2 * ATT_TQ), BF16) for _ in range(heads)],
        compiler_params=_params("parallel", "parallel", "arbitrary"),
        name="diff_attn",
    )(qt, k, vt, lam, subln_g.reshape(1, hd))


def kernel(x, a_norm_g, a_w_in, a_lb, a_onorm_g, a_w_out, kv_norm_g, w_kv, b_norm_g, b_w_q, b_lam,
           b_subln_g, b_w_out, ffn_norm_g, router_g_w, router_g_b, router_e_w, router_e_b,
           expert_w_gu, expert_w_down, final_norm_g):
    batch, seq, d = x.shape
    assert d == D_MODEL and a_norm_g.shape[0] == 1 and b_norm_g.shape[0] == 1
    assert seq % max(2 * HG_BLOCK, ATT_TK, ROUTER_TM) == 0
    t = batch * seq
    h = x.reshape(t, d)

    h = hgrn2_layer(h, a_norm_g[0], a_w_in[0].astype(BF16), a_lb, a_onorm_g[0],
                    a_w_out[0].astype(BF16), seq)
    h = hier_moe_layer(h, ffn_norm_g[0], router_g_w[0], router_g_b[0], router_e_w[0], router_e_b[0],
                       expert_w_gu, expert_w_down, 0, final_norm_g, final_norm=False)

    lambda_init = 0.8 - 0.6 * math.exp(-0.3 * 1)
    qt, k, vt = qkv_proj(h, b_norm_g[0], kv_norm_g, b_w_q[0].T.astype(BF16),
                         w_kv[:, :d].astype(BF16), w_kv[:, d:].T.astype(BF16), batch, seq)
    o = diff_attn(qt, k, vt, b_lam[0], b_subln_g[0], batch, seq, lambda_init)
    h = proj_res(o, b_w_out[0].astype(BF16), h)
    h = hier_moe_layer(h, ffn_norm_g[1], router_g_w[1], router_g_b[1], router_e_w[1], router_e_b[1],
                       expert_w_gu, expert_w_down, 1, final_norm_g, final_norm=True)
    return h.reshape(batch, seq, d)
```

```python
import functools
import math

import jax
import jax.numpy as jnp
from jax import lax
from jax.experimental import pallas as pl
from jax.experimental.pallas import tpu as pltpu

F32 = jnp.float32
BF16 = jnp.bfloat16
I32 = jnp.int32

D_MODEL = 1024
A_HEADS = 8
A_HEAD_DIM = 128
B_HEADS = 8
B_HEAD_DIM = 64
N_GROUPS = 4
EXPERTS_PER_GROUP = 8
N_EXPERTS = N_GROUPS * EXPERTS_PER_GROUP
D_EXPERT = 512
NORM_EPS = 1e-6
SUBLN_EPS = 1e-5
LOG2E = 1.4426950408889634

GLA_BLOCK = 128
GLA_HALF = GLA_BLOCK // 2
HG_BLOCK = 256
ATT_TQ = 256
ATT_TK = 512
ATT_HEADS = 8
ATT_CHUNK = 128
ATT_SUM_ROWS = 16
ROUTER_TM = 512
ROUTER_ROWS = 40
EXPERT_TM = 512
EXPERT_X_BUFS = 3
GATHER_TM = 256
ROW_DMA_UNROLL = 8
TOKEN_TILE = 8
VMEM_LIMIT = 56 * 1024 * 1024


def _nt(a, b):
    return lax.dot_general(a, b, (((1,), (1,)), ((), ())), preferred_element_type=F32)


def _nn(a, b):
    return jnp.dot(a, b, preferred_element_type=F32)


def _to_tiles(ref, x):
    tm = x.shape[0]
    for s in range(TOKEN_TILE):
        ref[pl.ds(s, tm, stride=TOKEN_TILE), :] = x[:, s * 128:(s + 1) * 128]


def _from_tiles(ref, tm):
    return jnp.concatenate([ref[pl.ds(s, tm, stride=TOKEN_TILE), :] for s in range(TOKEN_TILE)], axis=1)


def _params(*sem, flags=None):
    return pltpu.CompilerParams(dimension_semantics=sem, vmem_limit_bytes=VMEM_LIMIT, flags=flags)


def _hgrn2_project(x, g_ref, w_ref, proj_ref):
    d = x.shape[1]
    y = x * lax.rsqrt(jnp.mean(x * x, axis=-1, keepdims=True) + NORM_EPS)
    xn = (y * g_ref[...]).astype(BF16)
    for c in range(w_ref.shape[1] // d):
        proj_ref[:, c * d:(c + 1) * d] = _nn(xn, w_ref[:, c * d:(c + 1) * d])


def _gla_block(proj_ref, row0, lb, og, st_ref, out_ref, out_row0):
    d = D_MODEL
    rows = pl.ds(pl.multiple_of(row0, GLA_BLOCK), GLA_BLOCK)
    fz = proj_ref[rows, d:2 * d]
    logf = jnp.log(lb + (1.0 - lb) * jax.nn.sigmoid(fz))
    kk = (1.0 - lb) * jax.nn.sigmoid(-fz)
    qq = jax.nn.silu(proj_ref[rows, 0:d])

    r = lax.broadcasted_iota(I32, (GLA_BLOCK, GLA_BLOCK), 0)
    s = lax.broadcasted_iota(I32, (GLA_BLOCK, GLA_BLOCK), 1)
    tril = (s <= r).astype(BF16)
    hi = logf.astype(BF16)
    lo = (logf - hi.astype(F32)).astype(BF16)
    b = _nn(tril, hi) + _nn(tril, lo)

    h0, h1 = slice(0, GLA_HALF), slice(GLA_HALF, GLA_BLOCK)
    b_a_mid = b[GLA_HALF // 2 - 1:GLA_HALF // 2]
    b_a_end = b[GLA_HALF - 1:GLA_HALF]
    b_b_mid = b[GLA_HALF + GLA_HALF // 2 - 1:GLA_HALF + GLA_HALF // 2]
    b_end = b[GLA_BLOCK - 1:GLA_BLOCK]

    qa_n = (qq[h0] * jnp.exp(b[h0] - b_a_mid)).astype(BF16)
    ka_n = (kk[h0] * jnp.exp(b_a_mid - b[h0])).astype(BF16)
    qb_n = (qq[h1] * jnp.exp(b[h1] - b_b_mid)).astype(BF16)
    kb_n = (kk[h1] * jnp.exp(b_b_mid - b[h1])).astype(BF16)
    qb_x = (qq[h1] * jnp.exp(b[h1] - b_a_end)).astype(BF16)
    ka_x = (kk[h0] * jnp.exp(b_a_end - b[h0])).astype(BF16)
    q_dec = (qq * jnp.exp(b)).astype(BF16)
    k_end = (kk * jnp.exp(b_end - b)).astype(BF16)
    d_end = jnp.exp(b_end)

    rr = lax.broadcasted_iota(I32, (GLA_HALF, GLA_HALF), 0)
    ss = lax.broadcasted_iota(I32, (GLA_HALF, GLA_HALF), 1)
    causal = ss <= rr

    for h in range(A_HEADS):
        hs = slice(h * A_HEAD_DIM, (h + 1) * A_HEAD_DIM)
        v = proj_ref[rows, 2 * d + h * A_HEAD_DIM:2 * d + (h + 1) * A_HEAD_DIM]
        vb = v.astype(BF16)
        st = st_ref[h]
        st_b = st.astype(BF16)
        p_aa = jnp.where(causal, _nt(qa_n[:, hs], ka_n[:, hs]), 0.0).astype(BF16)
        p_bb = jnp.where(causal, _nt(qb_n[:, hs], kb_n[:, hs]), 0.0).astype(BF16)
        p_ba = _nt(qb_x[:, hs], ka_x[:, hs]).astype(BF16)
        inter = _nt(q_dec[:, hs], st_b)
        o_a = _nn(p_aa, vb[h0]) + inter[h0]
        o_b = _nn(p_bb, vb[h1]) + _nn(p_ba, vb[h0]) + inter[h1]
        st_ref[h] = d_end[:, hs] * st + _nn(v.T.astype(BF16), k_end[:, hs])
        gate = jax.nn.silu(proj_ref[rows, 3 * d + h * A_HEAD_DIM:3 * d + (h + 1) * A_HEAD_DIM])
        for half, o in ((0, o_a), (1, o_b)):
            on = o * lax.rsqrt(jnp.mean(o * o, axis=-1, keepdims=True) + NORM_EPS)
            r0 = out_row0 + half * GLA_HALF
            out_ref[r0:r0 + GLA_HALF, hs] = (on * og * gate[half * GLA_HALF:(half + 1) * GLA_HALF]
                                             ).astype(out_ref.dtype)


def _hgrn2_kernel(x_ref, xnext_ref, g_ref, win_ref, alb_ref, og_ref, wout_ref, o_ref,
                  proj_a, proj_b, gated_a, gated_b, st_ref, *, blocks_per_seq):
    step = pl.program_id(0)
    dyn0 = jnp.minimum(step, 0)
    alb = alb_ref[...]
    e = jnp.exp(alb - jnp.max(alb, axis=0, keepdims=True))
    lb = e[0:1] / jnp.sum(e, axis=0, keepdims=True)
    og = og_ref[...]

    @pl.when(step == 0)
    def _():
        _hgrn2_project(x_ref[0:HG_BLOCK, :], g_ref, win_ref, proj_a)

    @pl.when((2 * step) % blocks_per_seq == 0)
    def _():
        st_ref[...] = jnp.zeros_like(st_ref)

    _hgrn2_project(x_ref[HG_BLOCK:2 * HG_BLOCK, :], g_ref, win_ref, proj_b)
    for sub in range(HG_BLOCK // GLA_BLOCK):
        _gla_block(proj_a, dyn0 + sub * GLA_BLOCK, lb, og, st_ref, gated_a, sub * GLA_BLOCK)
    o_ref[0:HG_BLOCK, :] = x_ref[0:HG_BLOCK, :] + _nn(gated_a[...], wout_ref[...])

    _hgrn2_project(xnext_ref[...], g_ref, win_ref, proj_a)
    for sub in range(HG_BLOCK // GLA_BLOCK):
        _gla_block(proj_b, dyn0 + sub * GLA_BLOCK, lb, og, st_ref, gated_b, sub * GLA_BLOCK)
    o_ref[HG_BLOCK:2 * HG_BLOCK, :] = x_ref[HG_BLOCK:2 * HG_BLOCK, :] + _nn(gated_b[...], wout_ref[...])


def hgrn2_layer(x, norm_g, w_in_bf16, a_lb, onorm_g, w_out_bf16, seq):
    t, d = x.shape
    n = w_in_bf16.shape[1]
    nblocks = t // HG_BLOCK
    return pl.pallas_call(
        functools.partial(_hgrn2_kernel, blocks_per_seq=seq // HG_BLOCK),
        grid=(nblocks // 2,),
        in_specs=[pl.BlockSpec((2 * HG_BLOCK, d), lambda i: (i, 0)),
                  pl.BlockSpec((HG_BLOCK, d), lambda i: (jnp.minimum(2 * i + 2, nblocks - 1), 0)),
                  pl.BlockSpec((1, d), lambda i: (0, 0)),
                  pl.BlockSpec((d, n), lambda i: (0, 0)),
                  pl.BlockSpec(a_lb.shape, lambda i: (0, 0)),
                  pl.BlockSpec((1, A_HEAD_DIM), lambda i: (0, 0)),
                  pl.BlockSpec((d, d), lambda i: (0, 0))],
        out_specs=pl.BlockSpec((2 * HG_BLOCK, d), lambda i: (i, 0)),
        out_shape=jax.ShapeDtypeStruct((t, d), F32),
        scratch_shapes=[pltpu.VMEM((HG_BLOCK, n), F32),
                        pltpu.VMEM((HG_BLOCK, n), F32),
                        pltpu.VMEM((HG_BLOCK, d), BF16),
                        pltpu.VMEM((HG_BLOCK, d), BF16),
                        pltpu.VMEM((A_HEADS, A_HEAD_DIM, A_HEAD_DIM), F32)],
        compiler_params=_params("arbitrary"),
        name="hgrn2_layer",
    )(x, x, norm_g.reshape(1, d), w_in_bf16, a_lb, onorm_g.reshape(1, A_HEAD_DIM), w_out_bf16)


def _proj_res_kernel(a_ref, w_ref, r_ref, o_ref):
    o_ref[...] = r_ref[...] + _nn(a_ref[...], w_ref[...])


def proj_res(a_bf16, w_bf16, res, tm=512):
    t, d = res.shape
    return pl.pallas_call(
        _proj_res_kernel,
        grid=(t // tm,),
        in_specs=[pl.BlockSpec((tm, d), lambda i: (i, 0)),
                  pl.BlockSpec((d, d), lambda i: (0, 0)),
                  pl.BlockSpec((tm, d), lambda i: (i, 0))],
        out_specs=pl.BlockSpec((tm, d), lambda i: (i, 0)),
        out_shape=jax.ShapeDtypeStruct((t, d), F32),
        compiler_params=_params("parallel"),
        name="proj_res",
    )(a_bf16, w_bf16, res)


def _router_kernel(h_ref, g_ref, wh_ref, wl_ref, b_ref, xn_ref, eid_ref, wgt_ref, rank_ref, cnt_ref):
    i = pl.program_id(0)
    tm = h_ref.shape[0]

    @pl.when(i == 0)
    def _():
        cnt_ref[...] = jnp.zeros_like(cnt_ref)

    x = h_ref[...]
    xn = x * lax.rsqrt(jnp.mean(x * x, axis=-1, keepdims=True) + NORM_EPS) * g_ref[...]
    _to_tiles(xn_ref, xn)
    xh = xn.astype(BF16)
    xl = (xn - xh.astype(F32)).astype(BF16)
    wh = wh_ref[...]
    lg = _nt(wh, xh) + _nt(wl_ref[...], xh) + _nt(wh, xl) + b_ref[...]

    gl = lg[0:N_GROUPS]
    r4 = lax.broadcasted_iota(I32, gl.shape, 0)
    gmax = jnp.max(gl, axis=0, keepdims=True)
    grp = jnp.min(jnp.where(gl == gmax, r4, N_GROUPS), axis=0, keepdims=True)
    p_grp = 1.0 / jnp.sum(jnp.exp(gl - gmax), axis=0, keepdims=True)

    fine = lg[8:8 + EXPERTS_PER_GROUP]
    for gi in range(1, N_GROUPS):
        fine = jnp.where(grp == gi, lg[8 + gi * EXPERTS_PER_GROUP:8 + (gi + 1) * EXPERTS_PER_GROUP], fine)
    r8 = lax.broadcasted_iota(I32, fine.shape, 0)
    m1 = jnp.max(fine, axis=0, keepdims=True)
    i1 = jnp.min(jnp.where(fine == m1, r8, EXPERTS_PER_GROUP), axis=0, keepdims=True)
    rest = jnp.where(r8 == i1, -jnp.inf, fine)
    m2 = jnp.max(rest, axis=0, keepdims=True)
    i2 = jnp.min(jnp.where(rest == m2, r8, EXPERTS_PER_GROUP), axis=0, keepdims=True)
    e21 = jnp.exp(m2 - m1)
    t1 = 1.0 / (1.0 + e21)
    wgt_ref[0:1, :] = p_grp * t1
    wgt_ref[1:2, :] = p_grp * (e21 * t1)
    e1 = grp * EXPERTS_PER_GROUP + i1
    e2 = grp * EXPERTS_PER_GROUP + i2
    eid_ref[0:1, :] = e1
    eid_ref[1:2, :] = e2

    r32 = lax.broadcasted_iota(I32, (N_EXPERTS, tm), 0)
    is1 = r32 == e1
    is2 = r32 == e2
    member = jnp.logical_or(is1, is2)
    ta = lax.broadcasted_iota(I32, (tm, tm), 0)
    tb = lax.broadcasted_iota(I32, (tm, tm), 1)
    before = (ta < tb).astype(BF16)
    prior = _nn(member.astype(BF16), before) + cnt_ref[:, 0:1]
    rank_ref[0:1, :] = jnp.sum(jnp.where(is1, prior, 0.0), axis=0, keepdims=True).astype(I32)
    rank_ref[1:2, :] = jnp.sum(jnp.where(is2, prior, 0.0), axis=0, keepdims=True).astype(I32)
    cnt_ref[...] = cnt_ref[...] + jnp.sum(member.astype(F32), axis=1, keepdims=True)


def router(h, g, wg, bg, we, be, tm=ROUTER_TM):
    t, d = h.shape
    w_all = jnp.zeros((ROUTER_ROWS, d), F32).at[0:N_GROUPS].set(wg.T).at[8:8 + N_EXPERTS].set(we.T)
    b_all = jnp.zeros((ROUTER_ROWS, 1), F32).at[0:N_GROUPS, 0].set(bg).at[8:8 + N_EXPERTS, 0].set(be)
    wh = w_all.astype(BF16)
    wl = (w_all - wh.astype(F32)).astype(BF16)
    row2 = lambda i: (0, i)
    return pl.pallas_call(
        _router_kernel,
        grid=(t // tm,),
        in_specs=[pl.BlockSpec((tm, d), lambda i: (i, 0)),
                  pl.BlockSpec((1, d), lambda i: (0, 0)),
                  pl.BlockSpec((ROUTER_ROWS, d), lambda i: (0, 0)),
                  pl.BlockSpec((ROUTER_ROWS, d), lambda i: (0, 0)),
                  pl.BlockSpec((ROUTER_ROWS, 1), lambda i: (0, 0))],
        out_specs=[pl.BlockSpec((tm * TOKEN_TILE, 128), lambda i: (i, 0)),
                   pl.BlockSpec((2, tm), row2),
                   pl.BlockSpec((2, tm), row2),
                   pl.BlockSpec((2, tm), row2),
                   pl.BlockSpec((N_EXPERTS, 128), lambda i: (0, 0))],
        out_shape=[jax.ShapeDtypeStruct((t * TOKEN_TILE, 128), F32),
                   jax.ShapeDtypeStruct((2, t), I32),
                   jax.ShapeDtypeStruct((2, t), F32),
                   jax.ShapeDtypeStruct((2, t), I32),
                   jax.ShapeDtypeStruct((N_EXPERTS, 128), F32)],
        compiler_params=_params("arbitrary"),
        name="router",
    )(h, g.reshape(1, d), wh, wl, b_all)


def _dispatch_kernel(last_ref, pos_ref, x_hbm, o_ref, xbuf, zero_ref, load_sem, row_sem):
    i = pl.program_id(0)
    n = pl.num_programs(0)
    tm = xbuf.shape[1] // TOKEN_TILE
    in_rows = tm * TOKEN_TILE
    tile_rows = EXPERT_TM * TOKEN_TILE
    cur = i % 2
    sem = row_sem.at[0]

    def load(j, slot):
        src = pl.multiple_of(j * in_rows, in_rows)
        return pltpu.make_async_copy(x_hbm.at[pl.ds(src, in_rows)], xbuf.at[slot], load_sem.at[slot])

    def drain(slot):
        for _ in range(2):
            pltpu.make_async_copy(xbuf.at[slot], o_ref.at[pl.ds(0, in_rows)], row_sem.at[slot]).wait()

    @pl.when(i == 0)
    def _():
        load(0, 0).start()
        zero_ref[...] = jnp.zeros_like(zero_ref)

        def tile_fill(e):
            row = pl.multiple_of(last_ref[e] * TOKEN_TILE, tile_rows)
            return pltpu.make_async_copy(zero_ref, o_ref.at[pl.ds(row, tile_rows)], sem)

        for e in range(N_EXPERTS):
            @pl.when(last_ref[e] >= 0)
            def _():
                tile_fill(e).start()
        for e in range(N_EXPERTS):
            @pl.when(last_ref[e] >= 0)
            def _():
                tile_fill(e).wait()

        def spare_fill(j):
            row = pl.multiple_of(j * tile_rows, tile_rows)
            return pltpu.make_async_copy(zero_ref, o_ref.at[pl.ds(row, tile_rows)], sem)

        n_tiles = o_ref.shape[0] // tile_rows
        lax.fori_loop(last_ref[N_EXPERTS], n_tiles, lambda j, c: (spare_fill(j).start(), c)[1], 0)
        lax.fori_loop(last_ref[N_EXPERTS], n_tiles, lambda j, c: (spare_fill(j).wait(), c)[1], 0)

    load(i, cur).wait()

    @pl.when(i > 0)
    def _():
        drain(1 - cur)

    @pl.when(i + 1 < n)
    def _():
        load(i + 1, 1 - cur).start()

    def row_copy(r, k):
        dst = pl.multiple_of(pos_ref[0, 0, k * tm + r] * TOKEN_TILE, TOKEN_TILE)
        return pltpu.make_async_copy(xbuf.at[cur, pl.ds(r * TOKEN_TILE, TOKEN_TILE)],
                                     o_ref.at[pl.ds(dst, TOKEN_TILE)], row_sem.at[cur])

    def issue(blk, carry):
        for u in range(ROW_DMA_UNROLL):
            r = blk * ROW_DMA_UNROLL + u
            row_copy(r, 0).start(priority=u % 2)
            row_copy(r, 1).start(priority=(u + 1) % 2)
        return carry

    lax.fori_loop(0, tm // ROW_DMA_UNROLL, issue, 0)

    @pl.when(i == n - 1)
    def _():
        drain(cur)


def dispatch(xn, pos3, last_tile_row, n_rows, tm=GATHER_TM):
    t = xn.shape[0] // TOKEN_TILE
    return pl.pallas_call(
        _dispatch_kernel,
        grid_spec=pltpu.PrefetchScalarGridSpec(
            num_scalar_prefetch=1,
            grid=(t // tm,),
            in_specs=[pl.BlockSpec((1, 1, 2 * tm), lambda i, last: (i, 0, 0), memory_space=pltpu.SMEM),
                      pl.BlockSpec(memory_space=pl.ANY)],
            out_specs=pl.BlockSpec(memory_space=pl.ANY),
            scratch_shapes=[pltpu.VMEM((2, tm * TOKEN_TILE, 128), xn.dtype),
                            pltpu.VMEM((EXPERT_TM * TOKEN_TILE, 128), xn.dtype),
                            pltpu.SemaphoreType.DMA((2,)),
                            pltpu.SemaphoreType.DMA((2,))]),
        out_shape=jax.ShapeDtypeStruct((n_rows * TOKEN_TILE, 128), xn.dtype),
        compiler_params=_params("arbitrary"),
        name="moe_dispatch",
    )(last_tile_row, pos3, xn)


def _expert_kernel(te_ref, nv_ref, x_hbm, wgu_ref, wd_ref, y_ref, xbuf, wgu_b, wd_b, xsem):
    i = pl.program_id(0)
    n_valid = nv_ref[0]
    tm = xbuf.shape[1] // TOKEN_TILE
    in_rows = tm * TOKEN_TILE
    valid = i < n_valid
    changed = te_ref[i] != te_ref[jnp.maximum(i - 1, 0)]
    first = jnp.logical_or(i == 0, changed)

    def xload(j):
        slot = j % EXPERT_X_BUFS
        src = pl.multiple_of(j * in_rows, in_rows)
        return pltpu.make_async_copy(x_hbm.at[pl.ds(src, in_rows)], xbuf.at[slot], xsem.at[slot])

    @pl.when(i == 0)
    def _():
        for j in range(EXPERT_X_BUFS - 1):
            @pl.when(j < n_valid)
            def _():
                xload(j).start()

    @pl.when(i + EXPERT_X_BUFS - 1 < n_valid)
    def _():
        xload(i + EXPERT_X_BUFS - 1).start()

    @pl.when(jnp.logical_and(valid, first))
    def _():
        wgu_b[...] = wgu_ref[0, 0].astype(BF16)
        wd_b[...] = wd_ref[0, 0].astype(BF16)

    @pl.when(valid)
    def _():
        xload(i).wait()
        x = _from_tiles(xbuf.at[i % EXPERT_X_BUFS], tm).astype(BF16)
        au = _nn(x, wgu_b[...])
        a = au[:, :D_EXPERT]
        u = au[:, D_EXPERT:]
        mid = (jax.nn.silu(a) * u).astype(BF16)
        _to_tiles(y_ref, _nn(mid, wd_b[...]))

    @pl.when(jnp.logical_not(valid))
    def _():
        y_ref[...] = jnp.zeros_like(y_ref)


def experts(xg, w_gu, w_down, layer, tile_expert, n_valid, tm=EXPERT_TM):
    d = D_MODEL
    p = xg.shape[0] // TOKEN_TILE
    n_tiles = p // tm
    wsel = lambda i, te, nv: (layer, te[i], 0, 0)
    return pl.pallas_call(
        _expert_kernel,
        grid_spec=pltpu.PrefetchScalarGridSpec(
            num_scalar_prefetch=2,
            grid=(n_tiles,),
            in_specs=[pl.BlockSpec(memory_space=pl.ANY),
                      pl.BlockSpec((1, 1, d, 2 * D_EXPERT), wsel),
                      pl.BlockSpec((1, 1, D_EXPERT, d), wsel)],
            out_specs=pl.BlockSpec((tm * TOKEN_TILE, 128), lambda i, te, nv: (i, 0)),
            scratch_shapes=[pltpu.VMEM((EXPERT_X_BUFS, tm * TOKEN_TILE, 128), F32),
                            pltpu.VMEM((d, 2 * D_EXPERT), BF16),
                            pltpu.VMEM((D_EXPERT, d), BF16),
                            pltpu.SemaphoreType.DMA((EXPERT_X_BUFS,))]),
        out_shape=jax.ShapeDtypeStruct((p * TOKEN_TILE, 128), F32),
        compiler_params=_params("arbitrary"),
        name="moe_experts",
    )(tile_expert, n_valid, xg, w_gu, w_down)


def _qkv_tail(x, gq_ref, gkv_ref, wqt_ref, wk_ref, wvt_ref, qt_ref, k_ref, vt_ref):
    y = x * lax.rsqrt(jnp.mean(x * x, axis=-1, keepdims=True) + NORM_EPS)
    xq = (y * gq_ref[...]).astype(BF16)
    xkv = (y * gkv_ref[...]).astype(BF16)
    qt_ref[0] = (_nt(wqt_ref[...], xq) * (B_HEAD_DIM ** -0.5 * LOG2E)).astype(qt_ref.dtype)
    k_ref[...] = _nn(xkv, wk_ref[...]).astype(k_ref.dtype)
    vt_ref[0] = _nt(wvt_ref[...], xkv).astype(vt_ref.dtype)


def _combine_kernel(pos_ref, pos_next_ref, h_ref, w_ref, y_ref, *rest, tail):
    if tail == "norm":
        g_ref, o_ref, buf, sem = rest
    else:
        gq_ref, gkv_ref, wqt_ref, wk_ref, wvt_ref, o_ref, qt_ref, k_ref, vt_ref, buf, sem = rest
    i = pl.program_id(0)
    tm = h_ref.shape[0]
    cur = i % 2

    def gather_tile(p_ref, half):
        def row_copy(r, k):
            src = pl.multiple_of(p_ref[0, 0, k * tm + r] * TOKEN_TILE, TOKEN_TILE)
            return pltpu.make_async_copy(y_ref.at[pl.ds(src, TOKEN_TILE)],
                                         buf.at[2 * half + k, pl.ds(r * TOKEN_TILE, TOKEN_TILE)],
                                         sem.at[half])

        def issue(blk, carry):
            for u in range(ROW_DMA_UNROLL):
                r = blk * ROW_DMA_UNROLL + u
                row_copy(r, 0).start(priority=u % 2)
                row_copy(r, 1).start(priority=(u + 1) % 2)
            return carry

        lax.fori_loop(0, tm // ROW_DMA_UNROLL, issue, 0)

    @pl.when(i == 0)
    def _():
        gather_tile(pos_ref, 0)

    @pl.when(i + 1 < pl.num_programs(0))
    def _():
        gather_tile(pos_next_ref, 1 - cur)

    for k in range(2):
        pltpu.make_async_copy(y_ref.at[pl.ds(0, tm * TOKEN_TILE)], buf.at[2 * cur + k], sem.at[cur]).wait()
    w = w_ref[...]
    out = (h_ref[...] + w[:, 0:1] * _from_tiles(buf.at[2 * cur], tm)
           + w[:, 1:2] * _from_tiles(buf.at[2 * cur + 1], tm))
    if tail == "norm":
        o_ref[...] = out * lax.rsqrt(jnp.mean(out * out, axis=-1, keepdims=True) + NORM_EPS) * g_ref[...]
    else:
        o_ref[...] = out
        _qkv_tail(out, gq_ref, gkv_ref, wqt_ref, wk_ref, wvt_ref, qt_ref, k_ref, vt_ref)


def combine(h, wgt_t, pos3, y, tail, tail_args, batch, seq, tm=GATHER_TM):
    t, d = h.shape
    nt = t // tm
    nblk = seq // tm
    row = pl.BlockSpec((tm, d), lambda i: (i, 0))
    vec = pl.BlockSpec((1, d), lambda i: (0, 0))
    full = pl.BlockSpec((d, d), lambda i: (0, 0))
    tr = pl.BlockSpec((1, d, tm), lambda i: (i // nblk, 0, i % nblk))
    if tail == "norm":
        (g,) = tail_args
        extra_in, extra_specs = [g.reshape(1, d)], [vec]
        out_specs, out_shape = row, jax.ShapeDtypeStruct((t, d), F32)
    else:
        g_q, g_kv, wqt, wk, wvt = tail_args
        extra_in = [g_q.reshape(1, d), g_kv.reshape(1, d), wqt, wk, wvt]
        extra_specs = [vec, vec, full, full, full]
        out_specs = [row, tr, row, tr]
        out_shape = [jax.ShapeDtypeStruct((t, d), F32),
                     jax.ShapeDtypeStruct((batch, d, seq), BF16),
                     jax.ShapeDtypeStruct((t, d), BF16),
                     jax.ShapeDtypeStruct((batch, d, seq), BF16)]
    return pl.pallas_call(
        functools.partial(_combine_kernel, tail=tail),
        grid=(nt,),
        in_specs=[pl.BlockSpec((1, 1, 2 * tm), lambda i: (i, 0, 0), memory_space=pltpu.SMEM),
                  pl.BlockSpec((1, 1, 2 * tm), lambda i: (jnp.minimum(i + 1, nt - 1), 0, 0),
                               memory_space=pltpu.SMEM),
                  row,
                  pl.BlockSpec((tm, 2), lambda i: (i, 0)),
                  pl.BlockSpec(memory_space=pl.ANY)] + extra_specs,
        out_specs=out_specs,
        out_shape=out_shape,
        scratch_shapes=[pltpu.VMEM((4, tm * TOKEN_TILE, 128), F32), pltpu.SemaphoreType.DMA((2,))],
        compiler_params=_params("arbitrary"),
        name="moe_combine_" + tail,
    )(pos3, pos3, h, wgt_t, y, *extra_in)


def hier_moe_layer(h, norm_g, wg, bg, we, be, w_gu, w_down, layer, tail, tail_args, batch, seq):
    t, d = h.shape
    xn, eid, wgt, rank, cnt = router(h, norm_g, wg, bg, we, be)
    counts = cnt[:, 0].astype(I32)
    padded = ((counts + EXPERT_TM - 1) // EXPERT_TM) * EXPERT_TM
    ends = jnp.cumsum(padded)
    offs = ends - padded
    n_rows = 2 * t + N_EXPERTS * EXPERT_TM
    n_tiles = n_rows // EXPERT_TM
    n_valid = (ends[-1] // EXPERT_TM).astype(I32).reshape(1)
    tile_start = jnp.arange(n_tiles, dtype=I32) * EXPERT_TM
    tile_start = jnp.minimum(tile_start, ends[-1] - 1)
    tile_expert = jnp.sum((ends[None, :] <= tile_start[:, None]).astype(I32), axis=1)
    onehot = eid[None] == jnp.arange(N_EXPERTS, dtype=I32)[:, None, None]
    pos = jnp.sum(jnp.where(onehot, offs[:, None, None], 0), axis=0) + rank
    nt = t // GATHER_TM
    pos3 = pos.reshape(2, nt, GATHER_TM).transpose(1, 0, 2).reshape(nt, 1, 2 * GATHER_TM)

    last_tile_row = jnp.concatenate([jnp.where(padded > 0, ends - EXPERT_TM, -1).astype(I32), n_valid])
    xg = dispatch(xn, pos3, last_tile_row, n_rows)
    y = experts(xg, w_gu, w_down, layer, tile_expert, n_valid)
    return combine(h, wgt.T, pos3, y, tail, tail_args, batch, seq)


def _attn_kernel(q_ref, k_ref, vt_ref, lam_ref, g_ref, o_ref, qq_ref, m_ref, acc_ref,
                 a_ref, c_ref, *sp_refs, lambda_init, heads):
    s_refs, p_refs = sp_refs[:heads], sp_refs[heads:]
    qi = pl.program_id(2)
    tq, tk, hd = ATT_TQ, ATT_TK, 2 * B_HEAD_DIM
    feat = lax.broadcasted_iota(I32, (hd, tq), 0)
    for g in range(heads):
        qt = q_ref[0, g * hd:(g + 1) * hd, :]
        zero = jnp.zeros_like(qt)
        qq_ref[g, :, 0:tq] = jnp.where(feat < B_HEAD_DIM, qt, zero)
        qq_ref[g, :, tq:2 * tq] = jnp.where(feat >= B_HEAD_DIM, qt, zero)
    m_ref[...] = jnp.full_like(m_ref, -jnp.inf)
    acc_ref[...] = jnp.zeros_like(acc_ref)
    ones = jnp.ones((ATT_SUM_ROWS, tk), BF16)
    dyn0 = jnp.minimum(qi, 0)

    def step(j, masked, nk=tk):
        off = pl.multiple_of(j * tk, tk)
        if masked:
            krow = lax.broadcasted_iota(I32, (nk, 2 * tq), 0)
            qcol = lax.broadcasted_iota(I32, (nk, 2 * tq), 1)
            visible = off + krow <= qi * tq + jnp.where(qcol >= tq, qcol - tq, qcol)
        for g in range(heads):
            kb = k_ref[pl.ds(off, nk), g * hd:(g + 1) * hd]
            s = _nn(kb, qq_ref[g])
            if masked:
                s = jnp.where(visible, s, -jnp.inf)
            s_refs[g][0, 0:nk, :] = s
            c_ref[g] = jnp.max(s, axis=0, keepdims=True)
        for g in range(heads):
            m_old = m_ref[g]
            m_new = jnp.maximum(m_old, c_ref[g])
            a_ref[g] = jnp.exp2(m_old - m_new)
            m_ref[g] = m_new
            for c in range(0, nk, ATT_CHUNK):
                p_refs[g][0, c:c + ATT_CHUNK, :] = jnp.exp2(
                    s_refs[g][dyn0, c:c + ATT_CHUNK, :] - m_new).astype(BF16)
        for g in range(heads):
            vtb = vt_ref[0, g * hd:(g + 1) * hd, pl.ds(off, nk)]
            lhs = jnp.concatenate([vtb, ones[:, 0:nk]], axis=0)
            acc_ref[g] = a_ref[g] * acc_ref[g] + _nn(lhs, p_refs[g][dyn0, 0:nk, :])

    n_full = (qi * tq) // tk

    def full_step(j, carry):
        step(j, False)
        return carry

    lax.fori_loop(0, n_full, full_step, 0)

    first_part = (qi * tq) % tk + tq <= tk // 2

    @pl.when(first_part)
    def _():
        step(n_full, True, tk // 2)

    @pl.when(jnp.logical_not(first_part))
    def _():
        step(n_full, True)

    lam = lam_ref[...]
    lam_full = (jnp.exp(jnp.sum(lam[0:1] * lam[1:2], axis=-1, keepdims=True))
                - jnp.exp(jnp.sum(lam[2:3] * lam[3:4], axis=-1, keepdims=True)) + lambda_init)
    for g in range(heads):
        acc = acc_ref[g]
        on = acc[:hd] / acc[hd:hd + 1]
        ot = on[:, :tq] - lam_full * on[:, tq:]
        o = ot.T
        o = o * lax.rsqrt(jnp.mean(o * o, axis=-1, keepdims=True) + SUBLN_EPS) * g_ref[...]
        o_ref[:, g * hd:(g + 1) * hd] = (o * (1.0 - lambda_init)).astype(o_ref.dtype)


def diff_attn(qt, k, vt, lam, subln_g, batch, seq, lambda_init, heads=ATT_HEADS):
    t, d = k.shape
    nq = seq // ATT_TQ
    hd = 2 * B_HEAD_DIM
    w = heads * hd
    return pl.pallas_call(
        functools.partial(_attn_kernel, lambda_init=lambda_init, heads=heads),
        grid=(batch, B_HEADS // heads, nq),
        in_specs=[pl.BlockSpec((1, w, ATT_TQ), lambda b, h, i: (b, h, i)),
                  pl.BlockSpec((seq, w), lambda b, h, i: (b, h)),
                  pl.BlockSpec((1, w, seq), lambda b, h, i: (b, h, 0)),
                  pl.BlockSpec(lam.shape, lambda b, h, i: (0, 0)),
                  pl.BlockSpec((1, hd), lambda b, h, i: (0, 0))],
        out_specs=pl.BlockSpec((ATT_TQ, w), lambda b, h, i: (b * nq + i, h)),
        out_shape=jax.ShapeDtypeStruct((t, d), BF16),
        scratch_shapes=[pltpu.VMEM((heads, hd, 2 * ATT_TQ), BF16),
                        pltpu.VMEM((heads, 1, 2 * ATT_TQ), F32),
                        pltpu.VMEM((heads, hd + ATT_SUM_ROWS, 2 * ATT_TQ), F32),
                        pltpu.VMEM((heads, 1, 2 * ATT_TQ), F32),
                        pltpu.VMEM((heads, 1, 2 * ATT_TQ), F32)]
        + [pltpu.VMEM((1, ATT_TK, 2 * ATT_TQ), F32) for _ in range(heads)]
        + [pltpu.VMEM((1, ATT_TK, 2 * ATT_TQ), BF16) for _ in range(heads)],
        compiler_params=_params("parallel", "parallel", "arbitrary"),
        name="diff_attn",
    )(qt, k, vt, lam, subln_g.reshape(1, hd))


def kernel(x, a_norm_g, a_w_in, a_lb, a_onorm_g, a_w_out, kv_norm_g, w_kv, b_norm_g, b_w_q, b_lam,
           b_subln_g, b_w_out, ffn_norm_g, router_g_w, router_g_b, router_e_w, router_e_b,
           expert_w_gu, expert_w_down, final_norm_g):
    batch, seq, d = x.shape
    assert d == D_MODEL and a_norm_g.shape[0] == 1 and b_norm_g.shape[0] == 1
    assert seq % max(2 * HG_BLOCK, ATT_TK, ROUTER_TM) == 0
    t = batch * seq
    h = x.reshape(t, d)

    h = hgrn2_layer(h, a_norm_g[0], a_w_in[0].astype(BF16), a_lb, a_onorm_g[0],
                    a_w_out[0].astype(BF16), seq)
    qkv_args = (b_norm_g[0], kv_norm_g, b_w_q[0].T.astype(BF16),
                w_kv[:, :d].astype(BF16), w_kv[:, d:].T.astype(BF16))
    h, qt, k, vt = hier_moe_layer(h, ffn_norm_g[0], router_g_w[0], router_g_b[0], router_e_w[0],
                                  router_e_b[0], expert_w_gu, expert_w_down, 0, "qkv", qkv_args,
                                  batch, seq)

    lambda_init = 0.8 - 0.6 * math.exp(-0.3 * 1)
    o = diff_attn(qt, k, vt, b_lam[0], b_subln_g[0], batch, seq, lambda_init)
    h = proj_res(o, b_w_out[0].astype(BF16), h)
    h = hier_moe_layer(h, ffn_norm_g[1], router_g_w[1], router_g_b[1], router_e_w[1], router_e_b[1],
                       expert_w_gu, expert_w_down, 1, "norm", (final_norm_g,), batch, seq)
    return h.reshape(batch, seq, d)
```

```python
import functools
import math

import jax
import jax.numpy as jnp
from jax import lax
from jax.experimental import pallas as pl
from jax.experimental.pallas import tpu as pltpu

F32 = jnp.float32
BF16 = jnp.bfloat16
I32 = jnp.int32

D_MODEL = 1024
A_HEADS = 8
A_HEAD_DIM = 128
B_HEADS = 8
B_HEAD_DIM = 64
N_GROUPS = 4
EXPERTS_PER_GROUP = 8
N_EXPERTS = N_GROUPS * EXPERTS_PER_GROUP
D_EXPERT = 512
NORM_EPS = 1e-6
SUBLN_EPS = 1e-5
LOG2E = 1.4426950408889634

GLA_BLOCK = 128
GLA_HALF = GLA_BLOCK // 2
HG_BLOCK = 256
ATT_TQ = 256
ATT_TK = 512
ATT_HEADS = 8
ATT_CHUNK = 128
ATT_SUM_ROWS = 16
ROUTER_TM = 512
ROUTER_ROWS = 40
EXPERT_TM = 512
EXPERT_BURSTS = 8
EXPERT_CHUNK = 512
GATHER_TM = 256
ROW_DMA_UNROLL = 8
TOKEN_TILE = 8
COMBINE_BURSTS = 16
VMEM_LIMIT = 56 * 1024 * 1024


def _nt(a, b):
    return lax.dot_general(a, b, (((1,), (1,)), ((), ())), preferred_element_type=F32)


def _nn(a, b):
    return jnp.dot(a, b, preferred_element_type=F32)


def _to_tiles(ref, x):
    tm = x.shape[0]
    for s in range(TOKEN_TILE):
        ref[pl.ds(s, tm, stride=TOKEN_TILE), :] = x[:, s * 128:(s + 1) * 128]


def _from_tiles(ref, tm):
    return jnp.concatenate([ref[pl.ds(s, tm, stride=TOKEN_TILE), :] for s in range(TOKEN_TILE)], axis=1)


def _params(*sem, flags=None):
    return pltpu.CompilerParams(dimension_semantics=sem, vmem_limit_bytes=VMEM_LIMIT, flags=flags)


def _hgrn2_project(x, g_ref, w_ref, proj_ref):
    d = x.shape[1]
    y = x * lax.rsqrt(jnp.mean(x * x, axis=-1, keepdims=True) + NORM_EPS)
    xn = (y * g_ref[...]).astype(BF16)
    for c in range(w_ref.shape[1] // d):
        proj_ref[:, c * d:(c + 1) * d] = _nn(xn, w_ref[:, c * d:(c + 1) * d])


def _gla_block(proj_ref, row0, lb, og, st_ref, out_ref, out_row0):
    d = D_MODEL
    rows = pl.ds(pl.multiple_of(row0, GLA_BLOCK), GLA_BLOCK)
    fz = proj_ref[rows, d:2 * d]
    logf = jnp.log(lb + (1.0 - lb) * jax.nn.sigmoid(fz))
    kk = (1.0 - lb) * jax.nn.sigmoid(-fz)
    qq = jax.nn.silu(proj_ref[rows, 0:d])

    r = lax.broadcasted_iota(I32, (GLA_BLOCK, GLA_BLOCK), 0)
    s = lax.broadcasted_iota(I32, (GLA_BLOCK, GLA_BLOCK), 1)
    tril = (s <= r).astype(BF16)
    hi = logf.astype(BF16)
    lo = (logf - hi.astype(F32)).astype(BF16)
    b = _nn(tril, hi) + _nn(tril, lo)

    h0, h1 = slice(0, GLA_HALF), slice(GLA_HALF, GLA_BLOCK)
    b_a_mid = b[GLA_HALF // 2 - 1:GLA_HALF // 2]
    b_a_end = b[GLA_HALF - 1:GLA_HALF]
    b_b_mid = b[GLA_HALF + GLA_HALF // 2 - 1:GLA_HALF + GLA_HALF // 2]
    b_end = b[GLA_BLOCK - 1:GLA_BLOCK]

    qa_n = (qq[h0] * jnp.exp(b[h0] - b_a_mid)).astype(BF16)
    ka_n = (kk[h0] * jnp.exp(b_a_mid - b[h0])).astype(BF16)
    qb_n = (qq[h1] * jnp.exp(b[h1] - b_b_mid)).astype(BF16)
    kb_n = (kk[h1] * jnp.exp(b_b_mid - b[h1])).astype(BF16)
    qb_x = (qq[h1] * jnp.exp(b[h1] - b_a_end)).astype(BF16)
    ka_x = (kk[h0] * jnp.exp(b_a_end - b[h0])).astype(BF16)
    q_dec = (qq * jnp.exp(b)).astype(BF16)
    k_end = (kk * jnp.exp(b_end - b)).astype(BF16)
    d_end = jnp.exp(b_end)

    rr = lax.broadcasted_iota(I32, (GLA_HALF, GLA_HALF), 0)
    ss = lax.broadcasted_iota(I32, (GLA_HALF, GLA_HALF), 1)
    causal = ss <= rr

    for h in range(A_HEADS):
        hs = slice(h * A_HEAD_DIM, (h + 1) * A_HEAD_DIM)
        v = proj_ref[rows, 2 * d + h * A_HEAD_DIM:2 * d + (h + 1) * A_HEAD_DIM]
        vb = v.astype(BF16)
        st = st_ref[h]
        st_b = st.astype(BF16)
        p_aa = jnp.where(causal, _nt(qa_n[:, hs], ka_n[:, hs]), 0.0).astype(BF16)
        p_bb = jnp.where(causal, _nt(qb_n[:, hs], kb_n[:, hs]), 0.0).astype(BF16)
        p_ba = _nt(qb_x[:, hs], ka_x[:, hs]).astype(BF16)
        inter = _nt(q_dec[:, hs], st_b)
        o_a = _nn(p_aa, vb[h0]) + inter[h0]
        o_b = _nn(p_bb, vb[h1]) + _nn(p_ba, vb[h0]) + inter[h1]
        st_ref[h] = d_end[:, hs] * st + _nn(v.T.astype(BF16), k_end[:, hs])
        gate = jax.nn.silu(proj_ref[rows, 3 * d + h * A_HEAD_DIM:3 * d + (h + 1) * A_HEAD_DIM])
        for half, o in ((0, o_a), (1, o_b)):
            on = o * lax.rsqrt(jnp.mean(o * o, axis=-1, keepdims=True) + NORM_EPS)
            r0 = out_row0 + half * GLA_HALF
            out_ref[r0:r0 + GLA_HALF, hs] = (on * og * gate[half * GLA_HALF:(half + 1) * GLA_HALF]
                                             ).astype(out_ref.dtype)


def _hgrn2_kernel(x_ref, xnext_ref, g_ref, win_ref, alb_ref, og_ref, wout_ref, o_ref,
                  proj_a, proj_b, gated_a, gated_b, st_ref, *, blocks_per_seq):
    step = pl.program_id(0)
    dyn0 = jnp.minimum(step, 0)
    alb = alb_ref[...]
    e = jnp.exp(alb - jnp.max(alb, axis=0, keepdims=True))
    lb = e[0:1] / jnp.sum(e, axis=0, keepdims=True)
    og = og_ref[...]

    @pl.when(step == 0)
    def _():
        _hgrn2_project(x_ref[0:HG_BLOCK, :], g_ref, win_ref, proj_a)

    @pl.when((2 * step) % blocks_per_seq == 0)
    def _():
        st_ref[...] = jnp.zeros_like(st_ref)

    _hgrn2_project(x_ref[HG_BLOCK:2 * HG_BLOCK, :], g_ref, win_ref, proj_b)
    for sub in range(HG_BLOCK // GLA_BLOCK):
        _gla_block(proj_a, dyn0 + sub * GLA_BLOCK, lb, og, st_ref, gated_a, sub * GLA_BLOCK)
    o_ref[0:HG_BLOCK, :] = x_ref[0:HG_BLOCK, :] + _nn(gated_a[...], wout_ref[...])

    _hgrn2_project(xnext_ref[...], g_ref, win_ref, proj_a)
    for sub in range(HG_BLOCK // GLA_BLOCK):
        _gla_block(proj_b, dyn0 + sub * GLA_BLOCK, lb, og, st_ref, gated_b, sub * GLA_BLOCK)
    o_ref[HG_BLOCK:2 * HG_BLOCK, :] = x_ref[HG_BLOCK:2 * HG_BLOCK, :] + _nn(gated_b[...], wout_ref[...])


def hgrn2_layer(x, norm_g, w_in_bf16, a_lb, onorm_g, w_out_bf16, seq):
    t, d = x.shape
    n = w_in_bf16.shape[1]
    nblocks = t // HG_BLOCK
    return pl.pallas_call(
        functools.partial(_hgrn2_kernel, blocks_per_seq=seq // HG_BLOCK),
        grid=(nblocks // 2,),
        in_specs=[pl.BlockSpec((2 * HG_BLOCK, d), lambda i: (i, 0)),
                  pl.BlockSpec((HG_BLOCK, d), lambda i: (jnp.minimum(2 * i + 2, nblocks - 1), 0)),
                  pl.BlockSpec((1, d), lambda i: (0, 0)),
                  pl.BlockSpec((d, n), lambda i: (0, 0)),
                  pl.BlockSpec(a_lb.shape, lambda i: (0, 0)),
                  pl.BlockSpec((1, A_HEAD_DIM), lambda i: (0, 0)),
                  pl.BlockSpec((d, d), lambda i: (0, 0))],
        out_specs=pl.BlockSpec((2 * HG_BLOCK, d), lambda i: (i, 0)),
        out_shape=jax.ShapeDtypeStruct((t, d), F32),
        scratch_shapes=[pltpu.VMEM((HG_BLOCK, n), F32),
                        pltpu.VMEM((HG_BLOCK, n), F32),
                        pltpu.VMEM((HG_BLOCK, d), BF16),
                        pltpu.VMEM((HG_BLOCK, d), BF16),
                        pltpu.VMEM((A_HEADS, A_HEAD_DIM, A_HEAD_DIM), F32)],
        compiler_params=_params("arbitrary"),
        name="hgrn2_layer",
    )(x, x, norm_g.reshape(1, d), w_in_bf16, a_lb, onorm_g.reshape(1, A_HEAD_DIM), w_out_bf16)


def _proj_res_kernel(a_ref, w_ref, r_ref, o_ref):
    o_ref[...] = r_ref[...] + _nn(a_ref[...], w_ref[...])


def proj_res(a_bf16, w_bf16, res, tm=512):
    t, d = res.shape
    return pl.pallas_call(
        _proj_res_kernel,
        grid=(t // tm,),
        in_specs=[pl.BlockSpec((tm, d), lambda i: (i, 0)),
                  pl.BlockSpec((d, d), lambda i: (0, 0)),
                  pl.BlockSpec((tm, d), lambda i: (i, 0))],
        out_specs=pl.BlockSpec((tm, d), lambda i: (i, 0)),
        out_shape=jax.ShapeDtypeStruct((t, d), F32),
        compiler_params=_params("parallel"),
        name="proj_res",
    )(a_bf16, w_bf16, res)


def _router_kernel(h_ref, g_ref, wh_ref, wl_ref, b_ref, xn_ref, eid_ref, wgt_ref, rank_ref, cnt_ref):
    i = pl.program_id(0)
    tm = h_ref.shape[0]

    @pl.when(i == 0)
    def _():
        cnt_ref[...] = jnp.zeros_like(cnt_ref)

    x = h_ref[...]
    xn = x * lax.rsqrt(jnp.mean(x * x, axis=-1, keepdims=True) + NORM_EPS) * g_ref[...]
    _to_tiles(xn_ref, xn)
    xh = xn.astype(BF16)
    xl = (xn - xh.astype(F32)).astype(BF16)
    wh = wh_ref[...]
    lg = _nt(wh, xh) + _nt(wl_ref[...], xh) + _nt(wh, xl) + b_ref[...]

    gl = lg[0:N_GROUPS]
    r4 = lax.broadcasted_iota(I32, gl.shape, 0)
    gmax = jnp.max(gl, axis=0, keepdims=True)
    grp = jnp.min(jnp.where(gl == gmax, r4, N_GROUPS), axis=0, keepdims=True)
    p_grp = 1.0 / jnp.sum(jnp.exp(gl - gmax), axis=0, keepdims=True)

    fine = lg[8:8 + EXPERTS_PER_GROUP]
    for gi in range(1, N_GROUPS):
        fine = jnp.where(grp == gi, lg[8 + gi * EXPERTS_PER_GROUP:8 + (gi + 1) * EXPERTS_PER_GROUP], fine)
    r8 = lax.broadcasted_iota(I32, fine.shape, 0)
    m1 = jnp.max(fine, axis=0, keepdims=True)
    i1 = jnp.min(jnp.where(fine == m1, r8, EXPERTS_PER_GROUP), axis=0, keepdims=True)
    rest = jnp.where(r8 == i1, -jnp.inf, fine)
    m2 = jnp.max(rest, axis=0, keepdims=True)
    i2 = jnp.min(jnp.where(rest == m2, r8, EXPERTS_PER_GROUP), axis=0, keepdims=True)
    e21 = jnp.exp(m2 - m1)
    t1 = 1.0 / (1.0 + e21)
    wgt_ref[0:1, :] = p_grp * t1
    wgt_ref[1:2, :] = p_grp * (e21 * t1)
    e1 = grp * EXPERTS_PER_GROUP + i1
    e2 = grp * EXPERTS_PER_GROUP + i2
    eid_ref[0:1, :] = e1
    eid_ref[1:2, :] = e2

    r32 = lax.broadcasted_iota(I32, (N_EXPERTS, tm), 0)
    is1 = r32 == e1
    is2 = r32 == e2
    member = jnp.logical_or(is1, is2)
    ta = lax.broadcasted_iota(I32, (tm, tm), 0)
    tb = lax.broadcasted_iota(I32, (tm, tm), 1)
    before = (ta < tb).astype(BF16)
    prior = _nn(member.astype(BF16), before) + cnt_ref[:, 0:1]
    rank_ref[0:1, :] = jnp.sum(jnp.where(is1, prior, 0.0), axis=0, keepdims=True).astype(I32)
    rank_ref[1:2, :] = jnp.sum(jnp.where(is2, prior, 0.0), axis=0, keepdims=True).astype(I32)
    cnt_ref[...] = cnt_ref[...] + jnp.sum(member.astype(F32), axis=1, keepdims=True)


def router(h, g, wg, bg, we, be, tm=ROUTER_TM):
    t, d = h.shape
    w_all = jnp.zeros((ROUTER_ROWS, d), F32).at[0:N_GROUPS].set(wg.T).at[8:8 + N_EXPERTS].set(we.T)
    b_all = jnp.zeros((ROUTER_ROWS, 1), F32).at[0:N_GROUPS, 0].set(bg).at[8:8 + N_EXPERTS, 0].set(be)
    wh = w_all.astype(BF16)
    wl = (w_all - wh.astype(F32)).astype(BF16)
    row2 = lambda i: (0, i)
    return pl.pallas_call(
        _router_kernel,
        grid=(t // tm,),
        in_specs=[pl.BlockSpec((tm, d), lambda i: (i, 0)),
                  pl.BlockSpec((1, d), lambda i: (0, 0)),
                  pl.BlockSpec((ROUTER_ROWS, d), lambda i: (0, 0)),
                  pl.BlockSpec((ROUTER_ROWS, d), lambda i: (0, 0)),
                  pl.BlockSpec((ROUTER_ROWS, 1), lambda i: (0, 0))],
        out_specs=[pl.BlockSpec((tm * TOKEN_TILE, 128), lambda i: (i, 0)),
                   pl.BlockSpec((2, tm), row2),
                   pl.BlockSpec((2, tm), row2),
                   pl.BlockSpec((2, tm), row2),
                   pl.BlockSpec((N_EXPERTS, 128), lambda i: (0, 0))],
        out_shape=[jax.ShapeDtypeStruct((t * TOKEN_TILE, 128), F32),
                   jax.ShapeDtypeStruct((2, t), I32),
                   jax.ShapeDtypeStruct((2, t), F32),
                   jax.ShapeDtypeStruct((2, t), I32),
                   jax.ShapeDtypeStruct((N_EXPERTS, 128), F32)],
        compiler_params=_params("arbitrary"),
        name="router",
    )(h, g.reshape(1, d), wh, wl, b_all)


def _slot_table_kernel(pos_ref, zeros_ref, tbl_ref, sem):
    i = pl.program_id(0)
    tm = pos_ref.shape[2] // 2

    @pl.when(i == 0)
    def _():
        fill = pltpu.make_async_copy(zeros_ref, tbl_ref, sem)
        fill.start()
        fill.wait()

    def body(blk, carry):
        for u in range(ROW_DMA_UNROLL):
            r = blk * ROW_DMA_UNROLL + u
            tok = i * tm + r
            tbl_ref[pos_ref[0, 0, r]] = tok
            tbl_ref[pos_ref[0, 0, tm + r]] = tok
        return carry

    lax.fori_loop(0, tm // ROW_DMA_UNROLL, body, 0)


def slot_table(pos3, n_rows):
    nt, _, two_tm = pos3.shape
    return pl.pallas_call(
        _slot_table_kernel,
        grid=(nt,),
        in_specs=[pl.BlockSpec((1, 1, two_tm), lambda i: (i, 0, 0), memory_space=pltpu.SMEM),
                  pl.BlockSpec(memory_space=pl.ANY)],
        out_specs=pl.BlockSpec(memory_space=pltpu.SMEM),
        out_shape=jax.ShapeDtypeStruct((n_rows,), I32),
        scratch_shapes=[pltpu.SemaphoreType.DMA(())],
        compiler_params=_params("arbitrary"),
        name="moe_slot_table",
    )(pos3, jnp.zeros((n_rows,), I32))


def _expert_kernel(te_ref, nv_ref, tok_ref, tok_next_ref, x_hbm, wgu_ref, wd_ref, y_ref,
                   xbuf, wgu_b, wd_b, xsem):
    i = pl.program_id(0)
    n_valid = nv_ref[0]
    tm = xbuf.shape[1] // TOKEN_TILE
    valid = i < n_valid
    cur = i % 2
    changed = te_ref[i] != te_ref[jnp.maximum(i - 1, 0)]
    first = jnp.logical_or(i == 0, changed)

    def gather_rows(t_ref, slot, r0, nrows):
        def row_copy(r):
            src = pl.multiple_of(t_ref[0, 0, r] * TOKEN_TILE, TOKEN_TILE)
            return pltpu.make_async_copy(x_hbm.at[pl.ds(src, TOKEN_TILE)],
                                         xbuf.at[slot, pl.ds(r * TOKEN_TILE, TOKEN_TILE)],
                                         xsem.at[slot])

        def issue(blk, carry):
            for u in range(ROW_DMA_UNROLL):
                row_copy(r0 + blk * ROW_DMA_UNROLL + u).start(priority=u % 2)
            return carry

        lax.fori_loop(0, nrows // ROW_DMA_UNROLL, issue, 0)

    @pl.when(i == 0)
    def _():
        gather_rows(tok_ref, 0, 0, tm)

    bursts = iter(range(0, tm, tm // EXPERT_BURSTS))

    def next_burst():
        r0 = next(bursts, None)
        if r0 is not None:
            @pl.when(i + 1 < n_valid)
            def _():
                gather_rows(tok_next_ref, 1 - cur, r0, tm // EXPERT_BURSTS)

    @pl.when(jnp.logical_and(valid, first))
    def _():
        wgu_b[...] = wgu_ref[0, 0].astype(BF16)
        wd_b[...] = wd_ref[0, 0].astype(BF16)

    @pl.when(valid)
    def _():
        pltpu.make_async_copy(x_hbm.at[pl.ds(0, tm * TOKEN_TILE)], xbuf.at[cur], xsem.at[cur]).wait()
        x = _from_tiles(xbuf.at[cur], tm).astype(BF16)
        c = EXPERT_CHUNK
        au = []
        for j in range(2 * D_EXPERT // c):
            au.append(_nn(x, wgu_b[:, j * c:(j + 1) * c]))
            next_burst()
            next_burst()
        half = D_EXPERT // c
        mid = jnp.concatenate([jax.nn.silu(au[j]) * au[half + j] for j in range(half)],
                              axis=1).astype(BF16)
        for j in range(D_MODEL // c):
            yj = _nn(mid, wd_b[:, j * c:(j + 1) * c])
            for s in range(c // 128):
                y_ref[pl.ds(j * (c // 128) + s, tm, stride=TOKEN_TILE), :] = yj[:, s * 128:(s + 1) * 128]
            next_burst()
            next_burst()
        for _ in range(EXPERT_BURSTS):
            next_burst()

    @pl.when(jnp.logical_not(valid))
    def _():
        y_ref[...] = jnp.zeros_like(y_ref)


def experts(xn, tok3, w_gu, w_down, layer, tile_expert, n_valid, tm=EXPERT_TM):
    d = D_MODEL
    n_tiles = tok3.shape[0]
    wsel = lambda i, te, nv: (layer, te[i], 0, 0)
    return pl.pallas_call(
        _expert_kernel,
        grid_spec=pltpu.PrefetchScalarGridSpec(
            num_scalar_prefetch=2,
            grid=(n_tiles,),
            in_specs=[pl.BlockSpec((1, 1, tm), lambda i, te, nv: (jnp.minimum(i, nv[0] - 1), 0, 0),
                                   memory_space=pltpu.SMEM),
                      pl.BlockSpec((1, 1, tm), lambda i, te, nv: (jnp.minimum(i + 1, nv[0] - 1), 0, 0),
                                   memory_space=pltpu.SMEM),
                      pl.BlockSpec(memory_space=pl.ANY),
                      pl.BlockSpec((1, 1, d, 2 * D_EXPERT), wsel),
                      pl.BlockSpec((1, 1, D_EXPERT, d), wsel)],
            out_specs=pl.BlockSpec((tm * TOKEN_TILE, 128), lambda i, te, nv: (i, 0)),
            scratch_shapes=[pltpu.VMEM((2, tm * TOKEN_TILE, 128), F32),
                            pltpu.VMEM((d, 2 * D_EXPERT), BF16),
                            pltpu.VMEM((D_EXPERT, d), BF16),
                            pltpu.SemaphoreType.DMA((2,))]),
        out_shape=jax.ShapeDtypeStruct((n_tiles * tm * TOKEN_TILE, 128), F32),
        compiler_params=_params("arbitrary"),
        name="moe_experts",
    )(tile_expert, n_valid, tok3, tok3, xn, w_gu, w_down)


def _qkv_tail(x, gq_ref, gkv_ref, wqt_ref, wk_ref, wvt_ref, qt_ref, k_ref, vt_ref, between):
    y = x * lax.rsqrt(jnp.mean(x * x, axis=-1, keepdims=True) + NORM_EPS)
    xq = (y * gq_ref[...]).astype(BF16)
    xkv = (y * gkv_ref[...]).astype(BF16)
    c = x.shape[1] // 4
    for j in range(4):
        qt_ref[0, j * c:(j + 1) * c, :] = (_nt(wqt_ref[j * c:(j + 1) * c, :], xq)
                                           * (B_HEAD_DIM ** -0.5 * LOG2E)).astype(qt_ref.dtype)
        between()
    for j in range(4):
        k_ref[:, j * c:(j + 1) * c] = _nn(xkv, wk_ref[:, j * c:(j + 1) * c]).astype(k_ref.dtype)
        between()
    for j in range(4):
        vt_ref[0, j * c:(j + 1) * c, :] = _nt(wvt_ref[j * c:(j + 1) * c, :], xkv).astype(vt_ref.dtype)
        between()


def _combine_kernel(pos_ref, pos_next_ref, h_ref, w_ref, y_ref, *rest, tail):
    if tail == "norm":
        g_ref, o_ref, buf, sem = rest
    else:
        gq_ref, gkv_ref, wqt_ref, wk_ref, wvt_ref, o_ref, qt_ref, k_ref, vt_ref, buf, sem = rest
    i = pl.program_id(0)
    tm = h_ref.shape[0]
    cur = i % 2

    def gather_rows(p_ref, half, r0, nrows):
        def row_copy(r, k):
            src = pl.multiple_of(p_ref[0, 0, k * tm + r] * TOKEN_TILE, TOKEN_TILE)
            return pltpu.make_async_copy(y_ref.at[pl.ds(src, TOKEN_TILE)],
                                         buf.at[2 * half + k, pl.ds(r * TOKEN_TILE, TOKEN_TILE)],
                                         sem.at[half])

        def issue(blk, carry):
            for u in range(ROW_DMA_UNROLL):
                r = r0 + blk * ROW_DMA_UNROLL + u
                row_copy(r, 0).start(priority=u % 2)
                row_copy(r, 1).start(priority=(u + 1) % 2)
            return carry

        lax.fori_loop(0, nrows // ROW_DMA_UNROLL, issue, 0)

    @pl.when(i == 0)
    def _():
        gather_rows(pos_ref, 0, 0, tm)

    n_bursts = COMBINE_BURSTS if tail == "qkv" else 1
    bursts = iter(range(0, tm, tm // n_bursts))

    def next_burst():
        r0 = next(bursts, None)
        if r0 is not None:
            @pl.when(i + 1 < pl.num_programs(0))
            def _():
                gather_rows(pos_next_ref, 1 - cur, r0, tm // n_bursts)

    next_burst()
    for k in range(2):
        pltpu.make_async_copy(y_ref.at[pl.ds(0, tm * TOKEN_TILE)], buf.at[2 * cur + k], sem.at[cur]).wait()
    w = w_ref[...]
    out = (h_ref[...] + w[:, 0:1] * _from_tiles(buf.at[2 * cur], tm)
           + w[:, 1:2] * _from_tiles(buf.at[2 * cur + 1], tm))
    if tail == "norm":
        o_ref[...] = out * lax.rsqrt(jnp.mean(out * out, axis=-1, keepdims=True) + NORM_EPS) * g_ref[...]
    else:
        next_burst()
        o_ref[...] = out
        next_burst()
        _qkv_tail(out, gq_ref, gkv_ref, wqt_ref, wk_ref, wvt_ref, qt_ref, k_ref, vt_ref, next_burst)
        for _ in range(n_bursts):
            next_burst()


def combine(h, wgt_t, pos3, y, tail, tail_args, batch, seq, tm=GATHER_TM):
    t, d = h.shape
    nt = t // tm
    nblk = seq // tm
    row = pl.BlockSpec((tm, d), lambda i: (i, 0))
    vec = pl.BlockSpec((1, d), lambda i: (0, 0))
    full = pl.BlockSpec((d, d), lambda i: (0, 0))
    tr = pl.BlockSpec((1, d, tm), lambda i: (i // nblk, 0, i % nblk))
    if tail == "norm":
        (g,) = tail_args
        extra_in, extra_specs = [g.reshape(1, d)], [vec]
        out_specs, out_shape = row, jax.ShapeDtypeStruct((t, d), F32)
    else:
        g_q, g_kv, wqt, wk, wvt = tail_args
        extra_in = [g_q.reshape(1, d), g_kv.reshape(1, d), wqt, wk, wvt]
        extra_specs = [vec, vec, full, full, full]
        out_specs = [row, tr, row, tr]
        out_shape = [jax.ShapeDtypeStruct((t, d), F32),
                     jax.ShapeDtypeStruct((batch, d, seq), BF16),
                     jax.ShapeDtypeStruct((t, d), BF16),
                     jax.ShapeDtypeStruct((batch, d, seq), BF16)]
    return pl.pallas_call(
        functools.partial(_combine_kernel, tail=tail),
        grid=(nt,),
        in_specs=[pl.BlockSpec((1, 1, 2 * tm), lambda i: (i, 0, 0), memory_space=pltpu.SMEM),
                  pl.BlockSpec((1, 1, 2 * tm), lambda i: (jnp.minimum(i + 1, nt - 1), 0, 0),
                               memory_space=pltpu.SMEM),
                  row,
                  pl.BlockSpec((tm, 2), lambda i: (i, 0)),
                  pl.BlockSpec(memory_space=pl.ANY)] + extra_specs,
        out_specs=out_specs,
        out_shape=out_shape,
        scratch_shapes=[pltpu.VMEM((4, tm * TOKEN_TILE, 128), F32), pltpu.SemaphoreType.DMA((2,))],
        compiler_params=_params("arbitrary"),
        name="moe_combine_" + tail,
    )(pos3, pos3, h, wgt_t, y, *extra_in)


def hier_moe_layer(h, norm_g, wg, bg, we, be, w_gu, w_down, layer, tail, tail_args, batch, seq):
    t, d = h.shape
    xn, eid, wgt, rank, cnt = router(h, norm_g, wg, bg, we, be)
    counts = cnt[:, 0].astype(I32)
    padded = ((counts + EXPERT_TM - 1) // EXPERT_TM) * EXPERT_TM
    ends = jnp.cumsum(padded)
    offs = ends - padded
    n_rows = 2 * t + N_EXPERTS * EXPERT_TM
    n_tiles = n_rows // EXPERT_TM
    n_valid = (ends[-1] // EXPERT_TM).astype(I32).reshape(1)
    tile_start = jnp.arange(n_tiles, dtype=I32) * EXPERT_TM
    tile_start = jnp.minimum(tile_start, ends[-1] - 1)
    tile_expert = jnp.sum((ends[None, :] <= tile_start[:, None]).astype(I32), axis=1)
    onehot = eid[None] == jnp.arange(N_EXPERTS, dtype=I32)[:, None, None]
    pos = jnp.sum(jnp.where(onehot, offs[:, None, None], 0), axis=0) + rank
    nt = t // GATHER_TM
    pos3 = pos.reshape(2, nt, GATHER_TM).transpose(1, 0, 2).reshape(nt, 1, 2 * GATHER_TM)

    tok3 = slot_table(pos3, n_rows).reshape(n_tiles, 1, EXPERT_TM)
    y = experts(xn, tok3, w_gu, w_down, layer, tile_expert, n_valid)
    return combine(h, wgt.T, pos3, y, tail, tail_args, batch, seq)


def _attn_kernel(q_ref, k_ref, vt_ref, lam_ref, g_ref, o_ref, qq_ref, m_ref, acc_ref,
                 a_ref, c_ref, *sp_refs, lambda_init, heads):
    s_refs, p_refs = sp_refs[:heads], sp_refs[heads:]
    qi = pl.program_id(2)
    tq, tk, hd = ATT_TQ, ATT_TK, 2 * B_HEAD_DIM
    feat = lax.broadcasted_iota(I32, (hd, tq), 0)
    for g in range(heads):
        qt = q_ref[0, g * hd:(g + 1) * hd, :]
        zero = jnp.zeros_like(qt)
        qq_ref[g, :, 0:tq] = jnp.where(feat < B_HEAD_DIM, qt, zero)
        qq_ref[g, :, tq:2 * tq] = jnp.where(feat >= B_HEAD_DIM, qt, zero)
    m_ref[...] = jnp.full_like(m_ref, -jnp.inf)
    acc_ref[...] = jnp.zeros_like(acc_ref)
    ones = jnp.ones((ATT_SUM_ROWS, tk), BF16)
    dyn0 = jnp.minimum(qi, 0)

    def step(j, masked, nk=tk):
        off = pl.multiple_of(j * tk, tk)
        if masked:
            krow = lax.broadcasted_iota(I32, (nk, 2 * tq), 0)
            qcol = lax.broadcasted_iota(I32, (nk, 2 * tq), 1)
            visible = off + krow <= qi * tq + jnp.where(qcol >= tq, qcol - tq, qcol)
        for g in range(heads):
            kb = k_ref[pl.ds(off, nk), g * hd:(g + 1) * hd]
            s = _nn(kb, qq_ref[g])
            if masked:
                s = jnp.where(visible, s, -jnp.inf)
            s_refs[g][0, 0:nk, :] = s
            c_ref[g] = jnp.max(s, axis=0, keepdims=True)
        for g in range(heads):
            m_old = m_ref[g]
            m_new = jnp.maximum(m_old, c_ref[g])
            a_ref[g] = jnp.exp2(m_old - m_new)
            m_ref[g] = m_new
            for c in range(0, nk, ATT_CHUNK):
                p_refs[g][0, c:c + ATT_CHUNK, :] = jnp.exp2(
                    s_refs[g][dyn0, c:c + ATT_CHUNK, :] - m_new).astype(BF16)
        for g in range(heads):
            vtb = vt_ref[0, g * hd:(g + 1) * hd, pl.ds(off, nk)]
            lhs = jnp.concatenate([vtb, ones[:, 0:nk]], axis=0)
            acc_ref[g] = a_ref[g] * acc_ref[g] + _nn(lhs, p_refs[g][dyn0, 0:nk, :])

    n_full = (qi * tq) // tk

    def full_step(j, carry):
        step(j, False)
        return carry

    lax.fori_loop(0, n_full, full_step, 0)

    first_part = (qi * tq) % tk + tq <= tk // 2

    @pl.when(first_part)
    def _():
        step(n_full, True, tk // 2)

    @pl.when(jnp.logical_not(first_part))
    def _():
        step(n_full, True)

    lam = lam_ref[...]
    lam_full = (jnp.exp(jnp.sum(lam[0:1] * lam[1:2], axis=-1, keepdims=True))
                - jnp.exp(jnp.sum(lam[2:3] * lam[3:4], axis=-1, keepdims=True)) + lambda_init)
    for g in range(heads):
        acc = acc_ref[g]
        on = acc[:hd] / acc[hd:hd + 1]
        ot = on[:, :tq] - lam_full * on[:, tq:]
        o = ot.T
        o = o * lax.rsqrt(jnp.mean(o * o, axis=-1, keepdims=True) + SUBLN_EPS) * g_ref[...]
        o_ref[:, g * hd:(g + 1) * hd] = (o * (1.0 - lambda_init)).astype(o_ref.dtype)


def diff_attn(qt, k, vt, lam, subln_g, batch, seq, lambda_init, heads=ATT_HEADS):
    t, d = k.shape
    nq = seq // ATT_TQ
    hd = 2 * B_HEAD_DIM
    w = heads * hd
    return pl.pallas_call(
        functools.partial(_attn_kernel, lambda_init=lambda_init, heads=heads),
        grid=(batch, B_HEADS // heads, nq),
        in_specs=[pl.BlockSpec((1, w, ATT_TQ), lambda b, h, i: (b, h, i)),
                  pl.BlockSpec((seq, w), lambda b, h, i: (b, h)),
                  pl.BlockSpec((1, w, seq), lambda b, h, i: (b, h, 0)),
                  pl.BlockSpec(lam.shape, lambda b, h, i: (0, 0)),
                  pl.BlockSpec((1, hd), lambda b, h, i: (0, 0))],
        out_specs=pl.BlockSpec((ATT_TQ, w), lambda b, h, i: (b * nq + i, h)),
        out_shape=jax.ShapeDtypeStruct((t, d), BF16),
        scratch_shapes=[pltpu.VMEM((heads, hd, 2 * ATT_TQ), BF16),
                        pltpu.VMEM((heads, 1, 2 * ATT_TQ), F32),
                        pltpu.VMEM((heads, hd + ATT_SUM_ROWS, 2 * ATT_TQ), F32),
                        pltpu.VMEM((heads, 1, 2 * ATT_TQ), F32),
                        pltpu.VMEM((heads, 1, 2 * ATT_TQ), F32)]
        + [pltpu.VMEM((1, ATT_TK, 2 * ATT_TQ), F32) for _ in range(heads)]
        + [pltpu.VMEM((1, ATT_TK, 2 * ATT_TQ), BF16) for _ in range(heads)],
        compiler_params=_params("parallel", "parallel", "arbitrary"),
        name="diff_attn",
    )(qt, k, vt, lam, subln_g.reshape(1, hd))


def kernel(x, a_norm_g, a_w_in, a_lb, a_onorm_g, a_w_out, kv_norm_g, w_kv, b_norm_g, b_w_q, b_lam,
           b_subln_g, b_w_out, ffn_norm_g, router_g_w, router_g_b, router_e_w, router_e_b,
           expert_w_gu, expert_w_down, final_norm_g):
    batch, seq, d = x.shape
    assert d == D_MODEL and a_norm_g.shape[0] == 1 and b_norm_g.shape[0] == 1
    assert seq % max(2 * HG_BLOCK, ATT_TK, ROUTER_TM) == 0
    t = batch * seq
    h = x.reshape(t, d)

    h = hgrn2_layer(h, a_norm_g[0], a_w_in[0].astype(BF16), a_lb, a_onorm_g[0],
                    a_w_out[0].astype(BF16), seq)
    qkv_args = (b_norm_g[0], kv_norm_g, b_w_q[0].T.astype(BF16),
                w_kv[:, :d].astype(BF16), w_kv[:, d:].T.astype(BF16))
    h, qt, k, vt = hier_moe_layer(h, ffn_norm_g[0], router_g_w[0], router_g_b[0], router_e_w[0],
                                  router_e_b[0], expert_w_gu, expert_w_down, 0, "qkv", qkv_args,
                                  batch, seq)

    lambda_init = 0.8 - 0.6 * math.exp(-0.3 * 1)
    o = diff_attn(qt, k, vt, b_lam[0], b_subln_g[0], batch, seq, lambda_init)
    h = proj_res(o, b_w_out[0].astype(BF16), h)
    h = hier_moe_layer(h, ffn_norm_g[1], router_g_w[1], router_g_b[1], router_e_w[1], router_e_b[1],
                       expert_w_gu, expert_w_down, 1, "norm", (final_norm_g,), batch, seq)
    return h.reshape(batch, seq, d)
```

```python
import functools
import math

import jax
import jax.numpy as jnp
from jax import lax
from jax.experimental import pallas as pl
from jax.experimental.pallas import tpu as pltpu

F32 = jnp.float32
BF16 = jnp.bfloat16
I32 = jnp.int32

D_MODEL = 1024
A_HEADS = 8
A_HEAD_DIM = 128
B_HEADS = 8
B_HEAD_DIM = 64
N_GROUPS = 4
EXPERTS_PER_GROUP = 8
N_EXPERTS = N_GROUPS * EXPERTS_PER_GROUP
D_EXPERT = 512
NORM_EPS = 1e-6
SUBLN_EPS = 1e-5
LOG2E = 1.4426950408889634

GLA_BLOCK = 128
GLA_HALF = GLA_BLOCK // 2
HG_BLOCK = 256
ATT_TQ = 256
ATT_TK = 512
ATT_HEADS = 8
ATT_CHUNK = 128
ATT_SUM_ROWS = 16
ROUTER_TM = 512
ROUTER_ROWS = 40
EXPERT_TM = 512
EXPERT_X_BUFS = 3
GATHER_TM = 256
ROW_DMA_UNROLL = 8
TOKEN_TILE = 8
VMEM_LIMIT = 56 * 1024 * 1024


def _nt(a, b):
    return lax.dot_general(a, b, (((1,), (1,)), ((), ())), preferred_element_type=F32)


def _nn(a, b):
    return jnp.dot(a, b, preferred_element_type=F32)


def _to_tiles(ref, x):
    tm = x.shape[0]
    for s in range(TOKEN_TILE):
        ref[pl.ds(s, tm, stride=TOKEN_TILE), :] = x[:, s * 128:(s + 1) * 128]


def _from_tiles(ref, tm):
    return jnp.concatenate([ref[pl.ds(s, tm, stride=TOKEN_TILE), :] for s in range(TOKEN_TILE)], axis=1)


def _params(*sem, flags=None):
    return pltpu.CompilerParams(dimension_semantics=sem, vmem_limit_bytes=VMEM_LIMIT, flags=flags)


def _hgrn2_project(x, g_ref, w_ref, proj_ref):
    d = x.shape[1]
    y = x * lax.rsqrt(jnp.mean(x * x, axis=-1, keepdims=True) + NORM_EPS)
    xn = (y * g_ref[...]).astype(BF16)
    for c in range(w_ref.shape[1] // d):
        proj_ref[:, c * d:(c + 1) * d] = _nn(xn, w_ref[:, c * d:(c + 1) * d])


def _gla_block(proj_ref, row0, lb, og, st_ref, out_ref, out_row0):
    d = D_MODEL
    rows = pl.ds(pl.multiple_of(row0, GLA_BLOCK), GLA_BLOCK)
    fz = proj_ref[rows, d:2 * d]
    logf = jnp.log(lb + (1.0 - lb) * jax.nn.sigmoid(fz))
    kk = (1.0 - lb) * jax.nn.sigmoid(-fz)
    qq = jax.nn.silu(proj_ref[rows, 0:d])

    r = lax.broadcasted_iota(I32, (GLA_BLOCK, GLA_BLOCK), 0)
    s = lax.broadcasted_iota(I32, (GLA_BLOCK, GLA_BLOCK), 1)
    tril = (s <= r).astype(BF16)
    hi = logf.astype(BF16)
    lo = (logf - hi.astype(F32)).astype(BF16)
    b = _nn(tril, hi) + _nn(tril, lo)

    h0, h1 = slice(0, GLA_HALF), slice(GLA_HALF, GLA_BLOCK)
    b_a_mid = b[GLA_HALF // 2 - 1:GLA_HALF // 2]
    b_a_end = b[GLA_HALF - 1:GLA_HALF]
    b_b_mid = b[GLA_HALF + GLA_HALF // 2 - 1:GLA_HALF + GLA_HALF // 2]
    b_end = b[GLA_BLOCK - 1:GLA_BLOCK]

    qa_n = (qq[h0] * jnp.exp(b[h0] - b_a_mid)).astype(BF16)
    ka_n = (kk[h0] * jnp.exp(b_a_mid - b[h0])).astype(BF16)
    qb_n = (qq[h1] * jnp.exp(b[h1] - b_b_mid)).astype(BF16)
    kb_n = (kk[h1] * jnp.exp(b_b_mid - b[h1])).astype(BF16)
    qb_x = (qq[h1] * jnp.exp(b[h1] - b_a_end)).astype(BF16)
    ka_x = (kk[h0] * jnp.exp(b_a_end - b[h0])).astype(BF16)
    q_dec = (qq * jnp.exp(b)).astype(BF16)
    k_end = (kk * jnp.exp(b_end - b)).astype(BF16)
    d_end = jnp.exp(b_end)

    rr = lax.broadcasted_iota(I32, (GLA_HALF, GLA_HALF), 0)
    ss = lax.broadcasted_iota(I32, (GLA_HALF, GLA_HALF), 1)
    causal = ss <= rr

    for h in range(A_HEADS):
        hs = slice(h * A_HEAD_DIM, (h + 1) * A_HEAD_DIM)
        v = proj_ref[rows, 2 * d + h * A_HEAD_DIM:2 * d + (h + 1) * A_HEAD_DIM]
        vb = v.astype(BF16)
        st = st_ref[h]
        st_b = st.astype(BF16)
        p_aa = jnp.where(causal, _nt(qa_n[:, hs], ka_n[:, hs]), 0.0).astype(BF16)
        p_bb = jnp.where(causal, _nt(qb_n[:, hs], kb_n[:, hs]), 0.0).astype(BF16)
        p_ba = _nt(qb_x[:, hs], ka_x[:, hs]).astype(BF16)
        inter = _nt(q_dec[:, hs], st_b)
        o_a = _nn(p_aa, vb[h0]) + inter[h0]
        o_b = _nn(p_bb, vb[h1]) + _nn(p_ba, vb[h0]) + inter[h1]
        st_ref[h] = d_end[:, hs] * st + _nn(v.T.astype(BF16), k_end[:, hs])
        gate = jax.nn.silu(proj_ref[rows, 3 * d + h * A_HEAD_DIM:3 * d + (h + 1) * A_HEAD_DIM])
        for half, o in ((0, o_a), (1, o_b)):
            on = o * lax.rsqrt(jnp.mean(o * o, axis=-1, keepdims=True) + NORM_EPS)
            r0 = out_row0 + half * GLA_HALF
            out_ref[r0:r0 + GLA_HALF, hs] = (on * og * gate[half * GLA_HALF:(half + 1) * GLA_HALF]
                                             ).astype(out_ref.dtype)


def _hgrn2_kernel(x_ref, xnext_ref, g_ref, win_ref, alb_ref, og_ref, wout_ref, o_ref,
                  proj_a, proj_b, gated_a, gated_b, st_ref, *, blocks_per_seq):
    step = pl.program_id(0)
    dyn0 = jnp.minimum(step, 0)
    alb = alb_ref[...]
    e = jnp.exp(alb - jnp.max(alb, axis=0, keepdims=True))
    lb = e[0:1] / jnp.sum(e, axis=0, keepdims=True)
    og = og_ref[...]

    @pl.when(step == 0)
    def _():
        _hgrn2_project(x_ref[0:HG_BLOCK, :], g_ref, win_ref, proj_a)

    @pl.when((2 * step) % blocks_per_seq == 0)
    def _():
        st_ref[...] = jnp.zeros_like(st_ref)

    _hgrn2_project(x_ref[HG_BLOCK:2 * HG_BLOCK, :], g_ref, win_ref, proj_b)
    for sub in range(HG_BLOCK // GLA_BLOCK):
        _gla_block(proj_a, dyn0 + sub * GLA_BLOCK, lb, og, st_ref, gated_a, sub * GLA_BLOCK)
    o_ref[0:HG_BLOCK, :] = x_ref[0:HG_BLOCK, :] + _nn(gated_a[...], wout_ref[...])

    _hgrn2_project(xnext_ref[...], g_ref, win_ref, proj_a)
    for sub in range(HG_BLOCK // GLA_BLOCK):
        _gla_block(proj_b, dyn0 + sub * GLA_BLOCK, lb, og, st_ref, gated_b, sub * GLA_BLOCK)
    o_ref[HG_BLOCK:2 * HG_BLOCK, :] = x_ref[HG_BLOCK:2 * HG_BLOCK, :] + _nn(gated_b[...], wout_ref[...])


def hgrn2_layer(x, norm_g, w_in_bf16, a_lb, onorm_g, w_out_bf16, seq):
    t, d = x.shape
    n = w_in_bf16.shape[1]
    nblocks = t // HG_BLOCK
    return pl.pallas_call(
        functools.partial(_hgrn2_kernel, blocks_per_seq=seq // HG_BLOCK),
        grid=(nblocks // 2,),
        in_specs=[pl.BlockSpec((2 * HG_BLOCK, d), lambda i: (i, 0)),
                  pl.BlockSpec((HG_BLOCK, d), lambda i: (jnp.minimum(2 * i + 2, nblocks - 1), 0)),
                  pl.BlockSpec((1, d), lambda i: (0, 0)),
                  pl.BlockSpec((d, n), lambda i: (0, 0)),
                  pl.BlockSpec(a_lb.shape, lambda i: (0, 0)),
                  pl.BlockSpec((1, A_HEAD_DIM), lambda i: (0, 0)),
                  pl.BlockSpec((d, d), lambda i: (0, 0))],
        out_specs=pl.BlockSpec((2 * HG_BLOCK, d), lambda i: (i, 0)),
        out_shape=jax.ShapeDtypeStruct((t, d), F32),
        scratch_shapes=[pltpu.VMEM((HG_BLOCK, n), F32),
                        pltpu.VMEM((HG_BLOCK, n), F32),
                        pltpu.VMEM((HG_BLOCK, d), BF16),
                        pltpu.VMEM((HG_BLOCK, d), BF16),
                        pltpu.VMEM((A_HEADS, A_HEAD_DIM, A_HEAD_DIM), F32)],
        compiler_params=_params("arbitrary"),
        name="hgrn2_layer",
    )(x, x, norm_g.reshape(1, d), w_in_bf16, a_lb, onorm_g.reshape(1, A_HEAD_DIM), w_out_bf16)


def _route_tile(x, g_ref, wh_ref, wl_ref, b_ref, xn_ref, eid_ref, wgt_ref, rank_ref, cnt_ref):
    tm = x.shape[0]

    @pl.when(pl.program_id(0) == 0)
    def _():
        cnt_ref[...] = jnp.zeros_like(cnt_ref)

    xn = x * lax.rsqrt(jnp.mean(x * x, axis=-1, keepdims=True) + NORM_EPS) * g_ref[...]
    _to_tiles(xn_ref, xn)
    xh = xn.astype(BF16)
    xl = (xn - xh.astype(F32)).astype(BF16)
    wh = wh_ref[...]
    lg = _nt(wh, xh) + _nt(wl_ref[...], xh) + _nt(wh, xl) + b_ref[...]

    gl = lg[0:N_GROUPS]
    r4 = lax.broadcasted_iota(I32, gl.shape, 0)
    gmax = jnp.max(gl, axis=0, keepdims=True)
    grp = jnp.min(jnp.where(gl == gmax, r4, N_GROUPS), axis=0, keepdims=True)
    p_grp = 1.0 / jnp.sum(jnp.exp(gl - gmax), axis=0, keepdims=True)

    fine = lg[8:8 + EXPERTS_PER_GROUP]
    for gi in range(1, N_GROUPS):
        fine = jnp.where(grp == gi, lg[8 + gi * EXPERTS_PER_GROUP:8 + (gi + 1) * EXPERTS_PER_GROUP], fine)
    r8 = lax.broadcasted_iota(I32, fine.shape, 0)
    m1 = jnp.max(fine, axis=0, keepdims=True)
    i1 = jnp.min(jnp.where(fine == m1, r8, EXPERTS_PER_GROUP), axis=0, keepdims=True)
    rest = jnp.where(r8 == i1, -jnp.inf, fine)
    m2 = jnp.max(rest, axis=0, keepdims=True)
    i2 = jnp.min(jnp.where(rest == m2, r8, EXPERTS_PER_GROUP), axis=0, keepdims=True)
    e21 = jnp.exp(m2 - m1)
    t1 = 1.0 / (1.0 + e21)
    wgt_ref[0:1, :] = p_grp * t1
    wgt_ref[1:2, :] = p_grp * (e21 * t1)
    e1 = grp * EXPERTS_PER_GROUP + i1
    e2 = grp * EXPERTS_PER_GROUP + i2
    eid_ref[0:1, :] = e1
    eid_ref[1:2, :] = e2

    r32 = lax.broadcasted_iota(I32, (N_EXPERTS, tm), 0)
    is1 = r32 == e1
    is2 = r32 == e2
    member = jnp.logical_or(is1, is2)
    ta = lax.broadcasted_iota(I32, (tm, tm), 0)
    tb = lax.broadcasted_iota(I32, (tm, tm), 1)
    before = (ta < tb).astype(BF16)
    prior = _nn(member.astype(BF16), before) + cnt_ref[:, 0:1]
    rank_ref[0:1, :] = jnp.sum(jnp.where(is1, prior, 0.0), axis=0, keepdims=True).astype(I32)
    rank_ref[1:2, :] = jnp.sum(jnp.where(is2, prior, 0.0), axis=0, keepdims=True).astype(I32)
    cnt_ref[...] = cnt_ref[...] + jnp.sum(member.astype(F32), axis=1, keepdims=True)


def _router_kernel(h_ref, *refs):
    _route_tile(h_ref[...], *refs)


def _proj_res_router_kernel(a_ref, w_ref, r_ref, g_ref, wh_ref, wl_ref, b_ref, o_ref, *route_out):
    h = r_ref[...] + _nn(a_ref[...], w_ref[...])
    o_ref[...] = h
    _route_tile(h, g_ref, wh_ref, wl_ref, b_ref, *route_out)


def router(h, g, wg, bg, we, be, proj=None, tm=ROUTER_TM):
    t, d = h.shape
    w_all = jnp.zeros((ROUTER_ROWS, d), F32).at[0:N_GROUPS].set(wg.T).at[8:8 + N_EXPERTS].set(we.T)
    b_all = jnp.zeros((ROUTER_ROWS, 1), F32).at[0:N_GROUPS, 0].set(bg).at[8:8 + N_EXPERTS, 0].set(be)
    wh = w_all.astype(BF16)
    wl = (w_all - wh.astype(F32)).astype(BF16)
    row = pl.BlockSpec((tm, d), lambda i: (i, 0))
    row2 = lambda i: (0, i)
    route_in = [pl.BlockSpec((1, d), lambda i: (0, 0)),
                pl.BlockSpec((ROUTER_ROWS, d), lambda i: (0, 0)),
                pl.BlockSpec((ROUTER_ROWS, d), lambda i: (0, 0)),
                pl.BlockSpec((ROUTER_ROWS, 1), lambda i: (0, 0))]
    route_specs = [pl.BlockSpec((tm * TOKEN_TILE, 128), lambda i: (i, 0)),
                   pl.BlockSpec((2, tm), row2),
                   pl.BlockSpec((2, tm), row2),
                   pl.BlockSpec((2, tm), row2),
                   pl.BlockSpec((N_EXPERTS, 128), lambda i: (0, 0))]
    route_shapes = [jax.ShapeDtypeStruct((t * TOKEN_TILE, 128), F32),
                    jax.ShapeDtypeStruct((2, t), I32),
                    jax.ShapeDtypeStruct((2, t), F32),
                    jax.ShapeDtypeStruct((2, t), I32),
                    jax.ShapeDtypeStruct((N_EXPERTS, 128), F32)]
    route_args = (g.reshape(1, d), wh, wl, b_all)
    if proj is None:
        return pl.pallas_call(
            _router_kernel,
            grid=(t // tm,),
            in_specs=[row] + route_in,
            out_specs=route_specs,
            out_shape=route_shapes,
            compiler_params=_params("arbitrary"),
            name="router",
        )(h, *route_args)
    a, w = proj
    out = pl.pallas_call(
        _proj_res_router_kernel,
        grid=(t // tm,),
        in_specs=[row, pl.BlockSpec((d, d), lambda i: (0, 0)), row] + route_in,
        out_specs=[row] + route_specs,
        out_shape=[jax.ShapeDtypeStruct((t, d), F32)] + route_shapes,
        compiler_params=_params("arbitrary"),
        name="proj_res_router",
    )(a, w, h, *route_args)
    return out[0], out[1:]


def _dispatch_kernel(last_ref, pos_ref, x_hbm, o_ref, xbuf, zero_ref, load_sem, row_sem):
    i = pl.program_id(0)
    n = pl.num_programs(0)
    tm = xbuf.shape[1] // TOKEN_TILE
    in_rows = tm * TOKEN_TILE
    tile_rows = EXPERT_TM * TOKEN_TILE
    cur = i % 2
    sem = row_sem.at[0]

    def load(j, slot):
        src = pl.multiple_of(j * in_rows, in_rows)
        return pltpu.make_async_copy(x_hbm.at[pl.ds(src, in_rows)], xbuf.at[slot], load_sem.at[slot])

    def drain(slot):
        for _ in range(2):
            pltpu.make_async_copy(xbuf.at[slot], o_ref.at[pl.ds(0, in_rows)], row_sem.at[slot]).wait()

    @pl.when(i == 0)
    def _():
        load(0, 0).start()
        zero_ref[...] = jnp.zeros_like(zero_ref)

        def tile_fill(e):
            row = pl.multiple_of(last_ref[e] * TOKEN_TILE, tile_rows)
            return pltpu.make_async_copy(zero_ref, o_ref.at[pl.ds(row, tile_rows)], sem)

        for e in range(N_EXPERTS):
            @pl.when(last_ref[e] >= 0)
            def _():
                tile_fill(e).start()
        for e in range(N_EXPERTS):
            @pl.when(last_ref[e] >= 0)
            def _():
                tile_fill(e).wait()

        def spare_fill(j):
            row = pl.multiple_of(j * tile_rows, tile_rows)
            return pltpu.make_async_copy(zero_ref, o_ref.at[pl.ds(row, tile_rows)], sem)

        n_tiles = o_ref.shape[0] // tile_rows
        lax.fori_loop(last_ref[N_EXPERTS], n_tiles, lambda j, c: (spare_fill(j).start(), c)[1], 0)
        lax.fori_loop(last_ref[N_EXPERTS], n_tiles, lambda j, c: (spare_fill(j).wait(), c)[1], 0)

    load(i, cur).wait()

    @pl.when(i > 0)
    def _():
        drain(1 - cur)

    @pl.when(i + 1 < n)
    def _():
        load(i + 1, 1 - cur).start()

    def row_copy(r, k):
        dst = pl.multiple_of(pos_ref[0, 0, k * tm + r] * TOKEN_TILE, TOKEN_TILE)
        return pltpu.make_async_copy(xbuf.at[cur, pl.ds(r * TOKEN_TILE, TOKEN_TILE)],
                                     o_ref.at[pl.ds(dst, TOKEN_TILE)], row_sem.at[cur])

    def issue(blk, carry):
        for u in range(ROW_DMA_UNROLL):
            r = blk * ROW_DMA_UNROLL + u
            row_copy(r, 0).start(priority=u % 2)
            row_copy(r, 1).start(priority=(u + 1) % 2)
        return carry

    lax.fori_loop(0, tm // ROW_DMA_UNROLL, issue, 0)

    @pl.when(i == n - 1)
    def _():
        drain(cur)


def dispatch(xn, pos3, last_tile_row, n_rows, tm=GATHER_TM):
    t = xn.shape[0] // TOKEN_TILE
    return pl.pallas_call(
        _dispatch_kernel,
        grid_spec=pltpu.PrefetchScalarGridSpec(
            num_scalar_prefetch=1,
            grid=(t // tm,),
            in_specs=[pl.BlockSpec((1, 1, 2 * tm), lambda i, last: (i, 0, 0), memory_space=pltpu.SMEM),
                      pl.BlockSpec(memory_space=pl.ANY)],
            out_specs=pl.BlockSpec(memory_space=pl.ANY),
            scratch_shapes=[pltpu.VMEM((2, tm * TOKEN_TILE, 128), xn.dtype),
                            pltpu.VMEM((EXPERT_TM * TOKEN_TILE, 128), xn.dtype),
                            pltpu.SemaphoreType.DMA((2,)),
                            pltpu.SemaphoreType.DMA((2,))]),
        out_shape=jax.ShapeDtypeStruct((n_rows * TOKEN_TILE, 128), xn.dtype),
        compiler_params=_params("arbitrary"),
        name="moe_dispatch",
    )(last_tile_row, pos3, xn)


def _expert_kernel(te_ref, nv_ref, x_hbm, wgu_ref, wd_ref, y_ref, xbuf, wgu_b, wd_b, xsem):
    i = pl.program_id(0)
    n_valid = nv_ref[0]
    tm = xbuf.shape[1] // TOKEN_TILE
    in_rows = tm * TOKEN_TILE
    valid = i < n_valid
    changed = te_ref[i] != te_ref[jnp.maximum(i - 1, 0)]
    first = jnp.logical_or(i == 0, changed)

    def xload(j):
        slot = j % EXPERT_X_BUFS
        src = pl.multiple_of(j * in_rows, in_rows)
        return pltpu.make_async_copy(x_hbm.at[pl.ds(src, in_rows)], xbuf.at[slot], xsem.at[slot])

    @pl.when(i == 0)
    def _():
        for j in range(EXPERT_X_BUFS - 1):
            @pl.when(j < n_valid)
            def _():
                xload(j).start()

    @pl.when(i + EXPERT_X_BUFS - 1 < n_valid)
    def _():
        xload(i + EXPERT_X_BUFS - 1).start()

    @pl.when(jnp.logical_and(valid, first))
    def _():
        wgu_b[...] = wgu_ref[0, 0].astype(BF16)
        wd_b[...] = wd_ref[0, 0].astype(BF16)

    @pl.when(valid)
    def _():
        xload(i).wait()
        x = _from_tiles(xbuf.at[i % EXPERT_X_BUFS], tm).astype(BF16)
        au = _nn(x, wgu_b[...])
        a = au[:, :D_EXPERT]
        u = au[:, D_EXPERT:]
        mid = (jax.nn.silu(a) * u).astype(BF16)
        _to_tiles(y_ref, _nn(mid, wd_b[...]))

    @pl.when(jnp.logical_not(valid))
    def _():
        y_ref[...] = jnp.zeros_like(y_ref)


def experts(xg, w_gu, w_down, layer, tile_expert, n_valid, tm=EXPERT_TM):
    d = D_MODEL
    p = xg.shape[0] // TOKEN_TILE
    n_tiles = p // tm
    wsel = lambda i, te, nv: (layer, te[i], 0, 0)
    return pl.pallas_call(
        _expert_kernel,
        grid_spec=pltpu.PrefetchScalarGridSpec(
            num_scalar_prefetch=2,
            grid=(n_tiles,),
            in_specs=[pl.BlockSpec(memory_space=pl.ANY),
                      pl.BlockSpec((1, 1, d, 2 * D_EXPERT), wsel),
                      pl.BlockSpec((1, 1, D_EXPERT, d), wsel)],
            out_specs=pl.BlockSpec((tm * TOKEN_TILE, 128), lambda i, te, nv: (i, 0)),
            scratch_shapes=[pltpu.VMEM((EXPERT_X_BUFS, tm * TOKEN_TILE, 128), F32),
                            pltpu.VMEM((d, 2 * D_EXPERT), BF16),
                            pltpu.VMEM((D_EXPERT, d), BF16),
                            pltpu.SemaphoreType.DMA((EXPERT_X_BUFS,))]),
        out_shape=jax.ShapeDtypeStruct((p * TOKEN_TILE, 128), F32),
        compiler_params=_params("arbitrary"),
        name="moe_experts",
    )(tile_expert, n_valid, xg, w_gu, w_down)


def _qkv_tail(x, gq_ref, gkv_ref, wqt_ref, wk_ref, wvt_ref, qt_ref, k_ref, vt_ref):
    y = x * lax.rsqrt(jnp.mean(x * x, axis=-1, keepdims=True) + NORM_EPS)
    xq = (y * gq_ref[...]).astype(BF16)
    xkv = (y * gkv_ref[...]).astype(BF16)
    qt_ref[0] = (_nt(wqt_ref[...], xq) * (B_HEAD_DIM ** -0.5 * LOG2E)).astype(qt_ref.dtype)
    k_ref[...] = _nn(xkv, wk_ref[...]).astype(k_ref.dtype)
    vt_ref[0] = _nt(wvt_ref[...], xkv).astype(vt_ref.dtype)


def _combine_kernel(pos_ref, pos_next_ref, h_ref, w_ref, y_ref, *rest, tail):
    if tail == "norm":
        g_ref, o_ref, buf, sem = rest
    else:
        gq_ref, gkv_ref, wqt_ref, wk_ref, wvt_ref, o_ref, qt_ref, k_ref, vt_ref, buf, sem = rest
    i = pl.program_id(0)
    tm = h_ref.shape[0]
    cur = i % 2

    def gather_tile(p_ref, half):
        def row_copy(r, k):
            src = pl.multiple_of(p_ref[0, 0, k * tm + r] * TOKEN_TILE, TOKEN_TILE)
            return pltpu.make_async_copy(y_ref.at[pl.ds(src, TOKEN_TILE)],
                                         buf.at[2 * half + k, pl.ds(r * TOKEN_TILE, TOKEN_TILE)],
                                         sem.at[half])

        def issue(blk, carry):
            for u in range(ROW_DMA_UNROLL):
                r = blk * ROW_DMA_UNROLL + u
                row_copy(r, 0).start(priority=u % 2)
                row_copy(r, 1).start(priority=(u + 1) % 2)
            return carry

        lax.fori_loop(0, tm // ROW_DMA_UNROLL, issue, 0)

    @pl.when(i == 0)
    def _():
        gather_tile(pos_ref, 0)

    @pl.when(i + 1 < pl.num_programs(0))
    def _():
        gather_tile(pos_next_ref, 1 - cur)

    for k in range(2):
        pltpu.make_async_copy(y_ref.at[pl.ds(0, tm * TOKEN_TILE)], buf.at[2 * cur + k], sem.at[cur]).wait()
    w = w_ref[...]
    out = (h_ref[...] + w[:, 0:1] * _from_tiles(buf.at[2 * cur], tm)
           + w[:, 1:2] * _from_tiles(buf.at[2 * cur + 1], tm))
    if tail == "norm":
        o_ref[...] = out * lax.rsqrt(jnp.mean(out * out, axis=-1, keepdims=True) + NORM_EPS) * g_ref[...]
    else:
        o_ref[...] = out
        _qkv_tail(out, gq_ref, gkv_ref, wqt_ref, wk_ref, wvt_ref, qt_ref, k_ref, vt_ref)


def combine(h, wgt_t, pos3, y, tail, tail_args, batch, seq, tm=GATHER_TM):
    t, d = h.shape
    nt = t // tm
    nblk = seq // tm
    row = pl.BlockSpec((tm, d), lambda i: (i, 0))
    vec = pl.BlockSpec((1, d), lambda i: (0, 0))
    full = pl.BlockSpec((d, d), lambda i: (0, 0))
    tr = pl.BlockSpec((1, d, tm), lambda i: (i // nblk, 0, i % nblk))
    if tail == "norm":
        (g,) = tail_args
        extra_in, extra_specs = [g.reshape(1, d)], [vec]
        out_specs, out_shape = row, jax.ShapeDtypeStruct((t, d), F32)
    else:
        g_q, g_kv, wqt, wk, wvt = tail_args
        extra_in = [g_q.reshape(1, d), g_kv.reshape(1, d), wqt, wk, wvt]
        extra_specs = [vec, vec, full, full, full]
        out_specs = [row, tr, row, tr]
        out_shape = [jax.ShapeDtypeStruct((t, d), F32),
                     jax.ShapeDtypeStruct((batch, d, seq), BF16),
                     jax.ShapeDtypeStruct((t, d), BF16),
                     jax.ShapeDtypeStruct((batch, d, seq), BF16)]
    return pl.pallas_call(
        functools.partial(_combine_kernel, tail=tail),
        grid=(nt,),
        in_specs=[pl.BlockSpec((1, 1, 2 * tm), lambda i: (i, 0, 0), memory_space=pltpu.SMEM),
                  pl.BlockSpec((1, 1, 2 * tm), lambda i: (jnp.minimum(i + 1, nt - 1), 0, 0),
                               memory_space=pltpu.SMEM),
                  row,
                  pl.BlockSpec((tm, 2), lambda i: (i, 0)),
                  pl.BlockSpec(memory_space=pl.ANY)] + extra_specs,
        out_specs=out_specs,
        out_shape=out_shape,
        scratch_shapes=[pltpu.VMEM((4, tm * TOKEN_TILE, 128), F32), pltpu.SemaphoreType.DMA((2,))],
        compiler_params=_params("arbitrary"),
        name="moe_combine_" + tail,
    )(pos3, pos3, h, wgt_t, y, *extra_in)


def hier_moe_layer(h, routing, w_gu, w_down, layer, tail, tail_args, batch, seq):
    t, d = h.shape
    xn, eid, wgt, rank, cnt = routing
    counts = cnt[:, 0].astype(I32)
    padded = ((counts + EXPERT_TM - 1) // EXPERT_TM) * EXPERT_TM
    ends = jnp.cumsum(padded)
    offs = ends - padded
    n_rows = 2 * t + N_EXPERTS * EXPERT_TM
    n_tiles = n_rows // EXPERT_TM
    n_valid = (ends[-1] // EXPERT_TM).astype(I32).reshape(1)
    tile_start = jnp.arange(n_tiles, dtype=I32) * EXPERT_TM
    tile_start = jnp.minimum(tile_start, ends[-1] - 1)
    tile_expert = jnp.sum((ends[None, :] <= tile_start[:, None]).astype(I32), axis=1)
    onehot = eid[None] == jnp.arange(N_EXPERTS, dtype=I32)[:, None, None]
    pos = jnp.sum(jnp.where(onehot, offs[:, None, None], 0), axis=0) + rank
    nt = t // GATHER_TM
    pos3 = pos.reshape(2, nt, GATHER_TM).transpose(1, 0, 2).reshape(nt, 1, 2 * GATHER_TM)

    last_tile_row = jnp.concatenate([jnp.where(padded > 0, ends - EXPERT_TM, -1).astype(I32), n_valid])
    xg = dispatch(xn, pos3, last_tile_row, n_rows)
    y = experts(xg, w_gu, w_down, layer, tile_expert, n_valid)
    return combine(h, wgt.T, pos3, y, tail, tail_args, batch, seq)


def _attn_kernel(q_ref, k_ref, vt_ref, lam_ref, g_ref, o_ref, qq_ref, m_ref, acc_ref,
                 a_ref, c_ref, *sp_refs, lambda_init, heads):
    s_refs, p_refs = sp_refs[:heads], sp_refs[heads:]
    qi = pl.program_id(2)
    tq, tk, hd = ATT_TQ, ATT_TK, 2 * B_HEAD_DIM
    feat = lax.broadcasted_iota(I32, (hd, tq), 0)
    for g in range(heads):
        qt = q_ref[0, g * hd:(g + 1) * hd, :]
        zero = jnp.zeros_like(qt)
        qq_ref[g, :, 0:tq] = jnp.where(feat < B_HEAD_DIM, qt, zero)
        qq_ref[g, :, tq:2 * tq] = jnp.where(feat >= B_HEAD_DIM, qt, zero)
    m_ref[...] = jnp.full_like(m_ref, -jnp.inf)
    acc_ref[...] = jnp.zeros_like(acc_ref)
    ones = jnp.ones((ATT_SUM_ROWS, tk), BF16)
    dyn0 = jnp.minimum(qi, 0)

    def step(j, masked, nk=tk):
        off = pl.multiple_of(j * tk, tk)
        if masked:
            krow = lax.broadcasted_iota(I32, (nk, 2 * tq), 0)
            qcol = lax.broadcasted_iota(I32, (nk, 2 * tq), 1)
            visible = off + krow <= qi * tq + jnp.where(qcol >= tq, qcol - tq, qcol)
        for g in range(heads):
            kb = k_ref[pl.ds(off, nk), g * hd:(g + 1) * hd]
            s = _nn(kb, qq_ref[g])
            if masked:
                s = jnp.where(visible, s, -jnp.inf)
            s_refs[g][0, 0:nk, :] = s
            c_ref[g] = jnp.max(s, axis=0, keepdims=True)
        for g in range(heads):
            m_old = m_ref[g]
            m_new = jnp.maximum(m_old, c_ref[g])
            a_ref[g] = jnp.exp2(m_old - m_new)
            m_ref[g] = m_new
            for c in range(0, nk, ATT_CHUNK):
                p_refs[g][0, c:c + ATT_CHUNK, :] = jnp.exp2(
                    s_refs[g][dyn0, c:c + ATT_CHUNK, :] - m_new).astype(BF16)
        for g in range(heads):
            vtb = vt_ref[0, g * hd:(g + 1) * hd, pl.ds(off, nk)]
            lhs = jnp.concatenate([vtb, ones[:, 0:nk]], axis=0)
            acc_ref[g] = a_ref[g] * acc_ref[g] + _nn(lhs, p_refs[g][dyn0, 0:nk, :])

    n_full = (qi * tq) // tk

    def full_step(j, carry):
        step(j, False)
        return carry

    lax.fori_loop(0, n_full, full_step, 0)

    first_part = (qi * tq) % tk + tq <= tk // 2

    @pl.when(first_part)
    def _():
        step(n_full, True, tk // 2)

    @pl.when(jnp.logical_not(first_part))
    def _():
        step(n_full, True)

    lam = lam_ref[...]
    lam_full = (jnp.exp(jnp.sum(lam[0:1] * lam[1:2], axis=-1, keepdims=True))
                - jnp.exp(jnp.sum(lam[2:3] * lam[3:4], axis=-1, keepdims=True)) + lambda_init)
    for g in range(heads):
        acc = acc_ref[g]
        on = acc[:hd] / acc[hd:hd + 1]
        ot = on[:, :tq] - lam_full * on[:, tq:]
        o = ot.T
        o = o * lax.rsqrt(jnp.mean(o * o, axis=-1, keepdims=True) + SUBLN_EPS) * g_ref[...]
        o_ref[:, g * hd:(g + 1) * hd] = (o * (1.0 - lambda_init)).astype(o_ref.dtype)


def diff_attn(qt, k, vt, lam, subln_g, batch, seq, lambda_init, heads=ATT_HEADS):
    t, d = k.shape
    nq = seq // ATT_TQ
    hd = 2 * B_HEAD_DIM
    w = heads * hd
    return pl.pallas_call(
        functools.partial(_attn_kernel, lambda_init=lambda_init, heads=heads),
        grid=(batch, B_HEADS // heads, nq),
        in_specs=[pl.BlockSpec((1, w, ATT_TQ), lambda b, h, i: (b, h, i)),
                  pl.BlockSpec((seq, w), lambda b, h, i: (b, h)),
                  pl.BlockSpec((1, w, seq), lambda b, h, i: (b, h, 0)),
                  pl.BlockSpec(lam.shape, lambda b, h, i: (0, 0)),
                  pl.BlockSpec((1, hd), lambda b, h, i: (0, 0))],
        out_specs=pl.BlockSpec((ATT_TQ, w), lambda b, h, i: (b * nq + i, h)),
        out_shape=jax.ShapeDtypeStruct((t, d), BF16),
        scratch_shapes=[pltpu.VMEM((heads, hd, 2 * ATT_TQ), BF16),
                        pltpu.VMEM((heads, 1, 2 * ATT_TQ), F32),
                        pltpu.VMEM((heads, hd + ATT_SUM_ROWS, 2 * ATT_TQ), F32),
                        pltpu.VMEM((heads, 1, 2 * ATT_TQ), F32),
                        pltpu.VMEM((heads, 1, 2 * ATT_TQ), F32)]
        + [pltpu.VMEM((1, ATT_TK, 2 * ATT_TQ), F32) for _ in range(heads)]
        + [pltpu.VMEM((1, ATT_TK, 2 * ATT_TQ), BF16) for _ in range(heads)],
        compiler_params=_params("parallel", "parallel", "arbitrary"),
        name="diff_attn",
    )(qt, k, vt, lam, subln_g.reshape(1, hd))


def kernel(x, a_norm_g, a_w_in, a_lb, a_onorm_g, a_w_out, kv_norm_g, w_kv, b_norm_g, b_w_q, b_lam,
           b_subln_g, b_w_out, ffn_norm_g, router_g_w, router_g_b, router_e_w, router_e_b,
           expert_w_gu, expert_w_down, final_norm_g):
    batch, seq, d = x.shape
    assert d == D_MODEL and a_norm_g.shape[0] == 1 and b_norm_g.shape[0] == 1
    assert seq % max(2 * HG_BLOCK, ATT_TK, ROUTER_TM) == 0
    t = batch * seq
    h = x.reshape(t, d)

    h = hgrn2_layer(h, a_norm_g[0], a_w_in[0].astype(BF16), a_lb, a_onorm_g[0],
                    a_w_out[0].astype(BF16), seq)
    qkv_args = (b_norm_g[0], kv_norm_g, b_w_q[0].T.astype(BF16),
                w_kv[:, :d].astype(BF16), w_kv[:, d:].T.astype(BF16))
    routing = router(h, ffn_norm_g[0], router_g_w[0], router_g_b[0], router_e_w[0], router_e_b[0])
    h, qt, k, vt = hier_moe_layer(h, routing, expert_w_gu, expert_w_down, 0, "qkv", qkv_args, batch, seq)

    lambda_init = 0.8 - 0.6 * math.exp(-0.3 * 1)
    o = diff_attn(qt, k, vt, b_lam[0], b_subln_g[0], batch, seq, lambda_init)
    h, routing = router(h, ffn_norm_g[1], router_g_w[1], router_g_b[1], router_e_w[1], router_e_b[1],
                        proj=(o, b_w_out[0].astype(BF16)))
    h = hier_moe_layer(h, routing, expert_w_gu, expert_w_down, 1, "norm", (final_norm_g,), batch, seq)
    return h.reshape(batch, seq, d)
```

```python
import functools
import math

import jax
import jax.numpy as jnp
from jax import lax
from jax.experimental import pallas as pl
from jax.experimental.pallas import tpu as pltpu

F32 = jnp.float32
BF16 = jnp.bfloat16
I32 = jnp.int32

D_MODEL = 1024
A_HEADS = 8
A_HEAD_DIM = 128
B_HEADS = 8
B_HEAD_DIM = 64
N_GROUPS = 4
EXPERTS_PER_GROUP = 8
N_EXPERTS = N_GROUPS * EXPERTS_PER_GROUP
D_EXPERT = 512
NORM_EPS = 1e-6
SUBLN_EPS = 1e-5
LOG2E = 1.4426950408889634

GLA_BLOCK = 128
GLA_HALF = GLA_BLOCK // 2
HG_BLOCK = 256
ATT_TQ = 256
ATT_TK = 512
ATT_HEADS = 8
ATT_CHUNK = 128
ATT_SUM_ROWS = 16
ROUTER_TM = 512
ROUTER_ROWS = 40
EXPERT_TM = 512
EXPERT_X_BUFS = 3
GATHER_TM = 256
ROW_DMA_UNROLL = 8
TOKEN_TILE = 8
VMEM_LIMIT = 56 * 1024 * 1024


def _nt(a, b):
    return lax.dot_general(a, b, (((1,), (1,)), ((), ())), preferred_element_type=F32)


def _nn(a, b):
    return jnp.dot(a, b, preferred_element_type=F32)


def _to_tiles(ref, x):
    tm = x.shape[0]
    for s in range(TOKEN_TILE):
        ref[pl.ds(s, tm, stride=TOKEN_TILE), :] = x[:, s * 128:(s + 1) * 128]


def _from_tiles(ref, tm):
    return jnp.concatenate([ref[pl.ds(s, tm, stride=TOKEN_TILE), :] for s in range(TOKEN_TILE)], axis=1)


def _params(*sem, flags=None):
    return pltpu.CompilerParams(dimension_semantics=sem, vmem_limit_bytes=VMEM_LIMIT, flags=flags)


def _hgrn2_project(x, g_ref, w_ref, proj_ref):
    d = x.shape[1]
    y = x * lax.rsqrt(jnp.mean(x * x, axis=-1, keepdims=True) + NORM_EPS)
    xn = (y * g_ref[...]).astype(BF16)
    for c in range(w_ref.shape[1] // d):
        proj_ref[:, c * d:(c + 1) * d] = _nn(xn, w_ref[:, c * d:(c + 1) * d])


def _gla_block(proj_ref, row0, lb, og, st_ref, out_ref, out_row0):
    d = D_MODEL
    rows = pl.ds(pl.multiple_of(row0, GLA_BLOCK), GLA_BLOCK)
    fz = proj_ref[rows, d:2 * d]
    logf = jnp.log(lb + (1.0 - lb) * jax.nn.sigmoid(fz))
    kk = (1.0 - lb) * jax.nn.sigmoid(-fz)
    qq = jax.nn.silu(proj_ref[rows, 0:d])

    r = lax.broadcasted_iota(I32, (GLA_BLOCK, GLA_BLOCK), 0)
    s = lax.broadcasted_iota(I32, (GLA_BLOCK, GLA_BLOCK), 1)
    tril = (s <= r).astype(BF16)
    hi = logf.astype(BF16)
    lo = (logf - hi.astype(F32)).astype(BF16)
    b = _nn(tril, hi) + _nn(tril, lo)

    h0, h1 = slice(0, GLA_HALF), slice(GLA_HALF, GLA_BLOCK)
    b_a_mid = b[GLA_HALF // 2 - 1:GLA_HALF // 2]
    b_a_end = b[GLA_HALF - 1:GLA_HALF]
    b_b_mid = b[GLA_HALF + GLA_HALF // 2 - 1:GLA_HALF + GLA_HALF // 2]
    b_end = b[GLA_BLOCK - 1:GLA_BLOCK]

    qa_n = (qq[h0] * jnp.exp(b[h0] - b_a_mid)).astype(BF16)
    ka_n = (kk[h0] * jnp.exp(b_a_mid - b[h0])).astype(BF16)
    qb_n = (qq[h1] * jnp.exp(b[h1] - b_b_mid)).astype(BF16)
    kb_n = (kk[h1] * jnp.exp(b_b_mid - b[h1])).astype(BF16)
    qb_x = (qq[h1] * jnp.exp(b[h1] - b_a_end)).astype(BF16)
    ka_x = (kk[h0] * jnp.exp(b_a_end - b[h0])).astype(BF16)
    q_dec = (qq * jnp.exp(b)).astype(BF16)
    k_end = (kk * jnp.exp(b_end - b)).astype(BF16)
    d_end = jnp.exp(b_end)

    rr = lax.broadcasted_iota(I32, (GLA_HALF, GLA_HALF), 0)
    ss = lax.broadcasted_iota(I32, (GLA_HALF, GLA_HALF), 1)
    causal = ss <= rr

    for h in range(A_HEADS):
        hs = slice(h * A_HEAD_DIM, (h + 1) * A_HEAD_DIM)
        v = proj_ref[rows, 2 * d + h * A_HEAD_DIM:2 * d + (h + 1) * A_HEAD_DIM]
        vb = v.astype(BF16)
        st = st_ref[h]
        st_b = st.astype(BF16)
        p_aa = jnp.where(causal, _nt(qa_n[:, hs], ka_n[:, hs]), 0.0).astype(BF16)
        p_bb = jnp.where(causal, _nt(qb_n[:, hs], kb_n[:, hs]), 0.0).astype(BF16)
        p_ba = _nt(qb_x[:, hs], ka_x[:, hs]).astype(BF16)
        inter = _nt(q_dec[:, hs], st_b)
        o_a = _nn(p_aa, vb[h0]) + inter[h0]
        o_b = _nn(p_bb, vb[h1]) + _nn(p_ba, vb[h0]) + inter[h1]
        st_ref[h] = d_end[:, hs] * st + _nn(v.T.astype(BF16), k_end[:, hs])
        gate = jax.nn.silu(proj_ref[rows, 3 * d + h * A_HEAD_DIM:3 * d + (h + 1) * A_HEAD_DIM])
        for half, o in ((0, o_a), (1, o_b)):
            on = o * lax.rsqrt(jnp.mean(o * o, axis=-1, keepdims=True) + NORM_EPS)
            r0 = out_row0 + half * GLA_HALF
            out_ref[r0:r0 + GLA_HALF, hs] = (on * og * gate[half * GLA_HALF:(half + 1) * GLA_HALF]
                                             ).astype(out_ref.dtype)


def _hgrn2_kernel(x_ref, xnext_ref, g_ref, win_ref, alb_ref, og_ref, wout_ref, o_ref,
                  proj_a, proj_b, gated_a, gated_b, st_ref, *, blocks_per_seq):
    step = pl.program_id(0)
    dyn0 = jnp.minimum(step, 0)
    alb = alb_ref[...]
    e = jnp.exp(alb - jnp.max(alb, axis=0, keepdims=True))
    lb = e[0:1] / jnp.sum(e, axis=0, keepdims=True)
    og = og_ref[...]

    @pl.when(step == 0)
    def _():
        _hgrn2_project(x_ref[0:HG_BLOCK, :], g_ref, win_ref, proj_a)

    @pl.when((2 * step) % blocks_per_seq == 0)
    def _():
        st_ref[...] = jnp.zeros_like(st_ref)

    _hgrn2_project(x_ref[HG_BLOCK:2 * HG_BLOCK, :], g_ref, win_ref, proj_b)
    for sub in range(HG_BLOCK // GLA_BLOCK):
        _gla_block(proj_a, dyn0 + sub * GLA_BLOCK, lb, og, st_ref, gated_a, sub * GLA_BLOCK)
    o_ref[0:HG_BLOCK, :] = x_ref[0:HG_BLOCK, :] + _nn(gated_a[...], wout_ref[...])

    _hgrn2_project(xnext_ref[...], g_ref, win_ref, proj_a)
    for sub in range(HG_BLOCK // GLA_BLOCK):
        _gla_block(proj_b, dyn0 + sub * GLA_BLOCK, lb, og, st_ref, gated_b, sub * GLA_BLOCK)
    o_ref[HG_BLOCK:2 * HG_BLOCK, :] = x_ref[HG_BLOCK:2 * HG_BLOCK, :] + _nn(gated_b[...], wout_ref[...])


def hgrn2_layer(x, norm_g, w_in_bf16, a_lb, onorm_g, w_out_bf16, seq):
    t, d = x.shape
    n = w_in_bf16.shape[1]
    nblocks = t // HG_BLOCK
    return pl.pallas_call(
        functools.partial(_hgrn2_kernel, blocks_per_seq=seq // HG_BLOCK),
        grid=(nblocks // 2,),
        in_specs=[pl.BlockSpec((2 * HG_BLOCK, d), lambda i: (i, 0)),
                  pl.BlockSpec((HG_BLOCK, d), lambda i: (jnp.minimum(2 * i + 2, nblocks - 1), 0)),
                  pl.BlockSpec((1, d), lambda i: (0, 0)),
                  pl.BlockSpec((d, n), lambda i: (0, 0)),
                  pl.BlockSpec(a_lb.shape, lambda i: (0, 0)),
                  pl.BlockSpec((1, A_HEAD_DIM), lambda i: (0, 0)),
                  pl.BlockSpec((d, d), lambda i: (0, 0))],
        out_specs=pl.BlockSpec((2 * HG_BLOCK, d), lambda i: (i, 0)),
        out_shape=jax.ShapeDtypeStruct((t, d), F32),
        scratch_shapes=[pltpu.VMEM((HG_BLOCK, n), F32),
                        pltpu.VMEM((HG_BLOCK, n), F32),
                        pltpu.VMEM((HG_BLOCK, d), BF16),
                        pltpu.VMEM((HG_BLOCK, d), BF16),
                        pltpu.VMEM((A_HEADS, A_HEAD_DIM, A_HEAD_DIM), F32)],
        compiler_params=_params("arbitrary"),
        name="hgrn2_layer",
    )(x, x, norm_g.reshape(1, d), w_in_bf16, a_lb, onorm_g.reshape(1, A_HEAD_DIM), w_out_bf16)


def _route_tile(x, g_ref, wh_ref, wl_ref, b_ref, xn_ref, eid_ref, wgt_ref, rank_ref, cnt_ref):
    tm = x.shape[0]

    @pl.when(pl.program_id(0) == 0)
    def _():
        cnt_ref[...] = jnp.zeros_like(cnt_ref)

    xn = x * lax.rsqrt(jnp.mean(x * x, axis=-1, keepdims=True) + NORM_EPS) * g_ref[...]
    _to_tiles(xn_ref, xn)
    xh = xn.astype(BF16)
    xl = (xn - xh.astype(F32)).astype(BF16)
    wh = wh_ref[...]
    lg = _nt(wh, xh) + _nt(wl_ref[...], xh) + _nt(wh, xl) + b_ref[...]

    gl = lg[0:N_GROUPS]
    r4 = lax.broadcasted_iota(I32, gl.shape, 0)
    gmax = jnp.max(gl, axis=0, keepdims=True)
    grp = jnp.min(jnp.where(gl == gmax, r4, N_GROUPS), axis=0, keepdims=True)
    p_grp = 1.0 / jnp.sum(jnp.exp(gl - gmax), axis=0, keepdims=True)

    fine = lg[8:8 + EXPERTS_PER_GROUP]
    for gi in range(1, N_GROUPS):
        fine = jnp.where(grp == gi, lg[8 + gi * EXPERTS_PER_GROUP:8 + (gi + 1) * EXPERTS_PER_GROUP], fine)
    r8 = lax.broadcasted_iota(I32, fine.shape, 0)
    m1 = jnp.max(fine, axis=0, keepdims=True)
    i1 = jnp.min(jnp.where(fine == m1, r8, EXPERTS_PER_GROUP), axis=0, keepdims=True)
    rest = jnp.where(r8 == i1, -jnp.inf, fine)
    m2 = jnp.max(rest, axis=0, keepdims=True)
    i2 = jnp.min(jnp.where(rest == m2, r8, EXPERTS_PER_GROUP), axis=0, keepdims=True)
    e21 = jnp.exp(m2 - m1)
    t1 = 1.0 / (1.0 + e21)
    wgt_ref[0:1, :] = p_grp * t1
    wgt_ref[1:2, :] = p_grp * (e21 * t1)
    e1 = grp * EXPERTS_PER_GROUP + i1
    e2 = grp * EXPERTS_PER_GROUP + i2
    eid_ref[0:1, :] = e1
    eid_ref[1:2, :] = e2

    r32 = lax.broadcasted_iota(I32, (N_EXPERTS, tm), 0)
    is1 = r32 == e1
    is2 = r32 == e2
    member = jnp.logical_or(is1, is2)
    ta = lax.broadcasted_iota(I32, (tm, tm), 0)
    tb = lax.broadcasted_iota(I32, (tm, tm), 1)
    before = (ta < tb).astype(BF16)
    prior = _nn(member.astype(BF16), before) + cnt_ref[:, 0:1]
    rank_ref[0:1, :] = jnp.sum(jnp.where(is1, prior, 0.0), axis=0, keepdims=True).astype(I32)
    rank_ref[1:2, :] = jnp.sum(jnp.where(is2, prior, 0.0), axis=0, keepdims=True).astype(I32)
    cnt_ref[...] = cnt_ref[...] + jnp.sum(member.astype(F32), axis=1, keepdims=True)


def _router_kernel(h_ref, *refs):
    _route_tile(h_ref[...], *refs)


def _proj_res_router_kernel(a_ref, w_ref, r_ref, g_ref, wh_ref, wl_ref, b_ref, o_ref, *route_out):
    h = r_ref[...] + _nn(a_ref[...], w_ref[...])
    o_ref[...] = h
    _route_tile(h, g_ref, wh_ref, wl_ref, b_ref, *route_out)


def router(h, g, wg, bg, we, be, proj=None, tm=ROUTER_TM):
    t, d = h.shape
    w_all = jnp.zeros((ROUTER_ROWS, d), F32).at[0:N_GROUPS].set(wg.T).at[8:8 + N_EXPERTS].set(we.T)
    b_all = jnp.zeros((ROUTER_ROWS, 1), F32).at[0:N_GROUPS, 0].set(bg).at[8:8 + N_EXPERTS, 0].set(be)
    wh = w_all.astype(BF16)
    wl = (w_all - wh.astype(F32)).astype(BF16)
    row = pl.BlockSpec((tm, d), lambda i: (i, 0))
    row2 = lambda i: (0, i)
    route_in = [pl.BlockSpec((1, d), lambda i: (0, 0)),
                pl.BlockSpec((ROUTER_ROWS, d), lambda i: (0, 0)),
                pl.BlockSpec((ROUTER_ROWS, d), lambda i: (0, 0)),
                pl.BlockSpec((ROUTER_ROWS, 1), lambda i: (0, 0))]
    route_specs = [pl.BlockSpec((tm * TOKEN_TILE, 128), lambda i: (i, 0)),
                   pl.BlockSpec((2, tm), row2),
                   pl.BlockSpec((2, tm), row2),
                   pl.BlockSpec((2, tm), row2),
                   pl.BlockSpec((N_EXPERTS, 128), lambda i: (0, 0))]
    route_shapes = [jax.ShapeDtypeStruct((t * TOKEN_TILE, 128), F32),
                    jax.ShapeDtypeStruct((2, t), I32),
                    jax.ShapeDtypeStruct((2, t), F32),
                    jax.ShapeDtypeStruct((2, t), I32),
                    jax.ShapeDtypeStruct((N_EXPERTS, 128), F32)]
    route_args = (g.reshape(1, d), wh, wl, b_all)
    if proj is None:
        return pl.pallas_call(
            _router_kernel,
            grid=(t // tm,),
            in_specs=[row] + route_in,
            out_specs=route_specs,
            out_shape=route_shapes,
            compiler_params=_params("arbitrary"),
            name="router",
        )(h, *route_args)
    a, w = proj
    out = pl.pallas_call(
        _proj_res_router_kernel,
        grid=(t // tm,),
        in_specs=[row, pl.BlockSpec((d, d), lambda i: (0, 0)), row] + route_in,
        out_specs=[row] + route_specs,
        out_shape=[jax.ShapeDtypeStruct((t, d), F32)] + route_shapes,
        compiler_params=_params("arbitrary"),
        name="proj_res_router",
    )(a, w, h, *route_args)
    return out[0], out[1:]


def _dispatch_kernel(last_ref, pos_ref, x_hbm, o_ref, xbuf, zero_ref, load_sem, row_sem):
    i = pl.program_id(0)
    n = pl.num_programs(0)
    tm = xbuf.shape[1] // TOKEN_TILE
    in_rows = tm * TOKEN_TILE
    tile_rows = EXPERT_TM * TOKEN_TILE
    cur = i % 2
    sem = row_sem.at[0]

    def load(j, slot):
        src = pl.multiple_of(j * in_rows, in_rows)
        return pltpu.make_async_copy(x_hbm.at[pl.ds(src, in_rows)], xbuf.at[slot], load_sem.at[slot])

    def drain(slot):
        for _ in range(2):
            pltpu.make_async_copy(xbuf.at[slot], o_ref.at[pl.ds(0, in_rows)], row_sem.at[slot]).wait()

    @pl.when(i == 0)
    def _():
        load(0, 0).start()
        zero_ref[...] = jnp.zeros_like(zero_ref)

        def tile_fill(e):
            row = pl.multiple_of(last_ref[e] * TOKEN_TILE, tile_rows)
            return pltpu.make_async_copy(zero_ref, o_ref.at[pl.ds(row, tile_rows)], sem)

        for e in range(N_EXPERTS):
            @pl.when(last_ref[e] >= 0)
            def _():
                tile_fill(e).start()
        for e in range(N_EXPERTS):
            @pl.when(last_ref[e] >= 0)
            def _():
                tile_fill(e).wait()

        def spare_fill(j):
            row = pl.multiple_of(j * tile_rows, tile_rows)
            return pltpu.make_async_copy(zero_ref, o_ref.at[pl.ds(row, tile_rows)], sem)

        n_tiles = o_ref.shape[0] // tile_rows
        lax.fori_loop(last_ref[N_EXPERTS], n_tiles, lambda j, c: (spare_fill(j).start(), c)[1], 0)
        lax.fori_loop(last_ref[N_EXPERTS], n_tiles, lambda j, c: (spare_fill(j).wait(), c)[1], 0)

    load(i, cur).wait()

    @pl.when(i > 0)
    def _():
        drain(1 - cur)

    @pl.when(i + 1 < n)
    def _():
        load(i + 1, 1 - cur).start()

    def row_copy(r, k):
        dst = pl.multiple_of(pos_ref[0, 0, k * tm + r] * TOKEN_TILE, TOKEN_TILE)
        return pltpu.make_async_copy(xbuf.at[cur, pl.ds(r * TOKEN_TILE, TOKEN_TILE)],
                                     o_ref.at[pl.ds(dst, TOKEN_TILE)], row_sem.at[cur])

    def issue(blk, carry):
        for u in range(ROW_DMA_UNROLL):
            r = blk * ROW_DMA_UNROLL + u
            row_copy(r, 0).start(priority=u % 2)
            row_copy(r, 1).start(priority=(u + 1) % 2)
        return carry

    lax.fori_loop(0, tm // ROW_DMA_UNROLL, issue, 0)

    @pl.when(i == n - 1)
    def _():
        drain(cur)


def dispatch(xn, pos3, last_tile_row, n_rows, tm=GATHER_TM):
    t = xn.shape[0] // TOKEN_TILE
    return pl.pallas_call(
        _dispatch_kernel,
        grid_spec=pltpu.PrefetchScalarGridSpec(
            num_scalar_prefetch=1,
            grid=(t // tm,),
            in_specs=[pl.BlockSpec((1, 1, 2 * tm), lambda i, last: (i, 0, 0), memory_space=pltpu.SMEM),
                      pl.BlockSpec(memory_space=pl.ANY)],
            out_specs=pl.BlockSpec(memory_space=pl.ANY),
            scratch_shapes=[pltpu.VMEM((2, tm * TOKEN_TILE, 128), xn.dtype),
                            pltpu.VMEM((EXPERT_TM * TOKEN_TILE, 128), xn.dtype),
                            pltpu.SemaphoreType.DMA((2,)),
                            pltpu.SemaphoreType.DMA((2,))]),
        out_shape=jax.ShapeDtypeStruct((n_rows * TOKEN_TILE, 128), xn.dtype),
        compiler_params=_params("arbitrary"),
        name="moe_dispatch",
    )(last_tile_row, pos3, xn)


def _expert_kernel(te_ref, nv_ref, par_ref, nxt_ref, x_hbm, wgu_hbm, wd_hbm, y_ref,
                   xbuf, wgu_f, wd_f, wgu_b, wd_b, xsem, wsem, *, layer):
    i = pl.program_id(0)
    n_valid = nv_ref[0]
    tm = xbuf.shape[1] // TOKEN_TILE
    in_rows = tm * TOKEN_TILE
    valid = i < n_valid
    changed = te_ref[i] != te_ref[jnp.maximum(i - 1, 0)]
    first = jnp.logical_or(i == 0, changed)

    def xload(j):
        slot = j % EXPERT_X_BUFS
        src = pl.multiple_of(j * in_rows, in_rows)
        return pltpu.make_async_copy(x_hbm.at[pl.ds(src, in_rows)], xbuf.at[slot], xsem.at[slot])

    def wload(e, slot):
        return (pltpu.make_async_copy(wgu_hbm.at[layer, e], wgu_f.at[slot], wsem.at[slot]),
                pltpu.make_async_copy(wd_hbm.at[layer, e], wd_f.at[slot], wsem.at[slot]))

    @pl.when(i == 0)
    def _():
        for cp in wload(te_ref[0], par_ref[0]):
            cp.start()
        for j in range(EXPERT_X_BUFS - 1):
            @pl.when(j < n_valid)
            def _():
                xload(j).start()

    @pl.when(i + EXPERT_X_BUFS - 1 < n_valid)
    def _():
        xload(i + EXPERT_X_BUFS - 1).start()

    @pl.when(jnp.logical_and(valid, first))
    def _():
        slot = par_ref[i]
        for cp in wload(te_ref[i], slot):
            cp.wait()

        @pl.when(nxt_ref[i] >= 0)
        def _():
            for cp in wload(nxt_ref[i], 1 - slot):
                cp.start()

        wgu_b[...] = wgu_f[slot].astype(BF16)
        wd_b[...] = wd_f[slot].astype(BF16)

    @pl.when(valid)
    def _():
        xload(i).wait()
        x = _from_tiles(xbuf.at[i % EXPERT_X_BUFS], tm).astype(BF16)
        au = _nn(x, wgu_b[...])
        a = au[:, :D_EXPERT]
        u = au[:, D_EXPERT:]
        mid = (jax.nn.silu(a) * u).astype(BF16)
        _to_tiles(y_ref, _nn(mid, wd_b[...]))

    @pl.when(jnp.logical_not(valid))
    def _():
        y_ref[...] = jnp.zeros_like(y_ref)


def experts(xg, w_gu, w_down, layer, tile_expert, n_valid, seg_parity, seg_next, tm=EXPERT_TM):
    d = D_MODEL
    p = xg.shape[0] // TOKEN_TILE
    n_tiles = p // tm
    return pl.pallas_call(
        functools.partial(_expert_kernel, layer=layer),
        grid_spec=pltpu.PrefetchScalarGridSpec(
            num_scalar_prefetch=4,
            grid=(n_tiles,),
            in_specs=[pl.BlockSpec(memory_space=pl.ANY),
                      pl.BlockSpec(memory_space=pl.ANY),
                      pl.BlockSpec(memory_space=pl.ANY)],
            out_specs=pl.BlockSpec((tm * TOKEN_TILE, 128), lambda i, *_: (i, 0)),
            scratch_shapes=[pltpu.VMEM((EXPERT_X_BUFS, tm * TOKEN_TILE, 128), F32),
                            pltpu.VMEM((2, d, 2 * D_EXPERT), F32),
                            pltpu.VMEM((2, D_EXPERT, d), F32),
                            pltpu.VMEM((d, 2 * D_EXPERT), BF16),
                            pltpu.VMEM((D_EXPERT, d), BF16),
                            pltpu.SemaphoreType.DMA((EXPERT_X_BUFS,)),
                            pltpu.SemaphoreType.DMA((2,))]),
        out_shape=jax.ShapeDtypeStruct((p * TOKEN_TILE, 128), F32),
        compiler_params=_params("arbitrary"),
        name="moe_experts",
    )(tile_expert, n_valid, seg_parity, seg_next, xg, w_gu, w_down)


def _qkv_tail(x, gq_ref, gkv_ref, wqt_ref, wk_ref, wvt_ref, qt_ref, k_ref, vt_ref):
    y = x * lax.rsqrt(jnp.mean(x * x, axis=-1, keepdims=True) + NORM_EPS)
    xq = (y * gq_ref[...]).astype(BF16)
    xkv = (y * gkv_ref[...]).astype(BF16)
    qt_ref[0] = (_nt(wqt_ref[...], xq) * (B_HEAD_DIM ** -0.5 * LOG2E)).astype(qt_ref.dtype)
    k_ref[...] = _nn(xkv, wk_ref[...]).astype(k_ref.dtype)
    vt_ref[0] = _nt(wvt_ref[...], xkv).astype(vt_ref.dtype)


def _combine_kernel(pos_ref, pos_next_ref, h_ref, w_ref, y_ref, *rest, tail):
    if tail == "norm":
        g_ref, o_ref, buf, sem = rest
    else:
        gq_ref, gkv_ref, wqt_ref, wk_ref, wvt_ref, o_ref, qt_ref, k_ref, vt_ref, buf, sem = rest
    i = pl.program_id(0)
    tm = h_ref.shape[0]
    cur = i % 2

    def gather_tile(p_ref, half):
        def row_copy(r, k):
            src = pl.multiple_of(p_ref[0, 0, k * tm + r] * TOKEN_TILE, TOKEN_TILE)
            return pltpu.make_async_copy(y_ref.at[pl.ds(src, TOKEN_TILE)],
                                         buf.at[2 * half + k, pl.ds(r * TOKEN_TILE, TOKEN_TILE)],
                                         sem.at[half])

        def issue(blk, carry):
            for u in range(ROW_DMA_UNROLL):
                r = blk * ROW_DMA_UNROLL + u
                row_copy(r, 0).start(priority=u % 2)
                row_copy(r, 1).start(priority=(u + 1) % 2)
            return carry

        lax.fori_loop(0, tm // ROW_DMA_UNROLL, issue, 0)

    @pl.when(i == 0)
    def _():
        gather_tile(pos_ref, 0)

    @pl.when(i + 1 < pl.num_programs(0))
    def _():
        gather_tile(pos_next_ref, 1 - cur)

    for k in range(2):
        pltpu.make_async_copy(y_ref.at[pl.ds(0, tm * TOKEN_TILE)], buf.at[2 * cur + k], sem.at[cur]).wait()
    w = w_ref[...]
    out = (h_ref[...] + w[:, 0:1] * _from_tiles(buf.at[2 * cur], tm)
           + w[:, 1:2] * _from_tiles(buf.at[2 * cur + 1], tm))
    if tail == "norm":
        o_ref[...] = out * lax.rsqrt(jnp.mean(out * out, axis=-1, keepdims=True) + NORM_EPS) * g_ref[...]
    else:
        o_ref[...] = out
        _qkv_tail(out, gq_ref, gkv_ref, wqt_ref, wk_ref, wvt_ref, qt_ref, k_ref, vt_ref)


def combine(h, wgt_t, pos3, y, tail, tail_args, batch, seq, tm=GATHER_TM):
    t, d = h.shape
    nt = t // tm
    nblk = seq // tm
    row = pl.BlockSpec((tm, d), lambda i: (i, 0))
    vec = pl.BlockSpec((1, d), lambda i: (0, 0))
    full = pl.BlockSpec((d, d), lambda i: (0, 0))
    tr = pl.BlockSpec((1, d, tm), lambda i: (i // nblk, 0, i % nblk))
    if tail == "norm":
        (g,) = tail_args
        extra_in, extra_specs = [g.reshape(1, d)], [vec]
        out_specs, out_shape = row, jax.ShapeDtypeStruct((t, d), F32)
    else:
        g_q, g_kv, wqt, wk, wvt = tail_args
        extra_in = [g_q.reshape(1, d), g_kv.reshape(1, d), wqt, wk, wvt]
        extra_specs = [vec, vec, full, full, full]
        out_specs = [row, tr, row, tr]
        out_shape = [jax.ShapeDtypeStruct((t, d), F32),
                     jax.ShapeDtypeStruct((batch, d, seq), BF16),
                     jax.ShapeDtypeStruct((t, d), BF16),
                     jax.ShapeDtypeStruct((batch, d, seq), BF16)]
    return pl.pallas_call(
        functools.partial(_combine_kernel, tail=tail),
        grid=(nt,),
        in_specs=[pl.BlockSpec((1, 1, 2 * tm), lambda i: (i, 0, 0), memory_space=pltpu.SMEM),
                  pl.BlockSpec((1, 1, 2 * tm), lambda i: (jnp.minimum(i + 1, nt - 1), 0, 0),
                               memory_space=pltpu.SMEM),
                  row,
                  pl.BlockSpec((tm, 2), lambda i: (i, 0)),
                  pl.BlockSpec(memory_space=pl.ANY)] + extra_specs,
        out_specs=out_specs,
        out_shape=out_shape,
        scratch_shapes=[pltpu.VMEM((4, tm * TOKEN_TILE, 128), F32), pltpu.SemaphoreType.DMA((2,))],
        compiler_params=_params("arbitrary"),
        name="moe_combine_" + tail,
    )(pos3, pos3, h, wgt_t, y, *extra_in)


def hier_moe_layer(h, routing, w_gu, w_down, layer, tail, tail_args, batch, seq):
    t, d = h.shape
    xn, eid, wgt, rank, cnt = routing
    counts = cnt[:, 0].astype(I32)
    padded = ((counts + EXPERT_TM - 1) // EXPERT_TM) * EXPERT_TM
    ends = jnp.cumsum(padded)
    offs = ends - padded
    n_rows = 2 * t + N_EXPERTS * EXPERT_TM
    n_tiles = n_rows // EXPERT_TM
    n_valid = (ends[-1] // EXPERT_TM).astype(I32).reshape(1)
    tile_start = jnp.arange(n_tiles, dtype=I32) * EXPERT_TM
    tile_start = jnp.minimum(tile_start, ends[-1] - 1)
    tile_expert = jnp.sum((ends[None, :] <= tile_start[:, None]).astype(I32), axis=1)
    onehot = eid[None] == jnp.arange(N_EXPERTS, dtype=I32)[:, None, None]
    pos = jnp.sum(jnp.where(onehot, offs[:, None, None], 0), axis=0) + rank
    nt = t // GATHER_TM
    pos3 = pos.reshape(2, nt, GATHER_TM).transpose(1, 0, 2).reshape(nt, 1, 2 * GATHER_TM)

    last_tile_row = jnp.concatenate([jnp.where(padded > 0, ends - EXPERT_TM, -1).astype(I32), n_valid])
    xg = dispatch(xn, pos3, last_tile_row, n_rows)
    nonempty = padded > 0
    ordinal = jnp.cumsum(nonempty.astype(I32)) - 1
    ids = jnp.arange(N_EXPERTS, dtype=I32)
    later = jnp.logical_and(nonempty[None, :], ids[None, :] > ids[:, None])
    next_expert = jnp.min(jnp.where(later, ids[None, :], N_EXPERTS), axis=1)
    next_expert = jnp.where(next_expert == N_EXPERTS, -1, next_expert).astype(I32)
    y = experts(xg, w_gu, w_down, layer, tile_expert, n_valid,
                (ordinal[tile_expert] % 2).astype(I32), next_expert[tile_expert])
    return combine(h, wgt.T, pos3, y, tail, tail_args, batch, seq)


def _attn_kernel(q_ref, k_ref, vt_ref, lam_ref, g_ref, o_ref, qq_ref, m_ref, acc_ref,
                 a_ref, c_ref, *sp_refs, lambda_init, heads):
    s_refs, p_refs = sp_refs[:heads], sp_refs[heads:]
    qi = pl.program_id(2)
    tq, tk, hd = ATT_TQ, ATT_TK, 2 * B_HEAD_DIM
    feat = lax.broadcasted_iota(I32, (hd, tq), 0)
    for g in range(heads):
        qt = q_ref[0, g * hd:(g + 1) * hd, :]
        zero = jnp.zeros_like(qt)
        qq_ref[g, :, 0:tq] = jnp.where(feat < B_HEAD_DIM, qt, zero)
        qq_ref[g, :, tq:2 * tq] = jnp.where(feat >= B_HEAD_DIM, qt, zero)
    m_ref[...] = jnp.full_like(m_ref, -jnp.inf)
    acc_ref[...] = jnp.zeros_like(acc_ref)
    ones = jnp.ones((ATT_SUM_ROWS, tk), BF16)
    dyn0 = jnp.minimum(qi, 0)

    def step(j, masked, nk=tk):
        off = pl.multiple_of(j * tk, tk)
        if masked:
            krow = lax.broadcasted_iota(I32, (nk, 2 * tq), 0)
            qcol = lax.broadcasted_iota(I32, (nk, 2 * tq), 1)
            visible = off + krow <= qi * tq + jnp.where(qcol >= tq, qcol - tq, qcol)
        for g in range(heads):
            kb = k_ref[pl.ds(off, nk), g * hd:(g + 1) * hd]
            s = _nn(kb, qq_ref[g])
            if masked:
                s = jnp.where(visible, s, -jnp.inf)
            s_refs[g][0, 0:nk, :] = s
            c_ref[g] = jnp.max(s, axis=0, keepdims=True)
        for g in range(heads):
            m_old = m_ref[g]
            m_new = jnp.maximum(m_old, c_ref[g])
            a_ref[g] = jnp.exp2(m_old - m_new)
            m_ref[g] = m_new
            for c in range(0, nk, ATT_CHUNK):
                p_refs[g][0, c:c + ATT_CHUNK, :] = jnp.exp2(
                    s_refs[g][dyn0, c:c + ATT_CHUNK, :] - m_new).astype(BF16)
        for g in range(heads):
            vtb = vt_ref[0, g * hd:(g + 1) * hd, pl.ds(off, nk)]
            lhs = jnp.concatenate([vtb, ones[:, 0:nk]], axis=0)
            acc_ref[g] = a_ref[g] * acc_ref[g] + _nn(lhs, p_refs[g][dyn0, 0:nk, :])

    n_full = (qi * tq) // tk

    def full_step(j, carry):
        step(j, False)
        return carry

    lax.fori_loop(0, n_full, full_step, 0)

    first_part = (qi * tq) % tk + tq <= tk // 2

    @pl.when(first_part)
    def _():
        step(n_full, True, tk // 2)

    @pl.when(jnp.logical_not(first_part))
    def _():
        step(n_full, True)

    lam = lam_ref[...]
    lam_full = (jnp.exp(jnp.sum(lam[0:1] * lam[1:2], axis=-1, keepdims=True))
                - jnp.exp(jnp.sum(lam[2:3] * lam[3:4], axis=-1, keepdims=True)) + lambda_init)
    for g in range(heads):
        acc = acc_ref[g]
        on = acc[:hd] / acc[hd:hd + 1]
        ot = on[:, :tq] - lam_full * on[:, tq:]
        o = ot.T
        o = o * lax.rsqrt(jnp.mean(o * o, axis=-1, keepdims=True) + SUBLN_EPS) * g_ref[...]
        o_ref[:, g * hd:(g + 1) * hd] = (o * (1.0 - lambda_init)).astype(o_ref.dtype)


def diff_attn(qt, k, vt, lam, subln_g, batch, seq, lambda_init, heads=ATT_HEADS):
    t, d = k.shape
    nq = seq // ATT_TQ
    hd = 2 * B_HEAD_DIM
    w = heads * hd
    return pl.pallas_call(
        functools.partial(_attn_kernel, lambda_init=lambda_init, heads=heads),
        grid=(batch, B_HEADS // heads, nq),
        in_specs=[pl.BlockSpec((1, w, ATT_TQ), lambda b, h, i: (b, h, i)),
                  pl.BlockSpec((seq, w), lambda b, h, i: (b, h)),
                  pl.BlockSpec((1, w, seq), lambda b, h, i: (b, h, 0)),
                  pl.BlockSpec(lam.shape, lambda b, h, i: (0, 0)),
                  pl.BlockSpec((1, hd), lambda b, h, i: (0, 0))],
        out_specs=pl.BlockSpec((ATT_TQ, w), lambda b, h, i: (b * nq + i, h)),
        out_shape=jax.ShapeDtypeStruct((t, d), BF16),
        scratch_shapes=[pltpu.VMEM((heads, hd, 2 * ATT_TQ), BF16),
                        pltpu.VMEM((heads, 1, 2 * ATT_TQ), F32),
                        pltpu.VMEM((heads, hd + ATT_SUM_ROWS, 2 * ATT_TQ), F32),
                        pltpu.VMEM((heads, 1, 2 * ATT_TQ), F32),
                        pltpu.VMEM((heads, 1, 2 * ATT_TQ), F32)]
        + [pltpu.VMEM((1, ATT_TK, 2 * ATT_TQ), F32) for _ in range(heads)]
        + [pltpu.VMEM((1, ATT_TK, 2 * ATT_TQ), BF16) for _ in range(heads)],
        compiler_params=_params("parallel", "parallel", "arbitrary"),
        name="diff_attn",
    )(qt, k, vt, lam, subln_g.reshape(1, hd))


def kernel(x, a_norm_g, a_w_in, a_lb, a_onorm_g, a_w_out, kv_norm_g, w_kv, b_norm_g, b_w_q, b_lam,
           b_subln_g, b_w_out, ffn_norm_g, router_g_w, router_g_b, router_e_w, router_e_b,
           expert_w_gu, expert_w_down, final_norm_g):
    batch, seq, d = x.shape
    assert d == D_MODEL and a_norm_g.shape[0] == 1 and b_norm_g.shape[0] == 1
    assert seq % max(2 * HG_BLOCK, ATT_TK, ROUTER_TM) == 0
    t = batch * seq
    h = x.reshape(t, d)

    h = hgrn2_layer(h, a_norm_g[0], a_w_in[0].astype(BF16), a_lb, a_onorm_g[0],
                    a_w_out[0].astype(BF16), seq)
    qkv_args = (b_norm_g[0], kv_norm_g, b_w_q[0].T.astype(BF16),
                w_kv[:, :d].astype(BF16), w_kv[:, d:].T.astype(BF16))
    routing = router(h, ffn_norm_g[0], router_g_w[0], router_g_b[0], router_e_w[0], router_e_b[0])
    h, qt, k, vt = hier_moe_layer(h, routing, expert_w_gu, expert_w_down, 0, "qkv", qkv_args, batch, seq)

    lambda_init = 0.8 - 0.6 * math.exp(-0.3 * 1)
    o = diff_attn(qt, k, vt, b_lam[0], b_subln_g[0], batch, seq, lambda_init)
    h, routing = router(h, ffn_norm_g[1], router_g_w[1], router_g_b[1], router_e_w[1], router_e_b[1],
                        proj=(o, b_w_out[0].astype(BF16)))
    h = hier_moe_layer(h, routing, expert_w_gu, expert_w_down, 1, "norm", (final_norm_g,), batch, seq)
    return h.reshape(batch, seq, d)
```

```python
import functools
import math

import jax
import jax.numpy as jnp
from jax import lax
from jax.experimental import pallas as pl
from jax.experimental.pallas import tpu as pltpu

F32 = jnp.float32
BF16 = jnp.bfloat16
I32 = jnp.int32

D_MODEL = 1024
A_HEADS = 8
A_HEAD_DIM = 128
B_HEADS = 8
B_HEAD_DIM = 64
N_GROUPS = 4
EXPERTS_PER_GROUP = 8
N_EXPERTS = N_GROUPS * EXPERTS_PER_GROUP
D_EXPERT = 512
NORM_EPS = 1e-6
SUBLN_EPS = 1e-5
LOG2E = 1.4426950408889634

GLA_BLOCK = 128
GLA_HALF = GLA_BLOCK // 2
HG_BLOCK = 256
ATT_TQ = 256
ATT_TK = 512
ATT_HEADS = 8
ATT_CHUNK = 128
ATT_SUM_ROWS = 16
ROUTER_TM = 512
ROUTER_ROWS = 40
EXPERT_TM = 512
EXPERT_X_BUFS = 3
GATHER_TM = 512
ROW_DMA_UNROLL = 8
TOKEN_TILE = 8
VMEM_LIMIT = 56 * 1024 * 1024


def _nt(a, b):
    return lax.dot_general(a, b, (((1,), (1,)), ((), ())), preferred_element_type=F32)


def _nn(a, b):
    return jnp.dot(a, b, preferred_element_type=F32)


def _to_tiles(ref, x):
    tm = x.shape[0]
    for s in range(TOKEN_TILE):
        ref[pl.ds(s, tm, stride=TOKEN_TILE), :] = x[:, s * 128:(s + 1) * 128]


def _from_tiles(ref, tm):
    return jnp.concatenate([ref[pl.ds(s, tm, stride=TOKEN_TILE), :] for s in range(TOKEN_TILE)], axis=1)


def _params(*sem, flags=None):
    return pltpu.CompilerParams(dimension_semantics=sem, vmem_limit_bytes=VMEM_LIMIT, flags=flags)


def _hgrn2_project(x, g_ref, w_ref, proj_ref):
    d = x.shape[1]
    y = x * lax.rsqrt(jnp.mean(x * x, axis=-1, keepdims=True) + NORM_EPS)
    xn = (y * g_ref[...]).astype(BF16)
    for c in range(w_ref.shape[1] // d):
        proj_ref[:, c * d:(c + 1) * d] = _nn(xn, w_ref[:, c * d:(c + 1) * d])


def _gla_block(proj_ref, row0, lb, og, st_ref, out_ref, out_row0):
    d = D_MODEL
    rows = pl.ds(pl.multiple_of(row0, GLA_BLOCK), GLA_BLOCK)
    fz = proj_ref[rows, d:2 * d]
    logf = jnp.log(lb + (1.0 - lb) * jax.nn.sigmoid(fz))
    kk = (1.0 - lb) * jax.nn.sigmoid(-fz)
    qq = jax.nn.silu(proj_ref[rows, 0:d])

    r = lax.broadcasted_iota(I32, (GLA_BLOCK, GLA_BLOCK), 0)
    s = lax.broadcasted_iota(I32, (GLA_BLOCK, GLA_BLOCK), 1)
    tril = (s <= r).astype(BF16)
    hi = logf.astype(BF16)
    lo = (logf - hi.astype(F32)).astype(BF16)
    b = _nn(tril, hi) + _nn(tril, lo)

    h0, h1 = slice(0, GLA_HALF), slice(GLA_HALF, GLA_BLOCK)
    b_a_mid = b[GLA_HALF // 2 - 1:GLA_HALF // 2]
    b_a_end = b[GLA_HALF - 1:GLA_HALF]
    b_b_mid = b[GLA_HALF + GLA_HALF // 2 - 1:GLA_HALF + GLA_HALF // 2]
    b_end = b[GLA_BLOCK - 1:GLA_BLOCK]

    qa_n = (qq[h0] * jnp.exp(b[h0] - b_a_mid)).astype(BF16)
    ka_n = (kk[h0] * jnp.exp(b_a_mid - b[h0])).astype(BF16)
    qb_n = (qq[h1] * jnp.exp(b[h1] - b_b_mid)).astype(BF16)
    kb_n = (kk[h1] * jnp.exp(b_b_mid - b[h1])).astype(BF16)
    qb_x = (qq[h1] * jnp.exp(b[h1] - b_a_end)).astype(BF16)
    ka_x = (kk[h0] * jnp.exp(b_a_end - b[h0])).astype(BF16)
    q_dec = (qq * jnp.exp(b)).astype(BF16)
    k_end = (kk * jnp.exp(b_end - b)).astype(BF16)
    d_end = jnp.exp(b_end)

    rr = lax.broadcasted_iota(I32, (GLA_HALF, GLA_HALF), 0)
    ss = lax.broadcasted_iota(I32, (GLA_HALF, GLA_HALF), 1)
    causal = ss <= rr

    for h in range(A_HEADS):
        hs = slice(h * A_HEAD_DIM, (h + 1) * A_HEAD_DIM)
        v = proj_ref[rows, 2 * d + h * A_HEAD_DIM:2 * d + (h + 1) * A_HEAD_DIM]
        vb = v.astype(BF16)
        st = st_ref[h]
        st_b = st.astype(BF16)
        p_aa = jnp.where(causal, _nt(qa_n[:, hs], ka_n[:, hs]), 0.0).astype(BF16)
        p_bb = jnp.where(causal, _nt(qb_n[:, hs], kb_n[:, hs]), 0.0).astype(BF16)
        p_ba = _nt(qb_x[:, hs], ka_x[:, hs]).astype(BF16)
        inter = _nt(q_dec[:, hs], st_b)
        o_a = _nn(p_aa, vb[h0]) + inter[h0]
        o_b = _nn(p_bb, vb[h1]) + _nn(p_ba, vb[h0]) + inter[h1]
        st_ref[h] = d_end[:, hs] * st + _nn(v.T.astype(BF16), k_end[:, hs])
        gate = jax.nn.silu(proj_ref[rows, 3 * d + h * A_HEAD_DIM:3 * d + (h + 1) * A_HEAD_DIM])
        for half, o in ((0, o_a), (1, o_b)):
            on = o * lax.rsqrt(jnp.mean(o * o, axis=-1, keepdims=True) + NORM_EPS)
            r0 = out_row0 + half * GLA_HALF
            out_ref[r0:r0 + GLA_HALF, hs] = (on * og * gate[half * GLA_HALF:(half + 1) * GLA_HALF]
                                             ).astype(out_ref.dtype)


def _hgrn2_kernel(x_ref, xnext_ref, g_ref, win_ref, alb_ref, og_ref, wout_ref,
                  rg_ref, rwh_ref, rwl_ref, rb_ref, o_ref, xn_ref, eid_ref, wgt_ref, rank_ref, cnt_ref,
                  proj_a, proj_b, gated_a, gated_b, st_ref, *, blocks_per_seq):
    step = pl.program_id(0)
    dyn0 = jnp.minimum(step, 0)
    alb = alb_ref[...]
    e = jnp.exp(alb - jnp.max(alb, axis=0, keepdims=True))
    lb = e[0:1] / jnp.sum(e, axis=0, keepdims=True)
    og = og_ref[...]

    @pl.when(step == 0)
    def _():
        _hgrn2_project(x_ref[0:HG_BLOCK, :], g_ref, win_ref, proj_a)

    @pl.when((2 * step) % blocks_per_seq == 0)
    def _():
        st_ref[...] = jnp.zeros_like(st_ref)

    _hgrn2_project(x_ref[HG_BLOCK:2 * HG_BLOCK, :], g_ref, win_ref, proj_b)
    for sub in range(HG_BLOCK // GLA_BLOCK):
        _gla_block(proj_a, dyn0 + sub * GLA_BLOCK, lb, og, st_ref, gated_a, sub * GLA_BLOCK)
    out_a = x_ref[0:HG_BLOCK, :] + _nn(gated_a[...], wout_ref[...])
    o_ref[0:HG_BLOCK, :] = out_a

    _hgrn2_project(xnext_ref[...], g_ref, win_ref, proj_a)
    for sub in range(HG_BLOCK // GLA_BLOCK):
        _gla_block(proj_b, dyn0 + sub * GLA_BLOCK, lb, og, st_ref, gated_b, sub * GLA_BLOCK)
    out_b = x_ref[HG_BLOCK:2 * HG_BLOCK, :] + _nn(gated_b[...], wout_ref[...])
    o_ref[HG_BLOCK:2 * HG_BLOCK, :] = out_b

    _route_tile(jnp.concatenate([out_a, out_b], axis=0), rg_ref, rwh_ref, rwl_ref, rb_ref,
                xn_ref, eid_ref, wgt_ref, rank_ref, cnt_ref)


def hgrn2_layer(x, norm_g, w_in_bf16, a_lb, onorm_g, w_out_bf16, seq, route_params):
    t, d = x.shape
    n = w_in_bf16.shape[1]
    nblocks = t // HG_BLOCK
    assert 2 * HG_BLOCK == ROUTER_TM
    route_args, route_in, route_specs, route_shapes = _route_plumbing(t, d, ROUTER_TM, *route_params)
    out = pl.pallas_call(
        functools.partial(_hgrn2_kernel, blocks_per_seq=seq // HG_BLOCK),
        grid=(nblocks // 2,),
        in_specs=[pl.BlockSpec((2 * HG_BLOCK, d), lambda i: (i, 0)),
                  pl.BlockSpec((HG_BLOCK, d), lambda i: (jnp.minimum(2 * i + 2, nblocks - 1), 0)),
                  pl.BlockSpec((1, d), lambda i: (0, 0)),
                  pl.BlockSpec((d, n), lambda i: (0, 0)),
                  pl.BlockSpec(a_lb.shape, lambda i: (0, 0)),
                  pl.BlockSpec((1, A_HEAD_DIM), lambda i: (0, 0)),
                  pl.BlockSpec((d, d), lambda i: (0, 0))] + route_in,
        out_specs=[pl.BlockSpec((2 * HG_BLOCK, d), lambda i: (i, 0))] + route_specs,
        out_shape=[jax.ShapeDtypeStruct((t, d), F32)] + route_shapes,
        scratch_shapes=[pltpu.VMEM((HG_BLOCK, n), F32),
                        pltpu.VMEM((HG_BLOCK, n), F32),
                        pltpu.VMEM((HG_BLOCK, d), BF16),
                        pltpu.VMEM((HG_BLOCK, d), BF16),
                        pltpu.VMEM((A_HEADS, A_HEAD_DIM, A_HEAD_DIM), F32)],
        compiler_params=_params("arbitrary"),
        name="hgrn2_layer",
    )(x, x, norm_g.reshape(1, d), w_in_bf16, a_lb, onorm_g.reshape(1, A_HEAD_DIM), w_out_bf16,
      *route_args)
    return out[0], out[1:]


def _route_tile(x, g_ref, wh_ref, wl_ref, b_ref, xn_ref, eid_ref, wgt_ref, rank_ref, cnt_ref):
    tm = x.shape[0]

    @pl.when(pl.program_id(0) == 0)
    def _():
        cnt_ref[...] = jnp.zeros_like(cnt_ref)

    xn = x * lax.rsqrt(jnp.mean(x * x, axis=-1, keepdims=True) + NORM_EPS) * g_ref[...]
    _to_tiles(xn_ref, xn)
    xh = xn.astype(BF16)
    xl = (xn - xh.astype(F32)).astype(BF16)
    wh = wh_ref[...]
    lg = _nt(wh, xh) + _nt(wl_ref[...], xh) + _nt(wh, xl) + b_ref[...]

    gl = lg[0:N_GROUPS]
    r4 = lax.broadcasted_iota(I32, gl.shape, 0)
    gmax = jnp.max(gl, axis=0, keepdims=True)
    grp = jnp.min(jnp.where(gl == gmax, r4, N_GROUPS), axis=0, keepdims=True)
    p_grp = 1.0 / jnp.sum(jnp.exp(gl - gmax), axis=0, keepdims=True)

    fine = lg[8:8 + EXPERTS_PER_GROUP]
    for gi in range(1, N_GROUPS):
        fine = jnp.where(grp == gi, lg[8 + gi * EXPERTS_PER_GROUP:8 + (gi + 1) * EXPERTS_PER_GROUP], fine)
    r8 = lax.broadcasted_iota(I32, fine.shape, 0)
    m1 = jnp.max(fine, axis=0, keepdims=True)
    i1 = jnp.min(jnp.where(fine == m1, r8, EXPERTS_PER_GROUP), axis=0, keepdims=True)
    rest = jnp.where(r8 == i1, -jnp.inf, fine)
    m2 = jnp.max(rest, axis=0, keepdims=True)
    i2 = jnp.min(jnp.where(rest == m2, r8, EXPERTS_PER_GROUP), axis=0, keepdims=True)
    e21 = jnp.exp(m2 - m1)
    t1 = 1.0 / (1.0 + e21)
    wgt_ref[0:1, :] = p_grp * t1
    wgt_ref[1:2, :] = p_grp * (e21 * t1)
    e1 = grp * EXPERTS_PER_GROUP + i1
    e2 = grp * EXPERTS_PER_GROUP + i2
    eid_ref[0:1, :] = e1
    eid_ref[1:2, :] = e2

    r32 = lax.broadcasted_iota(I32, (N_EXPERTS, tm), 0)
    is1 = r32 == e1
    is2 = r32 == e2
    member = jnp.logical_or(is1, is2)
    ta = lax.broadcasted_iota(I32, (tm, tm), 0)
    tb = lax.broadcasted_iota(I32, (tm, tm), 1)
    before = (ta < tb).astype(BF16)
    prior = _nn(member.astype(BF16), before) + cnt_ref[:, 0:1]
    rank_ref[0:1, :] = jnp.sum(jnp.where(is1, prior, 0.0), axis=0, keepdims=True).astype(I32)
    rank_ref[1:2, :] = jnp.sum(jnp.where(is2, prior, 0.0), axis=0, keepdims=True).astype(I32)
    cnt_ref[...] = cnt_ref[...] + jnp.sum(member.astype(F32), axis=1, keepdims=True)


def _proj_res_router_kernel(a_ref, w_ref, r_ref, g_ref, wh_ref, wl_ref, b_ref, o_ref, *route_out):
    h = r_ref[...] + _nn(a_ref[...], w_ref[...])
    o_ref[...] = h
    _route_tile(h, g_ref, wh_ref, wl_ref, b_ref, *route_out)


def _route_plumbing(t, d, tm, g, wg, bg, we, be):
    w_all = jnp.zeros((ROUTER_ROWS, d), F32).at[0:N_GROUPS].set(wg.T).at[8:8 + N_EXPERTS].set(we.T)
    b_all = jnp.zeros((ROUTER_ROWS, 1), F32).at[0:N_GROUPS, 0].set(bg).at[8:8 + N_EXPERTS, 0].set(be)
    wh = w_all.astype(BF16)
    wl = (w_all - wh.astype(F32)).astype(BF16)
    row2 = lambda i: (0, i)
    route_in = [pl.BlockSpec((1, d), lambda i: (0, 0)),
                pl.BlockSpec((ROUTER_ROWS, d), lambda i: (0, 0)),
                pl.BlockSpec((ROUTER_ROWS, d), lambda i: (0, 0)),
                pl.BlockSpec((ROUTER_ROWS, 1), lambda i: (0, 0))]
    route_specs = [pl.BlockSpec((tm * TOKEN_TILE, 128), lambda i: (i, 0)),
                   pl.BlockSpec((2, tm), row2),
                   pl.BlockSpec((2, tm), row2),
                   pl.BlockSpec((2, tm), row2),
                   pl.BlockSpec((N_EXPERTS, 128), lambda i: (0, 0))]
    route_shapes = [jax.ShapeDtypeStruct((t * TOKEN_TILE, 128), F32),
                    jax.ShapeDtypeStruct((2, t), I32),
                    jax.ShapeDtypeStruct((2, t), F32),
                    jax.ShapeDtypeStruct((2, t), I32),
                    jax.ShapeDtypeStruct((N_EXPERTS, 128), F32)]
    return (g.reshape(1, d), wh, wl, b_all), route_in, route_specs, route_shapes


def proj_res_router(a, w, h, route_params, tm=ROUTER_TM):
    t, d = h.shape
    row = pl.BlockSpec((tm, d), lambda i: (i, 0))
    route_args, route_in, route_specs, route_shapes = _route_plumbing(t, d, tm, *route_params)
    out = pl.pallas_call(
        _proj_res_router_kernel,
        grid=(t // tm,),
        in_specs=[row, pl.BlockSpec((d, d), lambda i: (0, 0)), row] + route_in,
        out_specs=[row] + route_specs,
        out_shape=[jax.ShapeDtypeStruct((t, d), F32)] + route_shapes,
        compiler_params=_params("arbitrary"),
        name="proj_res_router",
    )(a, w, h, *route_args)
    return out[0], out[1:]


def _dispatch_kernel(last_ref, pos_ref, x_hbm, o_ref, xbuf, zero_ref, load_sem, row_sem):
    i = pl.program_id(0)
    n = pl.num_programs(0)
    tm = xbuf.shape[1] // TOKEN_TILE
    in_rows = tm * TOKEN_TILE
    tile_rows = EXPERT_TM * TOKEN_TILE
    cur = i % 2
    sem = row_sem.at[0]

    def load(j, slot):
        src = pl.multiple_of(j * in_rows, in_rows)
        return pltpu.make_async_copy(x_hbm.at[pl.ds(src, in_rows)], xbuf.at[slot], load_sem.at[slot])

    def drain(slot):
        for _ in range(2):
            pltpu.make_async_copy(xbuf.at[slot], o_ref.at[pl.ds(0, in_rows)], row_sem.at[slot]).wait()

    @pl.when(i == 0)
    def _():
        load(0, 0).start()
        zero_ref[...] = jnp.zeros_like(zero_ref)

        def tile_fill(e):
            row = pl.multiple_of(last_ref[e] * TOKEN_TILE, tile_rows)
            return pltpu.make_async_copy(zero_ref, o_ref.at[pl.ds(row, tile_rows)], sem)

        for e in range(N_EXPERTS):
            @pl.when(last_ref[e] >= 0)
            def _():
                tile_fill(e).start()
        for e in range(N_EXPERTS):
            @pl.when(last_ref[e] >= 0)
            def _():
                tile_fill(e).wait()

        def spare_fill(j):
            row = pl.multiple_of(j * tile_rows, tile_rows)
            return pltpu.make_async_copy(zero_ref, o_ref.at[pl.ds(row, tile_rows)], sem)

        n_tiles = o_ref.shape[0] // tile_rows
        lax.fori_loop(last_ref[N_EXPERTS], n_tiles, lambda j, c: (spare_fill(j).start(), c)[1], 0)
        lax.fori_loop(last_ref[N_EXPERTS], n_tiles, lambda j, c: (spare_fill(j).wait(), c)[1], 0)

    load(i, cur).wait()

    @pl.when(i > 0)
    def _():
        drain(1 - cur)

    @pl.when(i + 1 < n)
    def _():
        load(i + 1, 1 - cur).start()

    def row_copy(r, k):
        dst = pl.multiple_of(pos_ref[0, 0, k * tm + r] * TOKEN_TILE, TOKEN_TILE)
        return pltpu.make_async_copy(xbuf.at[cur, pl.ds(r * TOKEN_TILE, TOKEN_TILE)],
                                     o_ref.at[pl.ds(dst, TOKEN_TILE)], row_sem.at[cur])

    def issue(blk, carry):
        for u in range(ROW_DMA_UNROLL):
            r = blk * ROW_DMA_UNROLL + u
            row_copy(r, 0).start(priority=u % 2)
            row_copy(r, 1).start(priority=(u + 1) % 2)
        return carry

    lax.fori_loop(0, tm // ROW_DMA_UNROLL, issue, 0)

    @pl.when(i == n - 1)
    def _():
        drain(cur)


def dispatch(xn, pos3, last_tile_row, n_rows, tm=GATHER_TM):
    t = xn.shape[0] // TOKEN_TILE
    return pl.pallas_call(
        _dispatch_kernel,
        grid_spec=pltpu.PrefetchScalarGridSpec(
            num_scalar_prefetch=1,
            grid=(t // tm,),
            in_specs=[pl.BlockSpec((1, 1, 2 * tm), lambda i, last: (i, 0, 0), memory_space=pltpu.SMEM),
                      pl.BlockSpec(memory_space=pl.ANY)],
            out_specs=pl.BlockSpec(memory_space=pl.ANY),
            scratch_shapes=[pltpu.VMEM((2, tm * TOKEN_TILE, 128), xn.dtype),
                            pltpu.VMEM((EXPERT_TM * TOKEN_TILE, 128), xn.dtype),
                            pltpu.SemaphoreType.DMA((2,)),
                            pltpu.SemaphoreType.DMA((2,))]),
        out_shape=jax.ShapeDtypeStruct((n_rows * TOKEN_TILE, 128), xn.dtype),
        compiler_params=_params("arbitrary"),
        name="moe_dispatch",
    )(last_tile_row, pos3, xn)


def _expert_kernel(te_ref, nv_ref, par_ref, nxt_ref, x_hbm, wgu_hbm, wd_hbm, y_ref,
                   xbuf, wgu_f, wd_f, wgu_b, wd_b, xsem, wsem, *, layer):
    i = pl.program_id(0)
    n_valid = nv_ref[0]
    tm = xbuf.shape[1] // TOKEN_TILE
    in_rows = tm * TOKEN_TILE
    valid = i < n_valid
    changed = te_ref[i] != te_ref[jnp.maximum(i - 1, 0)]
    first = jnp.logical_or(i == 0, changed)

    def xload(j):
        slot = j % EXPERT_X_BUFS
        src = pl.multiple_of(j * in_rows, in_rows)
        return pltpu.make_async_copy(x_hbm.at[pl.ds(src, in_rows)], xbuf.at[slot], xsem.at[slot])

    def wload(e, slot):
        return (pltpu.make_async_copy(wgu_hbm.at[layer, e], wgu_f.at[slot], wsem.at[slot]),
                pltpu.make_async_copy(wd_hbm.at[layer, e], wd_f.at[slot], wsem.at[slot]))

    @pl.when(i == 0)
    def _():
        for cp in wload(te_ref[0], par_ref[0]):
            cp.start()
        for j in range(EXPERT_X_BUFS - 1):
            @pl.when(j < n_valid)
            def _():
                xload(j).start()

    @pl.when(i + EXPERT_X_BUFS - 1 < n_valid)
    def _():
        xload(i + EXPERT_X_BUFS - 1).start()

    @pl.when(jnp.logical_and(valid, first))
    def _():
        slot = par_ref[i]
        for cp in wload(te_ref[i], slot):
            cp.wait()

        @pl.when(nxt_ref[i] >= 0)
        def _():
            for cp in wload(nxt_ref[i], 1 - slot):
                cp.start()

        wgu_b[...] = wgu_f[slot].astype(BF16)
        wd_b[...] = wd_f[slot].astype(BF16)

    @pl.when(valid)
    def _():
        xload(i).wait()
        x = _from_tiles(xbuf.at[i % EXPERT_X_BUFS], tm).astype(BF16)
        au = _nn(x, wgu_b[...])
        a = au[:, :D_EXPERT]
        u = au[:, D_EXPERT:]
        mid = (jax.nn.silu(a) * u).astype(BF16)
        _to_tiles(y_ref, _nn(mid, wd_b[...]))

    @pl.when(jnp.logical_not(valid))
    def _():
        y_ref[...] = jnp.zeros_like(y_ref)


def experts(xg, w_gu, w_down, layer, tile_expert, n_valid, seg_parity, seg_next, tm=EXPERT_TM):
    d = D_MODEL
    p = xg.shape[0] // TOKEN_TILE
    n_tiles = p // tm
    return pl.pallas_call(
        functools.partial(_expert_kernel, layer=layer),
        grid_spec=pltpu.PrefetchScalarGridSpec(
            num_scalar_prefetch=4,
            grid=(n_tiles,),
            in_specs=[pl.BlockSpec(memory_space=pl.ANY),
                      pl.BlockSpec(memory_space=pl.ANY),
                      pl.BlockSpec(memory_space=pl.ANY)],
            out_specs=pl.BlockSpec((tm * TOKEN_TILE, 128), lambda i, *_: (i, 0)),
            scratch_shapes=[pltpu.VMEM((EXPERT_X_BUFS, tm * TOKEN_TILE, 128), F32),
                            pltpu.VMEM((2, d, 2 * D_EXPERT), F32),
                            pltpu.VMEM((2, D_EXPERT, d), F32),
                            pltpu.VMEM((d, 2 * D_EXPERT), BF16),
                            pltpu.VMEM((D_EXPERT, d), BF16),
                            pltpu.SemaphoreType.DMA((EXPERT_X_BUFS,)),
                            pltpu.SemaphoreType.DMA((2,))]),
        out_shape=jax.ShapeDtypeStruct((p * TOKEN_TILE, 128), F32),
        compiler_params=_params("arbitrary"),
        name="moe_experts",
    )(tile_expert, n_valid, seg_parity, seg_next, xg, w_gu, w_down)


def _qkv_tail(x, gq_ref, gkv_ref, wqt_ref, wk_ref, wvt_ref, qt_ref, k_ref, vt_ref):
    y = x * lax.rsqrt(jnp.mean(x * x, axis=-1, keepdims=True) + NORM_EPS)
    xq = (y * gq_ref[...]).astype(BF16)
    xkv = (y * gkv_ref[...]).astype(BF16)
    qt_ref[0] = (_nt(wqt_ref[...], xq) * (B_HEAD_DIM ** -0.5 * LOG2E)).astype(qt_ref.dtype)
    k_ref[...] = _nn(xkv, wk_ref[...]).astype(k_ref.dtype)
    vt_ref[0] = _nt(wvt_ref[...], xkv).astype(vt_ref.dtype)


def _combine_kernel(pos_ref, pos_next_ref, h_ref, w_ref, y_ref, *rest, tail):
    if tail == "norm":
        g_ref, o_ref, buf, sem = rest
    else:
        gq_ref, gkv_ref, wqt_ref, wk_ref, wvt_ref, o_ref, qt_ref, k_ref, vt_ref, buf, sem = rest
    i = pl.program_id(0)
    tm = h_ref.shape[0]
    cur = i % 2

    def gather_tile(p_ref, half):
        def row_copy(r, k):
            src = pl.multiple_of(p_ref[0, 0, k * tm + r] * TOKEN_TILE, TOKEN_TILE)
            return pltpu.make_async_copy(y_ref.at[pl.ds(src, TOKEN_TILE)],
                                         buf.at[2 * half + k, pl.ds(r * TOKEN_TILE, TOKEN_TILE)],
                                         sem.at[half])

        def issue(blk, carry):
            for u in range(ROW_DMA_UNROLL):
                r = blk * ROW_DMA_UNROLL + u
                row_copy(r, 0).start(priority=u % 2)
                row_copy(r, 1).start(priority=(u + 1) % 2)
            return carry

        lax.fori_loop(0, tm // ROW_DMA_UNROLL, issue, 0)

    @pl.when(i == 0)
    def _():
        gather_tile(pos_ref, 0)

    @pl.when(i + 1 < pl.num_programs(0))
    def _():
        gather_tile(pos_next_ref, 1 - cur)

    for k in range(2):
        pltpu.make_async_copy(y_ref.at[pl.ds(0, tm * TOKEN_TILE)], buf.at[2 * cur + k], sem.at[cur]).wait()
    w = w_ref[...]
    out = (h_ref[...] + w[:, 0:1] * _from_tiles(buf.at[2 * cur], tm)
           + w[:, 1:2] * _from_tiles(buf.at[2 * cur + 1], tm))
    if tail == "norm":
        o_ref[...] = out * lax.rsqrt(jnp.mean(out * out, axis=-1, keepdims=True) + NORM_EPS) * g_ref[...]
    else:
        o_ref[...] = out
        _qkv_tail(out, gq_ref, gkv_ref, wqt_ref, wk_ref, wvt_ref, qt_ref, k_ref, vt_ref)


def combine(h, wgt_t, pos3, y, tail, tail_args, batch, seq, tm=GATHER_TM):
    t, d = h.shape
    nt = t // tm
    nblk = seq // tm
    row = pl.BlockSpec((tm, d), lambda i: (i, 0))
    vec = pl.BlockSpec((1, d), lambda i: (0, 0))
    full = pl.BlockSpec((d, d), lambda i: (0, 0))
    tr = pl.BlockSpec((1, d, tm), lambda i: (i // nblk, 0, i % nblk))
    if tail == "norm":
        (g,) = tail_args
        extra_in, extra_specs = [g.reshape(1, d)], [vec]
        out_specs, out_shape = row, jax.ShapeDtypeStruct((t, d), F32)
    else:
        g_q, g_kv, wqt, wk, wvt = tail_args
        extra_in = [g_q.reshape(1, d), g_kv.reshape(1, d), wqt, wk, wvt]
        extra_specs = [vec, vec, full, full, full]
        out_specs = [row, tr, row, tr]
        out_shape = [jax.ShapeDtypeStruct((t, d), F32),
                     jax.ShapeDtypeStruct((batch, d, seq), BF16),
                     jax.ShapeDtypeStruct((t, d), BF16),
                     jax.ShapeDtypeStruct((batch, d, seq), BF16)]
    return pl.pallas_call(
        functools.partial(_combine_kernel, tail=tail),
        grid=(nt,),
        in_specs=[pl.BlockSpec((1, 1, 2 * tm), lambda i: (i, 0, 0), memory_space=pltpu.SMEM),
                  pl.BlockSpec((1, 1, 2 * tm), lambda i: (jnp.minimum(i + 1, nt - 1), 0, 0),
                               memory_space=pltpu.SMEM),
                  row,
                  pl.BlockSpec((tm, 2), lambda i: (i, 0)),
                  pl.BlockSpec(memory_space=pl.ANY)] + extra_specs,
        out_specs=out_specs,
        out_shape=out_shape,
        scratch_shapes=[pltpu.VMEM((4, tm * TOKEN_TILE, 128), F32), pltpu.SemaphoreType.DMA((2,))],
        compiler_params=_params("arbitrary"),
        name="moe_combine_" + tail,
    )(pos3, pos3, h, wgt_t, y, *extra_in)


def hier_moe_layer(h, routing, w_gu, w_down, layer, tail, tail_args, batch, seq):
    t, d = h.shape
    xn, eid, wgt, rank, cnt = routing
    counts = cnt[:, 0].astype(I32)
    padded = ((counts + EXPERT_TM - 1) // EXPERT_TM) * EXPERT_TM
    ends = jnp.cumsum(padded)
    offs = ends - padded
    n_rows = 2 * t + N_EXPERTS * EXPERT_TM
    n_tiles = n_rows // EXPERT_TM
    n_valid = (ends[-1] // EXPERT_TM).astype(I32).reshape(1)
    tile_start = jnp.arange(n_tiles, dtype=I32) * EXPERT_TM
    tile_start = jnp.minimum(tile_start, ends[-1] - 1)
    tile_expert = jnp.sum((ends[None, :] <= tile_start[:, None]).astype(I32), axis=1)
    onehot = eid[None] == jnp.arange(N_EXPERTS, dtype=I32)[:, None, None]
    pos = jnp.sum(jnp.where(onehot, offs[:, None, None], 0), axis=0) + rank
    nt = t // GATHER_TM
    pos3 = pos.reshape(2, nt, GATHER_TM).transpose(1, 0, 2).reshape(nt, 1, 2 * GATHER_TM)

    last_tile_row = jnp.concatenate([jnp.where(padded > 0, ends - EXPERT_TM, -1).astype(I32), n_valid])
    xg = dispatch(xn, pos3, last_tile_row, n_rows)
    nonempty = padded > 0
    ordinal = jnp.cumsum(nonempty.astype(I32)) - 1
    ids = jnp.arange(N_EXPERTS, dtype=I32)
    later = jnp.logical_and(nonempty[None, :], ids[None, :] > ids[:, None])
    next_expert = jnp.min(jnp.where(later, ids[None, :], N_EXPERTS), axis=1)
    next_expert = jnp.where(next_expert == N_EXPERTS, -1, next_expert).astype(I32)
    y = experts(xg, w_gu, w_down, layer, tile_expert, n_valid,
                (ordinal[tile_expert] % 2).astype(I32), next_expert[tile_expert])
    return combine(h, wgt.T, pos3, y, tail, tail_args, batch, seq)


def _attn_kernel(q_ref, k_ref, vt_ref, lam_ref, g_ref, o_ref, qq_ref, m_ref, acc_ref,
                 a_ref, c_ref, *sp_refs, lambda_init, heads):
    s_refs, p_refs = sp_refs[:heads], sp_refs[heads:]
    qi = pl.program_id(2)
    tq, tk, hd = ATT_TQ, ATT_TK, 2 * B_HEAD_DIM
    feat = lax.broadcasted_iota(I32, (hd, tq), 0)
    for g in range(heads):
        qt = q_ref[0, g * hd:(g + 1) * hd, :]
        zero = jnp.zeros_like(qt)
        qq_ref[g, :, 0:tq] = jnp.where(feat < B_HEAD_DIM, qt, zero)
        qq_ref[g, :, tq:2 * tq] = jnp.where(feat >= B_HEAD_DIM, qt, zero)
    m_ref[...] = jnp.full_like(m_ref, -jnp.inf)
    acc_ref[...] = jnp.zeros_like(acc_ref)
    ones = jnp.ones((ATT_SUM_ROWS, tk), BF16)
    dyn0 = jnp.minimum(qi, 0)

    def step(j, masked, nk=tk):
        off = pl.multiple_of(j * tk, tk)
        if masked:
            krow = lax.broadcasted_iota(I32, (nk, 2 * tq), 0)
            qcol = lax.broadcasted_iota(I32, (nk, 2 * tq), 1)
            visible = off + krow <= qi * tq + jnp.where(qcol >= tq, qcol - tq, qcol)
        for g in range(heads):
            kb = k_ref[pl.ds(off, nk), g * hd:(g + 1) * hd]
            s = _nn(kb, qq_ref[g])
            if masked:
                s = jnp.where(visible, s, -jnp.inf)
            s_refs[g][0, 0:nk, :] = s
            c_ref[g] = jnp.max(s, axis=0, keepdims=True)
        for g in range(heads):
            m_old = m_ref[g]
            m_new = jnp.maximum(m_old, c_ref[g])
            a_ref[g] = jnp.exp2(m_old - m_new)
            m_ref[g] = m_new
            for c in range(0, nk, ATT_CHUNK):
                p_refs[g][0, c:c + ATT_CHUNK, :] = jnp.exp2(
                    s_refs[g][dyn0, c:c + ATT_CHUNK, :] - m_new).astype(BF16)
        for g in range(heads):
            vtb = vt_ref[0, g * hd:(g + 1) * hd, pl.ds(off, nk)]
            lhs = jnp.concatenate([vtb, ones[:, 0:nk]], axis=0)
            acc_ref[g] = a_ref[g] * acc_ref[g] + _nn(lhs, p_refs[g][dyn0, 0:nk, :])

    n_full = (qi * tq) // tk

    def full_step(j, carry):
        step(j, False)
        return carry

    lax.fori_loop(0, n_full, full_step, 0)

    first_part = (qi * tq) % tk + tq <= tk // 2

    @pl.when(first_part)
    def _():
        step(n_full, True, tk // 2)

    @pl.when(jnp.logical_not(first_part))
    def _():
        step(n_full, True)

    lam = lam_ref[...]
    lam_full = (jnp.exp(jnp.sum(lam[0:1] * lam[1:2], axis=-1, keepdims=True))
                - jnp.exp(jnp.sum(lam[2:3] * lam[3:4], axis=-1, keepdims=True)) + lambda_init)
    for g in range(heads):
        acc = acc_ref[g]
        on = acc[:hd] / acc[hd:hd + 1]
        ot = on[:, :tq] - lam_full * on[:, tq:]
        o = ot.T
        o = o * lax.rsqrt(jnp.mean(o * o, axis=-1, keepdims=True) + SUBLN_EPS) * g_ref[...]
        o_ref[:, g * hd:(g + 1) * hd] = (o * (1.0 - lambda_init)).astype(o_ref.dtype)


def diff_attn(qt, k, vt, lam, subln_g, batch, seq, lambda_init, heads=ATT_HEADS):
    t, d = k.shape
    nq = seq // ATT_TQ
    hd = 2 * B_HEAD_DIM
    w = heads * hd
    return pl.pallas_call(
        functools.partial(_attn_kernel, lambda_init=lambda_init, heads=heads),
        grid=(batch, B_HEADS // heads, nq),
        in_specs=[pl.BlockSpec((1, w, ATT_TQ), lambda b, h, i: (b, h, i)),
                  pl.BlockSpec((seq, w), lambda b, h, i: (b, h)),
                  pl.BlockSpec((1, w, seq), lambda b, h, i: (b, h, 0)),
                  pl.BlockSpec(lam.shape, lambda b, h, i: (0, 0)),
                  pl.BlockSpec((1, hd), lambda b, h, i: (0, 0))],
        out_specs=pl.BlockSpec((ATT_TQ, w), lambda b, h, i: (b * nq + i, h)),
        out_shape=jax.ShapeDtypeStruct((t, d), BF16),
        scratch_shapes=[pltpu.VMEM((heads, hd, 2 * ATT_TQ), BF16),
                        pltpu.VMEM((heads, 1, 2 * ATT_TQ), F32),
                        pltpu.VMEM((heads, hd + ATT_SUM_ROWS, 2 * ATT_TQ), F32),
                        pltpu.VMEM((heads, 1, 2 * ATT_TQ), F32),
                        pltpu.VMEM((heads, 1, 2 * ATT_TQ), F32)]
        + [pltpu.VMEM((1, ATT_TK, 2 * ATT_TQ), F32) for _ in range(heads)]
        + [pltpu.VMEM((1, ATT_TK, 2 * ATT_TQ), BF16) for _ in range(heads)],
        compiler_params=_params("parallel", "parallel", "arbitrary"),
        name="diff_attn",
    )(qt, k, vt, lam, subln_g.reshape(1, hd))


def kernel(x, a_norm_g, a_w_in, a_lb, a_onorm_g, a_w_out, kv_norm_g, w_kv, b_norm_g, b_w_q, b_lam,
           b_subln_g, b_w_out, ffn_norm_g, router_g_w, router_g_b, router_e_w, router_e_b,
           expert_w_gu, expert_w_down, final_norm_g):
    batch, seq, d = x.shape
    assert d == D_MODEL and a_norm_g.shape[0] == 1 and b_norm_g.shape[0] == 1
    assert seq % max(2 * HG_BLOCK, ATT_TK, ROUTER_TM) == 0
    t = batch * seq
    h = x.reshape(t, d)

    h, routing = hgrn2_layer(h, a_norm_g[0], a_w_in[0].astype(BF16), a_lb, a_onorm_g[0],
                             a_w_out[0].astype(BF16), seq,
                             (ffn_norm_g[0], router_g_w[0], router_g_b[0], router_e_w[0], router_e_b[0]))
    qkv_args = (b_norm_g[0], kv_norm_g, b_w_q[0].T.astype(BF16),
                w_kv[:, :d].astype(BF16), w_kv[:, d:].T.astype(BF16))
    h, qt, k, vt = hier_moe_layer(h, routing, expert_w_gu, expert_w_down, 0, "qkv", qkv_args, batch, seq)

    lambda_init = 0.8 - 0.6 * math.exp(-0.3 * 1)
    o = diff_attn(qt, k, vt, b_lam[0], b_subln_g[0], batch, seq, lambda_init)
    h, routing = proj_res_router(o, b_w_out[0].astype(BF16), h,
                                 (ffn_norm_g[1], router_g_w[1], router_g_b[1], router_e_w[1], router_e_b[1]))
    h = hier_moe_layer(h, routing, expert_w_gu, expert_w_down, 1, "norm", (final_norm_g,), batch, seq)
    return h.reshape(batch, seq, d)
```

```python
import functools
import math

import jax
import jax.numpy as jnp
from jax import lax
from jax.experimental import pallas as pl
from jax.experimental.pallas import tpu as pltpu

F32 = jnp.float32
BF16 = jnp.bfloat16
I32 = jnp.int32

D_MODEL = 1024
A_HEADS = 8
A_HEAD_DIM = 128
B_HEADS = 8
B_HEAD_DIM = 64
N_GROUPS = 4
EXPERTS_PER_GROUP = 8
N_EXPERTS = N_GROUPS * EXPERTS_PER_GROUP
D_EXPERT = 512
NORM_EPS = 1e-6
SUBLN_EPS = 1e-5
LOG2E = 1.4426950408889634

GLA_BLOCK = 128
GLA_HALF = GLA_BLOCK // 2
HG_BLOCK = 256
ATT_TQ = 256
ATT_TK = 512
ATT_HEADS = 8
ATT_CHUNK = 128
ATT_SUM_ROWS = 16
ROUTER_TM = 512
ROUTER_ROWS = 40
EXPERT_TM = 512
EXPERT_X_BUFS = 3
DISPATCH_TM = 1024
COMBINE_TM = 512
ROW_DMA_UNROLL = 8
TOKEN_TILE = 8
VMEM_LIMIT = 56 * 1024 * 1024


def _nt(a, b):
    return lax.dot_general(a, b, (((1,), (1,)), ((), ())), preferred_element_type=F32)


def _nn(a, b):
    return jnp.dot(a, b, preferred_element_type=F32)


def _to_tiles(ref, x):
    tm = x.shape[0]
    for s in range(TOKEN_TILE):
        ref[pl.ds(s, tm, stride=TOKEN_TILE), :] = x[:, s * 128:(s + 1) * 128]


def _from_tiles(ref, tm):
    return jnp.concatenate([ref[pl.ds(s, tm, stride=TOKEN_TILE), :] for s in range(TOKEN_TILE)], axis=1)


def _params(*sem, flags=None):
    return pltpu.CompilerParams(dimension_semantics=sem, vmem_limit_bytes=VMEM_LIMIT, flags=flags)


def _hgrn2_project(x, g_ref, w_ref, proj_ref):
    d = x.shape[1]
    y = x * lax.rsqrt(jnp.mean(x * x, axis=-1, keepdims=True) + NORM_EPS)
    xn = (y * g_ref[...]).astype(BF16)
    for c in range(w_ref.shape[1] // d):
        proj_ref[:, c * d:(c + 1) * d] = _nn(xn, w_ref[:, c * d:(c + 1) * d])


def _gla_block(proj_ref, row0, lb, og, st_ref, out_ref, out_row0):
    d = D_MODEL
    rows = pl.ds(pl.multiple_of(row0, GLA_BLOCK), GLA_BLOCK)
    fz = proj_ref[rows, d:2 * d]
    logf = jnp.log(lb + (1.0 - lb) * jax.nn.sigmoid(fz))
    kk = (1.0 - lb) * jax.nn.sigmoid(-fz)
    qq = jax.nn.silu(proj_ref[rows, 0:d])

    row = lax.broadcasted_iota(I32, logf.shape, 0)
    b = logf
    shift = 1
    while shift < GLA_BLOCK:
        b = b + jnp.where(row >= shift, pltpu.roll(b, shift, axis=0), 0.0)
        shift *= 2

    h0, h1 = slice(0, GLA_HALF), slice(GLA_HALF, GLA_BLOCK)
    b_a_mid = b[GLA_HALF // 2 - 1:GLA_HALF // 2]
    b_a_end = b[GLA_HALF - 1:GLA_HALF]
    b_b_mid = b[GLA_HALF + GLA_HALF // 2 - 1:GLA_HALF + GLA_HALF // 2]
    b_end = b[GLA_BLOCK - 1:GLA_BLOCK]

    qa_n = (qq[h0] * jnp.exp(b[h0] - b_a_mid)).astype(BF16)
    ka_n = (kk[h0] * jnp.exp(b_a_mid - b[h0])).astype(BF16)
    qb_n = (qq[h1] * jnp.exp(b[h1] - b_b_mid)).astype(BF16)
    kb_n = (kk[h1] * jnp.exp(b_b_mid - b[h1])).astype(BF16)
    qb_x = (qq[h1] * jnp.exp(b[h1] - b_a_end)).astype(BF16)
    ka_x = (kk[h0] * jnp.exp(b_a_end - b[h0])).astype(BF16)
    q_dec = (qq * jnp.exp(b)).astype(BF16)
    k_end = (kk * jnp.exp(b_end - b)).astype(BF16)
    d_end = jnp.exp(b_end)

    rr = lax.broadcasted_iota(I32, (GLA_HALF, GLA_HALF), 0)
    ss = lax.broadcasted_iota(I32, (GLA_HALF, GLA_HALF), 1)
    causal = ss <= rr

    for h in range(A_HEADS):
        hs = slice(h * A_HEAD_DIM, (h + 1) * A_HEAD_DIM)
        v = proj_ref[rows, 2 * d + h * A_HEAD_DIM:2 * d + (h + 1) * A_HEAD_DIM]
        vb = v.astype(BF16)
        st = st_ref[h]
        st_b = st.astype(BF16)
        p_aa = jnp.where(causal, _nt(qa_n[:, hs], ka_n[:, hs]), 0.0).astype(BF16)
        p_bb = jnp.where(causal, _nt(qb_n[:, hs], kb_n[:, hs]), 0.0).astype(BF16)
        p_ba = _nt(qb_x[:, hs], ka_x[:, hs]).astype(BF16)
        inter = _nt(q_dec[:, hs], st_b)
        o_a = _nn(p_aa, vb[h0]) + inter[h0]
        o_b = _nn(p_bb, vb[h1]) + _nn(p_ba, vb[h0]) + inter[h1]
        st_ref[h] = d_end[:, hs] * st + _nn(v.T.astype(BF16), k_end[:, hs])
        gate = jax.nn.silu(proj_ref[rows, 3 * d + h * A_HEAD_DIM:3 * d + (h + 1) * A_HEAD_DIM])
        for half, o in ((0, o_a), (1, o_b)):
            on = o * lax.rsqrt(jnp.mean(o * o, axis=-1, keepdims=True) + NORM_EPS)
            r0 = out_row0 + half * GLA_HALF
            out_ref[r0:r0 + GLA_HALF, hs] = (on * og * gate[half * GLA_HALF:(half + 1) * GLA_HALF]
                                             ).astype(out_ref.dtype)


def _hgrn2_kernel(x_ref, xnext_ref, g_ref, win_ref, alb_ref, og_ref, wout_ref,
                  rg_ref, rwh_ref, rwl_ref, rb_ref, o_ref, xn_ref, eid_ref, wgt_ref, rank_ref, cnt_ref,
                  proj_a, proj_b, gated_a, gated_b, st_ref, *, blocks_per_seq):
    step = pl.program_id(0)
    dyn0 = jnp.minimum(step, 0)
    alb = alb_ref[...]
    e = jnp.exp(alb - jnp.max(alb, axis=0, keepdims=True))
    lb = e[0:1] / jnp.sum(e, axis=0, keepdims=True)
    og = og_ref[...]

    @pl.when(step == 0)
    def _():
        _hgrn2_project(x_ref[0:HG_BLOCK, :], g_ref, win_ref, proj_a)

    @pl.when((2 * step) % blocks_per_seq == 0)
    def _():
        st_ref[...] = jnp.zeros_like(st_ref)

    _hgrn2_project(x_ref[HG_BLOCK:2 * HG_BLOCK, :], g_ref, win_ref, proj_b)
    for sub in range(HG_BLOCK // GLA_BLOCK):
        _gla_block(proj_a, dyn0 + sub * GLA_BLOCK, lb, og, st_ref, gated_a, sub * GLA_BLOCK)
    out_a = x_ref[0:HG_BLOCK, :] + _nn(gated_a[...], wout_ref[...])
    o_ref[0:HG_BLOCK, :] = out_a

    _hgrn2_project(xnext_ref[...], g_ref, win_ref, proj_a)
    for sub in range(HG_BLOCK // GLA_BLOCK):
        _gla_block(proj_b, dyn0 + sub * GLA_BLOCK, lb, og, st_ref, gated_b, sub * GLA_BLOCK)
    out_b = x_ref[HG_BLOCK:2 * HG_BLOCK, :] + _nn(gated_b[...], wout_ref[...])
    o_ref[HG_BLOCK:2 * HG_BLOCK, :] = out_b

    _route_tile(jnp.concatenate([out_a, out_b], axis=0), rg_ref, rwh_ref, rwl_ref, rb_ref,
                xn_ref, eid_ref, wgt_ref, rank_ref, cnt_ref)


def hgrn2_layer(x, norm_g, w_in_bf16, a_lb, onorm_g, w_out_bf16, seq, route_params):
    t, d = x.shape
    n = w_in_bf16.shape[1]
    nblocks = t // HG_BLOCK
    assert 2 * HG_BLOCK == ROUTER_TM
    route_args, route_in, route_specs, route_shapes = _route_plumbing(t, d, ROUTER_TM, *route_params)
    out = pl.pallas_call(
        functools.partial(_hgrn2_kernel, blocks_per_seq=seq // HG_BLOCK),
        grid=(nblocks // 2,),
        in_specs=[pl.BlockSpec((2 * HG_BLOCK, d), lambda i: (i, 0)),
                  pl.BlockSpec((HG_BLOCK, d), lambda i: (jnp.minimum(2 * i + 2, nblocks - 1), 0)),
                  pl.BlockSpec((1, d), lambda i: (0, 0)),
                  pl.BlockSpec((d, n), lambda i: (0, 0)),
                  pl.BlockSpec(a_lb.shape, lambda i: (0, 0)),
                  pl.BlockSpec((1, A_HEAD_DIM), lambda i: (0, 0)),
                  pl.BlockSpec((d, d), lambda i: (0, 0))] + route_in,
        out_specs=[pl.BlockSpec((2 * HG_BLOCK, d), lambda i: (i, 0))] + route_specs,
        out_shape=[jax.ShapeDtypeStruct((t, d), F32)] + route_shapes,
        scratch_shapes=[pltpu.VMEM((HG_BLOCK, n), F32),
                        pltpu.VMEM((HG_BLOCK, n), F32),
                        pltpu.VMEM((HG_BLOCK, d), BF16),
                        pltpu.VMEM((HG_BLOCK, d), BF16),
                        pltpu.VMEM((A_HEADS, A_HEAD_DIM, A_HEAD_DIM), F32)],
        compiler_params=_params("arbitrary"),
        name="hgrn2_layer",
    )(x, x, norm_g.reshape(1, d), w_in_bf16, a_lb, onorm_g.reshape(1, A_HEAD_DIM), w_out_bf16,
      *route_args)
    return out[0], out[1:]


def _route_tile(x, g_ref, wh_ref, wl_ref, b_ref, xn_ref, eid_ref, wgt_ref, rank_ref, cnt_ref):
    tm = x.shape[0]

    @pl.when(pl.program_id(0) == 0)
    def _():
        cnt_ref[...] = jnp.zeros_like(cnt_ref)

    xn = x * lax.rsqrt(jnp.mean(x * x, axis=-1, keepdims=True) + NORM_EPS) * g_ref[...]
    _to_tiles(xn_ref, xn)
    xh = xn.astype(BF16)
    xl = (xn - xh.astype(F32)).astype(BF16)
    wh = wh_ref[...]
    lg = _nt(wh, xh) + _nt(wl_ref[...], xh) + _nt(wh, xl) + b_ref[...]

    gl = lg[0:N_GROUPS]
    r4 = lax.broadcasted_iota(I32, gl.shape, 0)
    gmax = jnp.max(gl, axis=0, keepdims=True)
    grp = jnp.min(jnp.where(gl == gmax, r4, N_GROUPS), axis=0, keepdims=True)
    p_grp = 1.0 / jnp.sum(jnp.exp(gl - gmax), axis=0, keepdims=True)

    fine = lg[8:8 + EXPERTS_PER_GROUP]
    for gi in range(1, N_GROUPS):
        fine = jnp.where(grp == gi, lg[8 + gi * EXPERTS_PER_GROUP:8 + (gi + 1) * EXPERTS_PER_GROUP], fine)
    r8 = lax.broadcasted_iota(I32, fine.shape, 0)
    m1 = jnp.max(fine, axis=0, keepdims=True)
    i1 = jnp.min(jnp.where(fine == m1, r8, EXPERTS_PER_GROUP), axis=0, keepdims=True)
    rest = jnp.where(r8 == i1, -jnp.inf, fine)
    m2 = jnp.max(rest, axis=0, keepdims=True)
    i2 = jnp.min(jnp.where(rest == m2, r8, EXPERTS_PER_GROUP), axis=0, keepdims=True)
    e21 = jnp.exp(m2 - m1)
    t1 = 1.0 / (1.0 + e21)
    wgt_ref[0:1, :] = p_grp * t1
    wgt_ref[1:2, :] = p_grp * (e21 * t1)
    e1 = grp * EXPERTS_PER_GROUP + i1
    e2 = grp * EXPERTS_PER_GROUP + i2
    eid_ref[0:1, :] = e1
    eid_ref[1:2, :] = e2

    r32 = lax.broadcasted_iota(I32, (N_EXPERTS, tm), 0)
    is1 = r32 == e1
    is2 = r32 == e2
    member = jnp.logical_or(is1, is2)
    ta = lax.broadcasted_iota(I32, (tm, tm), 0)
    tb = lax.broadcasted_iota(I32, (tm, tm), 1)
    before = (ta < tb).astype(BF16)
    prior = _nn(member.astype(BF16), before) + cnt_ref[:, 0:1]
    rank_ref[0:1, :] = jnp.sum(jnp.where(is1, prior, 0.0), axis=0, keepdims=True).astype(I32)
    rank_ref[1:2, :] = jnp.sum(jnp.where(is2, prior, 0.0), axis=0, keepdims=True).astype(I32)
    cnt_ref[...] = cnt_ref[...] + jnp.sum(member.astype(F32), axis=1, keepdims=True)


def _proj_res_router_kernel(a_ref, w_ref, r_ref, g_ref, wh_ref, wl_ref, b_ref, o_ref, *route_out):
    h = r_ref[...] + _nn(a_ref[...], w_ref[...])
    o_ref[...] = h
    _route_tile(h, g_ref, wh_ref, wl_ref, b_ref, *route_out)


def _route_plumbing(t, d, tm, g, wg, bg, we, be):
    w_all = jnp.zeros((ROUTER_ROWS, d), F32).at[0:N_GROUPS].set(wg.T).at[8:8 + N_EXPERTS].set(we.T)
    b_all = jnp.zeros((ROUTER_ROWS, 1), F32).at[0:N_GROUPS, 0].set(bg).at[8:8 + N_EXPERTS, 0].set(be)
    wh = w_all.astype(BF16)
    wl = (w_all - wh.astype(F32)).astype(BF16)
    row2 = lambda i: (0, i)
    route_in = [pl.BlockSpec((1, d), lambda i: (0, 0)),
                pl.BlockSpec((ROUTER_ROWS, d), lambda i: (0, 0)),
                pl.BlockSpec((ROUTER_ROWS, d), lambda i: (0, 0)),
                pl.BlockSpec((ROUTER_ROWS, 1), lambda i: (0, 0))]
    route_specs = [pl.BlockSpec((tm * TOKEN_TILE, 128), lambda i: (i, 0)),
                   pl.BlockSpec((2, tm), row2),
                   pl.BlockSpec((2, tm), row2),
                   pl.BlockSpec((2, tm), row2),
                   pl.BlockSpec((N_EXPERTS, 128), lambda i: (0, 0))]
    route_shapes = [jax.ShapeDtypeStruct((t * TOKEN_TILE, 128), F32),
                    jax.ShapeDtypeStruct((2, t), I32),
                    jax.ShapeDtypeStruct((2, t), F32),
                    jax.ShapeDtypeStruct((2, t), I32),
                    jax.ShapeDtypeStruct((N_EXPERTS, 128), F32)]
    return (g.reshape(1, d), wh, wl, b_all), route_in, route_specs, route_shapes


def proj_res_router(a, w, h, route_params, tm=ROUTER_TM):
    t, d = h.shape
    row = pl.BlockSpec((tm, d), lambda i: (i, 0))
    route_args, route_in, route_specs, route_shapes = _route_plumbing(t, d, tm, *route_params)
    out = pl.pallas_call(
        _proj_res_router_kernel,
        grid=(t // tm,),
        in_specs=[row, pl.BlockSpec((d, d), lambda i: (0, 0)), row] + route_in,
        out_specs=[row] + route_specs,
        out_shape=[jax.ShapeDtypeStruct((t, d), F32)] + route_shapes,
        compiler_params=_params("arbitrary"),
        name="proj_res_router",
    )(a, w, h, *route_args)
    return out[0], out[1:]


def _dispatch_kernel(last_ref, pos_ref, x_hbm, o_ref, xbuf, zero_ref, load_sem, row_sem):
    i = pl.program_id(0)
    n = pl.num_programs(0)
    tm = xbuf.shape[1] // TOKEN_TILE
    in_rows = tm * TOKEN_TILE
    tile_rows = EXPERT_TM * TOKEN_TILE
    cur = i % 2
    sem = row_sem.at[0]

    def load(j, slot):
        src = pl.multiple_of(j * in_rows, in_rows)
        return pltpu.make_async_copy(x_hbm.at[pl.ds(src, in_rows)], xbuf.at[slot], load_sem.at[slot])

    def drain(slot):
        for _ in range(2):
            pltpu.make_async_copy(xbuf.at[slot], o_ref.at[pl.ds(0, in_rows)], row_sem.at[slot]).wait()

    @pl.when(i == 0)
    def _():
        load(0, 0).start()
        zero_ref[...] = jnp.zeros_like(zero_ref)

        def tile_fill(e):
            row = pl.multiple_of(last_ref[e] * TOKEN_TILE, tile_rows)
            return pltpu.make_async_copy(zero_ref, o_ref.at[pl.ds(row, tile_rows)], sem)

        for e in range(N_EXPERTS):
            @pl.when(last_ref[e] >= 0)
            def _():
                tile_fill(e).start()
        for e in range(N_EXPERTS):
            @pl.when(last_ref[e] >= 0)
            def _():
                tile_fill(e).wait()

        def spare_fill(j):
            row = pl.multiple_of(j * tile_rows, tile_rows)
            return pltpu.make_async_copy(zero_ref, o_ref.at[pl.ds(row, tile_rows)], sem)

        n_tiles = o_ref.shape[0] // tile_rows
        lax.fori_loop(last_ref[N_EXPERTS], n_tiles, lambda j, c: (spare_fill(j).start(), c)[1], 0)
        lax.fori_loop(last_ref[N_EXPERTS], n_tiles, lambda j, c: (spare_fill(j).wait(), c)[1], 0)

    load(i, cur).wait()

    @pl.when(i > 0)
    def _():
        drain(1 - cur)

    @pl.when(i + 1 < n)
    def _():
        load(i + 1, 1 - cur).start()

    def row_copy(r, k):
        dst = pl.multiple_of(pos_ref[0, 0, k * tm + r] * TOKEN_TILE, TOKEN_TILE)
        return pltpu.make_async_copy(xbuf.at[cur, pl.ds(r * TOKEN_TILE, TOKEN_TILE)],
                                     o_ref.at[pl.ds(dst, TOKEN_TILE)], row_sem.at[cur])

    def issue(blk, carry):
        for u in range(ROW_DMA_UNROLL):
            r = blk * ROW_DMA_UNROLL + u
            row_copy(r, 0).start(priority=u % 2)
            row_copy(r, 1).start(priority=(u + 1) % 2)
        return carry

    lax.fori_loop(0, tm // ROW_DMA_UNROLL, issue, 0)

    @pl.when(i == n - 1)
    def _():
        drain(cur)


def dispatch(xn, pos3, last_tile_row, n_rows, tm=DISPATCH_TM):
    t = xn.shape[0] // TOKEN_TILE
    return pl.pallas_call(
        _dispatch_kernel,
        grid_spec=pltpu.PrefetchScalarGridSpec(
            num_scalar_prefetch=1,
            grid=(t // tm,),
            in_specs=[pl.BlockSpec((1, 1, 2 * tm), lambda i, last: (i, 0, 0), memory_space=pltpu.SMEM),
                      pl.BlockSpec(memory_space=pl.ANY)],
            out_specs=pl.BlockSpec(memory_space=pl.ANY),
            scratch_shapes=[pltpu.VMEM((2, tm * TOKEN_TILE, 128), xn.dtype),
                            pltpu.VMEM((EXPERT_TM * TOKEN_TILE, 128), xn.dtype),
                            pltpu.SemaphoreType.DMA((2,)),
                            pltpu.SemaphoreType.DMA((2,))]),
        out_shape=jax.ShapeDtypeStruct((n_rows * TOKEN_TILE, 128), xn.dtype),
        compiler_params=_params("arbitrary"),
        name="moe_dispatch",
    )(last_tile_row, pos3, xn)


def _expert_kernel(te_ref, nv_ref, par_ref, nxt_ref, x_hbm, wgu_hbm, wd_hbm, y_ref,
                   xbuf, wgu_f, wd_f, wgu_b, wd_b, xsem, wsem, *, layer):
    i = pl.program_id(0)
    n_valid = nv_ref[0]
    tm = xbuf.shape[1] // TOKEN_TILE
    in_rows = tm * TOKEN_TILE
    valid = i < n_valid
    changed = te_ref[i] != te_ref[jnp.maximum(i - 1, 0)]
    first = jnp.logical_or(i == 0, changed)

    def xload(j):
        slot = j % EXPERT_X_BUFS
        src = pl.multiple_of(j * in_rows, in_rows)
        return pltpu.make_async_copy(x_hbm.at[pl.ds(src, in_rows)], xbuf.at[slot], xsem.at[slot])

    def wload(e, slot):
        return (pltpu.make_async_copy(wgu_hbm.at[layer, e], wgu_f.at[slot], wsem.at[slot]),
                pltpu.make_async_copy(wd_hbm.at[layer, e], wd_f.at[slot], wsem.at[slot]))

    @pl.when(i == 0)
    def _():
        for cp in wload(te_ref[0], par_ref[0]):
            cp.start()
        for j in range(EXPERT_X_BUFS - 1):
            @pl.when(j < n_valid)
            def _():
                xload(j).start()

    @pl.when(i + EXPERT_X_BUFS - 1 < n_valid)
    def _():
        xload(i + EXPERT_X_BUFS - 1).start()

    @pl.when(jnp.logical_and(valid, first))
    def _():
        slot = par_ref[i]
        for cp in wload(te_ref[i], slot):
            cp.wait()

        @pl.when(nxt_ref[i] >= 0)
        def _():
            for cp in wload(nxt_ref[i], 1 - slot):
                cp.start()

        wgu_b[...] = wgu_f[slot].astype(BF16)
        wd_b[...] = wd_f[slot].astype(BF16)

    @pl.when(valid)
    def _():
        xload(i).wait()
        x = _from_tiles(xbuf.at[i % EXPERT_X_BUFS], tm).astype(BF16)
        au = _nn(x, wgu_b[...])
        a = au[:, :D_EXPERT]
        u = au[:, D_EXPERT:]
        mid = (jax.nn.silu(a) * u).astype(BF16)
        _to_tiles(y_ref, _nn(mid, wd_b[...]))

    @pl.when(jnp.logical_not(valid))
    def _():
        y_ref[...] = jnp.zeros_like(y_ref)


def experts(xg, w_gu, w_down, layer, tile_expert, n_valid, seg_parity, seg_next, tm=EXPERT_TM):
    d = D_MODEL
    p = xg.shape[0] // TOKEN_TILE
    n_tiles = p // tm
    return pl.pallas_call(
        functools.partial(_expert_kernel, layer=layer),
        grid_spec=pltpu.PrefetchScalarGridSpec(
            num_scalar_prefetch=4,
            grid=(n_tiles,),
            in_specs=[pl.BlockSpec(memory_space=pl.ANY),
                      pl.BlockSpec(memory_space=pl.ANY),
                      pl.BlockSpec(memory_space=pl.ANY)],
            out_specs=pl.BlockSpec((tm * TOKEN_TILE, 128), lambda i, *_: (i, 0)),
            scratch_shapes=[pltpu.VMEM((EXPERT_X_BUFS, tm * TOKEN_TILE, 128), F32),
                            pltpu.VMEM((2, d, 2 * D_EXPERT), F32),
                            pltpu.VMEM((2, D_EXPERT, d), F32),
                            pltpu.VMEM((d, 2 * D_EXPERT), BF16),
                            pltpu.VMEM((D_EXPERT, d), BF16),
                            pltpu.SemaphoreType.DMA((EXPERT_X_BUFS,)),
                            pltpu.SemaphoreType.DMA((2,))]),
        out_shape=jax.ShapeDtypeStruct((p * TOKEN_TILE, 128), F32),
        compiler_params=_params("arbitrary"),
        name="moe_experts",
    )(tile_expert, n_valid, seg_parity, seg_next, xg, w_gu, w_down)


def _qkv_tail(x, gq_ref, gkv_ref, wqt_ref, wk_ref, wvt_ref, qt_ref, k_ref, vt_ref):
    y = x * lax.rsqrt(jnp.mean(x * x, axis=-1, keepdims=True) + NORM_EPS)
    xq = (y * gq_ref[...]).astype(BF16)
    xkv = (y * gkv_ref[...]).astype(BF16)
    qt_ref[0] = (_nt(wqt_ref[...], xq) * (B_HEAD_DIM ** -0.5 * LOG2E)).astype(qt_ref.dtype)
    k_ref[...] = _nn(xkv, wk_ref[...]).astype(k_ref.dtype)
    vt_ref[0] = _nt(wvt_ref[...], xkv).astype(vt_ref.dtype)


def _combine_kernel(pos_ref, pos_next_ref, h_ref, w_ref, y_ref, *rest, tail):
    if tail == "norm":
        g_ref, o_ref, buf, sem = rest
    else:
        gq_ref, gkv_ref, wqt_ref, wk_ref, wvt_ref, o_ref, qt_ref, k_ref, vt_ref, buf, sem = rest
    i = pl.program_id(0)
    tm = h_ref.shape[0]
    cur = i % 2

    def gather_tile(p_ref, half):
        def row_copy(r, k):
            src = pl.multiple_of(p_ref[0, 0, k * tm + r] * TOKEN_TILE, TOKEN_TILE)
            return pltpu.make_async_copy(y_ref.at[pl.ds(src, TOKEN_TILE)],
                                         buf.at[2 * half + k, pl.ds(r * TOKEN_TILE, TOKEN_TILE)],
                                         sem.at[half])

        def issue(blk, carry):
            for u in range(ROW_DMA_UNROLL):
                r = blk * ROW_DMA_UNROLL + u
                row_copy(r, 0).start(priority=u % 2)
                row_copy(r, 1).start(priority=(u + 1) % 2)
            return carry

        lax.fori_loop(0, tm // ROW_DMA_UNROLL, issue, 0)

    @pl.when(i == 0)
    def _():
        gather_tile(pos_ref, 0)

    @pl.when(i + 1 < pl.num_programs(0))
    def _():
        gather_tile(pos_next_ref, 1 - cur)

    for k in range(2):
        pltpu.make_async_copy(y_ref.at[pl.ds(0, tm * TOKEN_TILE)], buf.at[2 * cur + k], sem.at[cur]).wait()
    w = w_ref[...]
    out = (h_ref[...] + w[:, 0:1] * _from_tiles(buf.at[2 * cur], tm)
           + w[:, 1:2] * _from_tiles(buf.at[2 * cur + 1], tm))
    if tail == "norm":
        o_ref[...] = out * lax.rsqrt(jnp.mean(out * out, axis=-1, keepdims=True) + NORM_EPS) * g_ref[...]
    else:
        o_ref[...] = out
        _qkv_tail(out, gq_ref, gkv_ref, wqt_ref, wk_ref, wvt_ref, qt_ref, k_ref, vt_ref)


def combine(h, wgt_t, pos3, y, tail, tail_args, batch, seq, tm=COMBINE_TM):
    t, d = h.shape
    nt = t // tm
    nblk = seq // tm
    row = pl.BlockSpec((tm, d), lambda i: (i, 0))
    vec = pl.BlockSpec((1, d), lambda i: (0, 0))
    full = pl.BlockSpec((d, d), lambda i: (0, 0))
    tr = pl.BlockSpec((1, d, tm), lambda i: (i // nblk, 0, i % nblk))
    if tail == "norm":
        (g,) = tail_args
        extra_in, extra_specs = [g.reshape(1, d)], [vec]
        out_specs, out_shape = row, jax.ShapeDtypeStruct((t, d), F32)
    else:
        g_q, g_kv, wqt, wk, wvt = tail_args
        extra_in = [g_q.reshape(1, d), g_kv.reshape(1, d), wqt, wk, wvt]
        extra_specs = [vec, vec, full, full, full]
        out_specs = [row, tr, row, tr]
        out_shape = [jax.ShapeDtypeStruct((t, d), F32),
                     jax.ShapeDtypeStruct((batch, d, seq), BF16),
                     jax.ShapeDtypeStruct((t, d), BF16),
                     jax.ShapeDtypeStruct((batch, d, seq), BF16)]
    return pl.pallas_call(
        functools.partial(_combine_kernel, tail=tail),
        grid=(nt,),
        in_specs=[pl.BlockSpec((1, 1, 2 * tm), lambda i: (i, 0, 0), memory_space=pltpu.SMEM),
                  pl.BlockSpec((1, 1, 2 * tm), lambda i: (jnp.minimum(i + 1, nt - 1), 0, 0),
                               memory_space=pltpu.SMEM),
                  row,
                  pl.BlockSpec((tm, 2), lambda i: (i, 0)),
                  pl.BlockSpec(memory_space=pl.ANY)] + extra_specs,
        out_specs=out_specs,
        out_shape=out_shape,
        scratch_shapes=[pltpu.VMEM((4, tm * TOKEN_TILE, 128), F32), pltpu.SemaphoreType.DMA((2,))],
        compiler_params=_params("arbitrary"),
        name="moe_combine_" + tail,
    )(pos3, pos3, h, wgt_t, y, *extra_in)


def hier_moe_layer(h, routing, w_gu, w_down, layer, tail, tail_args, batch, seq):
    t, d = h.shape
    xn, eid, wgt, rank, cnt = routing
    counts = cnt[:, 0].astype(I32)
    padded = ((counts + EXPERT_TM - 1) // EXPERT_TM) * EXPERT_TM
    ends = jnp.cumsum(padded)
    offs = ends - padded
    n_rows = 2 * t + N_EXPERTS * EXPERT_TM
    n_tiles = n_rows // EXPERT_TM
    n_valid = (ends[-1] // EXPERT_TM).astype(I32).reshape(1)
    tile_start = jnp.arange(n_tiles, dtype=I32) * EXPERT_TM
    tile_start = jnp.minimum(tile_start, ends[-1] - 1)
    tile_expert = jnp.sum((ends[None, :] <= tile_start[:, None]).astype(I32), axis=1)
    onehot = eid[None] == jnp.arange(N_EXPERTS, dtype=I32)[:, None, None]
    pos = jnp.sum(jnp.where(onehot, offs[:, None, None], 0), axis=0) + rank

    def tiled(tm):
        return pos.reshape(2, t // tm, tm).transpose(1, 0, 2).reshape(t // tm, 1, 2 * tm)

    last_tile_row = jnp.concatenate([jnp.where(padded > 0, ends - EXPERT_TM, -1).astype(I32), n_valid])
    xg = dispatch(xn, tiled(DISPATCH_TM), last_tile_row, n_rows)
    nonempty = padded > 0
    ordinal = jnp.cumsum(nonempty.astype(I32)) - 1
    ids = jnp.arange(N_EXPERTS, dtype=I32)
    later = jnp.logical_and(nonempty[None, :], ids[None, :] > ids[:, None])
    next_expert = jnp.min(jnp.where(later, ids[None, :], N_EXPERTS), axis=1)
    next_expert = jnp.where(next_expert == N_EXPERTS, -1, next_expert).astype(I32)
    y = experts(xg, w_gu, w_down, layer, tile_expert, n_valid,
                (ordinal[tile_expert] % 2).astype(I32), next_expert[tile_expert])
    return combine(h, wgt.T, tiled(COMBINE_TM), y, tail, tail_args, batch, seq)


def _attn_kernel(q_ref, k_ref, vt_ref, lam_ref, g_ref, o_ref, qq_ref, m_ref, acc_ref,
                 a_ref, c_ref, *sp_refs, lambda_init, heads):
    s_refs, p_refs = sp_refs[:heads], sp_refs[heads:]
    qi = pl.program_id(2)
    tq, tk, hd = ATT_TQ, ATT_TK, 2 * B_HEAD_DIM
    feat = lax.broadcasted_iota(I32, (hd, tq), 0)
    for g in range(heads):
        qt = q_ref[0, g * hd:(g + 1) * hd, :]
        zero = jnp.zeros_like(qt)
        qq_ref[g, :, 0:tq] = jnp.where(feat < B_HEAD_DIM, qt, zero)
        qq_ref[g, :, tq:2 * tq] = jnp.where(feat >= B_HEAD_DIM, qt, zero)
    m_ref[...] = jnp.full_like(m_ref, -jnp.inf)
    acc_ref[...] = jnp.zeros_like(acc_ref)
    ones = jnp.ones((ATT_SUM_ROWS, tk), BF16)
    dyn0 = jnp.minimum(qi, 0)

    def step(j, masked, nk=tk):
        off = pl.multiple_of(j * tk, tk)
        if masked:
            krow = lax.broadcasted_iota(I32, (nk, 2 * tq), 0)
            qcol = lax.broadcasted_iota(I32, (nk, 2 * tq), 1)
            visible = off + krow <= qi * tq + jnp.where(qcol >= tq, qcol - tq, qcol)
        for g in range(heads):
            kb = k_ref[pl.ds(off, nk), g * hd:(g + 1) * hd]
            s = _nn(kb, qq_ref[g])
            if masked:
                s = jnp.where(visible, s, -jnp.inf)
            s_refs[g][0, 0:nk, :] = s
            c_ref[g] = jnp.max(s, axis=0, keepdims=True)
        for g in range(heads):
            m_old = m_ref[g]
            m_new = jnp.maximum(m_old, c_ref[g])
            a_ref[g] = jnp.exp2(m_old - m_new)
            m_ref[g] = m_new
            for c in range(0, nk, ATT_CHUNK):
                p_refs[g][0, c:c + ATT_CHUNK, :] = jnp.exp2(
                    s_refs[g][dyn0, c:c + ATT_CHUNK, :] - m_new).astype(BF16)
        for g in range(heads):
            vtb = vt_ref[0, g * hd:(g + 1) * hd, pl.ds(off, nk)]
            lhs = jnp.concatenate([vtb, ones[:, 0:nk]], axis=0)
            acc_ref[g] = a_ref[g] * acc_ref[g] + _nn(lhs, p_refs[g][dyn0, 0:nk, :])

    n_full = (qi * tq) // tk

    def full_step(j, carry):
        step(j, False)
        return carry

    lax.fori_loop(0, n_full, full_step, 0)

    first_part = (qi * tq) % tk + tq <= tk // 2

    @pl.when(first_part)
    def _():
        step(n_full, True, tk // 2)

    @pl.when(jnp.logical_not(first_part))
    def _():
        step(n_full, True)

    lam = lam_ref[...]
    lam_full = (jnp.exp(jnp.sum(lam[0:1] * lam[1:2], axis=-1, keepdims=True))
                - jnp.exp(jnp.sum(lam[2:3] * lam[3:4], axis=-1, keepdims=True)) + lambda_init)
    for g in range(heads):
        acc = acc_ref[g]
        on = acc[:hd] / acc[hd:hd + 1]
        ot = on[:, :tq] - lam_full * on[:, tq:]
        o = ot.T
        o = o * lax.rsqrt(jnp.mean(o * o, axis=-1, keepdims=True) + SUBLN_EPS) * g_ref[...]
        o_ref[:, g * hd:(g + 1) * hd] = (o * (1.0 - lambda_init)).astype(o_ref.dtype)


def diff_attn(qt, k, vt, lam, subln_g, batch, seq, lambda_init, heads=ATT_HEADS):
    t, d = k.shape
    nq = seq // ATT_TQ
    hd = 2 * B_HEAD_DIM
    w = heads * hd
    return pl.pallas_call(
        functools.partial(_attn_kernel, lambda_init=lambda_init, heads=heads),
        grid=(batch, B_HEADS // heads, nq),
        in_specs=[pl.BlockSpec((1, w, ATT_TQ), lambda b, h, i: (b, h, i)),
                  pl.BlockSpec((seq, w), lambda b, h, i: (b, h)),
                  pl.BlockSpec((1, w, seq), lambda b, h, i: (b, h, 0)),
                  pl.BlockSpec(lam.shape, lambda b, h, i: (0, 0)),
                  pl.BlockSpec((1, hd), lambda b, h, i: (0, 0))],
        out_specs=pl.BlockSpec((ATT_TQ, w), lambda b, h, i: (b * nq + i, h)),
        out_shape=jax.ShapeDtypeStruct((t, d), BF16),
        scratch_shapes=[pltpu.VMEM((heads, hd, 2 * ATT_TQ), BF16),
                        pltpu.VMEM((heads, 1, 2 * ATT_TQ), F32),
                        pltpu.VMEM((heads, hd + ATT_SUM_ROWS, 2 * ATT_TQ), F32),
                        pltpu.VMEM((heads, 1, 2 * ATT_TQ), F32),
                        pltpu.VMEM((heads, 1, 2 * ATT_TQ), F32)]
        + [pltpu.VMEM((1, ATT_TK, 2 * ATT_TQ), F32) for _ in range(heads)]
        + [pltpu.VMEM((1, ATT_TK, 2 * ATT_TQ), BF16) for _ in range(heads)],
        compiler_params=_params("parallel", "parallel", "arbitrary"),
        name="diff_attn",
    )(qt, k, vt, lam, subln_g.reshape(1, hd))


def kernel(x, a_norm_g, a_w_in, a_lb, a_onorm_g, a_w_out, kv_norm_g, w_kv, b_norm_g, b_w_q, b_lam,
           b_subln_g, b_w_out, ffn_norm_g, router_g_w, router_g_b, router_e_w, router_e_b,
           expert_w_gu, expert_w_down, final_norm_g):
    batch, seq, d = x.shape
    assert d == D_MODEL and a_norm_g.shape[0] == 1 and b_norm_g.shape[0] == 1
    assert seq % max(2 * HG_BLOCK, ATT_TK, ROUTER_TM) == 0
    t = batch * seq
    h = x.reshape(t, d)

    h, routing = hgrn2_layer(h, a_norm_g[0], a_w_in[0].astype(BF16), a_lb, a_onorm_g[0],
                             a_w_out[0].astype(BF16), seq,
                             (ffn_norm_g[0], router_g_w[0], router_g_b[0], router_e_w[0], router_e_b[0]))
    qkv_args = (b_norm_g[0], kv_norm_g, b_w_q[0].T.astype(BF16),
                w_kv[:, :d].astype(BF16), w_kv[:, d:].T.astype(BF16))
    h, qt, k, vt = hier_moe_layer(h, routing, expert_w_gu, expert_w_down, 0, "qkv", qkv_args, batch, seq)

    lambda_init = 0.8 - 0.6 * math.exp(-0.3 * 1)
    o = diff_attn(qt, k, vt, b_lam[0], b_subln_g[0], batch, seq, lambda_init)
    h, routing = proj_res_router(o, b_w_out[0].astype(BF16), h,
                                 (ffn_norm_g[1], router_g_w[1], router_g_b[1], router_e_w[1], router_e_b[1]))
    h = hier_moe_layer(h, routing, expert_w_gu, expert_w_down, 1, "norm", (final_norm_g,), batch, seq)
    return h.reshape(batch, seq, d)
```

```python
import functools
import math

import jax
import jax.numpy as jnp
from jax import lax
from jax.experimental import pallas as pl
from jax.experimental.pallas import tpu as pltpu

F32 = jnp.float32
BF16 = jnp.bfloat16
I32 = jnp.int32

D_MODEL = 1024
A_HEADS = 8
A_HEAD_DIM = 128
B_HEADS = 8
B_HEAD_DIM = 64
N_GROUPS = 4
EXPERTS_PER_GROUP = 8
N_EXPERTS = N_GROUPS * EXPERTS_PER_GROUP
D_EXPERT = 512
NORM_EPS = 1e-6
SUBLN_EPS = 1e-5
LOG2E = 1.4426950408889634

GLA_BLOCK = 128
GLA_HALF = GLA_BLOCK // 2
HG_BLOCK = 256
ATT_TQ = 256
ATT_TK = 512
ATT_HEADS = 8
ATT_CHUNK = 128
ATT_SUM_ROWS = 16
ROUTER_TM = 512
ROUTER_EXPERT_ROW0 = 8
ROUTER_ROWS = ROUTER_EXPERT_ROW0 + N_EXPERTS
EXPERT_TM = 512
EXPERT_X_BUFS = 3
DISPATCH_TM = 2048
COMBINE_TM = 512
ROW_DMA_UNROLL = 8
TOKEN_TILE = 8
VMEM_LIMIT = 56 * 1024 * 1024


def _nt(a, b):
    return lax.dot_general(a, b, (((1,), (1,)), ((), ())), preferred_element_type=F32)


def _nn(a, b):
    return jnp.dot(a, b, preferred_element_type=F32)


def _to_tiles(ref, x):
    tm = x.shape[0]
    for s in range(TOKEN_TILE):
        ref[pl.ds(s, tm, stride=TOKEN_TILE), :] = x[:, s * 128:(s + 1) * 128]


def _from_tiles(ref, tm):
    return jnp.concatenate([ref[pl.ds(s, tm, stride=TOKEN_TILE), :] for s in range(TOKEN_TILE)], axis=1)


def _params(*sem):
    return pltpu.CompilerParams(dimension_semantics=sem, vmem_limit_bytes=VMEM_LIMIT)


def _hgrn2_project(x, g_ref, w_ref, proj_ref):
    d = x.shape[1]
    y = x * lax.rsqrt(jnp.mean(x * x, axis=-1, keepdims=True) + NORM_EPS)
    xn = (y * g_ref[...]).astype(BF16)
    for c in range(w_ref.shape[1] // d):
        proj_ref[:, c * d:(c + 1) * d] = _nn(xn, w_ref[:, c * d:(c + 1) * d])


def _gla_block(proj_ref, row0, lb, og, st_ref, out_ref, out_row0):
    d = D_MODEL
    rows = pl.ds(pl.multiple_of(row0, GLA_BLOCK), GLA_BLOCK)
    fz = proj_ref[rows, d:2 * d]
    logf = jnp.log(lb + (1.0 - lb) * jax.nn.sigmoid(fz))
    kk = (1.0 - lb) * jax.nn.sigmoid(-fz)
    qq = jax.nn.silu(proj_ref[rows, 0:d])

    row = lax.broadcasted_iota(I32, logf.shape, 0)
    b = logf
    shift = 1
    while shift < GLA_BLOCK:
        b = b + jnp.where(row >= shift, pltpu.roll(b, shift, axis=0), 0.0)
        shift *= 2

    h0, h1 = slice(0, GLA_HALF), slice(GLA_HALF, GLA_BLOCK)
    b_a_mid = b[GLA_HALF // 2 - 1:GLA_HALF // 2]
    b_a_end = b[GLA_HALF - 1:GLA_HALF]
    b_b_mid = b[GLA_HALF + GLA_HALF // 2 - 1:GLA_HALF + GLA_HALF // 2]
    b_end = b[GLA_BLOCK - 1:GLA_BLOCK]

    qa_n = (qq[h0] * jnp.exp(b[h0] - b_a_mid)).astype(BF16)
    ka_n = (kk[h0] * jnp.exp(b_a_mid - b[h0])).astype(BF16)
    qb_n = (qq[h1] * jnp.exp(b[h1] - b_b_mid)).astype(BF16)
    kb_n = (kk[h1] * jnp.exp(b_b_mid - b[h1])).astype(BF16)
    qb_x = (qq[h1] * jnp.exp(b[h1] - b_a_end)).astype(BF16)
    ka_x = (kk[h0] * jnp.exp(b_a_end - b[h0])).astype(BF16)
    q_dec = (qq * jnp.exp(b)).astype(BF16)
    k_end = (kk * jnp.exp(b_end - b)).astype(BF16)
    d_end = jnp.exp(b_end)

    rr = lax.broadcasted_iota(I32, (GLA_HALF, GLA_HALF), 0)
    ss = lax.broadcasted_iota(I32, (GLA_HALF, GLA_HALF), 1)
    causal = ss <= rr

    for h in range(A_HEADS):
        hs = slice(h * A_HEAD_DIM, (h + 1) * A_HEAD_DIM)
        v = proj_ref[rows, 2 * d + h * A_HEAD_DIM:2 * d + (h + 1) * A_HEAD_DIM]
        vb = v.astype(BF16)
        st = st_ref[h]
        st_b = st.astype(BF16)
        p_aa = jnp.where(causal, _nt(qa_n[:, hs], ka_n[:, hs]), 0.0).astype(BF16)
        p_bb = jnp.where(causal, _nt(qb_n[:, hs], kb_n[:, hs]), 0.0).astype(BF16)
        p_ba = _nt(qb_x[:, hs], ka_x[:, hs]).astype(BF16)
        inter = _nt(q_dec[:, hs], st_b)
        o_a = _nn(p_aa, vb[h0]) + inter[h0]
        o_b = _nn(p_bb, vb[h1]) + _nn(p_ba, vb[h0]) + inter[h1]
        st_ref[h] = d_end[:, hs] * st + _nn(v.T.astype(BF16), k_end[:, hs])
        gate = jax.nn.silu(proj_ref[rows, 3 * d + h * A_HEAD_DIM:3 * d + (h + 1) * A_HEAD_DIM])
        for half, o in ((0, o_a), (1, o_b)):
            on = o * lax.rsqrt(jnp.mean(o * o, axis=-1, keepdims=True) + NORM_EPS)
            r0 = out_row0 + half * GLA_HALF
            out_ref[r0:r0 + GLA_HALF, hs] = (on * og * gate[half * GLA_HALF:(half + 1) * GLA_HALF]
                                             ).astype(out_ref.dtype)


def _hgrn2_kernel(x_ref, xnext_ref, g_ref, win_ref, alb_ref, og_ref, wout_ref,
                  rg_ref, rwh_ref, rwl_ref, rb_ref, o_ref, xn_ref, eid_ref, wgt_ref, rank_ref, cnt_ref,
                  proj_a, proj_b, gated_a, gated_b, st_ref, *, blocks_per_seq):
    step = pl.program_id(0)
    dyn0 = jnp.minimum(step, 0)
    alb = alb_ref[...]
    e = jnp.exp(alb - jnp.max(alb, axis=0, keepdims=True))
    lb = e[0:1] / jnp.sum(e, axis=0, keepdims=True)
    og = og_ref[...]

    @pl.when(step == 0)
    def _():
        _hgrn2_project(x_ref[0:HG_BLOCK, :], g_ref, win_ref, proj_a)

    @pl.when((2 * step) % blocks_per_seq == 0)
    def _():
        st_ref[...] = jnp.zeros_like(st_ref)

    _hgrn2_project(x_ref[HG_BLOCK:2 * HG_BLOCK, :], g_ref, win_ref, proj_b)
    for sub in range(HG_BLOCK // GLA_BLOCK):
        _gla_block(proj_a, dyn0 + sub * GLA_BLOCK, lb, og, st_ref, gated_a, sub * GLA_BLOCK)
    out_a = x_ref[0:HG_BLOCK, :] + _nn(gated_a[...], wout_ref[...])
    o_ref[0:HG_BLOCK, :] = out_a

    _hgrn2_project(xnext_ref[...], g_ref, win_ref, proj_a)
    for sub in range(HG_BLOCK // GLA_BLOCK):
        _gla_block(proj_b, dyn0 + sub * GLA_BLOCK, lb, og, st_ref, gated_b, sub * GLA_BLOCK)
    out_b = x_ref[HG_BLOCK:2 * HG_BLOCK, :] + _nn(gated_b[...], wout_ref[...])
    o_ref[HG_BLOCK:2 * HG_BLOCK, :] = out_b

    _route_tile(jnp.concatenate([out_a, out_b], axis=0), rg_ref, rwh_ref, rwl_ref, rb_ref,
                xn_ref, eid_ref, wgt_ref, rank_ref, cnt_ref)


def hgrn2_layer(x, norm_g, w_in_bf16, a_lb, onorm_g, w_out_bf16, seq, route_params):
    t, d = x.shape
    n = w_in_bf16.shape[1]
    nblocks = t // HG_BLOCK
    assert 2 * HG_BLOCK == ROUTER_TM
    route_args, route_in, route_specs, route_shapes = _route_plumbing(t, d, ROUTER_TM, *route_params)
    out = pl.pallas_call(
        functools.partial(_hgrn2_kernel, blocks_per_seq=seq // HG_BLOCK),
        grid=(nblocks // 2,),
        in_specs=[pl.BlockSpec((2 * HG_BLOCK, d), lambda i: (i, 0)),
                  pl.BlockSpec((HG_BLOCK, d), lambda i: (jnp.minimum(2 * i + 2, nblocks - 1), 0)),
                  pl.BlockSpec((1, d), lambda i: (0, 0)),
                  pl.BlockSpec((d, n), lambda i: (0, 0)),
                  pl.BlockSpec(a_lb.shape, lambda i: (0, 0)),
                  pl.BlockSpec((1, A_HEAD_DIM), lambda i: (0, 0)),
                  pl.BlockSpec((d, d), lambda i: (0, 0))] + route_in,
        out_specs=[pl.BlockSpec((2 * HG_BLOCK, d), lambda i: (i, 0))] + route_specs,
        out_shape=[jax.ShapeDtypeStruct((t, d), F32)] + route_shapes,
        scratch_shapes=[pltpu.VMEM((HG_BLOCK, n), F32),
                        pltpu.VMEM((HG_BLOCK, n), F32),
                        pltpu.VMEM((HG_BLOCK, d), BF16),
                        pltpu.VMEM((HG_BLOCK, d), BF16),
                        pltpu.VMEM((A_HEADS, A_HEAD_DIM, A_HEAD_DIM), F32)],
        compiler_params=_params("arbitrary"),
        name="hgrn2_layer",
    )(x, x, norm_g.reshape(1, d), w_in_bf16, a_lb, onorm_g.reshape(1, A_HEAD_DIM), w_out_bf16,
      *route_args)
    return out[0], out[1:]


def _route_tile(x, g_ref, wh_ref, wl_ref, b_ref, xn_ref, eid_ref, wgt_ref, rank_ref, cnt_ref):
    tm = x.shape[0]

    @pl.when(pl.program_id(0) == 0)
    def _():
        cnt_ref[...] = jnp.zeros_like(cnt_ref)

    xn = x * lax.rsqrt(jnp.mean(x * x, axis=-1, keepdims=True) + NORM_EPS) * g_ref[...]
    _to_tiles(xn_ref, xn)
    xh = xn.astype(BF16)
    xl = (xn - xh.astype(F32)).astype(BF16)
    wh = wh_ref[...]
    lg = _nt(wh, xh) + _nt(wl_ref[...], xh) + _nt(wh, xl) + b_ref[...]

    gl = lg[0:N_GROUPS]
    r4 = lax.broadcasted_iota(I32, gl.shape, 0)
    gmax = jnp.max(gl, axis=0, keepdims=True)
    grp = jnp.min(jnp.where(gl == gmax, r4, N_GROUPS), axis=0, keepdims=True)
    p_grp = 1.0 / jnp.sum(jnp.exp(gl - gmax), axis=0, keepdims=True)

    r0 = ROUTER_EXPERT_ROW0
    fine = lg[r0:r0 + EXPERTS_PER_GROUP]
    for gi in range(1, N_GROUPS):
        fine = jnp.where(grp == gi, lg[r0 + gi * EXPERTS_PER_GROUP:r0 + (gi + 1) * EXPERTS_PER_GROUP], fine)
    r8 = lax.broadcasted_iota(I32, fine.shape, 0)
    m1 = jnp.max(fine, axis=0, keepdims=True)
    i1 = jnp.min(jnp.where(fine == m1, r8, EXPERTS_PER_GROUP), axis=0, keepdims=True)
    rest = jnp.where(r8 == i1, -jnp.inf, fine)
    m2 = jnp.max(rest, axis=0, keepdims=True)
    i2 = jnp.min(jnp.where(rest == m2, r8, EXPERTS_PER_GROUP), axis=0, keepdims=True)
    e21 = jnp.exp(m2 - m1)
    t1 = 1.0 / (1.0 + e21)
    wgt_ref[0:1, :] = p_grp * t1
    wgt_ref[1:2, :] = p_grp * (e21 * t1)
    e1 = grp * EXPERTS_PER_GROUP + i1
    e2 = grp * EXPERTS_PER_GROUP + i2
    eid_ref[0:1, :] = e1
    eid_ref[1:2, :] = e2

    r32 = lax.broadcasted_iota(I32, (N_EXPERTS, tm), 0)
    is1 = r32 == e1
    is2 = r32 == e2
    member = jnp.logical_or(is1, is2)
    ta = lax.broadcasted_iota(I32, (tm, tm), 0)
    tb = lax.broadcasted_iota(I32, (tm, tm), 1)
    before = (ta < tb).astype(BF16)
    prior = _nn(member.astype(BF16), before) + cnt_ref[:, 0:1]
    rank_ref[0:1, :] = jnp.sum(jnp.where(is1, prior, 0.0), axis=0, keepdims=True).astype(I32)
    rank_ref[1:2, :] = jnp.sum(jnp.where(is2, prior, 0.0), axis=0, keepdims=True).astype(I32)
    cnt_ref[...] = cnt_ref[...] + jnp.sum(member.astype(F32), axis=1, keepdims=True)


def _proj_res_router_kernel(a_ref, w_ref, r_ref, g_ref, wh_ref, wl_ref, b_ref, o_ref, *route_out):
    h = r_ref[...] + _nn(a_ref[...], w_ref[...])
    o_ref[...] = h
    _route_tile(h, g_ref, wh_ref, wl_ref, b_ref, *route_out)


def _route_plumbing(t, d, tm, g, wg, bg, we, be):
    r0 = ROUTER_EXPERT_ROW0
    w_all = jnp.zeros((ROUTER_ROWS, d), F32).at[0:N_GROUPS].set(wg.T).at[r0:r0 + N_EXPERTS].set(we.T)
    b_all = jnp.zeros((ROUTER_ROWS, 1), F32).at[0:N_GROUPS, 0].set(bg).at[r0:r0 + N_EXPERTS, 0].set(be)
    wh = w_all.astype(BF16)
    wl = (w_all - wh.astype(F32)).astype(BF16)
    row2 = lambda i: (0, i)
    route_in = [pl.BlockSpec((1, d), lambda i: (0, 0)),
                pl.BlockSpec((ROUTER_ROWS, d), lambda i: (0, 0)),
                pl.BlockSpec((ROUTER_ROWS, d), lambda i: (0, 0)),
                pl.BlockSpec((ROUTER_ROWS, 1), lambda i: (0, 0))]
    route_specs = [pl.BlockSpec((tm * TOKEN_TILE, 128), lambda i: (i, 0)),
                   pl.BlockSpec((2, tm), row2),
                   pl.BlockSpec((2, tm), row2),
                   pl.BlockSpec((2, tm), row2),
                   pl.BlockSpec((N_EXPERTS, 128), lambda i: (0, 0))]
    route_shapes = [jax.ShapeDtypeStruct((t * TOKEN_TILE, 128), F32),
                    jax.ShapeDtypeStruct((2, t), I32),
                    jax.ShapeDtypeStruct((2, t), F32),
                    jax.ShapeDtypeStruct((2, t), I32),
                    jax.ShapeDtypeStruct((N_EXPERTS, 128), F32)]
    return (g.reshape(1, d), wh, wl, b_all), route_in, route_specs, route_shapes


def proj_res_router(a, w, h, route_params, tm=ROUTER_TM):
    t, d = h.shape
    row = pl.BlockSpec((tm, d), lambda i: (i, 0))
    route_args, route_in, route_specs, route_shapes = _route_plumbing(t, d, tm, *route_params)
    out = pl.pallas_call(
        _proj_res_router_kernel,
        grid=(t // tm,),
        in_specs=[row, pl.BlockSpec((d, d), lambda i: (0, 0)), row] + route_in,
        out_specs=[row] + route_specs,
        out_shape=[jax.ShapeDtypeStruct((t, d), F32)] + route_shapes,
        compiler_params=_params("arbitrary"),
        name="proj_res_router",
    )(a, w, h, *route_args)
    return out[0], out[1:]


def _dispatch_kernel(last_ref, pos_ref, x_hbm, o_ref, xbuf, zero_ref, load_sem, row_sem):
    i = pl.program_id(0)
    n = pl.num_programs(0)
    tm = xbuf.shape[1] // TOKEN_TILE
    in_rows = tm * TOKEN_TILE
    tile_rows = EXPERT_TM * TOKEN_TILE
    cur = i % 2
    sem = row_sem.at[0]

    def load(j, slot):
        src = pl.multiple_of(j * in_rows, in_rows)
        return pltpu.make_async_copy(x_hbm.at[pl.ds(src, in_rows)], xbuf.at[slot], load_sem.at[slot])

    def drain(slot):
        for _ in range(2):
            pltpu.make_async_copy(xbuf.at[slot], o_ref.at[pl.ds(0, in_rows)], row_sem.at[slot]).wait()

    @pl.when(i == 0)
    def _():
        load(0, 0).start()
        zero_ref[...] = jnp.zeros_like(zero_ref)

        def tile_fill(e):
            row = pl.multiple_of(last_ref[e] * TOKEN_TILE, tile_rows)
            return pltpu.make_async_copy(zero_ref, o_ref.at[pl.ds(row, tile_rows)], sem)

        for e in range(N_EXPERTS):
            @pl.when(last_ref[e] >= 0)
            def _():
                tile_fill(e).start()
        for e in range(N_EXPERTS):
            @pl.when(last_ref[e] >= 0)
            def _():
                tile_fill(e).wait()

        def spare_fill(j):
            row = pl.multiple_of(j * tile_rows, tile_rows)
            return pltpu.make_async_copy(zero_ref, o_ref.at[pl.ds(row, tile_rows)], sem)

        n_tiles = o_ref.shape[0] // tile_rows
        lax.fori_loop(last_ref[N_EXPERTS], n_tiles, lambda j, c: (spare_fill(j).start(), c)[1], 0)
        lax.fori_loop(last_ref[N_EXPERTS], n_tiles, lambda j, c: (spare_fill(j).wait(), c)[1], 0)

    load(i, cur).wait()

    @pl.when(i > 0)
    def _():
        drain(1 - cur)

    @pl.when(i + 1 < n)
    def _():
        load(i + 1, 1 - cur).start()

    def row_copy(r, k):
        dst = pl.multiple_of(pos_ref[0, 0, k * tm + r] * TOKEN_TILE, TOKEN_TILE)
        return pltpu.make_async_copy(xbuf.at[cur, pl.ds(r * TOKEN_TILE, TOKEN_TILE)],
                                     o_ref.at[pl.ds(dst, TOKEN_TILE)], row_sem.at[cur])

    def issue(blk, carry):
        for u in range(ROW_DMA_UNROLL):
            r = blk * ROW_DMA_UNROLL + u
            row_copy(r, 0).start(priority=u % 2)
            row_copy(r, 1).start(priority=(u + 1) % 2)
        return carry

    lax.fori_loop(0, tm // ROW_DMA_UNROLL, issue, 0)

    @pl.when(i == n - 1)
    def _():
        drain(cur)


def dispatch(xn, pos3, last_tile_row, n_rows, tm=DISPATCH_TM):
    t = xn.shape[0] // TOKEN_TILE
    return pl.pallas_call(
        _dispatch_kernel,
        grid_spec=pltpu.PrefetchScalarGridSpec(
            num_scalar_prefetch=1,
            grid=(t // tm,),
            in_specs=[pl.BlockSpec((1, 1, 2 * tm), lambda i, last: (i, 0, 0), memory_space=pltpu.SMEM),
                      pl.BlockSpec(memory_space=pl.ANY)],
            out_specs=pl.BlockSpec(memory_space=pl.ANY),
            scratch_shapes=[pltpu.VMEM((2, tm * TOKEN_TILE, 128), xn.dtype),
                            pltpu.VMEM((EXPERT_TM * TOKEN_TILE, 128), xn.dtype),
                            pltpu.SemaphoreType.DMA((2,)),
                            pltpu.SemaphoreType.DMA((2,))]),
        out_shape=jax.ShapeDtypeStruct((n_rows * TOKEN_TILE, 128), xn.dtype),
        compiler_params=_params("arbitrary"),
        name="moe_dispatch",
    )(last_tile_row, pos3, xn)


def _expert_kernel(te_ref, nv_ref, par_ref, nxt_ref, x_hbm, wgu_hbm, wd_hbm, y_ref,
                   xbuf, wgu_f, wd_f, wgu_b, wd_b, xsem, wsem, *, layer):
    i = pl.program_id(0)
    n_valid = nv_ref[0]
    tm = xbuf.shape[1] // TOKEN_TILE
    in_rows = tm * TOKEN_TILE
    valid = i < n_valid
    changed = te_ref[i] != te_ref[jnp.maximum(i - 1, 0)]
    first = jnp.logical_or(i == 0, changed)

    def xload(j):
        slot = j % EXPERT_X_BUFS
        src = pl.multiple_of(j * in_rows, in_rows)
        return pltpu.make_async_copy(x_hbm.at[pl.ds(src, in_rows)], xbuf.at[slot], xsem.at[slot])

    def wload(e, slot):
        return (pltpu.make_async_copy(wgu_hbm.at[layer, e], wgu_f.at[slot], wsem.at[slot]),
                pltpu.make_async_copy(wd_hbm.at[layer, e], wd_f.at[slot], wsem.at[slot]))

    @pl.when(i == 0)
    def _():
        for cp in wload(te_ref[0], par_ref[0]):
            cp.start()
        for j in range(EXPERT_X_BUFS - 1):
            @pl.when(j < n_valid)
            def _():
                xload(j).start()

    @pl.when(i + EXPERT_X_BUFS - 1 < n_valid)
    def _():
        xload(i + EXPERT_X_BUFS - 1).start()

    @pl.when(jnp.logical_and(valid, first))
    def _():
        slot = par_ref[i]
        for cp in wload(te_ref[i], slot):
            cp.wait()

        @pl.when(nxt_ref[i] >= 0)
        def _():
            for cp in wload(nxt_ref[i], 1 - slot):
                cp.start()

        wgu_b[...] = wgu_f[slot].astype(BF16)
        wd_b[...] = wd_f[slot].astype(BF16)

    @pl.when(valid)
    def _():
        xload(i).wait()
        x = _from_tiles(xbuf.at[i % EXPERT_X_BUFS], tm).astype(BF16)
        au = _nn(x, wgu_b[...])
        a = au[:, :D_EXPERT]
        u = au[:, D_EXPERT:]
        mid = (jax.nn.silu(a) * u).astype(BF16)
        _to_tiles(y_ref, _nn(mid, wd_b[...]))

    @pl.when(jnp.logical_not(valid))
    def _():
        y_ref[...] = jnp.zeros_like(y_ref)


def experts(xg, w_gu, w_down, layer, tile_expert, n_valid, seg_parity, seg_next, tm=EXPERT_TM):
    d = D_MODEL
    p = xg.shape[0] // TOKEN_TILE
    n_tiles = p // tm
    return pl.pallas_call(
        functools.partial(_expert_kernel, layer=layer),
        grid_spec=pltpu.PrefetchScalarGridSpec(
            num_scalar_prefetch=4,
            grid=(n_tiles,),
            in_specs=[pl.BlockSpec(memory_space=pl.ANY),
                      pl.BlockSpec(memory_space=pl.ANY),
                      pl.BlockSpec(memory_space=pl.ANY)],
            out_specs=pl.BlockSpec((tm * TOKEN_TILE, 128), lambda i, *_: (i, 0)),
            scratch_shapes=[pltpu.VMEM((EXPERT_X_BUFS, tm * TOKEN_TILE, 128), F32),
                            pltpu.VMEM((2, d, 2 * D_EXPERT), F32),
                            pltpu.VMEM((2, D_EXPERT, d), F32),
                            pltpu.VMEM((d, 2 * D_EXPERT), BF16),
                            pltpu.VMEM((D_EXPERT, d), BF16),
                            pltpu.SemaphoreType.DMA((EXPERT_X_BUFS,)),
                            pltpu.SemaphoreType.DMA((2,))]),
        out_shape=jax.ShapeDtypeStruct((p * TOKEN_TILE, 128), F32),
        compiler_params=_params("arbitrary"),
        name="moe_experts",
    )(tile_expert, n_valid, seg_parity, seg_next, xg, w_gu, w_down)


def _qkv_tail(x, gq_ref, gkv_ref, wqt_ref, wk_ref, wvt_ref, qt_ref, k_ref, vt_ref):
    y = x * lax.rsqrt(jnp.mean(x * x, axis=-1, keepdims=True) + NORM_EPS)
    xq = (y * gq_ref[...]).astype(BF16)
    xkv = (y * gkv_ref[...]).astype(BF16)
    qt_ref[0] = (_nt(wqt_ref[...], xq) * (B_HEAD_DIM ** -0.5 * LOG2E)).astype(qt_ref.dtype)
    k_ref[...] = _nn(xkv, wk_ref[...]).astype(k_ref.dtype)
    vt_ref[0] = _nt(wvt_ref[...], xkv).astype(vt_ref.dtype)


def _combine_kernel(pos_ref, pos_next_ref, h_ref, w_ref, y_ref, *rest, tail):
    if tail == "norm":
        g_ref, o_ref, buf, sem = rest
    else:
        gq_ref, gkv_ref, wqt_ref, wk_ref, wvt_ref, o_ref, qt_ref, k_ref, vt_ref, buf, sem = rest
    i = pl.program_id(0)
    tm = h_ref.shape[0]
    cur = i % 2

    def gather_tile(p_ref, half):
        def row_copy(r, k):
            src = pl.multiple_of(p_ref[0, 0, k * tm + r] * TOKEN_TILE, TOKEN_TILE)
            return pltpu.make_async_copy(y_ref.at[pl.ds(src, TOKEN_TILE)],
                                         buf.at[2 * half + k, pl.ds(r * TOKEN_TILE, TOKEN_TILE)],
                                         sem.at[half])

        def issue(blk, carry):
            for u in range(ROW_DMA_UNROLL):
                r = blk * ROW_DMA_UNROLL + u
                row_copy(r, 0).start(priority=u % 2)
                row_copy(r, 1).start(priority=(u + 1) % 2)
            return carry

        lax.fori_loop(0, tm // ROW_DMA_UNROLL, issue, 0)

    @pl.when(i == 0)
    def _():
        gather_tile(pos_ref, 0)

    @pl.when(i + 1 < pl.num_programs(0))
    def _():
        gather_tile(pos_next_ref, 1 - cur)

    for k in range(2):
        pltpu.make_async_copy(y_ref.at[pl.ds(0, tm * TOKEN_TILE)], buf.at[2 * cur + k], sem.at[cur]).wait()
    w = w_ref[...]
    out = (h_ref[...] + w[:, 0:1] * _from_tiles(buf.at[2 * cur], tm)
           + w[:, 1:2] * _from_tiles(buf.at[2 * cur + 1], tm))
    if tail == "norm":
        o_ref[...] = out * lax.rsqrt(jnp.mean(out * out, axis=-1, keepdims=True) + NORM_EPS) * g_ref[...]
    else:
        o_ref[...] = out
        _qkv_tail(out, gq_ref, gkv_ref, wqt_ref, wk_ref, wvt_ref, qt_ref, k_ref, vt_ref)


def combine(h, wgt_t, pos3, y, tail, tail_args, batch, seq, tm=COMBINE_TM):
    t, d = h.shape
    nt = t // tm
    nblk = seq // tm
    row = pl.BlockSpec((tm, d), lambda i: (i, 0))
    vec = pl.BlockSpec((1, d), lambda i: (0, 0))
    full = pl.BlockSpec((d, d), lambda i: (0, 0))
    tr = pl.BlockSpec((1, d, tm), lambda i: (i // nblk, 0, i % nblk))
    if tail == "norm":
        (g,) = tail_args
        extra_in, extra_specs = [g.reshape(1, d)], [vec]
        out_specs, out_shape = row, jax.ShapeDtypeStruct((t, d), F32)
    else:
        g_q, g_kv, wqt, wk, wvt = tail_args
        extra_in = [g_q.reshape(1, d), g_kv.reshape(1, d), wqt, wk, wvt]
        extra_specs = [vec, vec, full, full, full]
        out_specs = [row, tr, row, tr]
        out_shape = [jax.ShapeDtypeStruct((t, d), F32),
                     jax.ShapeDtypeStruct((batch, d, seq), BF16),
                     jax.ShapeDtypeStruct((t, d), BF16),
                     jax.ShapeDtypeStruct((batch, d, seq), BF16)]
    return pl.pallas_call(
        functools.partial(_combine_kernel, tail=tail),
        grid=(nt,),
        in_specs=[pl.BlockSpec((1, 1, 2 * tm), lambda i: (i, 0, 0), memory_space=pltpu.SMEM),
                  pl.BlockSpec((1, 1, 2 * tm), lambda i: (jnp.minimum(i + 1, nt - 1), 0, 0),
                               memory_space=pltpu.SMEM),
                  row,
                  pl.BlockSpec((tm, 2), lambda i: (i, 0)),
                  pl.BlockSpec(memory_space=pl.ANY)] + extra_specs,
        out_specs=out_specs,
        out_shape=out_shape,
        scratch_shapes=[pltpu.VMEM((4, tm * TOKEN_TILE, 128), F32), pltpu.SemaphoreType.DMA((2,))],
        compiler_params=_params("arbitrary"),
        name="moe_combine_" + tail,
    )(pos3, pos3, h, wgt_t, y, *extra_in)


def hier_moe_layer(h, routing, w_gu, w_down, layer, tail, tail_args, batch, seq):
    t, d = h.shape
    xn, eid, wgt, rank, cnt = routing
    counts = cnt[:, 0].astype(I32)
    padded = ((counts + EXPERT_TM - 1) // EXPERT_TM) * EXPERT_TM
    ends = jnp.cumsum(padded)
    offs = ends - padded
    n_rows = 2 * t + N_EXPERTS * EXPERT_TM
    n_tiles = n_rows // EXPERT_TM
    n_valid = (ends[-1] // EXPERT_TM).astype(I32).reshape(1)
    tile_start = jnp.arange(n_tiles, dtype=I32) * EXPERT_TM
    tile_start = jnp.minimum(tile_start, ends[-1] - 1)
    tile_expert = jnp.sum((ends[None, :] <= tile_start[:, None]).astype(I32), axis=1)
    onehot = eid[None] == jnp.arange(N_EXPERTS, dtype=I32)[:, None, None]
    pos = jnp.sum(jnp.where(onehot, offs[:, None, None], 0), axis=0) + rank

    def tiled(tm):
        return pos.reshape(2, t // tm, tm).transpose(1, 0, 2).reshape(t // tm, 1, 2 * tm)

    last_tile_row = jnp.concatenate([jnp.where(padded > 0, ends - EXPERT_TM, -1).astype(I32), n_valid])
    xg = dispatch(xn, tiled(DISPATCH_TM), last_tile_row, n_rows)
    nonempty = padded > 0
    ordinal = jnp.cumsum(nonempty.astype(I32)) - 1
    ids = jnp.arange(N_EXPERTS, dtype=I32)
    later = jnp.logical_and(nonempty[None, :], ids[None, :] > ids[:, None])
    next_expert = jnp.min(jnp.where(later, ids[None, :], N_EXPERTS), axis=1)
    next_expert = jnp.where(next_expert == N_EXPERTS, -1, next_expert).astype(I32)
    y = experts(xg, w_gu, w_down, layer, tile_expert, n_valid,
                (ordinal[tile_expert] % 2).astype(I32), next_expert[tile_expert])
    return combine(h, wgt.T, tiled(COMBINE_TM), y, tail, tail_args, batch, seq)


def _attn_kernel(q_ref, k_ref, vt_ref, lam_ref, g_ref, o_ref, qq_ref, m_ref, acc_ref,
                 a_ref, c_ref, *sp_refs, lambda_init, heads):
    s_refs, p_refs = sp_refs[:heads], sp_refs[heads:]
    qi = pl.program_id(2)
    tq, tk, hd = ATT_TQ, ATT_TK, 2 * B_HEAD_DIM
    feat = lax.broadcasted_iota(I32, (hd, tq), 0)
    for g in range(heads):
        qt = q_ref[0, g * hd:(g + 1) * hd, :]
        zero = jnp.zeros_like(qt)
        qq_ref[g, :, 0:tq] = jnp.where(feat < B_HEAD_DIM, qt, zero)
        qq_ref[g, :, tq:2 * tq] = jnp.where(feat >= B_HEAD_DIM, qt, zero)
    m_ref[...] = jnp.full_like(m_ref, -jnp.inf)
    acc_ref[...] = jnp.zeros_like(acc_ref)
    ones = jnp.ones((ATT_SUM_ROWS, tk), BF16)
    dyn0 = jnp.minimum(qi, 0)

    def step(j, masked, nk=tk):
        off = pl.multiple_of(j * tk, tk)
        if masked:
            krow = lax.broadcasted_iota(I32, (nk, 2 * tq), 0)
            qcol = lax.broadcasted_iota(I32, (nk, 2 * tq), 1)
            visible = off + krow <= qi * tq + jnp.where(qcol >= tq, qcol - tq, qcol)
        for g in range(heads):
            kb = k_ref[pl.ds(off, nk), g * hd:(g + 1) * hd]
            s = _nn(kb, qq_ref[g])
            if masked:
                s = jnp.where(visible, s, -jnp.inf)
            s_refs[g][0, 0:nk, :] = s
            c_ref[g] = jnp.max(s, axis=0, keepdims=True)
        for g in range(heads):
            m_old = m_ref[g]
            m_new = jnp.maximum(m_old, c_ref[g])
            a_ref[g] = jnp.exp2(m_old - m_new)
            m_ref[g] = m_new
            for c in range(0, nk, ATT_CHUNK):
                p_refs[g][0, c:c + ATT_CHUNK, :] = jnp.exp2(
                    s_refs[g][dyn0, c:c + ATT_CHUNK, :] - m_new).astype(BF16)
        for g in range(heads):
            vtb = vt_ref[0, g * hd:(g + 1) * hd, pl.ds(off, nk)]
            lhs = jnp.concatenate([vtb, ones[:, 0:nk]], axis=0)
            acc_ref[g] = a_ref[g] * acc_ref[g] + _nn(lhs, p_refs[g][dyn0, 0:nk, :])

    n_full = (qi * tq) // tk

    def full_step(j, carry):
        step(j, False)
        return carry

    lax.fori_loop(0, n_full, full_step, 0)

    first_part = (qi * tq) % tk + tq <= tk // 2

    @pl.when(first_part)
    def _():
        step(n_full, True, tk // 2)

    @pl.when(jnp.logical_not(first_part))
    def _():
        step(n_full, True)

    lam = lam_ref[...]
    lam_full = (jnp.exp(jnp.sum(lam[0:1] * lam[1:2], axis=-1, keepdims=True))
                - jnp.exp(jnp.sum(lam[2:3] * lam[3:4], axis=-1, keepdims=True)) + lambda_init)
    for g in range(heads):
        acc = acc_ref[g]
        on = acc[:hd] / acc[hd:hd + 1]
        ot = on[:, :tq] - lam_full * on[:, tq:]
        o = ot.T
        o = o * lax.rsqrt(jnp.mean(o * o, axis=-1, keepdims=True) + SUBLN_EPS) * g_ref[...]
        o_ref[:, g * hd:(g + 1) * hd] = (o * (1.0 - lambda_init)).astype(o_ref.dtype)


def diff_attn(qt, k, vt, lam, subln_g, batch, seq, lambda_init, heads=ATT_HEADS):
    t, d = k.shape
    nq = seq // ATT_TQ
    hd = 2 * B_HEAD_DIM
    w = heads * hd
    return pl.pallas_call(
        functools.partial(_attn_kernel, lambda_init=lambda_init, heads=heads),
        grid=(batch, B_HEADS // heads, nq),
        in_specs=[pl.BlockSpec((1, w, ATT_TQ), lambda b, h, i: (b, h, i)),
                  pl.BlockSpec((seq, w), lambda b, h, i: (b, h)),
                  pl.BlockSpec((1, w, seq), lambda b, h, i: (b, h, 0)),
                  pl.BlockSpec(lam.shape, lambda b, h, i: (0, 0)),
                  pl.BlockSpec((1, hd), lambda b, h, i: (0, 0))],
        out_specs=pl.BlockSpec((ATT_TQ, w), lambda b, h, i: (b * nq + i, h)),
        out_shape=jax.ShapeDtypeStruct((t, d), BF16),
        scratch_shapes=[pltpu.VMEM((heads, hd, 2 * ATT_TQ), BF16),
                        pltpu.VMEM((heads, 1, 2 * ATT_TQ), F32),
                        pltpu.VMEM((heads, hd + ATT_SUM_ROWS, 2 * ATT_TQ), F32),
                        pltpu.VMEM((heads, 1, 2 * ATT_TQ), F32),
                        pltpu.VMEM((heads, 1, 2 * ATT_TQ), F32)]
        + [pltpu.VMEM((1, ATT_TK, 2 * ATT_TQ), F32) for _ in range(heads)]
        + [pltpu.VMEM((1, ATT_TK, 2 * ATT_TQ), BF16) for _ in range(heads)],
        compiler_params=_params("parallel", "parallel", "arbitrary"),
        name="diff_attn",
    )(qt, k, vt, lam, subln_g.reshape(1, hd))


def kernel(x, a_norm_g, a_w_in, a_lb, a_onorm_g, a_w_out, kv_norm_g, w_kv, b_norm_g, b_w_q, b_lam,
           b_subln_g, b_w_out, ffn_norm_g, router_g_w, router_g_b, router_e_w, router_e_b,
           expert_w_gu, expert_w_down, final_norm_g):
    batch, seq, d = x.shape
    assert d == D_MODEL and a_norm_g.shape[0] == 1 and b_norm_g.shape[0] == 1
    assert seq % max(2 * HG_BLOCK, ATT_TK, ROUTER_TM) == 0
    t = batch * seq
    h = x.reshape(t, d)

    h, routing = hgrn2_layer(h, a_norm_g[0], a_w_in[0].astype(BF16), a_lb, a_onorm_g[0],
                             a_w_out[0].astype(BF16), seq,
                             (ffn_norm_g[0], router_g_w[0], router_g_b[0], router_e_w[0], router_e_b[0]))
    qkv_args = (b_norm_g[0], kv_norm_g, b_w_q[0].T.astype(BF16),
                w_kv[:, :d].astype(BF16), w_kv[:, d:].T.astype(BF16))
    h, qt, k, vt = hier_moe_layer(h, routing, expert_w_gu, expert_w_down, 0, "qkv", qkv_args, batch, seq)

    lambda_init = 0.8 - 0.6 * math.exp(-0.3 * 1)
    o = diff_attn(qt, k, vt, b_lam[0], b_subln_g[0], batch, seq, lambda_init)
    h, routing = proj_res_router(o, b_w_out[0].astype(BF16), h,
                                 (ffn_norm_g[1], router_g_w[1], router_g_b[1], router_e_w[1], router_e_b[1]))
    h = hier_moe_layer(h, routing, expert_w_gu, expert_w_down, 1, "norm", (final_norm_g,), batch, seq)
    return h.reshape(batch, seq, d)
```

```python
import functools
import math

import jax
import jax.numpy as jnp
from jax import lax
from jax.experimental import pallas as pl
from jax.experimental.pallas import tpu as pltpu

F32 = jnp.float32
BF16 = jnp.bfloat16
I32 = jnp.int32

D_MODEL = 1024
A_HEADS = 8
A_HEAD_DIM = 128
B_HEADS = 8
B_HEAD_DIM = 64
N_GROUPS = 4
EXPERTS_PER_GROUP = 8
N_EXPERTS = N_GROUPS * EXPERTS_PER_GROUP
D_EXPERT = 512
NORM_EPS = 1e-6
SUBLN_EPS = 1e-5
LOG2E = 1.4426950408889634

GLA_BLOCK = 128
GLA_HALF = GLA_BLOCK // 2
HG_BLOCK = 256
ATT_TQ = 256
ATT_TK = 512
ATT_HEADS = 8
ATT_CHUNK = 128
ATT_SUM_ROWS = 16
ATT_LANE_PAD = 128
ROUTER_TM = 512
ROUTER_EXPERT_ROW0 = 8
ROUTER_ROWS = ROUTER_EXPERT_ROW0 + N_EXPERTS
EXPERT_TM = 512
EXPERT_X_BUFS = 3
DISPATCH_TM = 2048
COMBINE_TM = 512
ROW_DMA_UNROLL = 8
TOKEN_TILE = 8
VMEM_LIMIT = 56 * 1024 * 1024


def _nt(a, b):
    return lax.dot_general(a, b, (((1,), (1,)), ((), ())), preferred_element_type=F32)


def _nn(a, b):
    return jnp.dot(a, b, preferred_element_type=F32)


def _to_tiles(ref, x):
    tm = x.shape[0]
    for s in range(TOKEN_TILE):
        ref[pl.ds(s, tm, stride=TOKEN_TILE), :] = x[:, s * 128:(s + 1) * 128]


def _from_tiles(ref, tm):
    return jnp.concatenate([ref[pl.ds(s, tm, stride=TOKEN_TILE), :] for s in range(TOKEN_TILE)], axis=1)


def _params(*sem):
    return pltpu.CompilerParams(dimension_semantics=sem, vmem_limit_bytes=VMEM_LIMIT)


def _hgrn2_project(x, g_ref, w_ref, proj_ref):
    d = x.shape[1]
    y = x * lax.rsqrt(jnp.mean(x * x, axis=-1, keepdims=True) + NORM_EPS)
    xn = (y * g_ref[...]).astype(BF16)
    for c in range(w_ref.shape[1] // d):
        proj_ref[:, c * d:(c + 1) * d] = _nn(xn, w_ref[:, c * d:(c + 1) * d])


def _gla_block(proj_ref, row0, lb, og, st_ref, out_ref, out_row0):
    d = D_MODEL
    rows = pl.ds(pl.multiple_of(row0, GLA_BLOCK), GLA_BLOCK)
    fz = proj_ref[rows, d:2 * d]
    logf = jnp.log(lb + (1.0 - lb) * jax.nn.sigmoid(fz))
    kk = (1.0 - lb) * jax.nn.sigmoid(-fz)
    qq = jax.nn.silu(proj_ref[rows, 0:d])

    row = lax.broadcasted_iota(I32, logf.shape, 0)
    b = logf
    shift = 1
    while shift < GLA_BLOCK:
        b = b + jnp.where(row >= shift, pltpu.roll(b, shift, axis=0), 0.0)
        shift *= 2

    h0, h1 = slice(0, GLA_HALF), slice(GLA_HALF, GLA_BLOCK)
    b_a_mid = b[GLA_HALF // 2 - 1:GLA_HALF // 2]
    b_a_end = b[GLA_HALF - 1:GLA_HALF]
    b_b_mid = b[GLA_HALF + GLA_HALF // 2 - 1:GLA_HALF + GLA_HALF // 2]
    b_end = b[GLA_BLOCK - 1:GLA_BLOCK]

    qa_n = (qq[h0] * jnp.exp(b[h0] - b_a_mid)).astype(BF16)
    ka_n = (kk[h0] * jnp.exp(b_a_mid - b[h0])).astype(BF16)
    qb_n = (qq[h1] * jnp.exp(b[h1] - b_b_mid)).astype(BF16)
    kb_n = (kk[h1] * jnp.exp(b_b_mid - b[h1])).astype(BF16)
    qb_x = (qq[h1] * jnp.exp(b[h1] - b_a_end)).astype(BF16)
    ka_x = (kk[h0] * jnp.exp(b_a_end - b[h0])).astype(BF16)
    q_dec = (qq * jnp.exp(b)).astype(BF16)
    k_end = (kk * jnp.exp(b_end - b)).astype(BF16)
    d_end = jnp.exp(b_end)

    rr = lax.broadcasted_iota(I32, (GLA_HALF, GLA_HALF), 0)
    ss = lax.broadcasted_iota(I32, (GLA_HALF, GLA_HALF), 1)
    causal = ss <= rr

    for h in range(A_HEADS):
        hs = slice(h * A_HEAD_DIM, (h + 1) * A_HEAD_DIM)
        v = proj_ref[rows, 2 * d + h * A_HEAD_DIM:2 * d + (h + 1) * A_HEAD_DIM]
        vb = v.astype(BF16)
        st = st_ref[h]
        st_b = st.astype(BF16)
        p_aa = jnp.where(causal, _nt(qa_n[:, hs], ka_n[:, hs]), 0.0).astype(BF16)
        p_bb = jnp.where(causal, _nt(qb_n[:, hs], kb_n[:, hs]), 0.0).astype(BF16)
        p_ba = _nt(qb_x[:, hs], ka_x[:, hs]).astype(BF16)
        inter = _nt(q_dec[:, hs], st_b)
        o_a = _nn(p_aa, vb[h0]) + inter[h0]
        o_b = _nn(p_bb, vb[h1]) + _nn(p_ba, vb[h0]) + inter[h1]
        st_ref[h] = d_end[:, hs] * st + _nn(v.T.astype(BF16), k_end[:, hs])
        gate = jax.nn.silu(proj_ref[rows, 3 * d + h * A_HEAD_DIM:3 * d + (h + 1) * A_HEAD_DIM])
        for half, o in ((0, o_a), (1, o_b)):
            on = o * lax.rsqrt(jnp.mean(o * o, axis=-1, keepdims=True) + NORM_EPS)
            r0 = out_row0 + half * GLA_HALF
            out_ref[r0:r0 + GLA_HALF, hs] = (on * og * gate[half * GLA_HALF:(half + 1) * GLA_HALF]
                                             ).astype(out_ref.dtype)


def _hgrn2_kernel(x_ref, xnext_ref, g_ref, win_ref, alb_ref, og_ref, wout_ref,
                  rg_ref, rwh_ref, rwl_ref, rb_ref, o_ref, xn_ref, eid_ref, wgt_ref, rank_ref, cnt_ref,
                  proj_a, proj_b, gated_a, gated_b, st_ref, *, blocks_per_seq):
    step = pl.program_id(0)
    dyn0 = jnp.minimum(step, 0)
    alb = alb_ref[...]
    e = jnp.exp(alb - jnp.max(alb, axis=0, keepdims=True))
    lb = e[0:1] / jnp.sum(e, axis=0, keepdims=True)
    og = og_ref[...]

    @pl.when(step == 0)
    def _():
        _hgrn2_project(x_ref[0:HG_BLOCK, :], g_ref, win_ref, proj_a)

    @pl.when((2 * step) % blocks_per_seq == 0)
    def _():
        st_ref[...] = jnp.zeros_like(st_ref)

    _hgrn2_project(x_ref[HG_BLOCK:2 * HG_BLOCK, :], g_ref, win_ref, proj_b)
    for sub in range(HG_BLOCK // GLA_BLOCK):
        _gla_block(proj_a, dyn0 + sub * GLA_BLOCK, lb, og, st_ref, gated_a, sub * GLA_BLOCK)
    out_a = x_ref[0:HG_BLOCK, :] + _nn(gated_a[...], wout_ref[...])
    o_ref[0:HG_BLOCK, :] = out_a

    _hgrn2_project(xnext_ref[...], g_ref, win_ref, proj_a)
    for sub in range(HG_BLOCK // GLA_BLOCK):
        _gla_block(proj_b, dyn0 + sub * GLA_BLOCK, lb, og, st_ref, gated_b, sub * GLA_BLOCK)
    out_b = x_ref[HG_BLOCK:2 * HG_BLOCK, :] + _nn(gated_b[...], wout_ref[...])
    o_ref[HG_BLOCK:2 * HG_BLOCK, :] = out_b

    _route_tile(jnp.concatenate([out_a, out_b], axis=0), rg_ref, rwh_ref, rwl_ref, rb_ref,
                xn_ref, eid_ref, wgt_ref, rank_ref, cnt_ref)


def hgrn2_layer(x, norm_g, w_in_bf16, a_lb, onorm_g, w_out_bf16, seq, route_params):
    t, d = x.shape
    n = w_in_bf16.shape[1]
    nblocks = t // HG_BLOCK
    assert 2 * HG_BLOCK == ROUTER_TM
    route_args, route_in, route_specs, route_shapes = _route_plumbing(t, d, ROUTER_TM, *route_params)
    out = pl.pallas_call(
        functools.partial(_hgrn2_kernel, blocks_per_seq=seq // HG_BLOCK),
        grid=(nblocks // 2,),
        in_specs=[pl.BlockSpec((2 * HG_BLOCK, d), lambda i: (i, 0)),
                  pl.BlockSpec((HG_BLOCK, d), lambda i: (jnp.minimum(2 * i + 2, nblocks - 1), 0)),
                  pl.BlockSpec((1, d), lambda i: (0, 0)),
                  pl.BlockSpec((d, n), lambda i: (0, 0)),
                  pl.BlockSpec(a_lb.shape, lambda i: (0, 0)),
                  pl.BlockSpec((1, A_HEAD_DIM), lambda i: (0, 0)),
                  pl.BlockSpec((d, d), lambda i: (0, 0))] + route_in,
        out_specs=[pl.BlockSpec((2 * HG_BLOCK, d), lambda i: (i, 0))] + route_specs,
        out_shape=[jax.ShapeDtypeStruct((t, d), F32)] + route_shapes,
        scratch_shapes=[pltpu.VMEM((HG_BLOCK, n), F32),
                        pltpu.VMEM((HG_BLOCK, n), F32),
                        pltpu.VMEM((HG_BLOCK, d), BF16),
                        pltpu.VMEM((HG_BLOCK, d), BF16),
                        pltpu.VMEM((A_HEADS, A_HEAD_DIM, A_HEAD_DIM), F32)],
        compiler_params=_params("arbitrary"),
        name="hgrn2_layer",
    )(x, x, norm_g.reshape(1, d), w_in_bf16, a_lb, onorm_g.reshape(1, A_HEAD_DIM), w_out_bf16,
      *route_args)
    return out[0], out[1:]


def _route_tile(x, g_ref, wh_ref, wl_ref, b_ref, xn_ref, eid_ref, wgt_ref, rank_ref, cnt_ref):
    tm = x.shape[0]

    @pl.when(pl.program_id(0) == 0)
    def _():
        cnt_ref[...] = jnp.zeros_like(cnt_ref)

    xn = x * lax.rsqrt(jnp.mean(x * x, axis=-1, keepdims=True) + NORM_EPS) * g_ref[...]
    _to_tiles(xn_ref, xn)
    xh = xn.astype(BF16)
    xl = (xn - xh.astype(F32)).astype(BF16)
    wh = wh_ref[...]
    lg = _nt(wh, xh) + _nt(wl_ref[...], xh) + _nt(wh, xl) + b_ref[...]

    gl = lg[0:N_GROUPS]
    r4 = lax.broadcasted_iota(I32, gl.shape, 0)
    gmax = jnp.max(gl, axis=0, keepdims=True)
    grp = jnp.min(jnp.where(gl == gmax, r4, N_GROUPS), axis=0, keepdims=True)
    p_grp = 1.0 / jnp.sum(jnp.exp(gl - gmax), axis=0, keepdims=True)

    r0 = ROUTER_EXPERT_ROW0
    fine = lg[r0:r0 + EXPERTS_PER_GROUP]
    for gi in range(1, N_GROUPS):
        fine = jnp.where(grp == gi, lg[r0 + gi * EXPERTS_PER_GROUP:r0 + (gi + 1) * EXPERTS_PER_GROUP], fine)
    r8 = lax.broadcasted_iota(I32, fine.shape, 0)
    m1 = jnp.max(fine, axis=0, keepdims=True)
    i1 = jnp.min(jnp.where(fine == m1, r8, EXPERTS_PER_GROUP), axis=0, keepdims=True)
    rest = jnp.where(r8 == i1, -jnp.inf, fine)
    m2 = jnp.max(rest, axis=0, keepdims=True)
    i2 = jnp.min(jnp.where(rest == m2, r8, EXPERTS_PER_GROUP), axis=0, keepdims=True)
    e21 = jnp.exp(m2 - m1)
    t1 = 1.0 / (1.0 + e21)
    wgt_ref[0:1, :] = p_grp * t1
    wgt_ref[1:2, :] = p_grp * (e21 * t1)
    e1 = grp * EXPERTS_PER_GROUP + i1
    e2 = grp * EXPERTS_PER_GROUP + i2
    eid_ref[0:1, :] = e1
    eid_ref[1:2, :] = e2

    r32 = lax.broadcasted_iota(I32, (N_EXPERTS, tm), 0)
    is1 = r32 == e1
    is2 = r32 == e2
    member = jnp.logical_or(is1, is2)
    ta = lax.broadcasted_iota(I32, (tm, tm), 0)
    tb = lax.broadcasted_iota(I32, (tm, tm), 1)
    before = (ta < tb).astype(BF16)
    prior = _nn(member.astype(BF16), before) + cnt_ref[:, 0:1]
    rank_ref[0:1, :] = jnp.sum(jnp.where(is1, prior, 0.0), axis=0, keepdims=True).astype(I32)
    rank_ref[1:2, :] = jnp.sum(jnp.where(is2, prior, 0.0), axis=0, keepdims=True).astype(I32)
    cnt_ref[...] = cnt_ref[...] + jnp.sum(member.astype(F32), axis=1, keepdims=True)


def _proj_res_router_kernel(a_ref, w_ref, r_ref, g_ref, wh_ref, wl_ref, b_ref, o_ref, *route_out):
    h = r_ref[...] + _nn(a_ref[...], w_ref[...])
    o_ref[...] = h
    _route_tile(h, g_ref, wh_ref, wl_ref, b_ref, *route_out)


def _route_plumbing(t, d, tm, g, wg, bg, we, be):
    r0 = ROUTER_EXPERT_ROW0
    w_all = jnp.zeros((ROUTER_ROWS, d), F32).at[0:N_GROUPS].set(wg.T).at[r0:r0 + N_EXPERTS].set(we.T)
    b_all = jnp.zeros((ROUTER_ROWS, 1), F32).at[0:N_GROUPS, 0].set(bg).at[r0:r0 + N_EXPERTS, 0].set(be)
    wh = w_all.astype(BF16)
    wl = (w_all - wh.astype(F32)).astype(BF16)
    row2 = lambda i: (0, i)
    route_in = [pl.BlockSpec((1, d), lambda i: (0, 0)),
                pl.BlockSpec((ROUTER_ROWS, d), lambda i: (0, 0)),
                pl.BlockSpec((ROUTER_ROWS, d), lambda i: (0, 0)),
                pl.BlockSpec((ROUTER_ROWS, 1), lambda i: (0, 0))]
    route_specs = [pl.BlockSpec((tm * TOKEN_TILE, 128), lambda i: (i, 0)),
                   pl.BlockSpec((2, tm), row2),
                   pl.BlockSpec((2, tm), row2),
                   pl.BlockSpec((2, tm), row2),
                   pl.BlockSpec((N_EXPERTS, 128), lambda i: (0, 0))]
    route_shapes = [jax.ShapeDtypeStruct((t * TOKEN_TILE, 128), F32),
                    jax.ShapeDtypeStruct((2, t), I32),
                    jax.ShapeDtypeStruct((2, t), F32),
                    jax.ShapeDtypeStruct((2, t), I32),
                    jax.ShapeDtypeStruct((N_EXPERTS, 128), F32)]
    return (g.reshape(1, d), wh, wl, b_all), route_in, route_specs, route_shapes


def proj_res_router(a, w, h, route_params, tm=ROUTER_TM):
    t, d = h.shape
    row = pl.BlockSpec((tm, d), lambda i: (i, 0))
    route_args, route_in, route_specs, route_shapes = _route_plumbing(t, d, tm, *route_params)
    out = pl.pallas_call(
        _proj_res_router_kernel,
        grid=(t // tm,),
        in_specs=[row, pl.BlockSpec((d, d), lambda i: (0, 0)), row] + route_in,
        out_specs=[row] + route_specs,
        out_shape=[jax.ShapeDtypeStruct((t, d), F32)] + route_shapes,
        compiler_params=_params("arbitrary"),
        name="proj_res_router",
    )(a, w, h, *route_args)
    return out[0], out[1:]


def _dispatch_kernel(last_ref, pos_ref, x_hbm, o_ref, xbuf, zero_ref, load_sem, row_sem):
    i = pl.program_id(0)
    n = pl.num_programs(0)
    tm = xbuf.shape[1] // TOKEN_TILE
    in_rows = tm * TOKEN_TILE
    tile_rows = EXPERT_TM * TOKEN_TILE
    cur = i % 2
    sem = row_sem.at[0]

    def load(j, slot):
        src = pl.multiple_of(j * in_rows, in_rows)
        return pltpu.make_async_copy(x_hbm.at[pl.ds(src, in_rows)], xbuf.at[slot], load_sem.at[slot])

    def drain(slot):
        for _ in range(2):
            pltpu.make_async_copy(xbuf.at[slot], o_ref.at[pl.ds(0, in_rows)], row_sem.at[slot]).wait()

    @pl.when(i == 0)
    def _():
        load(0, 0).start()
        zero_ref[...] = jnp.zeros_like(zero_ref)

        def tile_fill(e):
            row = pl.multiple_of(last_ref[e] * TOKEN_TILE, tile_rows)
            return pltpu.make_async_copy(zero_ref, o_ref.at[pl.ds(row, tile_rows)], sem)

        for e in range(N_EXPERTS):
            @pl.when(last_ref[e] >= 0)
            def _():
                tile_fill(e).start()
        for e in range(N_EXPERTS):
            @pl.when(last_ref[e] >= 0)
            def _():
                tile_fill(e).wait()

        def spare_fill(j):
            row = pl.multiple_of(j * tile_rows, tile_rows)
            return pltpu.make_async_copy(zero_ref, o_ref.at[pl.ds(row, tile_rows)], sem)

        n_tiles = o_ref.shape[0] // tile_rows
        lax.fori_loop(last_ref[N_EXPERTS], n_tiles, lambda j, c: (spare_fill(j).start(), c)[1], 0)
        lax.fori_loop(last_ref[N_EXPERTS], n_tiles, lambda j, c: (spare_fill(j).wait(), c)[1], 0)

    load(i, cur).wait()

    @pl.when(i > 0)
    def _():
        drain(1 - cur)

    @pl.when(i + 1 < n)
    def _():
        load(i + 1, 1 - cur).start()

    def row_copy(r, k):
        dst = pl.multiple_of(pos_ref[0, 0, k * tm + r] * TOKEN_TILE, TOKEN_TILE)
        return pltpu.make_async_copy(xbuf.at[cur, pl.ds(r * TOKEN_TILE, TOKEN_TILE)],
                                     o_ref.at[pl.ds(dst, TOKEN_TILE)], row_sem.at[cur])

    def issue(blk, carry):
        for u in range(ROW_DMA_UNROLL):
            r = blk * ROW_DMA_UNROLL + u
            row_copy(r, 0).start(priority=u % 2)
            row_copy(r, 1).start(priority=(u + 1) % 2)
        return carry

    lax.fori_loop(0, tm // ROW_DMA_UNROLL, issue, 0)

    @pl.when(i == n - 1)
    def _():
        drain(cur)


def dispatch(xn, pos3, last_tile_row, n_rows, tm=DISPATCH_TM):
    t = xn.shape[0] // TOKEN_TILE
    return pl.pallas_call(
        _dispatch_kernel,
        grid_spec=pltpu.PrefetchScalarGridSpec(
            num_scalar_prefetch=1,
            grid=(t // tm,),
            in_specs=[pl.BlockSpec((1, 1, 2 * tm), lambda i, last: (i, 0, 0), memory_space=pltpu.SMEM),
                      pl.BlockSpec(memory_space=pl.ANY)],
            out_specs=pl.BlockSpec(memory_space=pl.ANY),
            scratch_shapes=[pltpu.VMEM((2, tm * TOKEN_TILE, 128), xn.dtype),
                            pltpu.VMEM((EXPERT_TM * TOKEN_TILE, 128), xn.dtype),
                            pltpu.SemaphoreType.DMA((2,)),
                            pltpu.SemaphoreType.DMA((2,))]),
        out_shape=jax.ShapeDtypeStruct((n_rows * TOKEN_TILE, 128), xn.dtype),
        compiler_params=_params("arbitrary"),
        name="moe_dispatch",
    )(last_tile_row, pos3, xn)


def _expert_kernel(te_ref, nv_ref, par_ref, nxt_ref, x_hbm, wgu_hbm, wd_hbm, y_ref,
                   xbuf, wgu_f, wd_f, wgu_b, wd_b, xsem, wsem, *, layer):
    i = pl.program_id(0)
    n_valid = nv_ref[0]
    tm = xbuf.shape[1] // TOKEN_TILE
    in_rows = tm * TOKEN_TILE
    valid = i < n_valid
    changed = te_ref[i] != te_ref[jnp.maximum(i - 1, 0)]
    first = jnp.logical_or(i == 0, changed)

    def xload(j):
        slot = j % EXPERT_X_BUFS
        src = pl.multiple_of(j * in_rows, in_rows)
        return pltpu.make_async_copy(x_hbm.at[pl.ds(src, in_rows)], xbuf.at[slot], xsem.at[slot])

    def wload(e, slot):
        return (pltpu.make_async_copy(wgu_hbm.at[layer, e], wgu_f.at[slot], wsem.at[slot]),
                pltpu.make_async_copy(wd_hbm.at[layer, e], wd_f.at[slot], wsem.at[slot]))

    @pl.when(i == 0)
    def _():
        for cp in wload(te_ref[0], par_ref[0]):
            cp.start()
        for j in range(EXPERT_X_BUFS - 1):
            @pl.when(j < n_valid)
            def _():
                xload(j).start()

    @pl.when(i + EXPERT_X_BUFS - 1 < n_valid)
    def _():
        xload(i + EXPERT_X_BUFS - 1).start()

    @pl.when(jnp.logical_and(valid, first))
    def _():
        slot = par_ref[i]
        for cp in wload(te_ref[i], slot):
            cp.wait()

        @pl.when(nxt_ref[i] >= 0)
        def _():
            for cp in wload(nxt_ref[i], 1 - slot):
                cp.start()

        wgu_b[...] = wgu_f[slot].astype(BF16)
        wd_b[...] = wd_f[slot].astype(BF16)

    @pl.when(valid)
    def _():
        xload(i).wait()
        x = _from_tiles(xbuf.at[i % EXPERT_X_BUFS], tm).astype(BF16)
        au = _nn(x, wgu_b[...])
        a = au[:, :D_EXPERT]
        u = au[:, D_EXPERT:]
        mid = (jax.nn.silu(a) * u).astype(BF16)
        _to_tiles(y_ref, _nn(mid, wd_b[...]))

    @pl.when(jnp.logical_not(valid))
    def _():
        y_ref[...] = jnp.zeros_like(y_ref)


def experts(xg, w_gu, w_down, layer, tile_expert, n_valid, seg_parity, seg_next, tm=EXPERT_TM):
    d = D_MODEL
    p = xg.shape[0] // TOKEN_TILE
    n_tiles = p // tm
    return pl.pallas_call(
        functools.partial(_expert_kernel, layer=layer),
        grid_spec=pltpu.PrefetchScalarGridSpec(
            num_scalar_prefetch=4,
            grid=(n_tiles,),
            in_specs=[pl.BlockSpec(memory_space=pl.ANY),
                      pl.BlockSpec(memory_space=pl.ANY),
                      pl.BlockSpec(memory_space=pl.ANY)],
            out_specs=pl.BlockSpec((tm * TOKEN_TILE, 128), lambda i, *_: (i, 0)),
            scratch_shapes=[pltpu.VMEM((EXPERT_X_BUFS, tm * TOKEN_TILE, 128), F32),
                            pltpu.VMEM((2, d, 2 * D_EXPERT), F32),
                            pltpu.VMEM((2, D_EXPERT, d), F32),
                            pltpu.VMEM((d, 2 * D_EXPERT), BF16),
                            pltpu.VMEM((D_EXPERT, d), BF16),
                            pltpu.SemaphoreType.DMA((EXPERT_X_BUFS,)),
                            pltpu.SemaphoreType.DMA((2,))]),
        out_shape=jax.ShapeDtypeStruct((p * TOKEN_TILE, 128), F32),
        compiler_params=_params("arbitrary"),
        name="moe_experts",
    )(tile_expert, n_valid, seg_parity, seg_next, xg, w_gu, w_down)


def _qkv_tail(x, gq_ref, gkv_ref, wqt_ref, wk_ref, wvt_ref, qt_ref, k_ref, vt_ref):
    y = x * lax.rsqrt(jnp.mean(x * x, axis=-1, keepdims=True) + NORM_EPS)
    xq = (y * gq_ref[...]).astype(BF16)
    xkv = (y * gkv_ref[...]).astype(BF16)
    qt_ref[0] = (_nt(wqt_ref[...], xq) * (B_HEAD_DIM ** -0.5 * LOG2E)).astype(qt_ref.dtype)
    k_ref[...] = _nn(xkv, wk_ref[...]).astype(k_ref.dtype)
    vt_ref[0] = _nt(wvt_ref[...], xkv).astype(vt_ref.dtype)


def _combine_kernel(pos_ref, pos_next_ref, h_ref, w_ref, y_ref, *rest, tail):
    if tail == "norm":
        g_ref, o_ref, buf, sem = rest
    else:
        gq_ref, gkv_ref, wqt_ref, wk_ref, wvt_ref, o_ref, qt_ref, k_ref, vt_ref, buf, sem = rest
    i = pl.program_id(0)
    tm = h_ref.shape[0]
    cur = i % 2

    def gather_tile(p_ref, half):
        def row_copy(r, k):
            src = pl.multiple_of(p_ref[0, 0, k * tm + r] * TOKEN_TILE, TOKEN_TILE)
            return pltpu.make_async_copy(y_ref.at[pl.ds(src, TOKEN_TILE)],
                                         buf.at[2 * half + k, pl.ds(r * TOKEN_TILE, TOKEN_TILE)],
                                         sem.at[half])

        def issue(blk, carry):
            for u in range(ROW_DMA_UNROLL):
                r = blk * ROW_DMA_UNROLL + u
                row_copy(r, 0).start(priority=u % 2)
                row_copy(r, 1).start(priority=(u + 1) % 2)
            return carry

        lax.fori_loop(0, tm // ROW_DMA_UNROLL, issue, 0)

    @pl.when(i == 0)
    def _():
        gather_tile(pos_ref, 0)

    @pl.when(i + 1 < pl.num_programs(0))
    def _():
        gather_tile(pos_next_ref, 1 - cur)

    for k in range(2):
        pltpu.make_async_copy(y_ref.at[pl.ds(0, tm * TOKEN_TILE)], buf.at[2 * cur + k], sem.at[cur]).wait()
    w = w_ref[...]
    out = (h_ref[...] + w[:, 0:1] * _from_tiles(buf.at[2 * cur], tm)
           + w[:, 1:2] * _from_tiles(buf.at[2 * cur + 1], tm))
    if tail == "norm":
        o_ref[...] = out * lax.rsqrt(jnp.mean(out * out, axis=-1, keepdims=True) + NORM_EPS) * g_ref[...]
    else:
        o_ref[...] = out
        _qkv_tail(out, gq_ref, gkv_ref, wqt_ref, wk_ref, wvt_ref, qt_ref, k_ref, vt_ref)


def combine(h, wgt_t, pos3, y, tail, tail_args, batch, seq, tm=COMBINE_TM):
    t, d = h.shape
    nt = t // tm
    nblk = seq // tm
    row = pl.BlockSpec((tm, d), lambda i: (i, 0))
    vec = pl.BlockSpec((1, d), lambda i: (0, 0))
    full = pl.BlockSpec((d, d), lambda i: (0, 0))
    tr = pl.BlockSpec((1, d, tm), lambda i: (i // nblk, 0, i % nblk))
    if tail == "norm":
        (g,) = tail_args
        extra_in, extra_specs = [g.reshape(1, d)], [vec]
        out_specs, out_shape = row, jax.ShapeDtypeStruct((t, d), F32)
    else:
        g_q, g_kv, wqt, wk, wvt = tail_args
        extra_in = [g_q.reshape(1, d), g_kv.reshape(1, d), wqt, wk, wvt]
        extra_specs = [vec, vec, full, full, full]
        out_specs = [row, tr, row, tr]
        out_shape = [jax.ShapeDtypeStruct((t, d), F32),
                     jax.ShapeDtypeStruct((batch, d, seq), BF16),
                     jax.ShapeDtypeStruct((t, d), BF16),
                     jax.ShapeDtypeStruct((batch, d, seq), BF16)]
    return pl.pallas_call(
        functools.partial(_combine_kernel, tail=tail),
        grid=(nt,),
        in_specs=[pl.BlockSpec((1, 1, 2 * tm), lambda i: (i, 0, 0), memory_space=pltpu.SMEM),
                  pl.BlockSpec((1, 1, 2 * tm), lambda i: (jnp.minimum(i + 1, nt - 1), 0, 0),
                               memory_space=pltpu.SMEM),
                  row,
                  pl.BlockSpec((tm, 2), lambda i: (i, 0)),
                  pl.BlockSpec(memory_space=pl.ANY)] + extra_specs,
        out_specs=out_specs,
        out_shape=out_shape,
        scratch_shapes=[pltpu.VMEM((4, tm * TOKEN_TILE, 128), F32), pltpu.SemaphoreType.DMA((2,))],
        compiler_params=_params("arbitrary"),
        name="moe_combine_" + tail,
    )(pos3, pos3, h, wgt_t, y, *extra_in)


def hier_moe_layer(h, routing, w_gu, w_down, layer, tail, tail_args, batch, seq):
    t, d = h.shape
    xn, eid, wgt, rank, cnt = routing
    counts = cnt[:, 0].astype(I32)
    padded = ((counts + EXPERT_TM - 1) // EXPERT_TM) * EXPERT_TM
    ends = jnp.cumsum(padded)
    offs = ends - padded
    n_rows = 2 * t + N_EXPERTS * EXPERT_TM
    n_tiles = n_rows // EXPERT_TM
    n_valid = (ends[-1] // EXPERT_TM).astype(I32).reshape(1)
    tile_start = jnp.arange(n_tiles, dtype=I32) * EXPERT_TM
    tile_start = jnp.minimum(tile_start, ends[-1] - 1)
    tile_expert = jnp.sum((ends[None, :] <= tile_start[:, None]).astype(I32), axis=1)
    onehot = eid[None] == jnp.arange(N_EXPERTS, dtype=I32)[:, None, None]
    pos = jnp.sum(jnp.where(onehot, offs[:, None, None], 0), axis=0) + rank

    def tiled(tm):
        return pos.reshape(2, t // tm, tm).transpose(1, 0, 2).reshape(t // tm, 1, 2 * tm)

    last_tile_row = jnp.concatenate([jnp.where(padded > 0, ends - EXPERT_TM, -1).astype(I32), n_valid])
    xg = dispatch(xn, tiled(DISPATCH_TM), last_tile_row, n_rows)
    nonempty = padded > 0
    ordinal = jnp.cumsum(nonempty.astype(I32)) - 1
    ids = jnp.arange(N_EXPERTS, dtype=I32)
    later = jnp.logical_and(nonempty[None, :], ids[None, :] > ids[:, None])
    next_expert = jnp.min(jnp.where(later, ids[None, :], N_EXPERTS), axis=1)
    next_expert = jnp.where(next_expert == N_EXPERTS, -1, next_expert).astype(I32)
    y = experts(xg, w_gu, w_down, layer, tile_expert, n_valid,
                (ordinal[tile_expert] % 2).astype(I32), next_expert[tile_expert])
    return combine(h, wgt.T, tiled(COMBINE_TM), y, tail, tail_args, batch, seq)


def _attn_kernel(q_ref, k_ref, vt_ref, lam_ref, g_ref, o_ref, qq_ref, m_ref, acc_ref,
                 a_ref, c_ref, *sp_refs, lambda_init, heads):
    s_refs, p_refs = sp_refs[:heads], sp_refs[heads:]
    qi = pl.program_id(2)
    tq, tk, hd = ATT_TQ, ATT_TK, 2 * B_HEAD_DIM
    feat = lax.broadcasted_iota(I32, (hd, tq), 0)
    for g in range(heads):
        qt = q_ref[0, g * hd:(g + 1) * hd, :]
        zero = jnp.zeros_like(qt)
        qq_ref[g, :, 0:tq] = jnp.where(feat < B_HEAD_DIM, qt, zero)
        qq_ref[g, :, tq:2 * tq] = jnp.where(feat >= B_HEAD_DIM, qt, zero)
    m_ref[...] = jnp.full_like(m_ref, -jnp.inf)
    acc_ref[...] = jnp.zeros_like(acc_ref)
    ones = jnp.ones((ATT_SUM_ROWS, tk), BF16)
    dyn0 = jnp.minimum(qi, 0)

    def step(j, masked, nk=tk):
        off = pl.multiple_of(j * tk, tk)
        if masked:
            krow = lax.broadcasted_iota(I32, (nk, 2 * tq), 0)
            qcol = lax.broadcasted_iota(I32, (nk, 2 * tq), 1)
            visible = off + krow <= qi * tq + jnp.where(qcol >= tq, qcol - tq, qcol)
        for g in range(heads):
            kb = k_ref[pl.ds(off, nk), g * hd:(g + 1) * hd]
            s = _nn(kb, qq_ref[g])
            if masked:
                s = jnp.where(visible, s, -jnp.inf)
            s_refs[g][0, 0:nk, 0:2 * tq] = s
            c_ref[g] = jnp.max(s, axis=0, keepdims=True)
        for g in range(heads):
            m_old = m_ref[g]
            m_new = jnp.maximum(m_old, c_ref[g])
            a_ref[g] = jnp.exp2(m_old - m_new)
            m_ref[g] = m_new
            for c in range(0, nk, ATT_CHUNK):
                p_refs[g][0, c:c + ATT_CHUNK, 0:2 * tq] = jnp.exp2(
                    s_refs[g][dyn0, c:c + ATT_CHUNK, 0:2 * tq] - m_new).astype(BF16)
        for g in range(heads):
            vtb = vt_ref[0, g * hd:(g + 1) * hd, pl.ds(off, nk)]
            lhs = jnp.concatenate([vtb, ones[:, 0:nk]], axis=0)
            acc_ref[g] = a_ref[g] * acc_ref[g] + _nn(lhs, p_refs[g][dyn0, 0:nk, 0:2 * tq])

    n_full = (qi * tq) // tk

    def full_step(j, carry):
        step(j, False)
        return carry

    lax.fori_loop(0, n_full, full_step, 0)

    first_part = (qi * tq) % tk + tq <= tk // 2

    @pl.when(first_part)
    def _():
        step(n_full, True, tk // 2)

    @pl.when(jnp.logical_not(first_part))
    def _():
        step(n_full, True)

    lam = lam_ref[...]
    lam_full = (jnp.exp(jnp.sum(lam[0:1] * lam[1:2], axis=-1, keepdims=True))
                - jnp.exp(jnp.sum(lam[2:3] * lam[3:4], axis=-1, keepdims=True)) + lambda_init)
    for g in range(heads):
        acc = acc_ref[g]
        on = acc[:hd] / acc[hd:hd + 1]
        ot = on[:, :tq] - lam_full * on[:, tq:]
        o = ot.T
        o = o * lax.rsqrt(jnp.mean(o * o, axis=-1, keepdims=True) + SUBLN_EPS) * g_ref[...]
        o_ref[:, g * hd:(g + 1) * hd] = (o * (1.0 - lambda_init)).astype(o_ref.dtype)


def diff_attn(qt, k, vt, lam, subln_g, batch, seq, lambda_init, heads=ATT_HEADS):
    t, d = k.shape
    nq = seq // ATT_TQ
    hd = 2 * B_HEAD_DIM
    w = heads * hd
    return pl.pallas_call(
        functools.partial(_attn_kernel, lambda_init=lambda_init, heads=heads),
        grid=(batch, B_HEADS // heads, nq),
        in_specs=[pl.BlockSpec((1, w, ATT_TQ), lambda b, h, i: (b, h, i)),
                  pl.BlockSpec((seq, w), lambda b, h, i: (b, h)),
                  pl.BlockSpec((1, w, seq), lambda b, h, i: (b, h, 0)),
                  pl.BlockSpec(lam.shape, lambda b, h, i: (0, 0)),
                  pl.BlockSpec((1, hd), lambda b, h, i: (0, 0))],
        out_specs=pl.BlockSpec((ATT_TQ, w), lambda b, h, i: (b * nq + i, h)),
        out_shape=jax.ShapeDtypeStruct((t, d), BF16),
        scratch_shapes=[pltpu.VMEM((heads, hd, 2 * ATT_TQ), BF16),
                        pltpu.VMEM((heads, 1, 2 * ATT_TQ), F32),
                        pltpu.VMEM((heads, hd + ATT_SUM_ROWS, 2 * ATT_TQ), F32),
                        pltpu.VMEM((heads, 1, 2 * ATT_TQ), F32),
                        pltpu.VMEM((heads, 1, 2 * ATT_TQ), F32)]
        + [pltpu.VMEM((1, ATT_TK, 2 * ATT_TQ + ATT_LANE_PAD), F32) for _ in range(heads)]
        + [pltpu.VMEM((1, ATT_TK, 2 * ATT_TQ + ATT_LANE_PAD), BF16) for _ in range(heads)],
        compiler_params=_params("parallel", "parallel", "arbitrary"),
        name="diff_attn",
    )(qt, k, vt, lam, subln_g.reshape(1, hd))


def kernel(x, a_norm_g, a_w_in, a_lb, a_onorm_g, a_w_out, kv_norm_g, w_kv, b_norm_g, b_w_q, b_lam,
           b_subln_g, b_w_out, ffn_norm_g, router_g_w, router_g_b, router_e_w, router_e_b,
           expert_w_gu, expert_w_down, final_norm_g):
    batch, seq, d = x.shape
    assert d == D_MODEL and a_norm_g.shape[0] == 1 and b_norm_g.shape[0] == 1
    assert seq % max(2 * HG_BLOCK, ATT_TK, ROUTER_TM) == 0
    t = batch * seq
    h = x.reshape(t, d)

    h, routing = hgrn2_layer(h, a_norm_g[0], a_w_in[0].astype(BF16), a_lb, a_onorm_g[0],
                             a_w_out[0].astype(BF16), seq,
                             (ffn_norm_g[0], router_g_w[0], router_g_b[0], router_e_w[0], router_e_b[0]))
    qkv_args = (b_norm_g[0], kv_norm_g, b_w_q[0].T.astype(BF16),
                w_kv[:, :d].astype(BF16), w_kv[:, d:].T.astype(BF16))
    h, qt, k, vt = hier_moe_layer(h, routing, expert_w_gu, expert_w_down, 0, "qkv", qkv_args, batch, seq)

    lambda_init = 0.8 - 0.6 * math.exp(-0.3 * 1)
    o = diff_attn(qt, k, vt, b_lam[0], b_subln_g[0], batch, seq, lambda_init)
    h, routing = proj_res_router(o, b_w_out[0].astype(BF16), h,
                                 (ffn_norm_g[1], router_g_w[1], router_g_b[1], router_e_w[1], router_e_b[1]))
    h = hier_moe_layer(h, routing, expert_w_gu, expert_w_down, 1, "norm", (final_norm_g,), batch, seq)
    return h.reshape(batch, seq, d)
```

```python
import functools
import math

import jax
import jax.numpy as jnp
from jax import lax
from jax.experimental import pallas as pl
from jax.experimental.pallas import tpu as pltpu

F32 = jnp.float32
BF16 = jnp.bfloat16
I32 = jnp.int32

D_MODEL = 1024
A_HEADS = 8
A_HEAD_DIM = 128
B_HEADS = 8
B_HEAD_DIM = 64
N_GROUPS = 4
EXPERTS_PER_GROUP = 8
N_EXPERTS = N_GROUPS * EXPERTS_PER_GROUP
D_EXPERT = 512
NORM_EPS = 1e-6
SUBLN_EPS = 1e-5
LOG2E = 1.4426950408889634

GLA_BLOCK = 128
GLA_HALF = GLA_BLOCK // 2
HG_BLOCK = 256
ATT_TQ = 256
ATT_TK = 512
ATT_HEADS = 8
ATT_CHUNK = 128
ATT_SUM_ROWS = 16
ROUTER_TM = 512
ROUTER_EXPERT_ROW0 = 8
ROUTER_ROWS = ROUTER_EXPERT_ROW0 + N_EXPERTS
EXPERT_TM = 512
EXPERT_X_BUFS = 3
DISPATCH_TM = 2048
COMBINE_TM = 512
ROW_DMA_UNROLL = 8
TOKEN_TILE = 8
VMEM_LIMIT = 56 * 1024 * 1024


def _nt(a, b):
    return lax.dot_general(a, b, (((1,), (1,)), ((), ())), preferred_element_type=F32)


def _nn(a, b):
    return jnp.dot(a, b, preferred_element_type=F32)


def _to_tiles(ref, x):
    tm = x.shape[0]
    for s in range(TOKEN_TILE):
        ref[pl.ds(s, tm, stride=TOKEN_TILE), :] = x[:, s * 128:(s + 1) * 128]


def _from_tiles(ref, tm):
    return jnp.concatenate([ref[pl.ds(s, tm, stride=TOKEN_TILE), :] for s in range(TOKEN_TILE)], axis=1)


def _params(*sem):
    return pltpu.CompilerParams(dimension_semantics=sem, vmem_limit_bytes=VMEM_LIMIT)


def _hgrn2_project(x, g_ref, w_ref, proj_ref):
    d = x.shape[1]
    y = x * lax.rsqrt(jnp.mean(x * x, axis=-1, keepdims=True) + NORM_EPS)
    xn = (y * g_ref[...]).astype(BF16)
    for c in range(w_ref.shape[1] // d):
        proj_ref[:, c * d:(c + 1) * d] = _nn(xn, w_ref[:, c * d:(c + 1) * d])


def _gla_block(proj_ref, row0, lb, og, st_ref, out_ref, out_row0):
    d = D_MODEL
    rows = pl.ds(pl.multiple_of(row0, GLA_BLOCK), GLA_BLOCK)
    fz = proj_ref[rows, d:2 * d]
    logf = jnp.log(lb + (1.0 - lb) * jax.nn.sigmoid(fz))
    kk = (1.0 - lb) * jax.nn.sigmoid(-fz)
    qq = jax.nn.silu(proj_ref[rows, 0:d])

    row = lax.broadcasted_iota(I32, logf.shape, 0)
    b = logf
    shift = 1
    while shift < GLA_BLOCK:
        b = b + jnp.where(row >= shift, pltpu.roll(b, shift, axis=0), 0.0)
        shift *= 2

    h0, h1 = slice(0, GLA_HALF), slice(GLA_HALF, GLA_BLOCK)
    b_a_mid = b[GLA_HALF // 2 - 1:GLA_HALF // 2]
    b_a_end = b[GLA_HALF - 1:GLA_HALF]
    b_b_mid = b[GLA_HALF + GLA_HALF // 2 - 1:GLA_HALF + GLA_HALF // 2]
    b_end = b[GLA_BLOCK - 1:GLA_BLOCK]

    qa_n = (qq[h0] * jnp.exp(b[h0] - b_a_mid)).astype(BF16)
    ka_n = (kk[h0] * jnp.exp(b_a_mid - b[h0])).astype(BF16)
    qb_n = (qq[h1] * jnp.exp(b[h1] - b_b_mid)).astype(BF16)
    kb_n = (kk[h1] * jnp.exp(b_b_mid - b[h1])).astype(BF16)
    qb_x = (qq[h1] * jnp.exp(b[h1] - b_a_end)).astype(BF16)
    ka_x = (kk[h0] * jnp.exp(b_a_end - b[h0])).astype(BF16)
    q_dec = (qq * jnp.exp(b)).astype(BF16)
    k_end = (kk * jnp.exp(b_end - b)).astype(BF16)
    d_end = jnp.exp(b_end)

    rr = lax.broadcasted_iota(I32, (GLA_HALF, GLA_HALF), 0)
    ss = lax.broadcasted_iota(I32, (GLA_HALF, GLA_HALF), 1)
    causal = ss <= rr

    for h in range(A_HEADS):
        hs = slice(h * A_HEAD_DIM, (h + 1) * A_HEAD_DIM)
        v = proj_ref[rows, 2 * d + h * A_HEAD_DIM:2 * d + (h + 1) * A_HEAD_DIM]
        vb = v.astype(BF16)
        st = st_ref[h]
        st_b = st.astype(BF16)
        p_aa = jnp.where(causal, _nt(qa_n[:, hs], ka_n[:, hs]), 0.0).astype(BF16)
        p_bb = jnp.where(causal, _nt(qb_n[:, hs], kb_n[:, hs]), 0.0).astype(BF16)
        p_ba = _nt(qb_x[:, hs], ka_x[:, hs]).astype(BF16)
        inter = _nt(q_dec[:, hs], st_b)
        o_a = _nn(p_aa, vb[h0]) + inter[h0]
        o_b = _nn(p_bb, vb[h1]) + _nn(p_ba, vb[h0]) + inter[h1]
        st_ref[h] = d_end[:, hs] * st + _nn(v.T.astype(BF16), k_end[:, hs])
        gate = jax.nn.silu(proj_ref[rows, 3 * d + h * A_HEAD_DIM:3 * d + (h + 1) * A_HEAD_DIM])
        for half, o in ((0, o_a), (1, o_b)):
            on = o * lax.rsqrt(jnp.mean(o * o, axis=-1, keepdims=True) + NORM_EPS)
            r0 = out_row0 + half * GLA_HALF
            out_ref[r0:r0 + GLA_HALF, hs] = (on * og * gate[half * GLA_HALF:(half + 1) * GLA_HALF]
                                             ).astype(out_ref.dtype)


def _hgrn2_kernel(x_ref, xnext_ref, g_ref, win_ref, alb_ref, og_ref, wout_ref,
                  rg_ref, rwh_ref, rwl_ref, rb_ref, o_ref, xn_ref, eid_ref, wgt_ref, rank_ref, cnt_ref,
                  proj_a, proj_b, gated_a, gated_b, st_ref, *, blocks_per_seq):
    step = pl.program_id(0)
    dyn0 = jnp.minimum(step, 0)
    alb = alb_ref[...]
    e = jnp.exp(alb - jnp.max(alb, axis=0, keepdims=True))
    lb = e[0:1] / jnp.sum(e, axis=0, keepdims=True)
    og = og_ref[...]

    @pl.when(step == 0)
    def _():
        _hgrn2_project(x_ref[0:HG_BLOCK, :], g_ref, win_ref, proj_a)

    @pl.when((2 * step) % blocks_per_seq == 0)
    def _():
        st_ref[...] = jnp.zeros_like(st_ref)

    _hgrn2_project(x_ref[HG_BLOCK:2 * HG_BLOCK, :], g_ref, win_ref, proj_b)
    for sub in range(HG_BLOCK // GLA_BLOCK):
        _gla_block(proj_a, dyn0 + sub * GLA_BLOCK, lb, og, st_ref, gated_a, sub * GLA_BLOCK)
    out_a = x_ref[0:HG_BLOCK, :] + _nn(gated_a[...], wout_ref[...])
    o_ref[0:HG_BLOCK, :] = out_a

    _hgrn2_project(xnext_ref[...], g_ref, win_ref, proj_a)
    for sub in range(HG_BLOCK // GLA_BLOCK):
        _gla_block(proj_b, dyn0 + sub * GLA_BLOCK, lb, og, st_ref, gated_b, sub * GLA_BLOCK)
    out_b = x_ref[HG_BLOCK:2 * HG_BLOCK, :] + _nn(gated_b[...], wout_ref[...])
    o_ref[HG_BLOCK:2 * HG_BLOCK, :] = out_b

    _route_tile(jnp.concatenate([out_a, out_b], axis=0), rg_ref, rwh_ref, rwl_ref, rb_ref,
                xn_ref, eid_ref, wgt_ref, rank_ref, cnt_ref)


def hgrn2_layer(x, norm_g, w_in_bf16, a_lb, onorm_g, w_out_bf16, seq, route_params):
    t, d = x.shape
    n = w_in_bf16.shape[1]
    nblocks = t // HG_BLOCK
    assert 2 * HG_BLOCK == ROUTER_TM
    route_args, route_in, route_specs, route_shapes = _route_plumbing(t, d, ROUTER_TM, *route_params)
    out = pl.pallas_call(
        functools.partial(_hgrn2_kernel, blocks_per_seq=seq // HG_BLOCK),
        grid=(nblocks // 2,),
        in_specs=[pl.BlockSpec((2 * HG_BLOCK, d), lambda i: (i, 0)),
                  pl.BlockSpec((HG_BLOCK, d), lambda i: (jnp.minimum(2 * i + 2, nblocks - 1), 0)),
                  pl.BlockSpec((1, d), lambda i: (0, 0)),
                  pl.BlockSpec((d, n), lambda i: (0, 0)),
                  pl.BlockSpec(a_lb.shape, lambda i: (0, 0)),
                  pl.BlockSpec((1, A_HEAD_DIM), lambda i: (0, 0)),
                  pl.BlockSpec((d, d), lambda i: (0, 0))] + route_in,
        out_specs=[pl.BlockSpec((2 * HG_BLOCK, d), lambda i: (i, 0))] + route_specs,
        out_shape=[jax.ShapeDtypeStruct((t, d), F32)] + route_shapes,
        scratch_shapes=[pltpu.VMEM((HG_BLOCK, n), F32),
                        pltpu.VMEM((HG_BLOCK, n), F32),
                        pltpu.VMEM((HG_BLOCK, d), BF16),
                        pltpu.VMEM((HG_BLOCK, d), BF16),
                        pltpu.VMEM((A_HEADS, A_HEAD_DIM, A_HEAD_DIM), F32)],
        compiler_params=_params("arbitrary"),
        name="hgrn2_layer",
    )(x, x, norm_g.reshape(1, d), w_in_bf16, a_lb, onorm_g.reshape(1, A_HEAD_DIM), w_out_bf16,
      *route_args)
    return out[0], out[1:]


def _route_tile(x, g_ref, wh_ref, wl_ref, b_ref, xn_ref, eid_ref, wgt_ref, rank_ref, cnt_ref):
    tm = x.shape[0]

    @pl.when(pl.program_id(0) == 0)
    def _():
        cnt_ref[...] = jnp.zeros_like(cnt_ref)

    xn = x * lax.rsqrt(jnp.mean(x * x, axis=-1, keepdims=True) + NORM_EPS) * g_ref[...]
    _to_tiles(xn_ref, xn)
    xh = xn.astype(BF16)
    xl = (xn - xh.astype(F32)).astype(BF16)
    wh = wh_ref[...]
    lg = _nt(wh, xh) + _nt(wl_ref[...], xh) + _nt(wh, xl) + b_ref[...]

    gl = lg[0:N_GROUPS]
    r4 = lax.broadcasted_iota(I32, gl.shape, 0)
    gmax = jnp.max(gl, axis=0, keepdims=True)
    grp = jnp.min(jnp.where(gl == gmax, r4, N_GROUPS), axis=0, keepdims=True)
    p_grp = 1.0 / jnp.sum(jnp.exp(gl - gmax), axis=0, keepdims=True)

    r0 = ROUTER_EXPERT_ROW0
    fine = lg[r0:r0 + EXPERTS_PER_GROUP]
    for gi in range(1, N_GROUPS):
        fine = jnp.where(grp == gi, lg[r0 + gi * EXPERTS_PER_GROUP:r0 + (gi + 1) * EXPERTS_PER_GROUP], fine)
    r8 = lax.broadcasted_iota(I32, fine.shape, 0)
    m1 = jnp.max(fine, axis=0, keepdims=True)
    i1 = jnp.min(jnp.where(fine == m1, r8, EXPERTS_PER_GROUP), axis=0, keepdims=True)
    rest = jnp.where(r8 == i1, -jnp.inf, fine)
    m2 = jnp.max(rest, axis=0, keepdims=True)
    i2 = jnp.min(jnp.where(rest == m2, r8, EXPERTS_PER_GROUP), axis=0, keepdims=True)
    e21 = jnp.exp(m2 - m1)
    t1 = 1.0 / (1.0 + e21)
    wgt_ref[0:1, :] = p_grp * t1
    wgt_ref[1:2, :] = p_grp * (e21 * t1)
    e1 = grp * EXPERTS_PER_GROUP + i1
    e2 = grp * EXPERTS_PER_GROUP + i2
    eid_ref[0:1, :] = e1
    eid_ref[1:2, :] = e2

    r32 = lax.broadcasted_iota(I32, (N_EXPERTS, tm), 0)
    is1 = r32 == e1
    is2 = r32 == e2
    member = jnp.logical_or(is1, is2)
    ta = lax.broadcasted_iota(I32, (tm, tm), 0)
    tb = lax.broadcasted_iota(I32, (tm, tm), 1)
    before = (ta < tb).astype(BF16)
    prior = _nn(member.astype(BF16), before) + cnt_ref[:, 0:1]
    rank_ref[0:1, :] = jnp.sum(jnp.where(is1, prior, 0.0), axis=0, keepdims=True).astype(I32)
    rank_ref[1:2, :] = jnp.sum(jnp.where(is2, prior, 0.0), axis=0, keepdims=True).astype(I32)
    cnt_ref[...] = cnt_ref[...] + jnp.sum(member.astype(F32), axis=1, keepdims=True)


def _proj_res_router_kernel(a_ref, w_ref, r_ref, g_ref, wh_ref, wl_ref, b_ref, o_ref, *route_out):
    h = r_ref[...] + _nn(a_ref[...], w_ref[...])
    o_ref[...] = h
    _route_tile(h, g_ref, wh_ref, wl_ref, b_ref, *route_out)


def _route_plumbing(t, d, tm, g, wg, bg, we, be):
    r0 = ROUTER_EXPERT_ROW0
    w_all = jnp.zeros((ROUTER_ROWS, d), F32).at[0:N_GROUPS].set(wg.T).at[r0:r0 + N_EXPERTS].set(we.T)
    b_all = jnp.zeros((ROUTER_ROWS, 1), F32).at[0:N_GROUPS, 0].set(bg).at[r0:r0 + N_EXPERTS, 0].set(be)
    wh = w_all.astype(BF16)
    wl = (w_all - wh.astype(F32)).astype(BF16)
    row2 = lambda i: (0, i)
    route_in = [pl.BlockSpec((1, d), lambda i: (0, 0)),
                pl.BlockSpec((ROUTER_ROWS, d), lambda i: (0, 0)),
                pl.BlockSpec((ROUTER_ROWS, d), lambda i: (0, 0)),
                pl.BlockSpec((ROUTER_ROWS, 1), lambda i: (0, 0))]
    route_specs = [pl.BlockSpec((tm * TOKEN_TILE, 128), lambda i: (i, 0)),
                   pl.BlockSpec((2, tm), row2),
                   pl.BlockSpec((2, tm), row2),
                   pl.BlockSpec((2, tm), row2),
                   pl.BlockSpec((N_EXPERTS, 128), lambda i: (0, 0))]
    route_shapes = [jax.ShapeDtypeStruct((t * TOKEN_TILE, 128), F32),
                    jax.ShapeDtypeStruct((2, t), I32),
                    jax.ShapeDtypeStruct((2, t), F32),
                    jax.ShapeDtypeStruct((2, t), I32),
                    jax.ShapeDtypeStruct((N_EXPERTS, 128), F32)]
    return (g.reshape(1, d), wh, wl, b_all), route_in, route_specs, route_shapes


def proj_res_router(a, w, h, route_params, tm=ROUTER_TM):
    t, d = h.shape
    row = pl.BlockSpec((tm, d), lambda i: (i, 0))
    route_args, route_in, route_specs, route_shapes = _route_plumbing(t, d, tm, *route_params)
    out = pl.pallas_call(
        _proj_res_router_kernel,
        grid=(t // tm,),
        in_specs=[row, pl.BlockSpec((d, d), lambda i: (0, 0)), row] + route_in,
        out_specs=[row] + route_specs,
        out_shape=[jax.ShapeDtypeStruct((t, d), F32)] + route_shapes,
        compiler_params=_params("arbitrary"),
        name="proj_res_router",
    )(a, w, h, *route_args)
    return out[0], out[1:]


def _dispatch_kernel(last_ref, pos_ref, x_hbm, o_ref, xbuf, zero_ref, load_sem, row_sem):
    i = pl.program_id(0)
    n = pl.num_programs(0)
    tm = xbuf.shape[1] // TOKEN_TILE
    in_rows = tm * TOKEN_TILE
    tile_rows = EXPERT_TM * TOKEN_TILE
    cur = i % 2
    sem = row_sem.at[0]

    def load(j, slot):
        src = pl.multiple_of(j * in_rows, in_rows)
        return pltpu.make_async_copy(x_hbm.at[pl.ds(src, in_rows)], xbuf.at[slot], load_sem.at[slot])

    def drain(slot):
        for _ in range(2):
            pltpu.make_async_copy(xbuf.at[slot], o_ref.at[pl.ds(0, in_rows)], row_sem.at[slot]).wait()

    @pl.when(i == 0)
    def _():
        load(0, 0).start()
        zero_ref[...] = jnp.zeros_like(zero_ref)

        def tile_fill(e):
            row = pl.multiple_of(last_ref[e] * TOKEN_TILE, tile_rows)
            return pltpu.make_async_copy(zero_ref, o_ref.at[pl.ds(row, tile_rows)], sem)

        for e in range(N_EXPERTS):
            @pl.when(last_ref[e] >= 0)
            def _():
                tile_fill(e).start()
        for e in range(N_EXPERTS):
            @pl.when(last_ref[e] >= 0)
            def _():
                tile_fill(e).wait()

        def spare_fill(j):
            row = pl.multiple_of(j * tile_rows, tile_rows)
            return pltpu.make_async_copy(zero_ref, o_ref.at[pl.ds(row, tile_rows)], sem)

        n_tiles = o_ref.shape[0] // tile_rows
        lax.fori_loop(last_ref[N_EXPERTS], n_tiles, lambda j, c: (spare_fill(j).start(), c)[1], 0)
        lax.fori_loop(last_ref[N_EXPERTS], n_tiles, lambda j, c: (spare_fill(j).wait(), c)[1], 0)

    load(i, cur).wait()

    @pl.when(i > 0)
    def _():
        drain(1 - cur)

    @pl.when(i + 1 < n)
    def _():
        load(i + 1, 1 - cur).start()

    def row_copy(r, k):
        dst = pl.multiple_of(pos_ref[0, 0, k * tm + r] * TOKEN_TILE, TOKEN_TILE)
        return pltpu.make_async_copy(xbuf.at[cur, pl.ds(r * TOKEN_TILE, TOKEN_TILE)],
                                     o_ref.at[pl.ds(dst, TOKEN_TILE)], row_sem.at[cur])

    def issue(blk, carry):
        for u in range(ROW_DMA_UNROLL):
            r = blk * ROW_DMA_UNROLL + u
            row_copy(r, 0).start(priority=u % 2)
            row_copy(r, 1).start(priority=(u + 1) % 2)
        return carry

    lax.fori_loop(0, tm // ROW_DMA_UNROLL, issue, 0)

    @pl.when(i == n - 1)
    def _():
        drain(cur)


def dispatch(xn, pos3, last_tile_row, n_rows, tm=DISPATCH_TM):
    t = xn.shape[0] // TOKEN_TILE
    return pl.pallas_call(
        _dispatch_kernel,
        grid_spec=pltpu.PrefetchScalarGridSpec(
            num_scalar_prefetch=1,
            grid=(t // tm,),
            in_specs=[pl.BlockSpec((1, 1, 2 * tm), lambda i, last: (i, 0, 0), memory_space=pltpu.SMEM),
                      pl.BlockSpec(memory_space=pl.ANY)],
            out_specs=pl.BlockSpec(memory_space=pl.ANY),
            scratch_shapes=[pltpu.VMEM((2, tm * TOKEN_TILE, 128), xn.dtype),
                            pltpu.VMEM((EXPERT_TM * TOKEN_TILE, 128), xn.dtype),
                            pltpu.SemaphoreType.DMA((2,)),
                            pltpu.SemaphoreType.DMA((2,))]),
        out_shape=jax.ShapeDtypeStruct((n_rows * TOKEN_TILE, 128), xn.dtype),
        compiler_params=_params("arbitrary"),
        name="moe_dispatch",
    )(last_tile_row, pos3, xn)


def _expert_kernel(te_ref, nv_ref, par_ref, nxt_ref, x_hbm, wgu_hbm, wd_hbm, y_ref,
                   xbuf, wgu_f, wd_f, wgu_b, wd_b, xsem, wsem, *, layer):
    i = pl.program_id(0)
    n_valid = nv_ref[0]
    tm = xbuf.shape[1] // TOKEN_TILE
    in_rows = tm * TOKEN_TILE
    valid = i < n_valid
    changed = te_ref[i] != te_ref[jnp.maximum(i - 1, 0)]
    first = jnp.logical_or(i == 0, changed)

    def xload(j):
        slot = j % EXPERT_X_BUFS
        src = pl.multiple_of(j * in_rows, in_rows)
        return pltpu.make_async_copy(x_hbm.at[pl.ds(src, in_rows)], xbuf.at[slot], xsem.at[slot])

    def wload(e, slot):
        return (pltpu.make_async_copy(wgu_hbm.at[layer, e], wgu_f.at[slot], wsem.at[slot]),
                pltpu.make_async_copy(wd_hbm.at[layer, e], wd_f.at[slot], wsem.at[slot]))

    @pl.when(i == 0)
    def _():
        for cp in wload(te_ref[0], par_ref[0]):
            cp.start()
        for j in range(EXPERT_X_BUFS - 1):
            @pl.when(j < n_valid)
            def _():
                xload(j).start()

    @pl.when(i + EXPERT_X_BUFS - 1 < n_valid)
    def _():
        xload(i + EXPERT_X_BUFS - 1).start()

    @pl.when(jnp.logical_and(valid, first))
    def _():
        slot = par_ref[i]
        for cp in wload(te_ref[i], slot):
            cp.wait()

        @pl.when(nxt_ref[i] >= 0)
        def _():
            for cp in wload(nxt_ref[i], 1 - slot):
                cp.start()

        wgu_b[...] = wgu_f[slot].astype(BF16)
        wd_b[...] = wd_f[slot].astype(BF16)

    @pl.when(valid)
    def _():
        xload(i).wait()
        x = _from_tiles(xbuf.at[i % EXPERT_X_BUFS], tm).astype(BF16)
        au = _nn(x, wgu_b[...])
        a = au[:, :D_EXPERT]
        u = au[:, D_EXPERT:]
        mid = (jax.nn.silu(a) * u).astype(BF16)
        _to_tiles(y_ref, _nn(mid, wd_b[...]))

    @pl.when(jnp.logical_not(valid))
    def _():
        y_ref[...] = jnp.zeros_like(y_ref)


def experts(xg, w_gu, w_down, layer, tile_expert, n_valid, seg_parity, seg_next, tm=EXPERT_TM):
    d = D_MODEL
    p = xg.shape[0] // TOKEN_TILE
    n_tiles = p // tm
    return pl.pallas_call(
        functools.partial(_expert_kernel, layer=layer),
        grid_spec=pltpu.PrefetchScalarGridSpec(
            num_scalar_prefetch=4,
            grid=(n_tiles,),
            in_specs=[pl.BlockSpec(memory_space=pl.ANY),
                      pl.BlockSpec(memory_space=pl.ANY),
                      pl.BlockSpec(memory_space=pl.ANY)],
            out_specs=pl.BlockSpec((tm * TOKEN_TILE, 128), lambda i, *_: (i, 0)),
            scratch_shapes=[pltpu.VMEM((EXPERT_X_BUFS, tm * TOKEN_TILE, 128), F32),
                            pltpu.VMEM((2, d, 2 * D_EXPERT), F32),
                            pltpu.VMEM((2, D_EXPERT, d), F32),
                            pltpu.VMEM((d, 2 * D_EXPERT), BF16),
                            pltpu.VMEM((D_EXPERT, d), BF16),
                            pltpu.SemaphoreType.DMA((EXPERT_X_BUFS,)),
                            pltpu.SemaphoreType.DMA((2,))]),
        out_shape=jax.ShapeDtypeStruct((p * TOKEN_TILE, 128), F32),
        compiler_params=_params("arbitrary"),
        name="moe_experts",
    )(tile_expert, n_valid, seg_parity, seg_next, xg, w_gu, w_down)


def _qkv_tail(x, gq_ref, gkv_ref, wqt_ref, wk_ref, wvt_ref, qt_ref, k_ref, vt_ref):
    y = x * lax.rsqrt(jnp.mean(x * x, axis=-1, keepdims=True) + NORM_EPS)
    xq = (y * gq_ref[...]).astype(BF16)
    xkv = (y * gkv_ref[...]).astype(BF16)
    qt_ref[0] = (_nt(wqt_ref[...], xq) * (B_HEAD_DIM ** -0.5 * LOG2E)).astype(qt_ref.dtype)
    k_ref[...] = _nn(xkv, wk_ref[...]).astype(k_ref.dtype)
    vt_ref[0] = _nt(wvt_ref[...], xkv).astype(vt_ref.dtype)


def _combine_kernel(pos_ref, pos_next_ref, h_ref, w_ref, y_ref, *rest, tail):
    if tail == "norm":
        g_ref, o_ref, buf, sem = rest
    else:
        gq_ref, gkv_ref, wqt_ref, wk_ref, wvt_ref, o_ref, qt_ref, k_ref, vt_ref, buf, sem = rest
    i = pl.program_id(0)
    tm = h_ref.shape[0]
    cur = i % 2

    def gather_tile(p_ref, half):
        def row_copy(r, k):
            src = pl.multiple_of(p_ref[0, 0, k * tm + r] * TOKEN_TILE, TOKEN_TILE)
            return pltpu.make_async_copy(y_ref.at[pl.ds(src, TOKEN_TILE)],
                                         buf.at[2 * half + k, pl.ds(r * TOKEN_TILE, TOKEN_TILE)],
                                         sem.at[half])

        def issue(blk, carry):
            for u in range(ROW_DMA_UNROLL):
                r = blk * ROW_DMA_UNROLL + u
                row_copy(r, 0).start(priority=u % 2)
                row_copy(r, 1).start(priority=(u + 1) % 2)
            return carry

        lax.fori_loop(0, tm // ROW_DMA_UNROLL, issue, 0)

    @pl.when(i == 0)
    def _():
        gather_tile(pos_ref, 0)

    @pl.when(i + 1 < pl.num_programs(0))
    def _():
        gather_tile(pos_next_ref, 1 - cur)

    for k in range(2):
        pltpu.make_async_copy(y_ref.at[pl.ds(0, tm * TOKEN_TILE)], buf.at[2 * cur + k], sem.at[cur]).wait()
    w = w_ref[...]
    out = (h_ref[...] + w[:, 0:1] * _from_tiles(buf.at[2 * cur], tm)
           + w[:, 1:2] * _from_tiles(buf.at[2 * cur + 1], tm))
    if tail == "norm":
        o_ref[...] = out * lax.rsqrt(jnp.mean(out * out, axis=-1, keepdims=True) + NORM_EPS) * g_ref[...]
    else:
        o_ref[...] = out
        _qkv_tail(out, gq_ref, gkv_ref, wqt_ref, wk_ref, wvt_ref, qt_ref, k_ref, vt_ref)


def combine(h, wgt_t, pos3, y, tail, tail_args, batch, seq, tm=COMBINE_TM):
    t, d = h.shape
    nt = t // tm
    nblk = seq // tm
    row = pl.BlockSpec((tm, d), lambda i: (i, 0))
    vec = pl.BlockSpec((1, d), lambda i: (0, 0))
    full = pl.BlockSpec((d, d), lambda i: (0, 0))
    tr = pl.BlockSpec((1, d, tm), lambda i: (i // nblk, 0, i % nblk))
    if tail == "norm":
        (g,) = tail_args
        extra_in, extra_specs = [g.reshape(1, d)], [vec]
        out_specs, out_shape = row, jax.ShapeDtypeStruct((t, d), F32)
    else:
        g_q, g_kv, wqt, wk, wvt = tail_args
        extra_in = [g_q.reshape(1, d), g_kv.reshape(1, d), wqt, wk, wvt]
        extra_specs = [vec, vec, full, full, full]
        out_specs = [row, tr, row, tr]
        out_shape = [jax.ShapeDtypeStruct((t, d), F32),
                     jax.ShapeDtypeStruct((batch, d, seq), BF16),
                     jax.ShapeDtypeStruct((t, d), BF16),
                     jax.ShapeDtypeStruct((batch, d, seq), BF16)]
    return pl.pallas_call(
        functools.partial(_combine_kernel, tail=tail),
        grid=(nt,),
        in_specs=[pl.BlockSpec((1, 1, 2 * tm), lambda i: (i, 0, 0), memory_space=pltpu.SMEM),
                  pl.BlockSpec((1, 1, 2 * tm), lambda i: (jnp.minimum(i + 1, nt - 1), 0, 0),
                               memory_space=pltpu.SMEM),
                  row,
                  pl.BlockSpec((tm, 2), lambda i: (i, 0)),
                  pl.BlockSpec(memory_space=pl.ANY)] + extra_specs,
        out_specs=out_specs,
        out_shape=out_shape,
        scratch_shapes=[pltpu.VMEM((4, tm * TOKEN_TILE, 128), F32), pltpu.SemaphoreType.DMA((2,))],
        compiler_params=_params("arbitrary"),
        name="moe_combine_" + tail,
    )(pos3, pos3, h, wgt_t, y, *extra_in)


def hier_moe_layer(h, routing, w_gu, w_down, layer, tail, tail_args, batch, seq):
    t, d = h.shape
    xn, eid, wgt, rank, cnt = routing
    counts = cnt[:, 0].astype(I32)
    padded = ((counts + EXPERT_TM - 1) // EXPERT_TM) * EXPERT_TM
    ends = jnp.cumsum(padded)
    offs = ends - padded
    n_rows = 2 * t + N_EXPERTS * EXPERT_TM
    n_tiles = n_rows // EXPERT_TM
    n_valid = (ends[-1] // EXPERT_TM).astype(I32).reshape(1)
    tile_start = jnp.arange(n_tiles, dtype=I32) * EXPERT_TM
    tile_start = jnp.minimum(tile_start, ends[-1] - 1)
    tile_expert = jnp.sum((ends[None, :] <= tile_start[:, None]).astype(I32), axis=1)
    onehot = eid[None] == jnp.arange(N_EXPERTS, dtype=I32)[:, None, None]
    pos = jnp.sum(jnp.where(onehot, offs[:, None, None], 0), axis=0) + rank

    def tiled(tm):
        return pos.reshape(2, t // tm, tm).transpose(1, 0, 2).reshape(t // tm, 1, 2 * tm)

    last_tile_row = jnp.concatenate([jnp.where(padded > 0, ends - EXPERT_TM, -1).astype(I32), n_valid])
    xg = dispatch(xn, tiled(DISPATCH_TM), last_tile_row, n_rows)
    nonempty = padded > 0
    ordinal = jnp.cumsum(nonempty.astype(I32)) - 1
    ids = jnp.arange(N_EXPERTS, dtype=I32)
    later = jnp.logical_and(nonempty[None, :], ids[None, :] > ids[:, None])
    next_expert = jnp.min(jnp.where(later, ids[None, :], N_EXPERTS), axis=1)
    next_expert = jnp.where(next_expert == N_EXPERTS, -1, next_expert).astype(I32)
    y = experts(xg, w_gu, w_down, layer, tile_expert, n_valid,
                (ordinal[tile_expert] % 2).astype(I32), next_expert[tile_expert])
    return combine(h, wgt.T, tiled(COMBINE_TM), y, tail, tail_args, batch, seq)


def _attn_kernel(q_ref, k_ref, vt_ref, lam_ref, g_ref, o_ref, qq_ref, m_ref, acc_ref,
                 a_ref, c_ref, *sp_refs, lambda_init, heads):
    s_refs, p_refs = sp_refs[:heads], sp_refs[heads:]
    qi = pl.program_id(2)
    tq, tk, hd = ATT_TQ, ATT_TK, 2 * B_HEAD_DIM
    feat = lax.broadcasted_iota(I32, (hd, tq), 0)
    for g in range(heads):
        qt = q_ref[0, g * hd:(g + 1) * hd, :]
        zero = jnp.zeros_like(qt)
        qq_ref[g, :, 0:tq] = jnp.where(feat < B_HEAD_DIM, qt, zero)
        qq_ref[g, :, tq:2 * tq] = jnp.where(feat >= B_HEAD_DIM, qt, zero)
    m_ref[...] = jnp.full_like(m_ref, -jnp.inf)
    acc_ref[...] = jnp.zeros_like(acc_ref)
    ones = jnp.ones((ATT_SUM_ROWS, tk), BF16)
    dyn0 = jnp.minimum(qi, 0)

    def step(j, masked, nk=tk):
        off = pl.multiple_of(j * tk, tk)
        if masked:
            krow = lax.broadcasted_iota(I32, (nk, 2 * tq), 0)
            qcol = lax.broadcasted_iota(I32, (nk, 2 * tq), 1)
            visible = off + krow <= qi * tq + jnp.where(qcol >= tq, qcol - tq, qcol)
        for g in range(heads):
            kb = k_ref[pl.ds(off, nk), g * hd:(g + 1) * hd]
            s = _nn(kb, qq_ref[g])
            if masked:
                s = jnp.where(visible, s, -jnp.inf)
            s_refs[g][0, 0:nk, :] = s
            c_ref[g] = jnp.max(s, axis=0, keepdims=True)
        for g in range(heads):
            m_old = m_ref[g]
            m_new = jnp.maximum(m_old, c_ref[g])
            a_ref[g] = jnp.exp2(m_old - m_new)
            m_ref[g] = m_new
            for c in range(0, nk, ATT_CHUNK):
                p_refs[g][0, c:c + ATT_CHUNK, :] = jnp.exp2(
                    (s_refs[g][dyn0, c:c + ATT_CHUNK, :] - m_new).astype(BF16))
        for g in range(heads):
            vtb = vt_ref[0, g * hd:(g + 1) * hd, pl.ds(off, nk)]
            lhs = jnp.concatenate([vtb, ones[:, 0:nk]], axis=0)
            acc_ref[g] = a_ref[g] * acc_ref[g] + _nn(lhs, p_refs[g][dyn0, 0:nk, :])

    n_full = (qi * tq) // tk

    def full_step(j, carry):
        step(j, False)
        return carry

    lax.fori_loop(0, n_full, full_step, 0)

    first_part = (qi * tq) % tk + tq <= tk // 2

    @pl.when(first_part)
    def _():
        step(n_full, True, tk // 2)

    @pl.when(jnp.logical_not(first_part))
    def _():
        step(n_full, True)

    lam = lam_ref[...]
    lam_full = (jnp.exp(jnp.sum(lam[0:1] * lam[1:2], axis=-1, keepdims=True))
                - jnp.exp(jnp.sum(lam[2:3] * lam[3:4], axis=-1, keepdims=True)) + lambda_init)
    for g in range(heads):
        acc = acc_ref[g]
        on = acc[:hd] / acc[hd:hd + 1]
        ot = on[:, :tq] - lam_full * on[:, tq:]
        o = ot.T
        o = o * lax.rsqrt(jnp.mean(o * o, axis=-1, keepdims=True) + SUBLN_EPS) * g_ref[...]
        o_ref[:, g * hd:(g + 1) * hd] = (o * (1.0 - lambda_init)).astype(o_ref.dtype)


def diff_attn(qt, k, vt, lam, subln_g, batch, seq, lambda_init, heads=ATT_HEADS):
    t, d = k.shape
    nq = seq // ATT_TQ
    hd = 2 * B_HEAD_DIM
    w = heads * hd
    return pl.pallas_call(
        functools.partial(_attn_kernel, lambda_init=lambda_init, heads=heads),
        grid=(batch, B_HEADS // heads, nq),
        in_specs=[pl.BlockSpec((1, w, ATT_TQ), lambda b, h, i: (b, h, i)),
                  pl.BlockSpec((seq, w), lambda b, h, i: (b, h)),
                  pl.BlockSpec((1, w, seq), lambda b, h, i: (b, h, 0)),
                  pl.BlockSpec(lam.shape, lambda b, h, i: (0, 0)),
                  pl.BlockSpec((1, hd), lambda b, h, i: (0, 0))],
        out_specs=pl.BlockSpec((ATT_TQ, w), lambda b, h, i: (b * nq + i, h)),
        out_shape=jax.ShapeDtypeStruct((t, d), BF16),
        scratch_shapes=[pltpu.VMEM((heads, hd, 2 * ATT_TQ), BF16),
                        pltpu.VMEM((heads, 1, 2 * ATT_TQ), F32),
                        pltpu.VMEM((heads, hd + ATT_SUM_ROWS, 2 * ATT_TQ), F32),
                        pltpu.VMEM((heads, 1, 2 * ATT_TQ), F32),
                        pltpu.VMEM((heads, 1, 2 * ATT_TQ), F32)]
        + [pltpu.VMEM((1, ATT_TK, 2 * ATT_TQ), F32) for _ in range(heads)]
        + [pltpu.VMEM((1, ATT_TK, 2 * ATT_TQ), BF16) for _ in range(heads)],
        compiler_params=_params("parallel", "parallel", "arbitrary"),
        name="diff_attn",
    )(qt, k, vt, lam, subln_g.reshape(1, hd))


def kernel(x, a_norm_g, a_w_in, a_lb, a_onorm_g, a_w_out, kv_norm_g, w_kv, b_norm_g, b_w_q, b_lam,
           b_subln_g, b_w_out, ffn_norm_g, router_g_w, router_g_b, router_e_w, router_e_b,
           expert_w_gu, expert_w_down, final_norm_g):
    batch, seq, d = x.shape
    assert d == D_MODEL and a_norm_g.shape[0] == 1 and b_norm_g.shape[0] == 1
    assert seq % max(2 * HG_BLOCK, ATT_TK, ROUTER_TM) == 0
    t = batch * seq
    h = x.reshape(t, d)

    h, routing = hgrn2_layer(h, a_norm_g[0], a_w_in[0].astype(BF16), a_lb, a_onorm_g[0],
                             a_w_out[0].astype(BF16), seq,
                             (ffn_norm_g[0], router_g_w[0], router_g_b[0], router_e_w[0], router_e_b[0]))
    qkv_args = (b_norm_g[0], kv_norm_g, b_w_q[0].T.astype(BF16),
                w_kv[:, :d].astype(BF16), w_kv[:, d:].T.astype(BF16))
    h, qt, k, vt = hier_moe_layer(h, routing, expert_w_gu, expert_w_down, 0, "qkv", qkv_args, batch, seq)

    lambda_init = 0.8 - 0.6 * math.exp(-0.3 * 1)
    o = diff_attn(qt, k, vt, b_lam[0], b_subln_g[0], batch, seq, lambda_init)
    h, routing = proj_res_router(o, b_w_out[0].astype(BF16), h,
                                 (ffn_norm_g[1], router_g_w[1], router_g_b[1], router_e_w[1], router_e_b[1]))
    h = hier_moe_layer(h, routing, expert_w_gu, expert_w_down, 1, "norm", (final_norm_g,), batch, seq)
    return h.reshape(batch, seq, d)
```

```python
import functools
import math

import jax
import jax.numpy as jnp
from jax import lax
from jax.experimental import pallas as pl
from jax.experimental.pallas import tpu as pltpu

F32 = jnp.float32
BF16 = jnp.bfloat16
I32 = jnp.int32

D_MODEL = 1024
A_HEADS = 8
A_HEAD_DIM = 128
B_HEADS = 8
B_HEAD_DIM = 64
N_GROUPS = 4
EXPERTS_PER_GROUP = 8
N_EXPERTS = N_GROUPS * EXPERTS_PER_GROUP
D_EXPERT = 512
NORM_EPS = 1e-6
SUBLN_EPS = 1e-5
LOG2E = 1.4426950408889634

GLA_BLOCK = 128
GLA_HALF = GLA_BLOCK // 2
HG_BLOCK = 256
ATT_TQ = 256
ATT_TK = 512
ATT_HEADS = 8
ATT_CHUNK = 128
ATT_SUM_ROWS = 16
ROUTER_TM = 512
ROUTER_EXPERT_ROW0 = 8
ROUTER_ROWS = ROUTER_EXPERT_ROW0 + N_EXPERTS
EXPERT_TM = 512
EXPERT_X_BUFS = 3
DISPATCH_TM = 2048
COMBINE_TM = 512
ROW_DMA_UNROLL = 8
TOKEN_TILE = 8
VMEM_LIMIT = 56 * 1024 * 1024


def _nt(a, b):
    return lax.dot_general(a, b, (((1,), (1,)), ((), ())), preferred_element_type=F32)


def _nn(a, b):
    return jnp.dot(a, b, preferred_element_type=F32)


def _to_tiles(ref, x):
    tm = x.shape[0]
    for s in range(TOKEN_TILE):
        ref[pl.ds(s, tm, stride=TOKEN_TILE), :] = x[:, s * 128:(s + 1) * 128]


def _from_tiles(ref, tm):
    return jnp.concatenate([ref[pl.ds(s, tm, stride=TOKEN_TILE), :] for s in range(TOKEN_TILE)], axis=1)


def _params(*sem):
    return pltpu.CompilerParams(dimension_semantics=sem, vmem_limit_bytes=VMEM_LIMIT)


def _hgrn2_project(x, g_ref, w_ref, proj_ref):
    d = x.shape[1]
    y = x * lax.rsqrt(jnp.mean(x * x, axis=-1, keepdims=True) + NORM_EPS)
    xn = (y * g_ref[...]).astype(BF16)
    for c in range(w_ref.shape[1] // d):
        proj_ref[:, c * d:(c + 1) * d] = _nn(xn, w_ref[:, c * d:(c + 1) * d])


def _gla_block(proj_ref, row0, lb, og, st_ref, out_ref, out_row0):
    d = D_MODEL
    rows = pl.ds(pl.multiple_of(row0, GLA_BLOCK), GLA_BLOCK)
    fz = proj_ref[rows, d:2 * d]
    logf = jnp.log(lb + (1.0 - lb) * jax.nn.sigmoid(fz))
    kk = (1.0 - lb) * jax.nn.sigmoid(-fz)
    qq = jax.nn.silu(proj_ref[rows, 0:d])

    row = lax.broadcasted_iota(I32, logf.shape, 0)
    b = logf
    shift = 1
    while shift < GLA_BLOCK:
        b = b + jnp.where(row >= shift, pltpu.roll(b, shift, axis=0), 0.0)
        shift *= 2

    h0, h1 = slice(0, GLA_HALF), slice(GLA_HALF, GLA_BLOCK)
    b_a_mid = b[GLA_HALF // 2 - 1:GLA_HALF // 2]
    b_a_end = b[GLA_HALF - 1:GLA_HALF]
    b_b_mid = b[GLA_HALF + GLA_HALF // 2 - 1:GLA_HALF + GLA_HALF // 2]
    b_end = b[GLA_BLOCK - 1:GLA_BLOCK]

    qa_n = (qq[h0] * jnp.exp(b[h0] - b_a_mid)).astype(BF16)
    ka_n = (kk[h0] * jnp.exp(b_a_mid - b[h0])).astype(BF16)
    qb_n = (qq[h1] * jnp.exp(b[h1] - b_b_mid)).astype(BF16)
    kb_n = (kk[h1] * jnp.exp(b_b_mid - b[h1])).astype(BF16)
    qb_x = (qq[h1] * jnp.exp(b[h1] - b_a_end)).astype(BF16)
    ka_x = (kk[h0] * jnp.exp(b_a_end - b[h0])).astype(BF16)
    q_dec = (qq * jnp.exp(b)).astype(BF16)
    k_end = (kk * jnp.exp(b_end - b)).astype(BF16)
    d_end = jnp.exp(b_end)

    rr = lax.broadcasted_iota(I32, (GLA_HALF, GLA_HALF), 0)
    ss = lax.broadcasted_iota(I32, (GLA_HALF, GLA_HALF), 1)
    causal = ss <= rr

    for h in range(A_HEADS):
        hs = slice(h * A_HEAD_DIM, (h + 1) * A_HEAD_DIM)
        v = proj_ref[rows, 2 * d + h * A_HEAD_DIM:2 * d + (h + 1) * A_HEAD_DIM]
        vb = v.astype(BF16)
        st = st_ref[h]
        st_b = st.astype(BF16)
        p_aa = jnp.where(causal, _nt(qa_n[:, hs], ka_n[:, hs]), 0.0).astype(BF16)
        p_bb = jnp.where(causal, _nt(qb_n[:, hs], kb_n[:, hs]), 0.0).astype(BF16)
        p_ba = _nt(qb_x[:, hs], ka_x[:, hs]).astype(BF16)
        inter = _nt(q_dec[:, hs], st_b)
        o_a = _nn(p_aa, vb[h0]) + inter[h0]
        o_b = _nn(p_bb, vb[h1]) + _nn(p_ba, vb[h0]) + inter[h1]
        st_ref[h] = d_end[:, hs] * st + _nn(v.T.astype(BF16), k_end[:, hs])
        gate = jax.nn.silu(proj_ref[rows, 3 * d + h * A_HEAD_DIM:3 * d + (h + 1) * A_HEAD_DIM])
        for half, o in ((0, o_a), (1, o_b)):
            on = o * lax.rsqrt(jnp.mean(o * o, axis=-1, keepdims=True) + NORM_EPS)
            r0 = out_row0 + half * GLA_HALF
            out_ref[r0:r0 + GLA_HALF, hs] = (on * og * gate[half * GLA_HALF:(half + 1) * GLA_HALF]
                                             ).astype(out_ref.dtype)


def _hgrn2_kernel(x_ref, xnext_ref, g_ref, win_ref, alb_ref, og_ref, wout_ref,
                  rg_ref, rwh_ref, rwl_ref, rb_ref, o_ref, xn_ref, eid_ref, wgt_ref, rank_ref, cnt_ref,
                  proj_a, proj_b, gated_a, gated_b, st_ref, *, blocks_per_seq):
    step = pl.program_id(0)
    dyn0 = jnp.minimum(step, 0)
    alb = alb_ref[...]
    e = jnp.exp(alb - jnp.max(alb, axis=0, keepdims=True))
    lb = e[0:1] / jnp.sum(e, axis=0, keepdims=True)
    og = og_ref[...]

    @pl.when(step == 0)
    def _():
        _hgrn2_project(x_ref[0:HG_BLOCK, :], g_ref, win_ref, proj_a)

    @pl.when((2 * step) % blocks_per_seq == 0)
    def _():
        st_ref[...] = jnp.zeros_like(st_ref)

    _hgrn2_project(x_ref[HG_BLOCK:2 * HG_BLOCK, :], g_ref, win_ref, proj_b)
    for sub in range(HG_BLOCK // GLA_BLOCK):
        _gla_block(proj_a, dyn0 + sub * GLA_BLOCK, lb, og, st_ref, gated_a, sub * GLA_BLOCK)
    out_a = x_ref[0:HG_BLOCK, :] + _nn(gated_a[...], wout_ref[...])
    o_ref[0:HG_BLOCK, :] = out_a

    _hgrn2_project(xnext_ref[...], g_ref, win_ref, proj_a)
    for sub in range(HG_BLOCK // GLA_BLOCK):
        _gla_block(proj_b, dyn0 + sub * GLA_BLOCK, lb, og, st_ref, gated_b, sub * GLA_BLOCK)
    out_b = x_ref[HG_BLOCK:2 * HG_BLOCK, :] + _nn(gated_b[...], wout_ref[...])
    o_ref[HG_BLOCK:2 * HG_BLOCK, :] = out_b

    _route_tile(jnp.concatenate([out_a, out_b], axis=0), rg_ref, rwh_ref, rwl_ref, rb_ref,
                xn_ref, eid_ref, wgt_ref, rank_ref, cnt_ref)


def hgrn2_layer(x, norm_g, w_in_bf16, a_lb, onorm_g, w_out_bf16, seq, route_params):
    t, d = x.shape
    n = w_in_bf16.shape[1]
    nblocks = t // HG_BLOCK
    assert 2 * HG_BLOCK == ROUTER_TM
    route_args, route_in, route_specs, route_shapes = _route_plumbing(t, d, ROUTER_TM, *route_params)
    out = pl.pallas_call(
        functools.partial(_hgrn2_kernel, blocks_per_seq=seq // HG_BLOCK),
        grid=(nblocks // 2,),
        in_specs=[pl.BlockSpec((2 * HG_BLOCK, d), lambda i: (i, 0)),
                  pl.BlockSpec((HG_BLOCK, d), lambda i: (jnp.minimum(2 * i + 2, nblocks - 1), 0)),
                  pl.BlockSpec((1, d), lambda i: (0, 0)),
                  pl.BlockSpec((d, n), lambda i: (0, 0)),
                  pl.BlockSpec(a_lb.shape, lambda i: (0, 0)),
                  pl.BlockSpec((1, A_HEAD_DIM), lambda i: (0, 0)),
                  pl.BlockSpec((d, d), lambda i: (0, 0))] + route_in,
        out_specs=[pl.BlockSpec((2 * HG_BLOCK, d), lambda i: (i, 0))] + route_specs,
        out_shape=[jax.ShapeDtypeStruct((t, d), F32)] + route_shapes,
        scratch_shapes=[pltpu.VMEM((HG_BLOCK, n), F32),
                        pltpu.VMEM((HG_BLOCK, n), F32),
                        pltpu.VMEM((HG_BLOCK, d), BF16),
                        pltpu.VMEM((HG_BLOCK, d), BF16),
                        pltpu.VMEM((A_HEADS, A_HEAD_DIM, A_HEAD_DIM), F32)],
        compiler_params=_params("arbitrary"),
        name="hgrn2_layer",
    )(x, x, norm_g.reshape(1, d), w_in_bf16, a_lb, onorm_g.reshape(1, A_HEAD_DIM), w_out_bf16,
      *route_args)
    return out[0], out[1:]


def _route_tile(x, g_ref, wh_ref, wl_ref, b_ref, xn_ref, eid_ref, wgt_ref, rank_ref, cnt_ref):
    tm = x.shape[0]

    @pl.when(pl.program_id(0) == 0)
    def _():
        cnt_ref[...] = jnp.zeros_like(cnt_ref)

    xn = x * lax.rsqrt(jnp.mean(x * x, axis=-1, keepdims=True) + NORM_EPS) * g_ref[...]
    _to_tiles(xn_ref, xn)
    xh = xn.astype(BF16)
    xl = (xn - xh.astype(F32)).astype(BF16)
    wh = wh_ref[...]
    lg = _nt(wh, xh) + _nt(wl_ref[...], xh) + _nt(wh, xl) + b_ref[...]

    gl = lg[0:N_GROUPS]
    r4 = lax.broadcasted_iota(I32, gl.shape, 0)
    gmax = jnp.max(gl, axis=0, keepdims=True)
    grp = jnp.min(jnp.where(gl == gmax, r4, N_GROUPS), axis=0, keepdims=True)
    p_grp = 1.0 / jnp.sum(jnp.exp(gl - gmax), axis=0, keepdims=True)

    r0 = ROUTER_EXPERT_ROW0
    fine = lg[r0:r0 + EXPERTS_PER_GROUP]
    for gi in range(1, N_GROUPS):
        fine = jnp.where(grp == gi, lg[r0 + gi * EXPERTS_PER_GROUP:r0 + (gi + 1) * EXPERTS_PER_GROUP], fine)
    r8 = lax.broadcasted_iota(I32, fine.shape, 0)
    m1 = jnp.max(fine, axis=0, keepdims=True)
    i1 = jnp.min(jnp.where(fine == m1, r8, EXPERTS_PER_GROUP), axis=0, keepdims=True)
    rest = jnp.where(r8 == i1, -jnp.inf, fine)
    m2 = jnp.max(rest, axis=0, keepdims=True)
    i2 = jnp.min(jnp.where(rest == m2, r8, EXPERTS_PER_GROUP), axis=0, keepdims=True)
    e21 = jnp.exp(m2 - m1)
    t1 = 1.0 / (1.0 + e21)
    wgt_ref[0:1, :] = p_grp * t1
    wgt_ref[1:2, :] = p_grp * (e21 * t1)
    e1 = grp * EXPERTS_PER_GROUP + i1
    e2 = grp * EXPERTS_PER_GROUP + i2
    eid_ref[0:1, :] = e1
    eid_ref[1:2, :] = e2

    r32 = lax.broadcasted_iota(I32, (N_EXPERTS, tm), 0)
    is1 = r32 == e1
    is2 = r32 == e2
    member = jnp.logical_or(is1, is2)
    ta = lax.broadcasted_iota(I32, (tm, tm), 0)
    tb = lax.broadcasted_iota(I32, (tm, tm), 1)
    before = (ta < tb).astype(BF16)
    prior = _nn(member.astype(BF16), before) + cnt_ref[:, 0:1]
    rank_ref[0:1, :] = jnp.sum(jnp.where(is1, prior, 0.0), axis=0, keepdims=True).astype(I32)
    rank_ref[1:2, :] = jnp.sum(jnp.where(is2, prior, 0.0), axis=0, keepdims=True).astype(I32)
    cnt_ref[...] = cnt_ref[...] + jnp.sum(member.astype(F32), axis=1, keepdims=True)


def _proj_res_router_kernel(a_ref, w_ref, r_ref, g_ref, wh_ref, wl_ref, b_ref, o_ref, *route_out):
    h = r_ref[...] + _nn(a_ref[...], w_ref[...])
    o_ref[...] = h
    _route_tile(h, g_ref, wh_ref, wl_ref, b_ref, *route_out)


def _route_plumbing(t, d, tm, g, wg, bg, we, be):
    r0 = ROUTER_EXPERT_ROW0
    w_all = jnp.zeros((ROUTER_ROWS, d), F32).at[0:N_GROUPS].set(wg.T).at[r0:r0 + N_EXPERTS].set(we.T)
    b_all = jnp.zeros((ROUTER_ROWS, 1), F32).at[0:N_GROUPS, 0].set(bg).at[r0:r0 + N_EXPERTS, 0].set(be)
    wh = w_all.astype(BF16)
    wl = (w_all - wh.astype(F32)).astype(BF16)
    row2 = lambda i: (0, i)
    route_in = [pl.BlockSpec((1, d), lambda i: (0, 0)),
                pl.BlockSpec((ROUTER_ROWS, d), lambda i: (0, 0)),
                pl.BlockSpec((ROUTER_ROWS, d), lambda i: (0, 0)),
                pl.BlockSpec((ROUTER_ROWS, 1), lambda i: (0, 0))]
    route_specs = [pl.BlockSpec((tm * TOKEN_TILE, 128), lambda i: (i, 0)),
                   pl.BlockSpec((2, tm), row2),
                   pl.BlockSpec((2, tm), row2),
                   pl.BlockSpec((2, tm), row2),
                   pl.BlockSpec((N_EXPERTS, 128), lambda i: (0, 0))]
    route_shapes = [jax.ShapeDtypeStruct((t * TOKEN_TILE, 128), F32),
                    jax.ShapeDtypeStruct((2, t), I32),
                    jax.ShapeDtypeStruct((2, t), F32),
                    jax.ShapeDtypeStruct((2, t), I32),
                    jax.ShapeDtypeStruct((N_EXPERTS, 128), F32)]
    return (g.reshape(1, d), wh, wl, b_all), route_in, route_specs, route_shapes


def proj_res_router(a, w, h, route_params, tm=ROUTER_TM):
    t, d = h.shape
    row = pl.BlockSpec((tm, d), lambda i: (i, 0))
    route_args, route_in, route_specs, route_shapes = _route_plumbing(t, d, tm, *route_params)
    out = pl.pallas_call(
        _proj_res_router_kernel,
        grid=(t // tm,),
        in_specs=[row, pl.BlockSpec((d, d), lambda i: (0, 0)), row] + route_in,
        out_specs=[row] + route_specs,
        out_shape=[jax.ShapeDtypeStruct((t, d), F32)] + route_shapes,
        compiler_params=_params("arbitrary"),
        name="proj_res_router",
    )(a, w, h, *route_args)
    return out[0], out[1:]


def _dispatch_kernel(last_ref, pos_ref, x_hbm, o_ref, xbuf, zero_ref, load_sem, row_sem):
    i = pl.program_id(0)
    n = pl.num_programs(0)
    tm = xbuf.shape[1] // TOKEN_TILE
    in_rows = tm * TOKEN_TILE
    tile_rows = EXPERT_TM * TOKEN_TILE
    cur = i % 2
    sem = row_sem.at[0]

    def load(j, slot):
        src = pl.multiple_of(j * in_rows, in_rows)
        return pltpu.make_async_copy(x_hbm.at[pl.ds(src, in_rows)], xbuf.at[slot], load_sem.at[slot])

    def drain(slot):
        for _ in range(2):
            pltpu.make_async_copy(xbuf.at[slot], o_ref.at[pl.ds(0, in_rows)], row_sem.at[slot]).wait()

    @pl.when(i == 0)
    def _():
        load(0, 0).start()
        zero_ref[...] = jnp.zeros_like(zero_ref)

        def tile_fill(e):
            row = pl.multiple_of(last_ref[e] * TOKEN_TILE, tile_rows)
            return pltpu.make_async_copy(zero_ref, o_ref.at[pl.ds(row, tile_rows)], sem)

        for e in range(N_EXPERTS):
            @pl.when(last_ref[e] >= 0)
            def _():
                tile_fill(e).start()
        for e in range(N_EXPERTS):
            @pl.when(last_ref[e] >= 0)
            def _():
                tile_fill(e).wait()

        def spare_fill(j):
            row = pl.multiple_of(j * tile_rows, tile_rows)
            return pltpu.make_async_copy(zero_ref, o_ref.at[pl.ds(row, tile_rows)], sem)

        n_tiles = o_ref.shape[0] // tile_rows
        lax.fori_loop(last_ref[N_EXPERTS], n_tiles, lambda j, c: (spare_fill(j).start(), c)[1], 0)
        lax.fori_loop(last_ref[N_EXPERTS], n_tiles, lambda j, c: (spare_fill(j).wait(), c)[1], 0)

    load(i, cur).wait()

    @pl.when(i > 0)
    def _():
        drain(1 - cur)

    @pl.when(i + 1 < n)
    def _():
        load(i + 1, 1 - cur).start()

    def row_copy(r, k):
        dst = pl.multiple_of(pos_ref[0, 0, k * tm + r] * TOKEN_TILE, TOKEN_TILE)
        return pltpu.make_async_copy(xbuf.at[cur, pl.ds(r * TOKEN_TILE, TOKEN_TILE)],
                                     o_ref.at[pl.ds(dst, TOKEN_TILE)], row_sem.at[cur])

    def issue(blk, carry):
        for u in range(ROW_DMA_UNROLL):
            r = blk * ROW_DMA_UNROLL + u
            row_copy(r, 0).start(priority=u % 2)
            row_copy(r, 1).start(priority=(u + 1) % 2)
        return carry

    lax.fori_loop(0, tm // ROW_DMA_UNROLL, issue, 0)

    @pl.when(i == n - 1)
    def _():
        drain(cur)


def dispatch(xn, pos3, last_tile_row, n_rows, tm=DISPATCH_TM):
    t = xn.shape[0] // TOKEN_TILE
    return pl.pallas_call(
        _dispatch_kernel,
        grid_spec=pltpu.PrefetchScalarGridSpec(
            num_scalar_prefetch=1,
            grid=(t // tm,),
            in_specs=[pl.BlockSpec((1, 1, 2 * tm), lambda i, last: (i, 0, 0), memory_space=pltpu.SMEM),
                      pl.BlockSpec(memory_space=pl.ANY)],
            out_specs=pl.BlockSpec(memory_space=pl.ANY),
            scratch_shapes=[pltpu.VMEM((2, tm * TOKEN_TILE, 128), xn.dtype),
                            pltpu.VMEM((EXPERT_TM * TOKEN_TILE, 128), xn.dtype),
                            pltpu.SemaphoreType.DMA((2,)),
                            pltpu.SemaphoreType.DMA((2,))]),
        out_shape=jax.ShapeDtypeStruct((n_rows * TOKEN_TILE, 128), xn.dtype),
        compiler_params=_params("arbitrary"),
        name="moe_dispatch",
    )(last_tile_row, pos3, xn)


def _expert_kernel(te_ref, nv_ref, par_ref, nxt_ref, x_hbm, wgu_hbm, wd_hbm, y_ref,
                   xbuf, wgu_f, wd_f, wgu_b, wd_b, xsem, wsem, *, layer):
    i = pl.program_id(0)
    n_valid = nv_ref[0]
    tm = xbuf.shape[1] // TOKEN_TILE
    in_rows = tm * TOKEN_TILE
    valid = i < n_valid
    changed = te_ref[i] != te_ref[jnp.maximum(i - 1, 0)]
    first = jnp.logical_or(i == 0, changed)

    def xload(j):
        slot = j % EXPERT_X_BUFS
        src = pl.multiple_of(j * in_rows, in_rows)
        return pltpu.make_async_copy(x_hbm.at[pl.ds(src, in_rows)], xbuf.at[slot], xsem.at[slot])

    def wload(e, slot):
        return (pltpu.make_async_copy(wgu_hbm.at[layer, e], wgu_f.at[slot], wsem.at[slot]),
                pltpu.make_async_copy(wd_hbm.at[layer, e], wd_f.at[slot], wsem.at[slot]))

    @pl.when(i == 0)
    def _():
        for cp in wload(te_ref[0], par_ref[0]):
            cp.start()
        for j in range(EXPERT_X_BUFS - 1):
            @pl.when(j < n_valid)
            def _():
                xload(j).start()

    @pl.when(i + EXPERT_X_BUFS - 1 < n_valid)
    def _():
        xload(i + EXPERT_X_BUFS - 1).start()

    @pl.when(jnp.logical_and(valid, first))
    def _():
        slot = par_ref[i]
        for cp in wload(te_ref[i], slot):
            cp.wait()

        @pl.when(nxt_ref[i] >= 0)
        def _():
            for cp in wload(nxt_ref[i], 1 - slot):
                cp.start()

        wgu_b[...] = wgu_f[slot].astype(BF16)
        wd_b[...] = wd_f[slot].astype(BF16)

    @pl.when(valid)
    def _():
        xload(i).wait()
        x = _from_tiles(xbuf.at[i % EXPERT_X_BUFS], tm).astype(BF16)
        au = _nn(x, wgu_b[...])
        a = au[:, :D_EXPERT]
        u = au[:, D_EXPERT:]
        mid = (jax.nn.silu(a) * u).astype(BF16)
        _to_tiles(y_ref, _nn(mid, wd_b[...]))

    @pl.when(jnp.logical_not(valid))
    def _():
        y_ref[...] = jnp.zeros_like(y_ref)


def experts(xg, w_gu, w_down, layer, tile_expert, n_valid, seg_parity, seg_next, tm=EXPERT_TM):
    d = D_MODEL
    p = xg.shape[0] // TOKEN_TILE
    n_tiles = p // tm
    return pl.pallas_call(
        functools.partial(_expert_kernel, layer=layer),
        grid_spec=pltpu.PrefetchScalarGridSpec(
            num_scalar_prefetch=4,
            grid=(n_tiles,),
            in_specs=[pl.BlockSpec(memory_space=pl.ANY),
                      pl.BlockSpec(memory_space=pl.ANY),
                      pl.BlockSpec(memory_space=pl.ANY)],
            out_specs=pl.BlockSpec((tm * TOKEN_TILE, 128), lambda i, *_: (i, 0)),
            scratch_shapes=[pltpu.VMEM((EXPERT_X_BUFS, tm * TOKEN_TILE, 128), F32),
                            pltpu.VMEM((2, d, 2 * D_EXPERT), F32),
                            pltpu.VMEM((2, D_EXPERT, d), F32),
                            pltpu.VMEM((d, 2 * D_EXPERT), BF16),
                            pltpu.VMEM((D_EXPERT, d), BF16),
                            pltpu.SemaphoreType.DMA((EXPERT_X_BUFS,)),
                            pltpu.SemaphoreType.DMA((2,))]),
        out_shape=jax.ShapeDtypeStruct((p * TOKEN_TILE, 128), F32),
        compiler_params=_params("arbitrary"),
        name="moe_experts",
    )(tile_expert, n_valid, seg_parity, seg_next, xg, w_gu, w_down)


def _qkv_tail(x, gq_ref, gkv_ref, wqt_ref, wk_ref, wvt_ref, qt_ref, k_ref, vt_ref):
    y = x * lax.rsqrt(jnp.mean(x * x, axis=-1, keepdims=True) + NORM_EPS)
    xq = (y * gq_ref[...]).astype(BF16)
    xkv = (y * gkv_ref[...]).astype(BF16)
    qt_ref[0] = (_nt(wqt_ref[...], xq) * (B_HEAD_DIM ** -0.5 * LOG2E)).astype(qt_ref.dtype)
    k_ref[...] = _nn(xkv, wk_ref[...]).astype(k_ref.dtype)
    vt_ref[0] = _nt(wvt_ref[...], xkv).astype(vt_ref.dtype)


def _combine_kernel(pos_ref, pos_next_ref, h_ref, w_ref, y_ref, *rest, tail):
    if tail == "norm":
        g_ref, o_ref, buf, sem = rest
    else:
        gq_ref, gkv_ref, wqt_ref, wk_ref, wvt_ref, o_ref, qt_ref, k_ref, vt_ref, buf, sem = rest
    i = pl.program_id(0)
    tm = h_ref.shape[0]
    cur = i % 2

    def gather_tile(p_ref, half):
        def row_copy(r, k):
            src = pl.multiple_of(p_ref[0, 0, k * tm + r] * TOKEN_TILE, TOKEN_TILE)
            return pltpu.make_async_copy(y_ref.at[pl.ds(src, TOKEN_TILE)],
                                         buf.at[2 * half + k, pl.ds(r * TOKEN_TILE, TOKEN_TILE)],
                                         sem.at[half])

        def issue(blk, carry):
            for u in range(ROW_DMA_UNROLL):
                r = blk * ROW_DMA_UNROLL + u
                row_copy(r, 0).start(priority=u % 2)
                row_copy(r, 1).start(priority=(u + 1) % 2)
            return carry

        lax.fori_loop(0, tm // ROW_DMA_UNROLL, issue, 0)

    @pl.when(i == 0)
    def _():
        gather_tile(pos_ref, 0)

    @pl.when(i + 1 < pl.num_programs(0))
    def _():
        gather_tile(pos_next_ref, 1 - cur)

    for k in range(2):
        pltpu.make_async_copy(y_ref.at[pl.ds(0, tm * TOKEN_TILE)], buf.at[2 * cur + k], sem.at[cur]).wait()
    w = w_ref[...]
    out = (h_ref[...] + w[:, 0:1] * _from_tiles(buf.at[2 * cur], tm)
           + w[:, 1:2] * _from_tiles(buf.at[2 * cur + 1], tm))
    if tail == "norm":
        o_ref[...] = out * lax.rsqrt(jnp.mean(out * out, axis=-1, keepdims=True) + NORM_EPS) * g_ref[...]
    else:
        o_ref[...] = out
        _qkv_tail(out, gq_ref, gkv_ref, wqt_ref, wk_ref, wvt_ref, qt_ref, k_ref, vt_ref)


def combine(h, wgt_t, pos3, y, tail, tail_args, batch, seq, tm=COMBINE_TM):
    t, d = h.shape
    nt = t // tm
    nblk = seq // tm
    row = pl.BlockSpec((tm, d), lambda i: (i, 0))
    vec = pl.BlockSpec((1, d), lambda i: (0, 0))
    full = pl.BlockSpec((d, d), lambda i: (0, 0))
    tr = pl.BlockSpec((1, d, tm), lambda i: (i // nblk, 0, i % nblk))
    if tail == "norm":
        (g,) = tail_args
        extra_in, extra_specs = [g.reshape(1, d)], [vec]
        out_specs, out_shape = row, jax.ShapeDtypeStruct((t, d), F32)
    else:
        g_q, g_kv, wqt, wk, wvt = tail_args
        extra_in = [g_q.reshape(1, d), g_kv.reshape(1, d), wqt, wk, wvt]
        extra_specs = [vec, vec, full, full, full]
        out_specs = [row, tr, row, tr]
        out_shape = [jax.ShapeDtypeStruct((t, d), F32),
                     jax.ShapeDtypeStruct((batch, d, seq), BF16),
                     jax.ShapeDtypeStruct((t, d), BF16),
                     jax.ShapeDtypeStruct((batch, d, seq), BF16)]
    return pl.pallas_call(
        functools.partial(_combine_kernel, tail=tail),
        grid=(nt,),
        in_specs=[pl.BlockSpec((1, 1, 2 * tm), lambda i: (i, 0, 0), memory_space=pltpu.SMEM),
                  pl.BlockSpec((1, 1, 2 * tm), lambda i: (jnp.minimum(i + 1, nt - 1), 0, 0),
                               memory_space=pltpu.SMEM),
                  row,
                  pl.BlockSpec((tm, 2), lambda i: (i, 0)),
                  pl.BlockSpec(memory_space=pl.ANY)] + extra_specs,
        out_specs=out_specs,
        out_shape=out_shape,
        scratch_shapes=[pltpu.VMEM((4, tm * TOKEN_TILE, 128), F32), pltpu.SemaphoreType.DMA((2,))],
        compiler_params=_params("arbitrary"),
        name="moe_combine_" + tail,
    )(pos3, pos3, h, wgt_t, y, *extra_in)


def hier_moe_layer(h, routing, w_gu, w_down, layer, tail, tail_args, batch, seq):
    t, d = h.shape
    xn, eid, wgt, rank, cnt = routing
    counts = cnt[:, 0].astype(I32)
    padded = ((counts + EXPERT_TM - 1) // EXPERT_TM) * EXPERT_TM
    ends = jnp.cumsum(padded)
    offs = ends - padded
    n_rows = 2 * t + N_EXPERTS * EXPERT_TM
    n_tiles = n_rows // EXPERT_TM
    n_valid = (ends[-1] // EXPERT_TM).astype(I32).reshape(1)
    tile_start = jnp.arange(n_tiles, dtype=I32) * EXPERT_TM
    tile_start = jnp.minimum(tile_start, ends[-1] - 1)
    tile_expert = jnp.sum((ends[None, :] <= tile_start[:, None]).astype(I32), axis=1)
    onehot = eid[None] == jnp.arange(N_EXPERTS, dtype=I32)[:, None, None]
    pos = jnp.sum(jnp.where(onehot, offs[:, None, None], 0), axis=0) + rank

    def tiled(tm):
        return pos.reshape(2, t // tm, tm).transpose(1, 0, 2).reshape(t // tm, 1, 2 * tm)

    last_tile_row = jnp.concatenate([jnp.where(padded > 0, ends - EXPERT_TM, -1).astype(I32), n_valid])
    xg = dispatch(xn, tiled(DISPATCH_TM), last_tile_row, n_rows)
    nonempty = padded > 0
    ordinal = jnp.cumsum(nonempty.astype(I32)) - 1
    ids = jnp.arange(N_EXPERTS, dtype=I32)
    later = jnp.logical_and(nonempty[None, :], ids[None, :] > ids[:, None])
    next_expert = jnp.min(jnp.where(later, ids[None, :], N_EXPERTS), axis=1)
    next_expert = jnp.where(next_expert == N_EXPERTS, -1, next_expert).astype(I32)
    y = experts(xg, w_gu, w_down, layer, tile_expert, n_valid,
                (ordinal[tile_expert] % 2).astype(I32), next_expert[tile_expert])
    return combine(h, wgt.T, tiled(COMBINE_TM), y, tail, tail_args, batch, seq)


def _attn_kernel(q_ref, k_ref, vt_ref, lam_ref, g_ref, o_ref, qq_ref, m_ref, acc_ref,
                 a_ref, c_ref, *sp_refs, lambda_init, heads):
    s_refs, p_refs = sp_refs[:heads], sp_refs[heads:]
    qi = pl.program_id(2)
    tq, tk, hd = ATT_TQ, ATT_TK, 2 * B_HEAD_DIM
    feat = lax.broadcasted_iota(I32, (hd, tq), 0)
    for g in range(heads):
        qt = q_ref[0, g * hd:(g + 1) * hd, :]
        zero = jnp.zeros_like(qt)
        qq_ref[g, :, 0:tq] = jnp.where(feat < B_HEAD_DIM, qt, zero)
        qq_ref[g, :, tq:2 * tq] = jnp.where(feat >= B_HEAD_DIM, qt, zero)
    m_ref[...] = jnp.full_like(m_ref, -jnp.inf)
    acc_ref[...] = jnp.zeros_like(acc_ref)
    ones = jnp.ones((ATT_SUM_ROWS, tk), BF16)
    dyn0 = jnp.minimum(qi, 0)

    def scores(j, masked, nk, ps):
        off = pl.multiple_of(j * tk, tk)
        if masked:
            krow = lax.broadcasted_iota(I32, (nk, 2 * tq), 0)
            qcol = lax.broadcasted_iota(I32, (nk, 2 * tq), 1)
            visible = off + krow <= qi * tq + jnp.where(qcol >= tq, qcol - tq, qcol)
        for g in range(heads):
            kb = k_ref[pl.ds(off, nk), g * hd:(g + 1) * hd]
            s = _nn(kb, qq_ref[g])
            if masked:
                s = jnp.where(visible, s, -jnp.inf)
            s_refs[g][0, 0:nk, :] = s
            c_ref[g] = jnp.max(s, axis=0, keepdims=True)
        for g in range(heads):
            m_old = m_ref[g]
            m_new = jnp.maximum(m_old, c_ref[g])
            a_ref[ps, g] = jnp.exp2(m_old - m_new)
            m_ref[g] = m_new
            for c in range(0, nk, ATT_CHUNK):
                p_refs[ps * heads + g][0, c:c + ATT_CHUNK, :] = jnp.exp2(
                    s_refs[g][dyn0, c:c + ATT_CHUNK, :] - m_new).astype(BF16)

    def accumulate(j, nk, ps):
        off = pl.multiple_of(j * tk, tk)
        for g in range(heads):
            vtb = vt_ref[0, g * hd:(g + 1) * hd, pl.ds(off, nk)]
            lhs = jnp.concatenate([vtb, ones[:, 0:nk]], axis=0)
            acc_ref[g] = a_ref[ps, g] * acc_ref[g] + _nn(lhs, p_refs[ps * heads + g][dyn0, 0:nk, :])

    def step(j, masked, nk=tk):
        scores(j, masked, nk, 0)
        accumulate(j, nk, 0)

    n_full = (qi * tq) // tk

    def two_steps(jj, carry):
        scores(2 * jj, False, tk, 0)
        scores(2 * jj + 1, False, tk, 1)
        accumulate(2 * jj, tk, 0)
        accumulate(2 * jj + 1, tk, 1)
        return carry

    lax.fori_loop(0, n_full // 2, two_steps, 0)

    @pl.when(n_full % 2 == 1)
    def _():
        step(n_full - 1, False)

    first_part = (qi * tq) % tk + tq <= tk // 2

    @pl.when(first_part)
    def _():
        step(n_full, True, tk // 2)

    @pl.when(jnp.logical_not(first_part))
    def _():
        step(n_full, True)

    lam = lam_ref[...]
    lam_full = (jnp.exp(jnp.sum(lam[0:1] * lam[1:2], axis=-1, keepdims=True))
                - jnp.exp(jnp.sum(lam[2:3] * lam[3:4], axis=-1, keepdims=True)) + lambda_init)
    for g in range(heads):
        acc = acc_ref[g]
        on = acc[:hd] / acc[hd:hd + 1]
        ot = on[:, :tq] - lam_full * on[:, tq:]
        o = ot.T
        o = o * lax.rsqrt(jnp.mean(o * o, axis=-1, keepdims=True) + SUBLN_EPS) * g_ref[...]
        o_ref[:, g * hd:(g + 1) * hd] = (o * (1.0 - lambda_init)).astype(o_ref.dtype)


def diff_attn(qt, k, vt, lam, subln_g, batch, seq, lambda_init, heads=ATT_HEADS):
    t, d = k.shape
    nq = seq // ATT_TQ
    hd = 2 * B_HEAD_DIM
    w = heads * hd
    return pl.pallas_call(
        functools.partial(_attn_kernel, lambda_init=lambda_init, heads=heads),
        grid=(batch, B_HEADS // heads, nq),
        in_specs=[pl.BlockSpec((1, w, ATT_TQ), lambda b, h, i: (b, h, i)),
                  pl.BlockSpec((seq, w), lambda b, h, i: (b, h)),
                  pl.BlockSpec((1, w, seq), lambda b, h, i: (b, h, 0)),
                  pl.BlockSpec(lam.shape, lambda b, h, i: (0, 0)),
                  pl.BlockSpec((1, hd), lambda b, h, i: (0, 0))],
        out_specs=pl.BlockSpec((ATT_TQ, w), lambda b, h, i: (b * nq + i, h)),
        out_shape=jax.ShapeDtypeStruct((t, d), BF16),
        scratch_shapes=[pltpu.VMEM((heads, hd, 2 * ATT_TQ), BF16),
                        pltpu.VMEM((heads, 1, 2 * ATT_TQ), F32),
                        pltpu.VMEM((heads, hd + ATT_SUM_ROWS, 2 * ATT_TQ), F32),
                        pltpu.VMEM((2, heads, 1, 2 * ATT_TQ), F32),
                        pltpu.VMEM((heads, 1, 2 * ATT_TQ), F32)]
        + [pltpu.VMEM((1, ATT_TK, 2 * ATT_TQ), F32) for _ in range(heads)]
        + [pltpu.VMEM((1, ATT_TK, 2 * ATT_TQ), BF16) for _ in range(2 * heads)],
        compiler_params=_params("parallel", "parallel", "arbitrary"),
        name="diff_attn",
    )(qt, k, vt, lam, subln_g.reshape(1, hd))


def kernel(x, a_norm_g, a_w_in, a_lb, a_onorm_g, a_w_out, kv_norm_g, w_kv, b_norm_g, b_w_q, b_lam,
           b_subln_g, b_w_out, ffn_norm_g, router_g_w, router_g_b, router_e_w, router_e_b,
           expert_w_gu, expert_w_down, final_norm_g):
    batch, seq, d = x.shape
    assert d == D_MODEL and a_norm_g.shape[0] == 1 and b_norm_g.shape[0] == 1
    assert seq % max(2 * HG_BLOCK, ATT_TK, ROUTER_TM) == 0
    t = batch * seq
    h = x.reshape(t, d)

    h, routing = hgrn2_layer(h, a_norm_g[0], a_w_in[0].astype(BF16), a_lb, a_onorm_g[0],
                             a_w_out[0].astype(BF16), seq,
                             (ffn_norm_g[0], router_g_w[0], router_g_b[0], router_e_w[0], router_e_b[0]))
    qkv_args = (b_norm_g[0], kv_norm_g, b_w_q[0].T.astype(BF16),
                w_kv[:, :d].astype(BF16), w_kv[:, d:].T.astype(BF16))
    h, qt, k, vt = hier_moe_layer(h, routing, expert_w_gu, expert_w_down, 0, "qkv", qkv_args, batch, seq)

    lambda_init = 0.8 - 0.6 * math.exp(-0.3 * 1)
    o = diff_attn(qt, k, vt, b_lam[0], b_subln_g[0], batch, seq, lambda_init)
    h, routing = proj_res_router(o, b_w_out[0].astype(BF16), h,
                                 (ffn_norm_g[1], router_g_w[1], router_g_b[1], router_e_w[1], router_e_b[1]))
    h = hier_moe_layer(h, routing, expert_w_gu, expert_w_down, 1, "norm", (final_norm_g,), batch, seq)
    return h.reshape(batch, seq, d)
```

```python
import functools
import math

import jax
import jax.numpy as jnp
from jax import lax
from jax.experimental import pallas as pl
from jax.experimental.pallas import tpu as pltpu

F32 = jnp.float32
BF16 = jnp.bfloat16
I32 = jnp.int32

D_MODEL = 1024
A_HEADS = 8
A_HEAD_DIM = 128
B_HEADS = 8
B_HEAD_DIM = 64
N_GROUPS = 4
EXPERTS_PER_GROUP = 8
N_EXPERTS = N_GROUPS * EXPERTS_PER_GROUP
D_EXPERT = 512
NORM_EPS = 1e-6
SUBLN_EPS = 1e-5
LOG2E = 1.4426950408889634

GLA_BLOCK = 128
GLA_HALF = GLA_BLOCK // 2
HG_BLOCK = 256
ATT_TQ = 256
ATT_TK = 512
ATT_HEADS = 8
ATT_CHUNK = 128
ATT_SUM_ROWS = 16
ROUTER_TM = 512
ROUTER_IN_BUFS = 3
ROUTER_EXPERT_ROW0 = 8
ROUTER_ROWS = ROUTER_EXPERT_ROW0 + N_EXPERTS
EXPERT_TM = 512
EXPERT_X_BUFS = 3
DISPATCH_TM = 2048
COMBINE_TM = 512
ROW_DMA_UNROLL = 8
TOKEN_TILE = 8
VMEM_LIMIT = 56 * 1024 * 1024


def _nt(a, b):
    return lax.dot_general(a, b, (((1,), (1,)), ((), ())), preferred_element_type=F32)


def _nn(a, b):
    return jnp.dot(a, b, preferred_element_type=F32)


def _to_tiles(ref, x):
    tm = x.shape[0]
    for s in range(TOKEN_TILE):
        ref[pl.ds(s, tm, stride=TOKEN_TILE), :] = x[:, s * 128:(s + 1) * 128]


def _from_tiles(ref, tm):
    return jnp.concatenate([ref[pl.ds(s, tm, stride=TOKEN_TILE), :] for s in range(TOKEN_TILE)], axis=1)


def _params(*sem):
    return pltpu.CompilerParams(dimension_semantics=sem, vmem_limit_bytes=VMEM_LIMIT)


def _hgrn2_project(x, g_ref, w_ref, proj_ref):
    d = x.shape[1]
    y = x * lax.rsqrt(jnp.mean(x * x, axis=-1, keepdims=True) + NORM_EPS)
    xn = (y * g_ref[...]).astype(BF16)
    for c in range(w_ref.shape[1] // d):
        proj_ref[:, c * d:(c + 1) * d] = _nn(xn, w_ref[:, c * d:(c + 1) * d])


def _gla_block(proj_ref, row0, lb, og, st_ref, out_ref, out_row0):
    d = D_MODEL
    rows = pl.ds(pl.multiple_of(row0, GLA_BLOCK), GLA_BLOCK)
    fz = proj_ref[rows, d:2 * d]
    logf = jnp.log(lb + (1.0 - lb) * jax.nn.sigmoid(fz))
    kk = (1.0 - lb) * jax.nn.sigmoid(-fz)
    qq = jax.nn.silu(proj_ref[rows, 0:d])

    row = lax.broadcasted_iota(I32, logf.shape, 0)
    b = logf
    shift = 1
    while shift < GLA_BLOCK:
        b = b + jnp.where(row >= shift, pltpu.roll(b, shift, axis=0), 0.0)
        shift *= 2

    h0, h1 = slice(0, GLA_HALF), slice(GLA_HALF, GLA_BLOCK)
    b_a_mid = b[GLA_HALF // 2 - 1:GLA_HALF // 2]
    b_a_end = b[GLA_HALF - 1:GLA_HALF]
    b_b_mid = b[GLA_HALF + GLA_HALF // 2 - 1:GLA_HALF + GLA_HALF // 2]
    b_end = b[GLA_BLOCK - 1:GLA_BLOCK]

    qa_n = (qq[h0] * jnp.exp(b[h0] - b_a_mid)).astype(BF16)
    ka_n = (kk[h0] * jnp.exp(b_a_mid - b[h0])).astype(BF16)
    qb_n = (qq[h1] * jnp.exp(b[h1] - b_b_mid)).astype(BF16)
    kb_n = (kk[h1] * jnp.exp(b_b_mid - b[h1])).astype(BF16)
    qb_x = (qq[h1] * jnp.exp(b[h1] - b_a_end)).astype(BF16)
    ka_x = (kk[h0] * jnp.exp(b_a_end - b[h0])).astype(BF16)
    q_dec = (qq * jnp.exp(b)).astype(BF16)
    k_end = (kk * jnp.exp(b_end - b)).astype(BF16)
    d_end = jnp.exp(b_end)

    rr = lax.broadcasted_iota(I32, (GLA_HALF, GLA_HALF), 0)
    ss = lax.broadcasted_iota(I32, (GLA_HALF, GLA_HALF), 1)
    causal = ss <= rr

    for h in range(A_HEADS):
        hs = slice(h * A_HEAD_DIM, (h + 1) * A_HEAD_DIM)
        v = proj_ref[rows, 2 * d + h * A_HEAD_DIM:2 * d + (h + 1) * A_HEAD_DIM]
        vb = v.astype(BF16)
        st = st_ref[h]
        st_b = st.astype(BF16)
        p_aa = jnp.where(causal, _nt(qa_n[:, hs], ka_n[:, hs]), 0.0).astype(BF16)
        p_bb = jnp.where(causal, _nt(qb_n[:, hs], kb_n[:, hs]), 0.0).astype(BF16)
        p_ba = _nt(qb_x[:, hs], ka_x[:, hs]).astype(BF16)
        inter = _nt(q_dec[:, hs], st_b)
        o_a = _nn(p_aa, vb[h0]) + inter[h0]
        o_b = _nn(p_bb, vb[h1]) + _nn(p_ba, vb[h0]) + inter[h1]
        st_ref[h] = d_end[:, hs] * st + _nn(v.T.astype(BF16), k_end[:, hs])
        gate = jax.nn.silu(proj_ref[rows, 3 * d + h * A_HEAD_DIM:3 * d + (h + 1) * A_HEAD_DIM])
        for half, o in ((0, o_a), (1, o_b)):
            on = o * lax.rsqrt(jnp.mean(o * o, axis=-1, keepdims=True) + NORM_EPS)
            r0 = out_row0 + half * GLA_HALF
            out_ref[r0:r0 + GLA_HALF, hs] = (on * og * gate[half * GLA_HALF:(half + 1) * GLA_HALF]
                                             ).astype(out_ref.dtype)


def _hgrn2_kernel(x_ref, xnext_ref, g_ref, win_ref, alb_ref, og_ref, wout_ref,
                  rg_ref, rwh_ref, rwl_ref, rb_ref, o_ref, xn_ref, eid_ref, wgt_ref, rank_ref, cnt_ref,
                  proj_a, proj_b, gated_a, gated_b, st_ref, *, blocks_per_seq):
    step = pl.program_id(0)
    dyn0 = jnp.minimum(step, 0)
    alb = alb_ref[...]
    e = jnp.exp(alb - jnp.max(alb, axis=0, keepdims=True))
    lb = e[0:1] / jnp.sum(e, axis=0, keepdims=True)
    og = og_ref[...]

    @pl.when(step == 0)
    def _():
        _hgrn2_project(x_ref[0:HG_BLOCK, :], g_ref, win_ref, proj_a)

    @pl.when((2 * step) % blocks_per_seq == 0)
    def _():
        st_ref[...] = jnp.zeros_like(st_ref)

    _hgrn2_project(x_ref[HG_BLOCK:2 * HG_BLOCK, :], g_ref, win_ref, proj_b)
    for sub in range(HG_BLOCK // GLA_BLOCK):
        _gla_block(proj_a, dyn0 + sub * GLA_BLOCK, lb, og, st_ref, gated_a, sub * GLA_BLOCK)
    out_a = x_ref[0:HG_BLOCK, :] + _nn(gated_a[...], wout_ref[...])
    o_ref[0:HG_BLOCK, :] = out_a

    _hgrn2_project(xnext_ref[...], g_ref, win_ref, proj_a)
    for sub in range(HG_BLOCK // GLA_BLOCK):
        _gla_block(proj_b, dyn0 + sub * GLA_BLOCK, lb, og, st_ref, gated_b, sub * GLA_BLOCK)
    out_b = x_ref[HG_BLOCK:2 * HG_BLOCK, :] + _nn(gated_b[...], wout_ref[...])
    o_ref[HG_BLOCK:2 * HG_BLOCK, :] = out_b

    _route_tile(jnp.concatenate([out_a, out_b], axis=0), rg_ref, rwh_ref, rwl_ref, rb_ref,
                xn_ref, eid_ref, wgt_ref, rank_ref, cnt_ref)


def hgrn2_layer(x, norm_g, w_in_bf16, a_lb, onorm_g, w_out_bf16, seq, route_params):
    t, d = x.shape
    n = w_in_bf16.shape[1]
    nblocks = t // HG_BLOCK
    assert 2 * HG_BLOCK == ROUTER_TM
    route_args, route_in, route_specs, route_shapes = _route_plumbing(t, d, ROUTER_TM, *route_params)
    out = pl.pallas_call(
        functools.partial(_hgrn2_kernel, blocks_per_seq=seq // HG_BLOCK),
        grid=(nblocks // 2,),
        in_specs=[pl.BlockSpec((2 * HG_BLOCK, d), lambda i: (i, 0)),
                  pl.BlockSpec((HG_BLOCK, d), lambda i: (jnp.minimum(2 * i + 2, nblocks - 1), 0)),
                  pl.BlockSpec((1, d), lambda i: (0, 0)),
                  pl.BlockSpec((d, n), lambda i: (0, 0)),
                  pl.BlockSpec(a_lb.shape, lambda i: (0, 0)),
                  pl.BlockSpec((1, A_HEAD_DIM), lambda i: (0, 0)),
                  pl.BlockSpec((d, d), lambda i: (0, 0))] + route_in,
        out_specs=[pl.BlockSpec((2 * HG_BLOCK, d), lambda i: (i, 0))] + route_specs,
        out_shape=[jax.ShapeDtypeStruct((t, d), F32)] + route_shapes,
        scratch_shapes=[pltpu.VMEM((HG_BLOCK, n), F32),
                        pltpu.VMEM((HG_BLOCK, n), F32),
                        pltpu.VMEM((HG_BLOCK, d), BF16),
                        pltpu.VMEM((HG_BLOCK, d), BF16),
                        pltpu.VMEM((A_HEADS, A_HEAD_DIM, A_HEAD_DIM), F32)],
        compiler_params=_params("arbitrary"),
        name="hgrn2_layer",
    )(x, x, norm_g.reshape(1, d), w_in_bf16, a_lb, onorm_g.reshape(1, A_HEAD_DIM), w_out_bf16,
      *route_args)
    return out[0], out[1:]


def _route_tile(x, g_ref, wh_ref, wl_ref, b_ref, xn_ref, eid_ref, wgt_ref, rank_ref, cnt_ref):
    tm = x.shape[0]

    @pl.when(pl.program_id(0) == 0)
    def _():
        cnt_ref[...] = jnp.zeros_like(cnt_ref)

    xn = x * lax.rsqrt(jnp.mean(x * x, axis=-1, keepdims=True) + NORM_EPS) * g_ref[...]
    _to_tiles(xn_ref, xn)
    xh = xn.astype(BF16)
    xl = (xn - xh.astype(F32)).astype(BF16)
    wh = wh_ref[...]
    lg = _nt(wh, xh) + _nt(wl_ref[...], xh) + _nt(wh, xl) + b_ref[...]

    gl = lg[0:N_GROUPS]
    r4 = lax.broadcasted_iota(I32, gl.shape, 0)
    gmax = jnp.max(gl, axis=0, keepdims=True)
    grp = jnp.min(jnp.where(gl == gmax, r4, N_GROUPS), axis=0, keepdims=True)
    p_grp = 1.0 / jnp.sum(jnp.exp(gl - gmax), axis=0, keepdims=True)

    r0 = ROUTER_EXPERT_ROW0
    fine = lg[r0:r0 + EXPERTS_PER_GROUP]
    for gi in range(1, N_GROUPS):
        fine = jnp.where(grp == gi, lg[r0 + gi * EXPERTS_PER_GROUP:r0 + (gi + 1) * EXPERTS_PER_GROUP], fine)
    r8 = lax.broadcasted_iota(I32, fine.shape, 0)
    m1 = jnp.max(fine, axis=0, keepdims=True)
    i1 = jnp.min(jnp.where(fine == m1, r8, EXPERTS_PER_GROUP), axis=0, keepdims=True)
    rest = jnp.where(r8 == i1, -jnp.inf, fine)
    m2 = jnp.max(rest, axis=0, keepdims=True)
    i2 = jnp.min(jnp.where(rest == m2, r8, EXPERTS_PER_GROUP), axis=0, keepdims=True)
    e21 = jnp.exp(m2 - m1)
    t1 = 1.0 / (1.0 + e21)
    wgt_ref[0:1, :] = p_grp * t1
    wgt_ref[1:2, :] = p_grp * (e21 * t1)
    e1 = grp * EXPERTS_PER_GROUP + i1
    e2 = grp * EXPERTS_PER_GROUP + i2
    eid_ref[0:1, :] = e1
    eid_ref[1:2, :] = e2

    r32 = lax.broadcasted_iota(I32, (N_EXPERTS, tm), 0)
    is1 = r32 == e1
    is2 = r32 == e2
    member = jnp.logical_or(is1, is2)
    ta = lax.broadcasted_iota(I32, (tm, tm), 0)
    tb = lax.broadcasted_iota(I32, (tm, tm), 1)
    before = (ta < tb).astype(BF16)
    prior = _nn(member.astype(BF16), before) + cnt_ref[:, 0:1]
    rank_ref[0:1, :] = jnp.sum(jnp.where(is1, prior, 0.0), axis=0, keepdims=True).astype(I32)
    rank_ref[1:2, :] = jnp.sum(jnp.where(is2, prior, 0.0), axis=0, keepdims=True).astype(I32)
    cnt_ref[...] = cnt_ref[...] + jnp.sum(member.astype(F32), axis=1, keepdims=True)


def _proj_res_router_kernel(a_hbm, w_ref, r_hbm, g_ref, wh_ref, wl_ref, b_ref, o_ref, *rest):
    *route_out, abuf, rbuf, sem = rest
    i = pl.program_id(0)
    tm = o_ref.shape[0]

    def loads(j):
        slot = j % ROUTER_IN_BUFS
        rows = pl.ds(pl.multiple_of(j * tm, tm), tm)
        return (pltpu.make_async_copy(a_hbm.at[rows], abuf.at[slot], sem.at[0, slot]),
                pltpu.make_async_copy(r_hbm.at[rows], rbuf.at[slot], sem.at[1, slot]))

    @pl.when(i == 0)
    def _():
        for j in range(ROUTER_IN_BUFS - 1):
            for cp in loads(j):
                cp.start()

    @pl.when(i + ROUTER_IN_BUFS - 1 < pl.num_programs(0))
    def _():
        for cp in loads(i + ROUTER_IN_BUFS - 1):
            cp.start()

    for cp in loads(i):
        cp.wait()
    slot = i % ROUTER_IN_BUFS
    h = rbuf[slot] + _nn(abuf[slot], w_ref[...])
    o_ref[...] = h
    _route_tile(h, g_ref, wh_ref, wl_ref, b_ref, *route_out)


def _route_plumbing(t, d, tm, g, wg, bg, we, be):
    r0 = ROUTER_EXPERT_ROW0
    w_all = jnp.zeros((ROUTER_ROWS, d), F32).at[0:N_GROUPS].set(wg.T).at[r0:r0 + N_EXPERTS].set(we.T)
    b_all = jnp.zeros((ROUTER_ROWS, 1), F32).at[0:N_GROUPS, 0].set(bg).at[r0:r0 + N_EXPERTS, 0].set(be)
    wh = w_all.astype(BF16)
    wl = (w_all - wh.astype(F32)).astype(BF16)
    row2 = lambda i: (0, i)
    route_in = [pl.BlockSpec((1, d), lambda i: (0, 0)),
                pl.BlockSpec((ROUTER_ROWS, d), lambda i: (0, 0)),
                pl.BlockSpec((ROUTER_ROWS, d), lambda i: (0, 0)),
                pl.BlockSpec((ROUTER_ROWS, 1), lambda i: (0, 0))]
    route_specs = [pl.BlockSpec((tm * TOKEN_TILE, 128), lambda i: (i, 0)),
                   pl.BlockSpec((2, tm), row2),
                   pl.BlockSpec((2, tm), row2),
                   pl.BlockSpec((2, tm), row2),
                   pl.BlockSpec((N_EXPERTS, 128), lambda i: (0, 0))]
    route_shapes = [jax.ShapeDtypeStruct((t * TOKEN_TILE, 128), F32),
                    jax.ShapeDtypeStruct((2, t), I32),
                    jax.ShapeDtypeStruct((2, t), F32),
                    jax.ShapeDtypeStruct((2, t), I32),
                    jax.ShapeDtypeStruct((N_EXPERTS, 128), F32)]
    return (g.reshape(1, d), wh, wl, b_all), route_in, route_specs, route_shapes


def proj_res_router(a, w, h, route_params, tm=ROUTER_TM):
    t, d = h.shape
    assert t // tm >= ROUTER_IN_BUFS - 1
    row = pl.BlockSpec((tm, d), lambda i: (i, 0))
    hbm = pl.BlockSpec(memory_space=pl.ANY)
    route_args, route_in, route_specs, route_shapes = _route_plumbing(t, d, tm, *route_params)
    out = pl.pallas_call(
        _proj_res_router_kernel,
        grid=(t // tm,),
        in_specs=[hbm, pl.BlockSpec((d, d), lambda i: (0, 0)), hbm] + route_in,
        out_specs=[row] + route_specs,
        scratch_shapes=[pltpu.VMEM((ROUTER_IN_BUFS, tm, d), a.dtype),
                        pltpu.VMEM((ROUTER_IN_BUFS, tm, d), F32),
                        pltpu.SemaphoreType.DMA((2, ROUTER_IN_BUFS))],
        out_shape=[jax.ShapeDtypeStruct((t, d), F32)] + route_shapes,
        compiler_params=_params("arbitrary"),
        name="proj_res_router",
    )(a, w, h, *route_args)
    return out[0], out[1:]


def _dispatch_kernel(last_ref, pos_ref, x_hbm, o_ref, xbuf, zero_ref, load_sem, row_sem):
    i = pl.program_id(0)
    n = pl.num_programs(0)
    tm = xbuf.shape[1] // TOKEN_TILE
    in_rows = tm * TOKEN_TILE
    tile_rows = EXPERT_TM * TOKEN_TILE
    cur = i % 2
    sem = row_sem.at[0]

    def load(j, slot):
        src = pl.multiple_of(j * in_rows, in_rows)
        return pltpu.make_async_copy(x_hbm.at[pl.ds(src, in_rows)], xbuf.at[slot], load_sem.at[slot])

    def drain(slot):
        for _ in range(2):
            pltpu.make_async_copy(xbuf.at[slot], o_ref.at[pl.ds(0, in_rows)], row_sem.at[slot]).wait()

    @pl.when(i == 0)
    def _():
        load(0, 0).start()
        zero_ref[...] = jnp.zeros_like(zero_ref)

        def tile_fill(e):
            row = pl.multiple_of(last_ref[e] * TOKEN_TILE, tile_rows)
            return pltpu.make_async_copy(zero_ref, o_ref.at[pl.ds(row, tile_rows)], sem)

        for e in range(N_EXPERTS):
            @pl.when(last_ref[e] >= 0)
            def _():
                tile_fill(e).start()
        for e in range(N_EXPERTS):
            @pl.when(last_ref[e] >= 0)
            def _():
                tile_fill(e).wait()

        def spare_fill(j):
            row = pl.multiple_of(j * tile_rows, tile_rows)
            return pltpu.make_async_copy(zero_ref, o_ref.at[pl.ds(row, tile_rows)], sem)

        n_tiles = o_ref.shape[0] // tile_rows
        lax.fori_loop(last_ref[N_EXPERTS], n_tiles, lambda j, c: (spare_fill(j).start(), c)[1], 0)
        lax.fori_loop(last_ref[N_EXPERTS], n_tiles, lambda j, c: (spare_fill(j).wait(), c)[1], 0)

    load(i, cur).wait()

    @pl.when(i > 0)
    def _():
        drain(1 - cur)

    @pl.when(i + 1 < n)
    def _():
        load(i + 1, 1 - cur).start()

    def row_copy(r, k):
        dst = pl.multiple_of(pos_ref[0, 0, k * tm + r] * TOKEN_TILE, TOKEN_TILE)
        return pltpu.make_async_copy(xbuf.at[cur, pl.ds(r * TOKEN_TILE, TOKEN_TILE)],
                                     o_ref.at[pl.ds(dst, TOKEN_TILE)], row_sem.at[cur])

    def issue(blk, carry):
        for u in range(ROW_DMA_UNROLL):
            r = blk * ROW_DMA_UNROLL + u
            row_copy(r, 0).start(priority=u % 2)
            row_copy(r, 1).start(priority=(u + 1) % 2)
        return carry

    lax.fori_loop(0, tm // ROW_DMA_UNROLL, issue, 0)

    @pl.when(i == n - 1)
    def _():
        drain(cur)


def dispatch(xn, pos3, last_tile_row, n_rows, tm=DISPATCH_TM):
    t = xn.shape[0] // TOKEN_TILE
    return pl.pallas_call(
        _dispatch_kernel,
        grid_spec=pltpu.PrefetchScalarGridSpec(
            num_scalar_prefetch=1,
            grid=(t // tm,),
            in_specs=[pl.BlockSpec((1, 1, 2 * tm), lambda i, last: (i, 0, 0), memory_space=pltpu.SMEM),
                      pl.BlockSpec(memory_space=pl.ANY)],
            out_specs=pl.BlockSpec(memory_space=pl.ANY),
            scratch_shapes=[pltpu.VMEM((2, tm * TOKEN_TILE, 128), xn.dtype),
                            pltpu.VMEM((EXPERT_TM * TOKEN_TILE, 128), xn.dtype),
                            pltpu.SemaphoreType.DMA((2,)),
                            pltpu.SemaphoreType.DMA((2,))]),
        out_shape=jax.ShapeDtypeStruct((n_rows * TOKEN_TILE, 128), xn.dtype),
        compiler_params=_params("arbitrary"),
        name="moe_dispatch",
    )(last_tile_row, pos3, xn)


def _expert_kernel(te_ref, nv_ref, par_ref, nxt_ref, x_hbm, wgu_hbm, wd_hbm, y_ref,
                   xbuf, wgu_f, wd_f, wgu_b, wd_b, xsem, wsem, *, layer):
    i = pl.program_id(0)
    n_valid = nv_ref[0]
    tm = xbuf.shape[1] // TOKEN_TILE
    in_rows = tm * TOKEN_TILE
    valid = i < n_valid
    changed = te_ref[i] != te_ref[jnp.maximum(i - 1, 0)]
    first = jnp.logical_or(i == 0, changed)

    def xload(j):
        slot = j % EXPERT_X_BUFS
        src = pl.multiple_of(j * in_rows, in_rows)
        return pltpu.make_async_copy(x_hbm.at[pl.ds(src, in_rows)], xbuf.at[slot], xsem.at[slot])

    def wload(e, slot):
        return (pltpu.make_async_copy(wgu_hbm.at[layer, e], wgu_f.at[slot], wsem.at[slot]),
                pltpu.make_async_copy(wd_hbm.at[layer, e], wd_f.at[slot], wsem.at[slot]))

    @pl.when(i == 0)
    def _():
        for cp in wload(te_ref[0], par_ref[0]):
            cp.start()
        for j in range(EXPERT_X_BUFS - 1):
            @pl.when(j < n_valid)
            def _():
                xload(j).start()

    @pl.when(i + EXPERT_X_BUFS - 1 < n_valid)
    def _():
        xload(i + EXPERT_X_BUFS - 1).start()

    @pl.when(jnp.logical_and(valid, first))
    def _():
        slot = par_ref[i]
        for cp in wload(te_ref[i], slot):
            cp.wait()

        @pl.when(nxt_ref[i] >= 0)
        def _():
            for cp in wload(nxt_ref[i], 1 - slot):
                cp.start()

        wgu_b[...] = wgu_f[slot].astype(BF16)
        wd_b[...] = wd_f[slot].astype(BF16)

    @pl.when(valid)
    def _():
        xload(i).wait()
        x = _from_tiles(xbuf.at[i % EXPERT_X_BUFS], tm).astype(BF16)
        au = _nn(x, wgu_b[...])
        a = au[:, :D_EXPERT]
        u = au[:, D_EXPERT:]
        mid = (jax.nn.silu(a) * u).astype(BF16)
        _to_tiles(y_ref, _nn(mid, wd_b[...]))

    @pl.when(jnp.logical_not(valid))
    def _():
        y_ref[...] = jnp.zeros_like(y_ref)


def experts(xg, w_gu, w_down, layer, tile_expert, n_valid, seg_parity, seg_next, tm=EXPERT_TM):
    d = D_MODEL
    p = xg.shape[0] // TOKEN_TILE
    n_tiles = p // tm
    return pl.pallas_call(
        functools.partial(_expert_kernel, layer=layer),
        grid_spec=pltpu.PrefetchScalarGridSpec(
            num_scalar_prefetch=4,
            grid=(n_tiles,),
            in_specs=[pl.BlockSpec(memory_space=pl.ANY),
                      pl.BlockSpec(memory_space=pl.ANY),
                      pl.BlockSpec(memory_space=pl.ANY)],
            out_specs=pl.BlockSpec((tm * TOKEN_TILE, 128), lambda i, *_: (i, 0)),
            scratch_shapes=[pltpu.VMEM((EXPERT_X_BUFS, tm * TOKEN_TILE, 128), F32),
                            pltpu.VMEM((2, d, 2 * D_EXPERT), F32),
                            pltpu.VMEM((2, D_EXPERT, d), F32),
                            pltpu.VMEM((d, 2 * D_EXPERT), BF16),
                            pltpu.VMEM((D_EXPERT, d), BF16),
                            pltpu.SemaphoreType.DMA((EXPERT_X_BUFS,)),
                            pltpu.SemaphoreType.DMA((2,))]),
        out_shape=jax.ShapeDtypeStruct((p * TOKEN_TILE, 128), F32),
        compiler_params=_params("arbitrary"),
        name="moe_experts",
    )(tile_expert, n_valid, seg_parity, seg_next, xg, w_gu, w_down)


def _qkv_tail(x, gq_ref, gkv_ref, wqt_ref, wk_ref, wvt_ref, qt_ref, k_ref, vt_ref):
    y = x * lax.rsqrt(jnp.mean(x * x, axis=-1, keepdims=True) + NORM_EPS)
    xq = (y * gq_ref[...]).astype(BF16)
    xkv = (y * gkv_ref[...]).astype(BF16)
    qt_ref[0] = (_nt(wqt_ref[...], xq) * (B_HEAD_DIM ** -0.5 * LOG2E)).astype(qt_ref.dtype)
    k_ref[...] = _nn(xkv, wk_ref[...]).astype(k_ref.dtype)
    vt_ref[0] = _nt(wvt_ref[...], xkv).astype(vt_ref.dtype)


def _combine_kernel(pos_ref, pos_next_ref, h_ref, w_ref, y_ref, *rest, tail):
    if tail == "norm":
        g_ref, o_ref, buf, sem = rest
    else:
        gq_ref, gkv_ref, wqt_ref, wk_ref, wvt_ref, o_ref, qt_ref, k_ref, vt_ref, buf, sem = rest
    i = pl.program_id(0)
    tm = h_ref.shape[0]
    cur = i % 2

    def gather_tile(p_ref, half):
        def row_copy(r, k):
            src = pl.multiple_of(p_ref[0, 0, k * tm + r] * TOKEN_TILE, TOKEN_TILE)
            return pltpu.make_async_copy(y_ref.at[pl.ds(src, TOKEN_TILE)],
                                         buf.at[2 * half + k, pl.ds(r * TOKEN_TILE, TOKEN_TILE)],
                                         sem.at[half])

        def issue(blk, carry):
            for u in range(ROW_DMA_UNROLL):
                r = blk * ROW_DMA_UNROLL + u
                row_copy(r, 0).start(priority=u % 2)
                row_copy(r, 1).start(priority=(u + 1) % 2)
            return carry

        lax.fori_loop(0, tm // ROW_DMA_UNROLL, issue, 0)

    @pl.when(i == 0)
    def _():
        gather_tile(pos_ref, 0)

    @pl.when(i + 1 < pl.num_programs(0))
    def _():
        gather_tile(pos_next_ref, 1 - cur)

    for k in range(2):
        pltpu.make_async_copy(y_ref.at[pl.ds(0, tm * TOKEN_TILE)], buf.at[2 * cur + k], sem.at[cur]).wait()
    w = w_ref[...]
    out = (h_ref[...] + w[:, 0:1] * _from_tiles(buf.at[2 * cur], tm)
           + w[:, 1:2] * _from_tiles(buf.at[2 * cur + 1], tm))
    if tail == "norm":
        o_ref[...] = out * lax.rsqrt(jnp.mean(out * out, axis=-1, keepdims=True) + NORM_EPS) * g_ref[...]
    else:
        o_ref[...] = out
        _qkv_tail(out, gq_ref, gkv_ref, wqt_ref, wk_ref, wvt_ref, qt_ref, k_ref, vt_ref)


def combine(h, wgt_t, pos3, y, tail, tail_args, batch, seq, tm=COMBINE_TM):
    t, d = h.shape
    nt = t // tm
    nblk = seq // tm
    row = pl.BlockSpec((tm, d), lambda i: (i, 0))
    vec = pl.BlockSpec((1, d), lambda i: (0, 0))
    full = pl.BlockSpec((d, d), lambda i: (0, 0))
    tr = pl.BlockSpec((1, d, tm), lambda i: (i // nblk, 0, i % nblk))
    if tail == "norm":
        (g,) = tail_args
        extra_in, extra_specs = [g.reshape(1, d)], [vec]
        out_specs, out_shape = row, jax.ShapeDtypeStruct((t, d), F32)
    else:
        g_q, g_kv, wqt, wk, wvt = tail_args
        extra_in = [g_q.reshape(1, d), g_kv.reshape(1, d), wqt, wk, wvt]
        extra_specs = [vec, vec, full, full, full]
        out_specs = [row, tr, row, tr]
        out_shape = [jax.ShapeDtypeStruct((t, d), F32),
                     jax.ShapeDtypeStruct((batch, d, seq), BF16),
                     jax.ShapeDtypeStruct((t, d), BF16),
                     jax.ShapeDtypeStruct((batch, d, seq), BF16)]
    return pl.pallas_call(
        functools.partial(_combine_kernel, tail=tail),
        grid=(nt,),
        in_specs=[pl.BlockSpec((1, 1, 2 * tm), lambda i: (i, 0, 0), memory_space=pltpu.SMEM),
                  pl.BlockSpec((1, 1, 2 * tm), lambda i: (jnp.minimum(i + 1, nt - 1), 0, 0),
                               memory_space=pltpu.SMEM),
                  row,
                  pl.BlockSpec((tm, 2), lambda i: (i, 0)),
                  pl.BlockSpec(memory_space=pl.ANY)] + extra_specs,
        out_specs=out_specs,
        out_shape=out_shape,
        scratch_shapes=[pltpu.VMEM((4, tm * TOKEN_TILE, 128), F32), pltpu.SemaphoreType.DMA((2,))],
        compiler_params=_params("arbitrary"),
        name="moe_combine_" + tail,
    )(pos3, pos3, h, wgt_t, y, *extra_in)


def hier_moe_layer(h, routing, w_gu, w_down, layer, tail, tail_args, batch, seq):
    t, d = h.shape
    xn, eid, wgt, rank, cnt = routing
    counts = cnt[:, 0].astype(I32)
    padded = ((counts + EXPERT_TM - 1) // EXPERT_TM) * EXPERT_TM
    ends = jnp.cumsum(padded)
    offs = ends - padded
    n_rows = 2 * t + N_EXPERTS * EXPERT_TM
    n_tiles = n_rows // EXPERT_TM
    n_valid = (ends[-1] // EXPERT_TM).astype(I32).reshape(1)
    tile_start = jnp.arange(n_tiles, dtype=I32) * EXPERT_TM
    tile_start = jnp.minimum(tile_start, ends[-1] - 1)
    tile_expert = jnp.sum((ends[None, :] <= tile_start[:, None]).astype(I32), axis=1)
    onehot = eid[None] == jnp.arange(N_EXPERTS, dtype=I32)[:, None, None]
    pos = jnp.sum(jnp.where(onehot, offs[:, None, None], 0), axis=0) + rank

    def tiled(tm):
        return pos.reshape(2, t // tm, tm).transpose(1, 0, 2).reshape(t // tm, 1, 2 * tm)

    last_tile_row = jnp.concatenate([jnp.where(padded > 0, ends - EXPERT_TM, -1).astype(I32), n_valid])
    xg = dispatch(xn, tiled(DISPATCH_TM), last_tile_row, n_rows)
    nonempty = padded > 0
    ordinal = jnp.cumsum(nonempty.astype(I32)) - 1
    ids = jnp.arange(N_EXPERTS, dtype=I32)
    later = jnp.logical_and(nonempty[None, :], ids[None, :] > ids[:, None])
    next_expert = jnp.min(jnp.where(later, ids[None, :], N_EXPERTS), axis=1)
    next_expert = jnp.where(next_expert == N_EXPERTS, -1, next_expert).astype(I32)
    y = experts(xg, w_gu, w_down, layer, tile_expert, n_valid,
                (ordinal[tile_expert] % 2).astype(I32), next_expert[tile_expert])
    return combine(h, wgt.T, tiled(COMBINE_TM), y, tail, tail_args, batch, seq)


def _attn_kernel(q_ref, k_ref, vt_ref, lam_ref, g_ref, o_ref, qq_ref, m_ref, acc_ref,
                 a_ref, c_ref, *sp_refs, lambda_init, heads):
    s_refs, p_refs = sp_refs[:heads], sp_refs[heads:]
    qi = pl.program_id(2)
    tq, tk, hd = ATT_TQ, ATT_TK, 2 * B_HEAD_DIM
    feat = lax.broadcasted_iota(I32, (hd, tq), 0)
    for g in range(heads):
        qt = q_ref[0, g * hd:(g + 1) * hd, :]
        zero = jnp.zeros_like(qt)
        qq_ref[g, :, 0:tq] = jnp.where(feat < B_HEAD_DIM, qt, zero)
        qq_ref[g, :, tq:2 * tq] = jnp.where(feat >= B_HEAD_DIM, qt, zero)
    m_ref[...] = jnp.full_like(m_ref, -jnp.inf)
    acc_ref[...] = jnp.zeros_like(acc_ref)
    ones = jnp.ones((ATT_SUM_ROWS, tk), BF16)
    dyn0 = jnp.minimum(qi, 0)

    def scores(j, masked, nk, ps):
        off = pl.multiple_of(j * tk, tk)
        if masked:
            krow = lax.broadcasted_iota(I32, (nk, 2 * tq), 0)
            qcol = lax.broadcasted_iota(I32, (nk, 2 * tq), 1)
            visible = off + krow <= qi * tq + jnp.where(qcol >= tq, qcol - tq, qcol)
        for g in range(heads):
            kb = k_ref[pl.ds(off, nk), g * hd:(g + 1) * hd]
            s = _nn(kb, qq_ref[g])
            if masked:
                s = jnp.where(visible, s, -jnp.inf)
            s_refs[g][0, 0:nk, :] = s
            c_ref[g] = jnp.max(s, axis=0, keepdims=True)
        for g in range(heads):
            m_old = m_ref[g]
            m_new = jnp.maximum(m_old, c_ref[g])
            a_ref[ps, g] = jnp.exp2(m_old - m_new)
            m_ref[g] = m_new
            for c in range(0, nk, ATT_CHUNK):
                p_refs[ps * heads + g][0, c:c + ATT_CHUNK, :] = jnp.exp2(
                    s_refs[g][dyn0, c:c + ATT_CHUNK, :] - m_new).astype(BF16)

    def accumulate(j, nk, ps):
        off = pl.multiple_of(j * tk, tk)
        for g in range(heads):
            vtb = vt_ref[0, g * hd:(g + 1) * hd, pl.ds(off, nk)]
            lhs = jnp.concatenate([vtb, ones[:, 0:nk]], axis=0)
            acc_ref[g] = a_ref[ps, g] * acc_ref[g] + _nn(lhs, p_refs[ps * heads + g][dyn0, 0:nk, :])

    def step(j, masked, nk=tk):
        scores(j, masked, nk, 0)
        accumulate(j, nk, 0)

    n_full = (qi * tq) // tk

    def two_steps(jj, carry):
        scores(2 * jj, False, tk, 0)
        scores(2 * jj + 1, False, tk, 1)
        accumulate(2 * jj, tk, 0)
        accumulate(2 * jj + 1, tk, 1)
        return carry

    lax.fori_loop(0, n_full // 2, two_steps, 0)

    @pl.when(n_full % 2 == 1)
    def _():
        step(n_full - 1, False)

    first_part = (qi * tq) % tk + tq <= tk // 2

    @pl.when(first_part)
    def _():
        step(n_full, True, tk // 2)

    @pl.when(jnp.logical_not(first_part))
    def _():
        step(n_full, True)

    lam = lam_ref[...]
    lam_full = (jnp.exp(jnp.sum(lam[0:1] * lam[1:2], axis=-1, keepdims=True))
                - jnp.exp(jnp.sum(lam[2:3] * lam[3:4], axis=-1, keepdims=True)) + lambda_init)
    for g in range(heads):
        acc = acc_ref[g]
        on = acc[:hd] / acc[hd:hd + 1]
        ot = on[:, :tq] - lam_full * on[:, tq:]
        o = ot.T
        o = o * lax.rsqrt(jnp.mean(o * o, axis=-1, keepdims=True) + SUBLN_EPS) * g_ref[...]
        o_ref[:, g * hd:(g + 1) * hd] = (o * (1.0 - lambda_init)).astype(o_ref.dtype)


def diff_attn(qt, k, vt, lam, subln_g, batch, seq, lambda_init, heads=ATT_HEADS):
    t, d = k.shape
    nq = seq // ATT_TQ
    hd = 2 * B_HEAD_DIM
    w = heads * hd
    return pl.pallas_call(
        functools.partial(_attn_kernel, lambda_init=lambda_init, heads=heads),
        grid=(batch, B_HEADS // heads, nq),
        in_specs=[pl.BlockSpec((1, w, ATT_TQ), lambda b, h, i: (b, h, i)),
                  pl.BlockSpec((seq, w), lambda b, h, i: (b, h)),
                  pl.BlockSpec((1, w, seq), lambda b, h, i: (b, h, 0)),
                  pl.BlockSpec(lam.shape, lambda b, h, i: (0, 0)),
                  pl.BlockSpec((1, hd), lambda b, h, i: (0, 0))],
        out_specs=pl.BlockSpec((ATT_TQ, w), lambda b, h, i: (b * nq + i, h)),
        out_shape=jax.ShapeDtypeStruct((t, d), BF16),
        scratch_shapes=[pltpu.VMEM((heads, hd, 2 * ATT_TQ), BF16),
                        pltpu.VMEM((heads, 1, 2 * ATT_TQ), F32),
                        pltpu.VMEM((heads, hd + ATT_SUM_ROWS, 2 * ATT_TQ), F32),
                        pltpu.VMEM((2, heads, 1, 2 * ATT_TQ), F32),
                        pltpu.VMEM((heads, 1, 2 * ATT_TQ), F32)]
        + [pltpu.VMEM((1, ATT_TK, 2 * ATT_TQ), F32) for _ in range(heads)]
        + [pltpu.VMEM((1, ATT_TK, 2 * ATT_TQ), BF16) for _ in range(2 * heads)],
        compiler_params=_params("parallel", "parallel", "arbitrary"),
        name="diff_attn",
    )(qt, k, vt, lam, subln_g.reshape(1, hd))


def kernel(x, a_norm_g, a_w_in, a_lb, a_onorm_g, a_w_out, kv_norm_g, w_kv, b_norm_g, b_w_q, b_lam,
           b_subln_g, b_w_out, ffn_norm_g, router_g_w, router_g_b, router_e_w, router_e_b,
           expert_w_gu, expert_w_down, final_norm_g):
    batch, seq, d = x.shape
    assert d == D_MODEL and a_norm_g.shape[0] == 1 and b_norm_g.shape[0] == 1
    assert seq % max(2 * HG_BLOCK, ATT_TK, ROUTER_TM) == 0
    t = batch * seq
    h = x.reshape(t, d)

    h, routing = hgrn2_layer(h, a_norm_g[0], a_w_in[0].astype(BF16), a_lb, a_onorm_g[0],
                             a_w_out[0].astype(BF16), seq,
                             (ffn_norm_g[0], router_g_w[0], router_g_b[0], router_e_w[0], router_e_b[0]))
    qkv_args = (b_norm_g[0], kv_norm_g, b_w_q[0].T.astype(BF16),
                w_kv[:, :d].astype(BF16), w_kv[:, d:].T.astype(BF16))
    h, qt, k, vt = hier_moe_layer(h, routing, expert_w_gu, expert_w_down, 0, "qkv", qkv_args, batch, seq)

    lambda_init = 0.8 - 0.6 * math.exp(-0.3 * 1)
    o = diff_attn(qt, k, vt, b_lam[0], b_subln_g[0], batch, seq, lambda_init)
    h, routing = proj_res_router(o, b_w_out[0].astype(BF16), h,
                                 (ffn_norm_g[1], router_g_w[1], router_g_b[1], router_e_w[1], router_e_b[1]))
    h = hier_moe_layer(h, routing, expert_w_gu, expert_w_down, 1, "norm", (final_norm_g,), batch, seq)
    return h.reshape(batch, seq, d)
```

```python
import functools
import math

import jax
import jax.numpy as jnp
from jax import lax
from jax.experimental import pallas as pl
from jax.experimental.pallas import tpu as pltpu

F32 = jnp.float32
BF16 = jnp.bfloat16
I32 = jnp.int32

D_MODEL = 1024
A_HEADS = 8
A_HEAD_DIM = 128
B_HEADS = 8
B_HEAD_DIM = 64
N_GROUPS = 4
EXPERTS_PER_GROUP = 8
N_EXPERTS = N_GROUPS * EXPERTS_PER_GROUP
D_EXPERT = 512
NORM_EPS = 1e-6
SUBLN_EPS = 1e-5
LOG2E = 1.4426950408889634

GLA_BLOCK = 128
GLA_HALF = GLA_BLOCK // 2
HG_BLOCK = 256
ATT_TQ = 256
ATT_TK = 512
ATT_HEADS = 8
ATT_CHUNK = 128
ATT_SUM_ROWS = 16
ROUTER_TM = 512
ROUTER_EXPERT_ROW0 = 8
ROUTER_ROWS = ROUTER_EXPERT_ROW0 + N_EXPERTS
EXPERT_TM = 512
EXPERT_X_BUFS = 3
DISPATCH_TM = 2048
COMBINE_TM = 512
ROW_DMA_UNROLL = 8
TOKEN_TILE = 8
VMEM_LIMIT = 56 * 1024 * 1024


def _nt(a, b):
    return lax.dot_general(a, b, (((1,), (1,)), ((), ())), preferred_element_type=F32)


def _nn(a, b):
    return jnp.dot(a, b, preferred_element_type=F32)


def _to_tiles(ref, x):
    tm = x.shape[0]
    for s in range(TOKEN_TILE):
        ref[pl.ds(s, tm, stride=TOKEN_TILE), :] = x[:, s * 128:(s + 1) * 128]


def _from_tiles(ref, tm):
    return jnp.concatenate([ref[pl.ds(s, tm, stride=TOKEN_TILE), :] for s in range(TOKEN_TILE)], axis=1)


def _params(*sem):
    return pltpu.CompilerParams(dimension_semantics=sem, vmem_limit_bytes=VMEM_LIMIT)


def _hgrn2_project(x, g_ref, w_ref, proj_ref):
    d = x.shape[1]
    y = x * lax.rsqrt(jnp.mean(x * x, axis=-1, keepdims=True) + NORM_EPS)
    xn = (y * g_ref[...]).astype(BF16)
    for c in range(w_ref.shape[1] // d):
        proj_ref[:, c * d:(c + 1) * d] = _nn(xn, w_ref[:, c * d:(c + 1) * d])


def _gla_block(proj_ref, row0, lb, og, st_ref, out_ref, out_row0):
    d = D_MODEL
    rows = pl.ds(pl.multiple_of(row0, GLA_BLOCK), GLA_BLOCK)
    fz = proj_ref[rows, d:2 * d]
    logf = jnp.log(lb + (1.0 - lb) * jax.nn.sigmoid(fz))
    kk = (1.0 - lb) * jax.nn.sigmoid(-fz)
    qq = jax.nn.silu(proj_ref[rows, 0:d])

    row = lax.broadcasted_iota(I32, logf.shape, 0)
    b = logf
    shift = 1
    while shift < GLA_BLOCK:
        b = b + jnp.where(row >= shift, pltpu.roll(b, shift, axis=0), 0.0)
        shift *= 2

    h0, h1 = slice(0, GLA_HALF), slice(GLA_HALF, GLA_BLOCK)
    b_a_mid = b[GLA_HALF // 2 - 1:GLA_HALF // 2]
    b_a_end = b[GLA_HALF - 1:GLA_HALF]
    b_b_mid = b[GLA_HALF + GLA_HALF // 2 - 1:GLA_HALF + GLA_HALF // 2]
    b_end = b[GLA_BLOCK - 1:GLA_BLOCK]

    qa_n = (qq[h0] * jnp.exp(b[h0] - b_a_mid)).astype(BF16)
    ka_n = (kk[h0] * jnp.exp(b_a_mid - b[h0])).astype(BF16)
    qb_n = (qq[h1] * jnp.exp(b[h1] - b_b_mid)).astype(BF16)
    kb_n = (kk[h1] * jnp.exp(b_b_mid - b[h1])).astype(BF16)
    qb_x = (qq[h1] * jnp.exp(b[h1] - b_a_end)).astype(BF16)
    ka_x = (kk[h0] * jnp.exp(b_a_end - b[h0])).astype(BF16)
    q_dec = (qq * jnp.exp(b)).astype(BF16)
    k_end = (kk * jnp.exp(b_end - b)).astype(BF16)
    d_end = jnp.exp(b_end)

    rr = lax.broadcasted_iota(I32, (GLA_HALF, GLA_HALF), 0)
    ss = lax.broadcasted_iota(I32, (GLA_HALF, GLA_HALF), 1)
    causal = ss <= rr

    for h in range(A_HEADS):
        hs = slice(h * A_HEAD_DIM, (h + 1) * A_HEAD_DIM)
        v = proj_ref[rows, 2 * d + h * A_HEAD_DIM:2 * d + (h + 1) * A_HEAD_DIM]
        vb = v.astype(BF16)
        st = st_ref[h]
        st_b = st.astype(BF16)
        p_aa = jnp.where(causal, _nt(qa_n[:, hs], ka_n[:, hs]), 0.0).astype(BF16)
        p_bb = jnp.where(causal, _nt(qb_n[:, hs], kb_n[:, hs]), 0.0).astype(BF16)
        p_ba = _nt(qb_x[:, hs], ka_x[:, hs]).astype(BF16)
        inter = _nt(q_dec[:, hs], st_b)
        o_a = _nn(p_aa, vb[h0]) + inter[h0]
        o_b = _nn(p_bb, vb[h1]) + _nn(p_ba, vb[h0]) + inter[h1]
        st_ref[h] = d_end[:, hs] * st + _nn(v.T.astype(BF16), k_end[:, hs])
        gate = jax.nn.silu(proj_ref[rows, 3 * d + h * A_HEAD_DIM:3 * d + (h + 1) * A_HEAD_DIM])
        for half, o in ((0, o_a), (1, o_b)):
            on = o * lax.rsqrt(jnp.mean(o * o, axis=-1, keepdims=True) + NORM_EPS)
            r0 = out_row0 + half * GLA_HALF
            out_ref[r0:r0 + GLA_HALF, hs] = (on * og * gate[half * GLA_HALF:(half + 1) * GLA_HALF]
                                             ).astype(out_ref.dtype)


def _hgrn2_kernel(x_ref, xnext_ref, g_ref, win_ref, alb_ref, og_ref, wout_ref,
                  rg_ref, rwhl_ref, rb_ref, o_ref, xn_ref, eid_ref, wgt_ref, rank_ref, cnt_ref,
                  proj_a, proj_b, gated_a, gated_b, st_ref, *, blocks_per_seq):
    step = pl.program_id(0)
    dyn0 = jnp.minimum(step, 0)
    alb = alb_ref[...]
    e = jnp.exp(alb - jnp.max(alb, axis=0, keepdims=True))
    lb = e[0:1] / jnp.sum(e, axis=0, keepdims=True)
    og = og_ref[...]

    @pl.when(step == 0)
    def _():
        _hgrn2_project(x_ref[0:HG_BLOCK, :], g_ref, win_ref, proj_a)

    @pl.when((2 * step) % blocks_per_seq == 0)
    def _():
        st_ref[...] = jnp.zeros_like(st_ref)

    _hgrn2_project(x_ref[HG_BLOCK:2 * HG_BLOCK, :], g_ref, win_ref, proj_b)
    for sub in range(HG_BLOCK // GLA_BLOCK):
        _gla_block(proj_a, dyn0 + sub * GLA_BLOCK, lb, og, st_ref, gated_a, sub * GLA_BLOCK)
    out_a = x_ref[0:HG_BLOCK, :] + _nn(gated_a[...], wout_ref[...])
    o_ref[0:HG_BLOCK, :] = out_a

    _hgrn2_project(xnext_ref[...], g_ref, win_ref, proj_a)
    for sub in range(HG_BLOCK // GLA_BLOCK):
        _gla_block(proj_b, dyn0 + sub * GLA_BLOCK, lb, og, st_ref, gated_b, sub * GLA_BLOCK)
    out_b = x_ref[HG_BLOCK:2 * HG_BLOCK, :] + _nn(gated_b[...], wout_ref[...])
    o_ref[HG_BLOCK:2 * HG_BLOCK, :] = out_b

    _route_tile(jnp.concatenate([out_a, out_b], axis=0), rg_ref, rwhl_ref, rb_ref,
                xn_ref, eid_ref, wgt_ref, rank_ref, cnt_ref)


def hgrn2_layer(x, norm_g, w_in_bf16, a_lb, onorm_g, w_out_bf16, seq, route_params):
    t, d = x.shape
    n = w_in_bf16.shape[1]
    nblocks = t // HG_BLOCK
    assert 2 * HG_BLOCK == ROUTER_TM
    route_args, route_in, route_specs, route_shapes = _route_plumbing(t, d, ROUTER_TM, *route_params)
    out = pl.pallas_call(
        functools.partial(_hgrn2_kernel, blocks_per_seq=seq // HG_BLOCK),
        grid=(nblocks // 2,),
        in_specs=[pl.BlockSpec((2 * HG_BLOCK, d), lambda i: (i, 0)),
                  pl.BlockSpec((HG_BLOCK, d), lambda i: (jnp.minimum(2 * i + 2, nblocks - 1), 0)),
                  pl.BlockSpec((1, d), lambda i: (0, 0)),
                  pl.BlockSpec((d, n), lambda i: (0, 0)),
                  pl.BlockSpec(a_lb.shape, lambda i: (0, 0)),
                  pl.BlockSpec((1, A_HEAD_DIM), lambda i: (0, 0)),
                  pl.BlockSpec((d, d), lambda i: (0, 0))] + route_in,
        out_specs=[pl.BlockSpec((2 * HG_BLOCK, d), lambda i: (i, 0))] + route_specs,
        out_shape=[jax.ShapeDtypeStruct((t, d), F32)] + route_shapes,
        scratch_shapes=[pltpu.VMEM((HG_BLOCK, n), F32),
                        pltpu.VMEM((HG_BLOCK, n), F32),
                        pltpu.VMEM((HG_BLOCK, d), BF16),
                        pltpu.VMEM((HG_BLOCK, d), BF16),
                        pltpu.VMEM((A_HEADS, A_HEAD_DIM, A_HEAD_DIM), F32)],
        compiler_params=_params("arbitrary"),
        name="hgrn2_layer",
    )(x, x, norm_g.reshape(1, d), w_in_bf16, a_lb, onorm_g.reshape(1, A_HEAD_DIM), w_out_bf16,
      *route_args)
    return out[0], out[1:]


def _route_tile(x, g_ref, whl_ref, b_ref, xn_ref, eid_ref, wgt_ref, rank_ref, cnt_ref):
    tm = x.shape[0]

    @pl.when(pl.program_id(0) == 0)
    def _():
        cnt_ref[...] = jnp.zeros_like(cnt_ref)

    xn = x * lax.rsqrt(jnp.mean(x * x, axis=-1, keepdims=True) + NORM_EPS) * g_ref[...]
    _to_tiles(xn_ref, xn)
    xh = xn.astype(BF16)
    xl = (xn - xh.astype(F32)).astype(BF16)
    hl = _nt(whl_ref[...], xh)
    lg = hl[:ROUTER_ROWS] + hl[ROUTER_ROWS:] + _nt(whl_ref[0:ROUTER_ROWS, :], xl) + b_ref[...]

    gl = lg[0:N_GROUPS]
    r4 = lax.broadcasted_iota(I32, gl.shape, 0)
    gmax = jnp.max(gl, axis=0, keepdims=True)
    grp = jnp.min(jnp.where(gl == gmax, r4, N_GROUPS), axis=0, keepdims=True)
    p_grp = 1.0 / jnp.sum(jnp.exp(gl - gmax), axis=0, keepdims=True)

    r0 = ROUTER_EXPERT_ROW0
    fine = lg[r0:r0 + EXPERTS_PER_GROUP]
    for gi in range(1, N_GROUPS):
        fine = jnp.where(grp == gi, lg[r0 + gi * EXPERTS_PER_GROUP:r0 + (gi + 1) * EXPERTS_PER_GROUP], fine)
    r8 = lax.broadcasted_iota(I32, fine.shape, 0)
    m1 = jnp.max(fine, axis=0, keepdims=True)
    i1 = jnp.min(jnp.where(fine == m1, r8, EXPERTS_PER_GROUP), axis=0, keepdims=True)
    rest = jnp.where(r8 == i1, -jnp.inf, fine)
    m2 = jnp.max(rest, axis=0, keepdims=True)
    i2 = jnp.min(jnp.where(rest == m2, r8, EXPERTS_PER_GROUP), axis=0, keepdims=True)
    e21 = jnp.exp(m2 - m1)
    t1 = 1.0 / (1.0 + e21)
    wgt_ref[0:1, :] = p_grp * t1
    wgt_ref[1:2, :] = p_grp * (e21 * t1)
    e1 = grp * EXPERTS_PER_GROUP + i1
    e2 = grp * EXPERTS_PER_GROUP + i2
    eid_ref[0:1, :] = e1
    eid_ref[1:2, :] = e2

    r32 = lax.broadcasted_iota(I32, (N_EXPERTS, tm), 0)
    is1 = r32 == e1
    is2 = r32 == e2
    member = jnp.logical_or(is1, is2)
    ta = lax.broadcasted_iota(I32, (tm, tm), 0)
    tb = lax.broadcasted_iota(I32, (tm, tm), 1)
    before = (ta < tb).astype(BF16)
    prior = _nn(member.astype(BF16), before) + cnt_ref[:, 0:1]
    rank_ref[0:1, :] = jnp.sum(jnp.where(is1, prior, 0.0), axis=0, keepdims=True).astype(I32)
    rank_ref[1:2, :] = jnp.sum(jnp.where(is2, prior, 0.0), axis=0, keepdims=True).astype(I32)
    cnt_ref[...] = cnt_ref[...] + jnp.sum(member.astype(F32), axis=1, keepdims=True)


def _proj_res_router_kernel(a_ref, w_ref, r_ref, g_ref, whl_ref, b_ref, o_ref, *route_out):
    h = r_ref[...] + _nn(a_ref[...], w_ref[...])
    o_ref[...] = h
    _route_tile(h, g_ref, whl_ref, b_ref, *route_out)


def _route_plumbing(t, d, tm, g, wg, bg, we, be):
    r0 = ROUTER_EXPERT_ROW0
    w_all = jnp.zeros((ROUTER_ROWS, d), F32).at[0:N_GROUPS].set(wg.T).at[r0:r0 + N_EXPERTS].set(we.T)
    b_all = jnp.zeros((ROUTER_ROWS, 1), F32).at[0:N_GROUPS, 0].set(bg).at[r0:r0 + N_EXPERTS, 0].set(be)
    wh = w_all.astype(BF16)
    wl = (w_all - wh.astype(F32)).astype(BF16)
    row2 = lambda i: (0, i)
    route_in = [pl.BlockSpec((1, d), lambda i: (0, 0)),
                pl.BlockSpec((2 * ROUTER_ROWS, d), lambda i: (0, 0)),
                pl.BlockSpec((ROUTER_ROWS, 1), lambda i: (0, 0))]
    route_specs = [pl.BlockSpec((tm * TOKEN_TILE, 128), lambda i: (i, 0)),
                   pl.BlockSpec((2, tm), row2),
                   pl.BlockSpec((2, tm), row2),
                   pl.BlockSpec((2, tm), row2),
                   pl.BlockSpec((N_EXPERTS, 128), lambda i: (0, 0))]
    route_shapes = [jax.ShapeDtypeStruct((t * TOKEN_TILE, 128), F32),
                    jax.ShapeDtypeStruct((2, t), I32),
                    jax.ShapeDtypeStruct((2, t), F32),
                    jax.ShapeDtypeStruct((2, t), I32),
                    jax.ShapeDtypeStruct((N_EXPERTS, 128), F32)]
    return (g.reshape(1, d), jnp.concatenate([wh, wl]), b_all), route_in, route_specs, route_shapes


def proj_res_router(a, w, h, route_params, tm=ROUTER_TM):
    t, d = h.shape
    row = pl.BlockSpec((tm, d), lambda i: (i, 0))
    route_args, route_in, route_specs, route_shapes = _route_plumbing(t, d, tm, *route_params)
    out = pl.pallas_call(
        _proj_res_router_kernel,
        grid=(t // tm,),
        in_specs=[row, pl.BlockSpec((d, d), lambda i: (0, 0)), row] + route_in,
        out_specs=[row] + route_specs,
        out_shape=[jax.ShapeDtypeStruct((t, d), F32)] + route_shapes,
        compiler_params=_params("arbitrary"),
        name="proj_res_router",
    )(a, w, h, *route_args)
    return out[0], out[1:]


def _dispatch_kernel(last_ref, pos_ref, x_hbm, o_ref, xbuf, zero_ref, load_sem, row_sem):
    i = pl.program_id(0)
    n = pl.num_programs(0)
    tm = xbuf.shape[1] // TOKEN_TILE
    in_rows = tm * TOKEN_TILE
    tile_rows = EXPERT_TM * TOKEN_TILE
    cur = i % 2
    sem = row_sem.at[0]

    def load(j, slot):
        src = pl.multiple_of(j * in_rows, in_rows)
        return pltpu.make_async_copy(x_hbm.at[pl.ds(src, in_rows)], xbuf.at[slot], load_sem.at[slot])

    def drain(slot):
        for _ in range(2):
            pltpu.make_async_copy(xbuf.at[slot], o_ref.at[pl.ds(0, in_rows)], row_sem.at[slot]).wait()

    @pl.when(i == 0)
    def _():
        load(0, 0).start()
        zero_ref[...] = jnp.zeros_like(zero_ref)

        def tile_fill(e):
            row = pl.multiple_of(last_ref[e] * TOKEN_TILE, tile_rows)
            return pltpu.make_async_copy(zero_ref, o_ref.at[pl.ds(row, tile_rows)], sem)

        for e in range(N_EXPERTS):
            @pl.when(last_ref[e] >= 0)
            def _():
                tile_fill(e).start()
        for e in range(N_EXPERTS):
            @pl.when(last_ref[e] >= 0)
            def _():
                tile_fill(e).wait()

        def spare_fill(j):
            row = pl.multiple_of(j * tile_rows, tile_rows)
            return pltpu.make_async_copy(zero_ref, o_ref.at[pl.ds(row, tile_rows)], sem)

        n_tiles = o_ref.shape[0] // tile_rows
        lax.fori_loop(last_ref[N_EXPERTS], n_tiles, lambda j, c: (spare_fill(j).start(), c)[1], 0)
        lax.fori_loop(last_ref[N_EXPERTS], n_tiles, lambda j, c: (spare_fill(j).wait(), c)[1], 0)

    load(i, cur).wait()

    @pl.when(i > 0)
    def _():
        drain(1 - cur)

    @pl.when(i + 1 < n)
    def _():
        load(i + 1, 1 - cur).start()

    def row_copy(r, k):
        dst = pl.multiple_of(pos_ref[0, 0, k * tm + r] * TOKEN_TILE, TOKEN_TILE)
        return pltpu.make_async_copy(xbuf.at[cur, pl.ds(r * TOKEN_TILE, TOKEN_TILE)],
                                     o_ref.at[pl.ds(dst, TOKEN_TILE)], row_sem.at[cur])

    def issue(blk, carry):
        for u in range(ROW_DMA_UNROLL):
            r = blk * ROW_DMA_UNROLL + u
            row_copy(r, 0).start(priority=u % 2)
            row_copy(r, 1).start(priority=(u + 1) % 2)
        return carry

    lax.fori_loop(0, tm // ROW_DMA_UNROLL, issue, 0)

    @pl.when(i == n - 1)
    def _():
        drain(cur)


def dispatch(xn, pos3, last_tile_row, n_rows, tm=DISPATCH_TM):
    t = xn.shape[0] // TOKEN_TILE
    return pl.pallas_call(
        _dispatch_kernel,
        grid_spec=pltpu.PrefetchScalarGridSpec(
            num_scalar_prefetch=1,
            grid=(t // tm,),
            in_specs=[pl.BlockSpec((1, 1, 2 * tm), lambda i, last: (i, 0, 0), memory_space=pltpu.SMEM),
                      pl.BlockSpec(memory_space=pl.ANY)],
            out_specs=pl.BlockSpec(memory_space=pl.ANY),
            scratch_shapes=[pltpu.VMEM((2, tm * TOKEN_TILE, 128), xn.dtype),
                            pltpu.VMEM((EXPERT_TM * TOKEN_TILE, 128), xn.dtype),
                            pltpu.SemaphoreType.DMA((2,)),
                            pltpu.SemaphoreType.DMA((2,))]),
        out_shape=jax.ShapeDtypeStruct((n_rows * TOKEN_TILE, 128), xn.dtype),
        compiler_params=_params("arbitrary"),
        name="moe_dispatch",
    )(last_tile_row, pos3, xn)


def _expert_kernel(te_ref, nv_ref, par_ref, nxt_ref, x_hbm, wgu_hbm, wd_hbm, y_ref,
                   xbuf, wgu_f, wd_f, wgu_b, wd_b, xsem, wsem, *, layer):
    i = pl.program_id(0)
    n_valid = nv_ref[0]
    tm = xbuf.shape[1] // TOKEN_TILE
    in_rows = tm * TOKEN_TILE
    valid = i < n_valid
    changed = te_ref[i] != te_ref[jnp.maximum(i - 1, 0)]
    first = jnp.logical_or(i == 0, changed)

    def xload(j):
        slot = j % EXPERT_X_BUFS
        src = pl.multiple_of(j * in_rows, in_rows)
        return pltpu.make_async_copy(x_hbm.at[pl.ds(src, in_rows)], xbuf.at[slot], xsem.at[slot])

    def wload(e, slot):
        return (pltpu.make_async_copy(wgu_hbm.at[layer, e], wgu_f.at[slot], wsem.at[slot]),
                pltpu.make_async_copy(wd_hbm.at[layer, e], wd_f.at[slot], wsem.at[slot]))

    @pl.when(i == 0)
    def _():
        for cp in wload(te_ref[0], par_ref[0]):
            cp.start()
        for j in range(EXPERT_X_BUFS - 1):
            @pl.when(j < n_valid)
            def _():
                xload(j).start()

    @pl.when(i + EXPERT_X_BUFS - 1 < n_valid)
    def _():
        xload(i + EXPERT_X_BUFS - 1).start()

    @pl.when(jnp.logical_and(valid, first))
    def _():
        slot = par_ref[i]
        for cp in wload(te_ref[i], slot):
            cp.wait()

        @pl.when(nxt_ref[i] >= 0)
        def _():
            for cp in wload(nxt_ref[i], 1 - slot):
                cp.start()

        wgu_b[...] = wgu_f[slot].astype(BF16)
        wd_b[...] = wd_f[slot].astype(BF16)

    @pl.when(valid)
    def _():
        xload(i).wait()
        x = _from_tiles(xbuf.at[i % EXPERT_X_BUFS], tm).astype(BF16)
        au = _nn(x, wgu_b[...])
        a = au[:, :D_EXPERT]
        u = au[:, D_EXPERT:]
        mid = (jax.nn.silu(a) * u).astype(BF16)
        _to_tiles(y_ref, _nn(mid, wd_b[...]))

    @pl.when(jnp.logical_not(valid))
    def _():
        y_ref[...] = jnp.zeros_like(y_ref)


def experts(xg, w_gu, w_down, layer, tile_expert, n_valid, seg_parity, seg_next, tm=EXPERT_TM):
    d = D_MODEL
    p = xg.shape[0] // TOKEN_TILE
    n_tiles = p // tm
    return pl.pallas_call(
        functools.partial(_expert_kernel, layer=layer),
        grid_spec=pltpu.PrefetchScalarGridSpec(
            num_scalar_prefetch=4,
            grid=(n_tiles,),
            in_specs=[pl.BlockSpec(memory_space=pl.ANY),
                      pl.BlockSpec(memory_space=pl.ANY),
                      pl.BlockSpec(memory_space=pl.ANY)],
            out_specs=pl.BlockSpec((tm * TOKEN_TILE, 128), lambda i, *_: (i, 0)),
            scratch_shapes=[pltpu.VMEM((EXPERT_X_BUFS, tm * TOKEN_TILE, 128), F32),
                            pltpu.VMEM((2, d, 2 * D_EXPERT), F32),
                            pltpu.VMEM((2, D_EXPERT, d), F32),
                            pltpu.VMEM((d, 2 * D_EXPERT), BF16),
                            pltpu.VMEM((D_EXPERT, d), BF16),
                            pltpu.SemaphoreType.DMA((EXPERT_X_BUFS,)),
                            pltpu.SemaphoreType.DMA((2,))]),
        out_shape=jax.ShapeDtypeStruct((p * TOKEN_TILE, 128), F32),
        compiler_params=_params("arbitrary"),
        name="moe_experts",
    )(tile_expert, n_valid, seg_parity, seg_next, xg, w_gu, w_down)


def _qkv_tail(x, gq_ref, gkv_ref, wqt_ref, wk_ref, wvt_ref, qt_ref, k_ref, vt_ref):
    y = x * lax.rsqrt(jnp.mean(x * x, axis=-1, keepdims=True) + NORM_EPS)
    xq = (y * gq_ref[...]).astype(BF16)
    xkv = (y * gkv_ref[...]).astype(BF16)
    qt_ref[0] = (_nt(wqt_ref[...], xq) * (B_HEAD_DIM ** -0.5 * LOG2E)).astype(qt_ref.dtype)
    k_ref[...] = _nn(xkv, wk_ref[...]).astype(k_ref.dtype)
    vt_ref[0] = _nt(wvt_ref[...], xkv).astype(vt_ref.dtype)


def _combine_kernel(pos_ref, pos_next_ref, h_ref, w_ref, y_ref, *rest, tail):
    if tail == "norm":
        g_ref, o_ref, buf, sem = rest
    else:
        gq_ref, gkv_ref, wqt_ref, wk_ref, wvt_ref, o_ref, qt_ref, k_ref, vt_ref, buf, sem = rest
    i = pl.program_id(0)
    tm = h_ref.shape[0]
    cur = i % 2

    def gather_tile(p_ref, half):
        def row_copy(r, k):
            src = pl.multiple_of(p_ref[0, 0, k * tm + r] * TOKEN_TILE, TOKEN_TILE)
            return pltpu.make_async_copy(y_ref.at[pl.ds(src, TOKEN_TILE)],
                                         buf.at[2 * half + k, pl.ds(r * TOKEN_TILE, TOKEN_TILE)],
                                         sem.at[half])

        def issue(blk, carry):
            for u in range(ROW_DMA_UNROLL):
                r = blk * ROW_DMA_UNROLL + u
                row_copy(r, 0).start(priority=u % 2)
                row_copy(r, 1).start(priority=(u + 1) % 2)
            return carry

        lax.fori_loop(0, tm // ROW_DMA_UNROLL, issue, 0)

    @pl.when(i == 0)
    def _():
        gather_tile(pos_ref, 0)

    @pl.when(i + 1 < pl.num_programs(0))
    def _():
        gather_tile(pos_next_ref, 1 - cur)

    for k in range(2):
        pltpu.make_async_copy(y_ref.at[pl.ds(0, tm * TOKEN_TILE)], buf.at[2 * cur + k], sem.at[cur]).wait()
    w = w_ref[...]
    out = (h_ref[...] + w[:, 0:1] * _from_tiles(buf.at[2 * cur], tm)
           + w[:, 1:2] * _from_tiles(buf.at[2 * cur + 1], tm))
    if tail == "norm":
        o_ref[...] = out * lax.rsqrt(jnp.mean(out * out, axis=-1, keepdims=True) + NORM_EPS) * g_ref[...]
    else:
        o_ref[...] = out
        _qkv_tail(out, gq_ref, gkv_ref, wqt_ref, wk_ref, wvt_ref, qt_ref, k_ref, vt_ref)


def combine(h, wgt_t, pos3, y, tail, tail_args, batch, seq, tm=COMBINE_TM):
    t, d = h.shape
    nt = t // tm
    nblk = seq // tm
    row = pl.BlockSpec((tm, d), lambda i: (i, 0))
    vec = pl.BlockSpec((1, d), lambda i: (0, 0))
    full = pl.BlockSpec((d, d), lambda i: (0, 0))
    tr = pl.BlockSpec((1, d, tm), lambda i: (i // nblk, 0, i % nblk))
    if tail == "norm":
        (g,) = tail_args
        extra_in, extra_specs = [g.reshape(1, d)], [vec]
        out_specs, out_shape = row, jax.ShapeDtypeStruct((t, d), F32)
    else:
        g_q, g_kv, wqt, wk, wvt = tail_args
        extra_in = [g_q.reshape(1, d), g_kv.reshape(1, d), wqt, wk, wvt]
        extra_specs = [vec, vec, full, full, full]
        out_specs = [row, tr, row, tr]
        out_shape = [jax.ShapeDtypeStruct((t, d), F32),
                     jax.ShapeDtypeStruct((batch, d, seq), BF16),
                     jax.ShapeDtypeStruct((t, d), BF16),
                     jax.ShapeDtypeStruct((batch, d, seq), BF16)]
    return pl.pallas_call(
        functools.partial(_combine_kernel, tail=tail),
        grid=(nt,),
        in_specs=[pl.BlockSpec((1, 1, 2 * tm), lambda i: (i, 0, 0), memory_space=pltpu.SMEM),
                  pl.BlockSpec((1, 1, 2 * tm), lambda i: (jnp.minimum(i + 1, nt - 1), 0, 0),
                               memory_space=pltpu.SMEM),
                  row,
                  pl.BlockSpec((tm, 2), lambda i: (i, 0)),
                  pl.BlockSpec(memory_space=pl.ANY)] + extra_specs,
        out_specs=out_specs,
        out_shape=out_shape,
        scratch_shapes=[pltpu.VMEM((4, tm * TOKEN_TILE, 128), F32), pltpu.SemaphoreType.DMA((2,))],
        compiler_params=_params("arbitrary"),
        name="moe_combine_" + tail,
    )(pos3, pos3, h, wgt_t, y, *extra_in)


def hier_moe_layer(h, routing, w_gu, w_down, layer, tail, tail_args, batch, seq):
    t, d = h.shape
    xn, eid, wgt, rank, cnt = routing
    counts = cnt[:, 0].astype(I32)
    padded = ((counts + EXPERT_TM - 1) // EXPERT_TM) * EXPERT_TM
    ends = jnp.cumsum(padded)
    offs = ends - padded
    n_rows = 2 * t + N_EXPERTS * EXPERT_TM
    n_tiles = n_rows // EXPERT_TM
    n_valid = (ends[-1] // EXPERT_TM).astype(I32).reshape(1)
    tile_start = jnp.arange(n_tiles, dtype=I32) * EXPERT_TM
    tile_start = jnp.minimum(tile_start, ends[-1] - 1)
    tile_expert = jnp.sum((ends[None, :] <= tile_start[:, None]).astype(I32), axis=1)
    onehot = eid[None] == jnp.arange(N_EXPERTS, dtype=I32)[:, None, None]
    pos = jnp.sum(jnp.where(onehot, offs[:, None, None], 0), axis=0) + rank

    def tiled(tm):
        return pos.reshape(2, t // tm, tm).transpose(1, 0, 2).reshape(t // tm, 1, 2 * tm)

    last_tile_row = jnp.concatenate([jnp.where(padded > 0, ends - EXPERT_TM, -1).astype(I32), n_valid])
    xg = dispatch(xn, tiled(DISPATCH_TM), last_tile_row, n_rows)
    nonempty = padded > 0
    ordinal = jnp.cumsum(nonempty.astype(I32)) - 1
    ids = jnp.arange(N_EXPERTS, dtype=I32)
    later = jnp.logical_and(nonempty[None, :], ids[None, :] > ids[:, None])
    next_expert = jnp.min(jnp.where(later, ids[None, :], N_EXPERTS), axis=1)
    next_expert = jnp.where(next_expert == N_EXPERTS, -1, next_expert).astype(I32)
    y = experts(xg, w_gu, w_down, layer, tile_expert, n_valid,
                (ordinal[tile_expert] % 2).astype(I32), next_expert[tile_expert])
    return combine(h, wgt.T, tiled(COMBINE_TM), y, tail, tail_args, batch, seq)


def _attn_kernel(q_ref, k_ref, vt_ref, lam_ref, g_ref, o_ref, qq_ref, m_ref, acc_ref,
                 a_ref, c_ref, *sp_refs, lambda_init, heads):
    s_refs, p_refs = sp_refs[:heads], sp_refs[heads:]
    qi = pl.program_id(2)
    tq, tk, hd = ATT_TQ, ATT_TK, 2 * B_HEAD_DIM
    feat = lax.broadcasted_iota(I32, (hd, tq), 0)
    for g in range(heads):
        qt = q_ref[0, g * hd:(g + 1) * hd, :]
        zero = jnp.zeros_like(qt)
        qq_ref[g, :, 0:tq] = jnp.where(feat < B_HEAD_DIM, qt, zero)
        qq_ref[g, :, tq:2 * tq] = jnp.where(feat >= B_HEAD_DIM, qt, zero)
    m_ref[...] = jnp.full_like(m_ref, -jnp.inf)
    acc_ref[...] = jnp.zeros_like(acc_ref)
    ones = jnp.ones((ATT_SUM_ROWS, tk), BF16)
    dyn0 = jnp.minimum(qi, 0)

    def scores(j, masked, nk, ps):
        off = pl.multiple_of(j * tk, tk)
        if masked:
            krow = lax.broadcasted_iota(I32, (nk, 2 * tq), 0)
            qcol = lax.broadcasted_iota(I32, (nk, 2 * tq), 1)
            visible = off + krow <= qi * tq + jnp.where(qcol >= tq, qcol - tq, qcol)
        for g in range(heads):
            kb = k_ref[pl.ds(off, nk), g * hd:(g + 1) * hd]
            s = _nn(kb, qq_ref[g])
            if masked:
                s = jnp.where(visible, s, -jnp.inf)
            s_refs[g][0, 0:nk, :] = s
            c_ref[g] = jnp.max(s, axis=0, keepdims=True)
        for g in range(heads):
            m_old = m_ref[g]
            m_new = jnp.maximum(m_old, c_ref[g])
            a_ref[ps, g] = jnp.exp2(m_old - m_new)
            m_ref[g] = m_new
            for c in range(0, nk, ATT_CHUNK):
                p_refs[ps * heads + g][0, c:c + ATT_CHUNK, :] = jnp.exp2(
                    s_refs[g][dyn0, c:c + ATT_CHUNK, :] - m_new).astype(BF16)

    def accumulate(j, nk, ps):
        off = pl.multiple_of(j * tk, tk)
        for g in range(heads):
            vtb = vt_ref[0, g * hd:(g + 1) * hd, pl.ds(off, nk)]
            lhs = jnp.concatenate([vtb, ones[:, 0:nk]], axis=0)
            acc_ref[g] = a_ref[ps, g] * acc_ref[g] + _nn(lhs, p_refs[ps * heads + g][dyn0, 0:nk, :])

    def step(j, masked, nk=tk):
        scores(j, masked, nk, 0)
        accumulate(j, nk, 0)

    n_full = (qi * tq) // tk

    def two_steps(jj, carry):
        scores(2 * jj, False, tk, 0)
        scores(2 * jj + 1, False, tk, 1)
        accumulate(2 * jj, tk, 0)
        accumulate(2 * jj + 1, tk, 1)
        return carry

    lax.fori_loop(0, n_full // 2, two_steps, 0)

    @pl.when(n_full % 2 == 1)
    def _():
        step(n_full - 1, False)

    first_part = (qi * tq) % tk + tq <= tk // 2

    @pl.when(first_part)
    def _():
        step(n_full, True, tk // 2)

    @pl.when(jnp.logical_not(first_part))
    def _():
        step(n_full, True)

    lam = lam_ref[...]
    lam_full = (jnp.exp(jnp.sum(lam[0:1] * lam[1:2], axis=-1, keepdims=True))
                - jnp.exp(jnp.sum(lam[2:3] * lam[3:4], axis=-1, keepdims=True)) + lambda_init)
    for g in range(heads):
        acc = acc_ref[g]
        on = acc[:hd] / acc[hd:hd + 1]
        ot = on[:, :tq] - lam_full * on[:, tq:]
        o = ot.T
        o = o * lax.rsqrt(jnp.mean(o * o, axis=-1, keepdims=True) + SUBLN_EPS) * g_ref[...]
        o_ref[:, g * hd:(g + 1) * hd] = (o * (1.0 - lambda_init)).astype(o_ref.dtype)


def diff_attn(qt, k, vt, lam, subln_g, batch, seq, lambda_init, heads=ATT_HEADS):
    t, d = k.shape
    nq = seq // ATT_TQ
    hd = 2 * B_HEAD_DIM
    w = heads * hd
    return pl.pallas_call(
        functools.partial(_attn_kernel, lambda_init=lambda_init, heads=heads),
        grid=(batch, B_HEADS // heads, nq),
        in_specs=[pl.BlockSpec((1, w, ATT_TQ), lambda b, h, i: (b, h, i)),
                  pl.BlockSpec((seq, w), lambda b, h, i: (b, h)),
                  pl.BlockSpec((1, w, seq), lambda b, h, i: (b, h, 0)),
                  pl.BlockSpec(lam.shape, lambda b, h, i: (0, 0)),
                  pl.BlockSpec((1, hd), lambda b, h, i: (0, 0))],
        out_specs=pl.BlockSpec((ATT_TQ, w), lambda b, h, i: (b * nq + i, h)),
        out_shape=jax.ShapeDtypeStruct((t, d), BF16),
        scratch_shapes=[pltpu.VMEM((heads, hd, 2 * ATT_TQ), BF16),
                        pltpu.VMEM((heads, 1, 2 * ATT_TQ), F32),
                        pltpu.VMEM((heads, hd + ATT_SUM_ROWS, 2 * ATT_TQ), F32),
                        pltpu.VMEM((2, heads, 1, 2 * ATT_TQ), F32),
                        pltpu.VMEM((heads, 1, 2 * ATT_TQ), F32)]
        + [pltpu.VMEM((1, ATT_TK, 2 * ATT_TQ), F32) for _ in range(heads)]
        + [pltpu.VMEM((1, ATT_TK, 2 * ATT_TQ), BF16) for _ in range(2 * heads)],
        compiler_params=_params("parallel", "parallel", "arbitrary"),
        name="diff_attn",
    )(qt, k, vt, lam, subln_g.reshape(1, hd))


def kernel(x, a_norm_g, a_w_in, a_lb, a_onorm_g, a_w_out, kv_norm_g, w_kv, b_norm_g, b_w_q, b_lam,
           b_subln_g, b_w_out, ffn_norm_g, router_g_w, router_g_b, router_e_w, router_e_b,
           expert_w_gu, expert_w_down, final_norm_g):
    batch, seq, d = x.shape
    assert d == D_MODEL and a_norm_g.shape[0] == 1 and b_norm_g.shape[0] == 1
    assert seq % max(2 * HG_BLOCK, ATT_TK, ROUTER_TM) == 0
    t = batch * seq
    h = x.reshape(t, d)

    h, routing = hgrn2_layer(h, a_norm_g[0], a_w_in[0].astype(BF16), a_lb, a_onorm_g[0],
                             a_w_out[0].astype(BF16), seq,
                             (ffn_norm_g[0], router_g_w[0], router_g_b[0], router_e_w[0], router_e_b[0]))
    qkv_args = (b_norm_g[0], kv_norm_g, b_w_q[0].T.astype(BF16),
                w_kv[:, :d].astype(BF16), w_kv[:, d:].T.astype(BF16))
    h, qt, k, vt = hier_moe_layer(h, routing, expert_w_gu, expert_w_down, 0, "qkv", qkv_args, batch, seq)

    lambda_init = 0.8 - 0.6 * math.exp(-0.3 * 1)
    o = diff_attn(qt, k, vt, b_lam[0], b_subln_g[0], batch, seq, lambda_init)
    h, routing = proj_res_router(o, b_w_out[0].astype(BF16), h,
                                 (ffn_norm_g[1], router_g_w[1], router_g_b[1], router_e_w[1], router_e_b[1]))
    h = hier_moe_layer(h, routing, expert_w_gu, expert_w_down, 1, "norm", (final_norm_g,), batch, seq)
    return h.reshape(batch, seq, d)
```
